```python
import jax, jax.numpy as jnp
from jax import lax
import numpy as np

D_MODEL = 2048
BATCH = 2
SEQ = 4096
DEPTH = 1

N_META = 16
EPS = 1e-6
D_MIX = D_MODEL
D_CONV = 1024
CONV_GROUPS = 16
CONV_WIDTH = 3
N_HEADS = 8
QK_NOPE = 128
QK_ROPE = 64
V_DIM = 128
Q_LORA = 512
KV_LORA = 256
ROPE_THETA = 10000.0
Q_BLOCK = 128
D_IN = 3 * D_CONV + Q_LORA + KV_LORA + QK_ROPE
N_EXPERTS = 32
TOP_K = 4
D_FF = D_MODEL
SWIGLU_LIMIT = 7.0
SWIGLU_ALPHA = 1.702
MOE_BLOCK = 128

kernel_name = "hymba_shortconv_mla_moe_layer"


def rmsnorm(x, g):
    xf = x.astype(jnp.float32)
    y = xf * lax.rsqrt(jnp.mean(xf * xf, axis=-1, keepdims=True) + EPS)
    return (y * g.astype(jnp.float32)).astype(x.dtype)


def group_rmsnorm(x, g, n_groups):
    shp = x.shape
    xf = x.astype(jnp.float32).reshape(shp[:-1] + (n_groups, shp[-1] // n_groups))
    y = xf * lax.rsqrt(jnp.mean(xf * xf, axis=-1, keepdims=True) + EPS)
    return (y.reshape(shp) * g.astype(jnp.float32)).astype(x.dtype)


def short_conv(b_gate, c_gate, u_in, conv_w):
    u = c_gate * u_in
    L = u.shape[1]
    u_pad = jnp.pad(u, ((0, 0), (CONV_WIDTH - 1, 0), (0, 0)))
    y = conv_w[CONV_WIDTH - 1] * u_pad[:, CONV_WIDTH - 1:CONV_WIDTH - 1 + L]
    for j in range(CONV_WIDTH - 1):
        y = y + conv_w[j] * u_pad[:, j:j + L]
    return b_gate * y


def rope_cos_sin(L):
    half = QK_ROPE // 2
    pos = jnp.arange(L, dtype=jnp.float32)
    inv_freq = ROPE_THETA ** (-jnp.arange(half, dtype=jnp.float32) / half)
    ang = pos[:, None] * inv_freq[None, :]
    return jnp.cos(ang), jnp.sin(ang)


def apply_rope(x, cos, sin):
    half = x.shape[-1] // 2
    xf = x.astype(jnp.float32)
    x1, x2 = xf[..., :half], xf[..., half:]
    out = jnp.concatenate([x1 * cos - x2 * sin, x1 * sin + x2 * cos], axis=-1)
    return out.astype(x.dtype)


def pad_seq(t, n):
    widths = [(0, 0)] * t.ndim
    widths[1] = (0, n)
    return jnp.pad(t, widths)


def mla_attention(q_lat, kv_lat, k_rope, g_q, w_q_up, g_kv, w_kv_up):
    B, L, _ = q_lat.shape
    q = (rmsnorm(q_lat, g_q) @ w_q_up).reshape(B, L, N_HEADS, QK_NOPE + QK_ROPE)
    q_nope, q_pe = q[..., :QK_NOPE], q[..., QK_NOPE:]
    kv = (rmsnorm(kv_lat, g_kv) @ w_kv_up).reshape(B, L, N_HEADS, QK_NOPE + V_DIM)
    k_nope, v = kv[..., :QK_NOPE], kv[..., QK_NOPE:]
    cos, sin = rope_cos_sin(L)
    q_pe = apply_rope(q_pe, cos[:, None, :], sin[:, None, :])
    k_pe = apply_rope(k_rope, cos, sin)

    n_blk = -(-L // Q_BLOCK)
    Lp = n_blk * Q_BLOCK
    extra = Lp - L
    q_nope, q_pe = pad_seq(q_nope, extra), pad_seq(q_pe, extra)
    k_nope, k_pe, v = pad_seq(k_nope, extra), pad_seq(k_pe, extra), pad_seq(v, extra)
    qn_blk = q_nope.reshape(B, n_blk, Q_BLOCK, N_HEADS, QK_NOPE).transpose(1, 0, 2, 3, 4)
    qr_blk = q_pe.reshape(B, n_blk, Q_BLOCK, N_HEADS, QK_ROPE).transpose(1, 0, 2, 3, 4)
    q_pos = jnp.arange(Lp).reshape(n_blk, Q_BLOCK)
    k_pos = jnp.arange(Lp)
    scale = (QK_NOPE + QK_ROPE) ** -0.5

    def one_block(args):
        qn, qr, qp = args
        s = jnp.einsum("bqhd,bkhd->bhqk", qn, k_nope) + jnp.einsum("bqhr,bkr->bhqk", qr, k_pe)
        s = s.astype(jnp.float32) * scale
        causal = k_pos[None, :] <= qp[:, None]
        s = jnp.where(causal[None, None], s, -jnp.inf)
        p = jax.nn.softmax(s, axis=-1).astype(v.dtype)
        return jnp.einsum("bhqk,bkhd->bqhd", p, v)

    o = lax.map(one_block, (qn_blk, qr_blk, q_pos))
    o = o.transpose(1, 0, 2, 3, 4).reshape(B, Lp, N_HEADS * V_DIM)
    return o[:, :L]


def moe_ffn(xn, w_router, b_router, w_gate_up, b_gate_up, w_down, b_down):
    B, L, D = xn.shape
    T = B * L
    xf = xn.reshape(T, D)
    logits = xf.astype(jnp.float32) @ w_router.astype(jnp.float32) + b_router.astype(jnp.float32)
    top_v, top_i = lax.top_k(logits, TOP_K)
    gates = jax.nn.softmax(top_v, axis=-1)

    A = T * TOP_K
    flat_e = top_i.reshape(A)
    flat_tok = jnp.arange(A) // TOP_K
    flat_g = gates.reshape(A)
    order = jnp.argsort(flat_e)
    s_e, s_tok, s_g = flat_e[order], flat_tok[order], flat_g[order]

    counts = jnp.bincount(flat_e, length=N_EXPERTS)
    padded = (counts + MOE_BLOCK - 1) // MOE_BLOCK * MOE_BLOCK
    pad_end = jnp.cumsum(padded)
    pad_start = pad_end - padded
    grp_start = jnp.cumsum(counts) - counts
    rank = jnp.arange(A) - grp_start[s_e]
    dest = pad_start[s_e] + rank
    n_blocks = (A + N_EXPERTS * (MOE_BLOCK - 1) + MOE_BLOCK - 1) // MOE_BLOCK
    rows = jnp.zeros((n_blocks * MOE_BLOCK, D), xn.dtype).at[dest].set(xf[s_tok])
    blk_start = jnp.arange(n_blocks) * MOE_BLOCK
    blk_e = jnp.minimum(jnp.sum(pad_end[None, :] <= blk_start[:, None], axis=1), N_EXPERTS - 1)

    def expert_block(args):
        xb, e = args
        gu = xb @ w_gate_up[e] + b_gate_up[e]
        x_glu = jnp.minimum(gu[:, :D_FF], SWIGLU_LIMIT)
        x_lin = jnp.clip(gu[:, D_FF:], -SWIGLU_LIMIT, SWIGLU_LIMIT)
        act = (x_lin + 1.0) * (x_glu * jax.nn.sigmoid(SWIGLU_ALPHA * x_glu))
        return act @ w_down[e] + b_down[e]

    y_rows = lax.map(expert_block, (rows.reshape(n_blocks, MOE_BLOCK, D), blk_e)).reshape(-1, D)
    y = y_rows[dest] * s_g[:, None].astype(xn.dtype)
    out = jnp.zeros((T, D), xn.dtype).at[s_tok].add(y)
    return out.reshape(B, L, D)


def setup_inputs(seed: int = 0) -> dict:
    key = jax.random.key(seed)
    ks = jax.random.split(key, 24)

    def nrm(k, shape, scale):
        return jax.random.normal(k, shape, jnp.float32) * scale

    def gain(k, shape):
        return 1.0 + 0.02 * jax.random.normal(k, shape, jnp.float32)

    Ld = DEPTH
    return {
        "x": nrm(ks[0], (BATCH, SEQ, D_MODEL), 1.0),
        "meta_tokens": nrm(ks[1], (N_META, D_MODEL), 1.0),
        "g_mix": gain(ks[2], (Ld, D_MODEL)),
        "w_in": nrm(ks[3], (Ld, D_MODEL, D_IN), D_MODEL ** -0.5),
        "conv_w": nrm(ks[4], (Ld, CONV_WIDTH, D_CONV), CONV_WIDTH ** -0.5),
        "g_q": gain(ks[5], (Ld, Q_LORA)),
        "w_q_up": nrm(ks[6], (Ld, Q_LORA, N_HEADS * (QK_NOPE + QK_ROPE)), Q_LORA ** -0.5),
        "g_kv": gain(ks[7], (Ld, KV_LORA)),
        "w_kv_up": nrm(ks[8], (Ld, KV_LORA, N_HEADS * (QK_NOPE + V_DIM)), KV_LORA ** -0.5),
        "g_conv_out": gain(ks[9], (Ld, D_CONV)),
        "g_attn_out": gain(ks[10], (Ld, N_HEADS * V_DIM)),
        "w_out": nrm(ks[11], (Ld, D_MIX, D_MODEL), D_MIX ** -0.5),
        "g_ffn": gain(ks[12], (Ld, D_MODEL)),
        "w_router": nrm(ks[13], (Ld, D_MODEL, N_EXPERTS), D_MODEL ** -0.5),
        "b_router": nrm(ks[14], (Ld, N_EXPERTS), 0.01),
        "w_gate_up": nrm(ks[15], (Ld, N_EXPERTS, D_MODEL, 2 * D_FF), D_MODEL ** -0.5),
        "b_gate_up": nrm(ks[16], (Ld, N_EXPERTS, 2 * D_FF), 0.02),
        "w_down": nrm(ks[17], (Ld, N_EXPERTS, D_FF, D_MODEL), D_FF ** -0.5),
        "b_down": nrm(ks[18], (Ld, N_EXPERTS, D_MODEL), 0.02),
        "g_final": gain(ks[19], (D_MODEL,)),
    }


def reference(x, meta_tokens, g_mix, w_in, conv_w, g_q, w_q_up, g_kv, w_kv_up, g_conv_out, g_attn_out,
              w_out, g_ffn, w_router, b_router, w_gate_up, b_gate_up, w_down, b_down, g_final):
    B = x.shape[0]
    meta = jnp.broadcast_to(meta_tokens[None].astype(x.dtype), (B, N_META, D_MODEL))
    h = jnp.concatenate([meta, x], axis=1)

    o_b, o_c, o_u = D_CONV, 2 * D_CONV, 3 * D_CONV
    o_q = o_u + Q_LORA
    o_kv = o_q + KV_LORA
    for l in range(DEPTH):
        hn = rmsnorm(h, g_mix[l])
        proj = hn @ w_in[l]
        conv_out = short_conv(proj[..., :o_b], proj[..., o_b:o_c], proj[..., o_c:o_u], conv_w[l])
        attn_out = mla_attention(proj[..., o_u:o_q], proj[..., o_q:o_kv], proj[..., o_kv:],
                                 g_q[l], w_q_up[l], g_kv[l], w_kv_up[l])
        mixed = jnp.concatenate([group_rmsnorm(conv_out, g_conv_out[l], CONV_GROUPS),
                                 group_rmsnorm(attn_out, g_attn_out[l], N_HEADS)], axis=-1)
        h = h + mixed @ w_out[l]
        h = h + moe_ffn(rmsnorm(h, g_ffn[l]), w_router[l], b_router[l], w_gate_up[l], b_gate_up[l],
                        w_down[l], b_down[l])

    y = rmsnorm(h, g_final)
    return y[:, N_META:]
```

```python
import functools

import jax
import jax.numpy as jnp
from jax import lax
from jax.experimental import pallas as pl
from jax.experimental.pallas import tpu as pltpu

N_META = 16
EPS = 1e-6
D_CONV = 1024
CONV_GROUPS = 16
CONV_WIDTH = 3
N_HEADS = 8
QK_NOPE = 128
QK_ROPE = 64
V_DIM = 128
Q_LORA = 512
KV_LORA = 256
ROPE_THETA = 10000.0
N_EXPERTS = 32
TOP_K = 4
SWIGLU_LIMIT = 7.0
SWIGLU_ALPHA = 1.702

LANES = 128
SUBLANES = 8
META_ROWS = 128
HEAD_W = 2 * LANES
ATTN_W = N_HEADS * V_DIM
VMEM_LIMIT = 56 * 1024 * 1024

O_B, O_C, O_U = 0, D_CONV, 2 * D_CONV
O_Q = 3 * D_CONV
O_KV = O_Q + Q_LORA
O_KPE = O_KV + KV_LORA
O_KROT = O_KPE + LANES
W1_COLS = O_KROT + LANES

F32 = jnp.float32
BF16 = jnp.bfloat16
NT_DIMS = (((1,), (1,)), ((), ()))


def _rms(x, g):
    return x * lax.rsqrt(jnp.mean(x * x, axis=-1, keepdims=True) + EPS) * g


def _dot(a, b):
    return jnp.dot(a, b, preferred_element_type=F32)


def _resident(shape):
    zeros = (0,) * len(shape)
    return pl.BlockSpec(shape, lambda *_: zeros, pipeline_mode=pl.Buffered(1))


def _mix_in_kernel(x_ref, gmix_ref, w1_ref, cw_ref, gq_ref, wq_ref, gkv_ref, wkv_ref, gco_ref, gmat_ref,
                   tab_ref, uinit_ref,
                   convn_ref, q_ref, kn_ref, kpe_ref, v_ref, utail_ref,
                   ubuf_ref, *, tm, scale):
    i = pl.program_id(1)
    hn = _rms(x_ref[0], gmix_ref[...]).astype(BF16)

    def proj(lo, hi):
        return _dot(hn, w1_ref[:, lo:hi])

    @pl.when(i == 0)
    def _():
        ubuf_ref[0:SUBLANES] = uinit_ref[...]

    @pl.when(i > 0)
    def _():
        ubuf_ref[0:SUBLANES] = ubuf_ref[tm:tm + SUBLANES]

    u = proj(O_C, O_U) * proj(O_U, O_Q)
    ubuf_ref[SUBLANES:SUBLANES + tm] = u
    cw = cw_ref[...]
    y = (cw[2:3] * u + cw[1:2] * ubuf_ref[SUBLANES - 1:SUBLANES - 1 + tm]
         + cw[0:1] * ubuf_ref[SUBLANES - 2:SUBLANES - 2 + tm])
    co = proj(O_B, O_C) * y
    ss = _dot((co * co).astype(BF16), gmat_ref[...])
    group = D_CONV // CONV_GROUPS
    convn_ref[0] = (co * lax.rsqrt(ss * (1.0 / group) + EPS) * gco_ref[...]).astype(BF16)
    utail_ref[...] = ubuf_ref[tm:tm + SUBLANES]

    cos = tab_ref[:, :LANES]
    sin = tab_ref[:, LANES:]

    qn = _rms(proj(O_Q, O_KV), gq_ref[...]).astype(BF16)
    qa = _dot(qn, wq_ref[:, :N_HEADS * HEAD_W])
    qb = _dot(qn, wq_ref[:, N_HEADS * HEAD_W:])
    for h in range(N_HEADS):
        c0 = h * HEAD_W
        q_ref[0, :, c0:c0 + LANES] = (qa[:, c0:c0 + LANES] * scale).astype(BF16)
        pe = qa[:, c0 + LANES:c0 + HEAD_W] * cos + qb[:, h * LANES:(h + 1) * LANES] * sin
        q_ref[0, :, c0 + LANES:c0 + HEAD_W] = (pe * scale).astype(BF16)

    kvn = _rms(proj(O_KV, O_KPE), gkv_ref[...]).astype(BF16)
    kv = _dot(kvn, wkv_ref[...])
    kn_ref[0] = kv[:, :ATTN_W].astype(BF16)
    v_ref[0] = kv[:, ATTN_W:].astype(BF16)
    kpe_ref[0] = (proj(O_KPE, O_KROT) * cos + proj(O_KROT, W1_COLS) * sin).astype(BF16)


def _mix_in(x3, uinit, tab, w, *, tm):
    b, l, d = x3.shape
    nt = l // tm
    kern = functools.partial(_mix_in_kernel, tm=tm, scale=float((QK_NOPE + QK_ROPE) ** -0.5))
    row = lambda width: pl.BlockSpec((1, tm, width), lambda bi, i: (bi, i, 0))
    return pl.pallas_call(
        kern,
        grid=(b, nt),
        in_specs=[
            row(d),
            _resident((1, d)),
            _resident((d, W1_COLS)),
            _resident((SUBLANES, D_CONV)),
            _resident((1, Q_LORA)),
            _resident((Q_LORA, N_HEADS * (HEAD_W + LANES))),
            _resident((1, KV_LORA)),
            _resident((KV_LORA, 2 * ATTN_W)),
            _resident((1, D_CONV)),
            _resident((D_CONV, D_CONV)),
            pl.BlockSpec((tm, 2 * LANES), lambda bi, i: (i, 0)),
            _resident((SUBLANES, D_CONV)),
        ],
        out_specs=[
            row(D_CONV), row(N_HEADS * HEAD_W), row(ATTN_W), row(LANES), row(ATTN_W),
            pl.BlockSpec((SUBLANES, D_CONV), lambda bi, i: (bi * nt + i, 0)),
        ],
        out_shape=[
            jax.ShapeDtypeStruct((b, l, D_CONV), BF16),
            jax.ShapeDtypeStruct((b, l, N_HEADS * HEAD_W), BF16),
            jax.ShapeDtypeStruct((b, l, ATTN_W), BF16),
            jax.ShapeDtypeStruct((b, l, LANES), BF16),
            jax.ShapeDtypeStruct((b, l, ATTN_W), BF16),
            jax.ShapeDtypeStruct((b * nt * SUBLANES, D_CONV), F32),
        ],
        scratch_shapes=[pltpu.VMEM((tm + SUBLANES, D_CONV), F32)],
        compiler_params=pltpu.CompilerParams(
            dimension_semantics=("arbitrary", "arbitrary"), vmem_limit_bytes=VMEM_LIMIT),
        name="mix_in",
    )(x3, w["g_mix"], w["w1"], w["conv_w"], w["g_q"], w["wq"], w["g_kv"], w["wkv"], w["g_conv_out"],
      w["gmat"], tab, uinit)


def _attn_kernel(q_ref, kn_ref, kpe_ref, v_ref, mkn_ref, mkpe_ref, mv_ref, g_ref, o_ref,
                 m_ref, l_ref, acc_ref, *, tq):
    qi = pl.program_id(2)
    q = q_ref[0]

    km = jnp.concatenate([mkn_ref[0], mkpe_ref[0]], axis=-1)
    s = lax.dot_general(q, km, NT_DIMS, preferred_element_type=F32)
    col = lax.broadcasted_iota(jnp.int32, s.shape, 1)
    s = jnp.where(col >= META_ROWS - N_META, s, -jnp.inf)
    m0 = jnp.max(s, axis=-1, keepdims=True)
    p = jnp.exp(s - m0)
    m_ref[...] = m0
    l_ref[...] = jnp.sum(p, axis=-1, keepdims=True)
    acc_ref[...] = _dot(p.astype(BF16), mv_ref[0])

    def step(kb, diagonal):
        off = pl.multiple_of(kb * tq, tq)
        k = jnp.concatenate([kn_ref[0, pl.ds(off, tq), :], kpe_ref[0, pl.ds(off, tq), :]], axis=-1)
        s = lax.dot_general(q, k, NT_DIMS, preferred_element_type=F32)
        if diagonal:
            r = lax.broadcasted_iota(jnp.int32, s.shape, 0)
            c = lax.broadcasted_iota(jnp.int32, s.shape, 1)
            s = jnp.where(c <= r, s, -jnp.inf)
        m_prev = m_ref[...]
        m_new = jnp.maximum(m_prev, jnp.max(s, axis=-1, keepdims=True))
        alpha = jnp.exp(m_prev - m_new)
        p = jnp.exp(s - m_new)
        l_ref[...] = alpha * l_ref[...] + jnp.sum(p, axis=-1, keepdims=True)
        acc_ref[...] = alpha * acc_ref[...] + _dot(p.astype(BF16), v_ref[0, pl.ds(off, tq), :])
        m_ref[...] = m_new

    def body(kb, carry):
        step(kb, False)
        return carry

    lax.fori_loop(0, qi, body, 0)
    step(qi, True)

    o = acc_ref[...] / l_ref[...]
    o_ref[0] = _rms(o, g_ref[...]).astype(BF16)


def _attention(q, kn, kpe, v, mkn, mkpe, mv, g_attn, *, tq):
    b, l, _ = q.shape
    nq = l // tq
    kern = functools.partial(_attn_kernel, tq=tq)
    return pl.pallas_call(
        kern,
        grid=(b, N_HEADS, nq),
        in_specs=[
            pl.BlockSpec((1, tq, HEAD_W), lambda bi, h, i: (bi, i, h)),
            pl.BlockSpec((1, l, LANES), lambda bi, h, i: (bi, 0, h)),
            pl.BlockSpec((1, l, LANES), lambda bi, h, i: (bi, 0, 0)),
            pl.BlockSpec((1, l, V_DIM), lambda bi, h, i: (bi, 0, h)),
            pl.BlockSpec((1, META_ROWS, LANES), lambda bi, h, i: (0, 0, h)),
            pl.BlockSpec((1, META_ROWS, LANES), lambda bi, h, i: (0, 0, 0)),
            pl.BlockSpec((1, META_ROWS, V_DIM), lambda bi, h, i: (0, 0, h)),
            pl.BlockSpec((1, V_DIM), lambda bi, h, i: (0, h)),
        ],
        out_specs=pl.BlockSpec((1, tq, V_DIM), lambda bi, h, i: (bi, i, h)),
        out_shape=jax.ShapeDtypeStruct((b, l, ATTN_W), BF16),
        scratch_shapes=[pltpu.VMEM((tq, 1), F32), pltpu.VMEM((tq, 1), F32), pltpu.VMEM((tq, V_DIM), F32)],
        compiler_params=pltpu.CompilerParams(
            dimension_semantics=("arbitrary", "arbitrary", "arbitrary"), vmem_limit_bytes=VMEM_LIMIT),
        name="attn",
    )(q, kn, kpe, v, mkn, mkpe, mv, g_attn)


def _mix_out_kernel(convn_ref, attn_ref, x_ref, wo_ref, gffn_ref, wr_ref, br_ref,
                    h1_ref, xn_ref, ri_ref, rf_ref, cnt_ref, carry_ref, *, tm):
    @pl.when(pl.program_id(0) == 0)
    def _():
        carry_ref[...] = jnp.zeros_like(carry_ref)

    h1 = x_ref[...] + _dot(convn_ref[...], wo_ref[:D_CONV]) + _dot(attn_ref[...], wo_ref[D_CONV:])
    h1_ref[...] = h1
    xn = _rms(h1, gffn_ref[...])
    xn_ref[...] = xn

    xh = xn.astype(BF16)
    xl = (xn - xh.astype(F32)).astype(BF16)
    ph = _dot(xh, wr_ref[...])
    logits = ph[:, :LANES] + ph[:, LANES:] + _dot(xl, wr_ref[:, :LANES]) + br_ref[...]

    lane = lax.broadcasted_iota(jnp.int32, (tm, LANES), 1).astype(F32)
    work = logits
    top_v, top_i, onehots = [], [], []
    for _ in range(TOP_K):
        mk = jnp.max(work, axis=-1, keepdims=True)
        ik = jnp.min(jnp.where(work == mk, lane, float(LANES)), axis=-1, keepdims=True)
        oh = lane == ik
        work = jnp.where(oh, -jnp.inf, work)
        top_v.append(mk)
        top_i.append(ik)
        onehots.append(oh)

    ex = [jnp.exp(v - top_v[0]) for v in top_v]
    denom = ex[0] + ex[1] + ex[2] + ex[3]
    gates = [e / denom for e in ex]

    sel = jnp.zeros((tm, LANES), F32)
    for oh in onehots:
        sel = sel + oh.astype(F32)
    r = lax.broadcasted_iota(jnp.int32, (tm, tm), 0)
    c = lax.broadcasted_iota(jnp.int32, (tm, tm), 1)
    tri = jnp.where(c < r, 1.0, 0.0).astype(BF16)
    cum = _dot(tri, sel.astype(BF16)) + carry_ref[...]
    carry_ref[...] = carry_ref[...] + jnp.sum(sel, axis=0, keepdims=True)
    cnt_ref[...] = carry_ref[...]

    ri = jnp.zeros((tm, LANES), F32)
    rf = jnp.zeros((tm, LANES), F32)
    for k in range(TOP_K):
        rank_k = jnp.sum(jnp.where(onehots[k], cum, 0.0), axis=-1, keepdims=True)
        ri = jnp.where(lane == float(k), top_i[k], ri)
        ri = jnp.where(lane == float(TOP_K + k), rank_k, ri)
        rf = jnp.where(lane == float(k), gates[k], rf)
    ri_ref[...] = ri.astype(jnp.int32)
    rf_ref[...] = rf


def _mix_out(convn, attn, x2, w, *, tm):
    t, d = x2.shape
    kern = functools.partial(_mix_out_kernel, tm=tm)
    row = lambda width: pl.BlockSpec((tm, width), lambda i: (i, 0))
    return pl.pallas_call(
        kern,
        grid=(t // tm,),
        in_specs=[
            row(D_CONV), row(ATTN_W), row(d),
            _resident((D_CONV + ATTN_W, d)),
            _resident((1, d)),
            _resident((d, 2 * LANES)),
            _resident((1, LANES)),
        ],
        out_specs=[row(d), row(d), row(LANES), row(LANES), pl.BlockSpec((1, LANES), lambda i: (0, 0))],
        out_shape=[
            jax.ShapeDtypeStruct((t, d), F32),
            jax.ShapeDtypeStruct((t, d), F32),
            jax.ShapeDtypeStruct((t, LANES), jnp.int32),
            jax.ShapeDtypeStruct((t, LANES), F32),
            jax.ShapeDtypeStruct((1, LANES), F32),
        ],
        scratch_shapes=[pltpu.VMEM((1, LANES), F32)],
        compiler_params=pltpu.CompilerParams(
            dimension_semantics=("arbitrary",), vmem_limit_bytes=VMEM_LIMIT),
        name="mix_out",
    )(convn, attn, x2, w["wo"], w["g_ffn"], w["wr"], w["br"])


def _row_copy(src_hbm, src_row, dst_hbm, dst_row, sem):
    return pltpu.make_async_copy(src_hbm.at[pl.ds(src_row, 1)], dst_hbm.at[pl.ds(dst_row, 1)], sem)


def _dispatch_kernel(dest_ref, pend_ref, xn_hbm, xg_hbm, zbuf_ref, sem, zsem, *, tmd, sb):
    i = pl.program_id(0)

    @pl.when(i == 0)
    def _():
        zbuf_ref[...] = jnp.zeros_like(zbuf_ref)

        def zero_copy(e):
            end = pend_ref[e]
            start = pl.multiple_of(jnp.maximum(end - sb, 0), sb)
            return pltpu.make_async_copy(zbuf_ref, xg_hbm.at[pl.ds(start, sb)], zsem)

        def nonempty(e):
            prev = pend_ref[jnp.maximum(e - 1, 0)]
            return pend_ref[e] > jnp.where(e > 0, prev, 0)

        def start(e, carry):
            @pl.when(nonempty(e))
            def _():
                zero_copy(e).start()
            return carry

        def wait(e, carry):
            @pl.when(nonempty(e))
            def _():
                zero_copy(e).wait()
            return carry

        lax.fori_loop(0, N_EXPERTS, start, 0)
        lax.fori_loop(0, N_EXPERTS, wait, 0)
        _zero_tail(zbuf_ref, xg_hbm, pend_ref[N_EXPERTS - 1], sb, zsem)

    base = i * tmd

    def issue(r, carry):
        t = base + r
        for k in range(TOP_K):
            _row_copy(xn_hbm, t, xg_hbm, dest_ref[t * TOP_K + k], sem).start()
        return carry

    def drain(r, carry):
        for k in range(TOP_K):
            _row_copy(xn_hbm, 0, xg_hbm, 0, sem).wait()
        return carry

    lax.fori_loop(0, tmd, issue, 0)
    lax.fori_loop(0, tmd, drain, 0)


def _dispatch(dest, pad_end, xn, *, rows, tmd, sb):
    t, d = xn.shape
    kern = functools.partial(_dispatch_kernel, tmd=tmd, sb=sb)
    return pl.pallas_call(
        kern,
        grid_spec=pltpu.PrefetchScalarGridSpec(
            num_scalar_prefetch=2,
            grid=(t // tmd,),
            in_specs=[pl.BlockSpec(memory_space=pl.ANY)],
            out_specs=pl.BlockSpec(memory_space=pl.ANY),
            scratch_shapes=[pltpu.VMEM((sb, d), F32), pltpu.SemaphoreType.DMA, pltpu.SemaphoreType.DMA],
        ),
        out_shape=jax.ShapeDtypeStruct((rows, d), F32),
        compiler_params=pltpu.CompilerParams(
            dimension_semantics=("arbitrary",), vmem_limit_bytes=VMEM_LIMIT),
        name="dispatch",
    )(dest, pad_end, xn)


def _expert_kernel(ie_ref, irow_ref, inr_ref, tail_ref, xg_hbm, wg_ref, wl_ref, wd_ref, bg_ref, bl_ref, bd_ref, yg_hbm,
                   xst_ref, xb_ref, acc_ref, wgb_ref, wlb_ref, wdb_ref, sem_in, sem_out, *, tm, sb, nj):
    it = pl.program_id(0)
    j = pl.program_id(1)
    nrows = inr_ref[it]
    row0 = irow_ref[it]
    nsub = tm // sb

    def sub_rows(r):
        return pl.ds(r * sb, sb)

    def in_copy(r):
        src = xg_hbm.at[pl.ds(pl.multiple_of(row0 + r * sb, sb), sb)]
        return pltpu.make_async_copy(src, xst_ref.at[sub_rows(r)], sem_in.at[r])

    def out_copy(r):
        dst = yg_hbm.at[pl.ds(pl.multiple_of(row0 + r * sb, sb), sb)]
        return pltpu.make_async_copy(acc_ref.at[sub_rows(r)], dst, sem_out.at[r])

    def for_valid_sub_blocks(fn):
        for r in range(nsub):
            pl.when(r * sb < nrows)(functools.partial(fn, r))

    @pl.when(j == 0)
    def _():
        for_valid_sub_blocks(lambda r: in_copy(r).start())

        def land(r):
            in_copy(r).wait()
            xb_ref[sub_rows(r)] = xst_ref[sub_rows(r)].astype(BF16)

        for_valid_sub_blocks(land)

    @pl.when(nrows > 0)
    def _():
        wgb_ref[...] = wg_ref[0].astype(BF16)
        wlb_ref[...] = wl_ref[0].astype(BF16)
        wdb_ref[...] = wd_ref[0].astype(BF16)

    def compute(r):
        xs = xb_ref[sub_rows(r)]
        g = jnp.minimum(_dot(xs, wgb_ref[...]) + bg_ref[0], SWIGLU_LIMIT)
        lin = jnp.clip(_dot(xs, wlb_ref[...]) + bl_ref[0], -SWIGLU_LIMIT, SWIGLU_LIMIT)
        act = ((lin + 1.0) * (g * jax.nn.sigmoid(SWIGLU_ALPHA * g))).astype(BF16)
        y = _dot(act, wdb_ref[...])

        @pl.when(j == 0)
        def _():
            acc_ref[sub_rows(r)] = y + bd_ref[0]

        @pl.when(j > 0)
        def _():
            acc_ref[sub_rows(r)] += y

    for_valid_sub_blocks(compute)

    @pl.when(j == nj - 1)
    def _():
        for_valid_sub_blocks(lambda r: out_copy(r).start())
        for_valid_sub_blocks(lambda r: out_copy(r).wait())

    @pl.when((it == pl.num_programs(0) - 1) & (j == nj - 1))
    def _():
        acc_ref[sub_rows(0)] = jnp.zeros((sb, acc_ref.shape[1]), F32)
        _zero_tail(acc_ref.at[sub_rows(0)], yg_hbm, tail_ref[0], sb, sem_out.at[0])


def _zero_tail(zeros_vmem, dst_hbm, first_row, sb, sem):
    n_blocks = (dst_hbm.shape[0] - first_row) // sb

    def fill(s, carry):
        cp = pltpu.make_async_copy(zeros_vmem, dst_hbm.at[pl.ds(pl.multiple_of(first_row + s * sb, sb), sb)], sem)
        cp.start()
        cp.wait()
        return carry

    lax.fori_loop(0, n_blocks, fill, 0)


def _experts(item_e, item_row, item_nrows, tail, xg, w_gate_up, b_gate_up, w_down, b_down, *, tm, sb, fc):
    rows, d = xg.shape
    n_items = item_e.shape[0]
    dff = w_down.shape[1]
    nj = dff // fc
    nsub = tm // sb
    kern = functools.partial(_expert_kernel, tm=tm, sb=sb, nj=nj)

    def jeff(it, j, inr):
        return jnp.where(inr[it] > 0, j, nj - 1)

    return pl.pallas_call(
        kern,
        grid_spec=pltpu.PrefetchScalarGridSpec(
            num_scalar_prefetch=4,
            grid=(n_items, nj),
            in_specs=[
                pl.BlockSpec(memory_space=pl.ANY),
                pl.BlockSpec((1, d, fc), lambda it, j, ie, ir, inr, tl: (ie[it], 0, jeff(it, j, inr))),
                pl.BlockSpec((1, d, fc), lambda it, j, ie, ir, inr, tl: (ie[it], 0, nj + jeff(it, j, inr))),
                pl.BlockSpec((1, fc, d), lambda it, j, ie, ir, inr, tl: (ie[it], jeff(it, j, inr), 0)),
                pl.BlockSpec((1, 1, fc), lambda it, j, ie, ir, inr, tl: (ie[it], 0, jeff(it, j, inr))),
                pl.BlockSpec((1, 1, fc), lambda it, j, ie, ir, inr, tl: (ie[it], 0, nj + jeff(it, j, inr))),
                pl.BlockSpec((1, 1, d), lambda it, j, ie, ir, inr, tl: (ie[it], 0, 0)),
            ],
            out_specs=pl.BlockSpec(memory_space=pl.ANY),
            scratch_shapes=[
                pltpu.VMEM((tm, d), F32),
                pltpu.VMEM((tm, d), BF16),
                pltpu.VMEM((tm, d), F32),
                pltpu.VMEM((d, fc), BF16),
                pltpu.VMEM((d, fc), BF16),
                pltpu.VMEM((fc, d), BF16),
                pltpu.SemaphoreType.DMA((nsub,)),
                pltpu.SemaphoreType.DMA((nsub,)),
            ],
        ),
        out_shape=jax.ShapeDtypeStruct((rows, d), F32),
        compiler_params=pltpu.CompilerParams(
            dimension_semantics=("arbitrary", "arbitrary"), vmem_limit_bytes=VMEM_LIMIT),
        name="experts",
    )(item_e, item_row, item_nrows, tail, xg, w_gate_up, w_gate_up, w_down,
      b_gate_up, b_gate_up, b_down)


def _combine_kernel(dest_ref, yg_hbm, h1_ref, rf_ref, gfin_ref, o_ref, gbuf_ref, sem, *, tmc):
    base = pl.program_id(0) * tmc

    def issue(r, carry):
        for k in range(TOP_K):
            src = yg_hbm.at[pl.ds(dest_ref[(base + r) * TOP_K + k], 1)]
            pltpu.make_async_copy(src, gbuf_ref.at[k, pl.ds(r, 1)], sem).start()
        return carry

    def drain(r, carry):
        for k in range(TOP_K):
            pltpu.make_async_copy(yg_hbm.at[pl.ds(0, 1)], gbuf_ref.at[k, pl.ds(0, 1)], sem).wait()
        return carry

    lax.fori_loop(0, tmc, issue, 0)
    lax.fori_loop(0, tmc, drain, 0)

    gates = rf_ref[...]
    y = h1_ref[...]
    for k in range(TOP_K):
        y = y + gates[:, k:k + 1] * gbuf_ref[k]
    o_ref[...] = _rms(y, gfin_ref[...])


def _combine(dest, yg, h1, rf, g_final, *, tmc):
    t, d = h1.shape
    kern = functools.partial(_combine_kernel, tmc=tmc)
    return pl.pallas_call(
        kern,
        grid_spec=pltpu.PrefetchScalarGridSpec(
            num_scalar_prefetch=1,
            grid=(t // tmc,),
            in_specs=[
                pl.BlockSpec(memory_space=pl.ANY),
                pl.BlockSpec((tmc, d), lambda i, dest: (i, 0)),
                pl.BlockSpec((tmc, LANES), lambda i, dest: (i, 0)),
                pl.BlockSpec((1, d), lambda i, dest: (0, 0)),
            ],
            out_specs=pl.BlockSpec((tmc, d), lambda i, dest: (i, 0)),
            scratch_shapes=[pltpu.VMEM((TOP_K, tmc, d), F32), pltpu.SemaphoreType.DMA],
        ),
        out_shape=jax.ShapeDtypeStruct((t, d), F32),
        compiler_params=pltpu.CompilerParams(
            dimension_semantics=("arbitrary",), vmem_limit_bytes=VMEM_LIMIT),
        name="combine",
    )(dest, yg, h1, rf, g_final)


def _rotate_half_cols(w):
    half = w.shape[-1] // 2
    return jnp.concatenate([-w[..., half:], w[..., :half]], axis=-1)


def _prep_weights(g_mix, w_in, conv_w, g_q, w_q_up, g_kv, w_kv_up, g_conv_out, w_out, g_ffn, w_router, b_router):
    d = w_in.shape[0]
    k_rope = w_in[:, O_KPE:O_KPE + QK_ROPE]
    pad = jnp.zeros((d, LANES - QK_ROPE), F32)
    w1 = jnp.concatenate([w_in[:, :O_KPE], k_rope, pad, _rotate_half_cols(k_rope), pad], axis=1).astype(BF16)

    wq = w_q_up.reshape(Q_LORA, N_HEADS, QK_NOPE + QK_ROPE)
    nope, pe = wq[:, :, :QK_NOPE], wq[:, :, QK_NOPE:]
    hpad = jnp.zeros((Q_LORA, N_HEADS, LANES - QK_ROPE), F32)
    wqa = jnp.concatenate([nope, pe, hpad], axis=2).reshape(Q_LORA, N_HEADS * HEAD_W)
    wqb = jnp.concatenate([_rotate_half_cols(pe), hpad], axis=2).reshape(Q_LORA, N_HEADS * LANES)
    wq_all = jnp.concatenate([wqa, wqb], axis=1).astype(BF16)

    wkv = w_kv_up.reshape(KV_LORA, N_HEADS, QK_NOPE + V_DIM)
    wkv2 = jnp.concatenate([wkv[:, :, :QK_NOPE].reshape(KV_LORA, ATTN_W),
                            wkv[:, :, QK_NOPE:].reshape(KV_LORA, ATTN_W)], axis=1).astype(BF16)

    wr_pad = jnp.zeros((d, LANES), F32).at[:, :N_EXPERTS].set(w_router)
    wr_hi = wr_pad.astype(BF16)
    wr_lo = (wr_pad - wr_hi.astype(F32)).astype(BF16)
    br = jnp.full((1, LANES), -1e30, F32).at[0, :N_EXPERTS].set(b_router)

    grp = jnp.arange(D_CONV) // (D_CONV // CONV_GROUPS)
    cw = jnp.zeros((SUBLANES, D_CONV), F32).at[:CONV_WIDTH].set(conv_w)
    return {
        "g_mix": g_mix[None], "w1": w1, "conv_w": cw, "g_q": g_q[None], "wq": wq_all, "g_kv": g_kv[None],
        "wkv": wkv2, "g_conv_out": g_conv_out[None], "gmat": (grp[:, None] == grp[None, :]).astype(BF16),
        "wo": w_out.astype(BF16), "g_ffn": g_ffn[None], "wr": jnp.concatenate([wr_hi, wr_lo], axis=1), "br": br,
    }


def _rope_table(pos):
    half = QK_ROPE // 2
    inv_freq = ROPE_THETA ** (-jnp.arange(half, dtype=F32) / half)
    ang = pos[:, None] * inv_freq[None, :]
    c, s = jnp.cos(ang), jnp.sin(ang)
    z = jnp.zeros((pos.shape[0], LANES - QK_ROPE), F32)
    return jnp.concatenate([c, c, z, s, s, z], axis=1)


def _schedule(counts, eidx, rank, *, tm, sb, n_items):
    padded = (counts + sb - 1) // sb * sb
    pad_end = jnp.cumsum(padded)
    pad_start = pad_end - padded
    dest = (pad_start[eidx] + rank).reshape(-1).astype(jnp.int32)

    per_e = (counts + tm - 1) // tm
    it_end = jnp.cumsum(per_e)
    it_start = it_end - per_e
    ii = jnp.arange(n_items)
    e_of = jnp.minimum(jnp.searchsorted(it_end, ii, side="right"), N_EXPERTS - 1)
    valid = ii < it_end[-1]
    li = ii - it_start[e_of]
    item_row = jnp.where(valid, pad_start[e_of] + li * tm, 0)
    item_nrows = jnp.where(valid, jnp.clip(counts[e_of] - li * tm, 0, tm), 0)
    e_last = e_of[jnp.maximum(it_end[-1] - 1, 0)]
    item_e = jnp.where(valid, e_of, e_last)
    i32 = lambda a: a.astype(jnp.int32)
    return dest, i32(pad_end), i32(item_e), i32(item_row), i32(item_nrows)


def _moe_tiles(t):
    sb = 256
    tm = 5 * sb
    a = t * TOP_K
    rows = (a + N_EXPERTS * (sb - 1) + sb - 1) // sb * sb
    n_items = N_EXPERTS + a // tm
    return sb, tm, rows, n_items


def _layer(x, meta_tokens, w, w_gate_up, b_gate_up, w_down, b_down, g_attn_out, g_final, *,
           tm_in, tq, tm_out, tmd, tmc, fc):
    b, seq, d = x.shape
    t = b * seq

    meta_blk = jnp.zeros((1, META_ROWS, d), F32).at[0, META_ROWS - N_META:].set(meta_tokens)
    meta_pos = jnp.maximum(jnp.arange(META_ROWS, dtype=F32) - (META_ROWS - N_META), 0.0)
    zero_tail = jnp.zeros((SUBLANES, D_CONV), F32)
    _, _, mkn, mkpe, mv, u_tail = _mix_in(meta_blk, zero_tail, _rope_table(meta_pos), w, tm=META_ROWS)

    real_pos = jnp.arange(seq, dtype=F32) + N_META
    convn, q, kn, kpe, v, _ = _mix_in(x, u_tail, _rope_table(real_pos), w, tm=tm_in)
    attn = _attention(q, kn, kpe, v, mkn, mkpe, mv, g_attn_out[None], tq=tq)

    h1, xn, ri, rf, cnt = _mix_out(convn.reshape(t, D_CONV), attn.reshape(t, ATTN_W), x.reshape(t, d), w, tm=tm_out)

    sb, tm_e, rows, n_items = _moe_tiles(t)
    counts = cnt[0, :N_EXPERTS].astype(jnp.int32)
    dest, pad_end, item_e, item_row, item_nrows = _schedule(
        counts, ri[:, :TOP_K], ri[:, TOP_K:2 * TOP_K], tm=tm_e, sb=sb, n_items=n_items)

    xg = _dispatch(dest, pad_end, xn, rows=rows, tmd=tmd, sb=sb)
    dff = w_down.shape[1]
    yg = _experts(item_e, item_row, item_nrows, pad_end[N_EXPERTS - 1:], xg, w_gate_up, b_gate_up.reshape(N_EXPERTS, 1, 2 * dff),
                  w_down, b_down.reshape(N_EXPERTS, 1, d), tm=tm_e, sb=sb, fc=fc)
    out = _combine(dest, yg, h1, rf, g_final[None], tmc=tmc)
    return out.reshape(b, seq, d)


def kernel(x, meta_tokens, g_mix, w_in, conv_w, g_q, w_q_up, g_kv, w_kv_up, g_conv_out, g_attn_out, w_out, g_ffn,
           w_router, b_router, w_gate_up, b_gate_up, w_down, b_down, g_final):
    w = _prep_weights(g_mix[0], w_in[0], conv_w[0], g_q[0], w_q_up[0], g_kv[0], w_kv_up[0], g_conv_out[0],
                      w_out[0], g_ffn[0], w_router[0], b_router[0])
    seq = x.shape[1]
    return _layer(x, meta_tokens, w, w_gate_up[0], b_gate_up[0], w_down[0], b_down[0], g_attn_out[0], g_final,
                  tm_in=min(256, seq), tq=min(512, seq), tm_out=min(256, seq), tmd=min(512, seq),
                  tmc=min(128, seq), fc=256)
```

```python
import functools

import jax
import jax.numpy as jnp
from jax import lax
from jax.experimental import pallas as pl
from jax.experimental.pallas import tpu as pltpu

N_META = 16
EPS = 1e-6
D_CONV = 1024
CONV_GROUPS = 16
CONV_WIDTH = 3
N_HEADS = 8
QK_NOPE = 128
QK_ROPE = 64
V_DIM = 128
Q_LORA = 512
KV_LORA = 256
ROPE_THETA = 10000.0
N_EXPERTS = 32
TOP_K = 4
SWIGLU_LIMIT = 7.0
SWIGLU_ALPHA = 1.702

LANES = 128
SUBLANES = 8
META_ROWS = 128
HEAD_W = 2 * LANES
ATTN_W = N_HEADS * V_DIM
VMEM_LIMIT = 56 * 1024 * 1024

O_B, O_C, O_U = 0, D_CONV, 2 * D_CONV
O_Q = 3 * D_CONV
O_KV = O_Q + Q_LORA
O_KPE = O_KV + KV_LORA
O_KROT = O_KPE + LANES
W1_COLS = O_KROT + LANES

F32 = jnp.float32
BF16 = jnp.bfloat16


def _rms(x, g):
    return x * lax.rsqrt(jnp.mean(x * x, axis=-1, keepdims=True) + EPS) * g


def _dot(a, b):
    return jnp.dot(a, b, preferred_element_type=F32)


def _resident(shape):
    zeros = (0,) * len(shape)
    return pl.BlockSpec(shape, lambda *_: zeros, pipeline_mode=pl.Buffered(1))


def _mix_in_kernel(x_ref, gmix_ref, w1_ref, cw_ref, gq_ref, wq_ref, gkv_ref, wkv_ref, gco_ref, gmat_ref,
                   tab_ref, uinit_ref,
                   convn_ref, qt_ref, kn_ref, kpe_ref, vt_ref, utail_ref,
                   ubuf_ref, *, tm, scale):
    i = pl.program_id(1)
    hn = _rms(x_ref[0], gmix_ref[...]).astype(BF16)

    def proj(lo, hi):
        return _dot(hn, w1_ref[:, lo:hi])

    @pl.when(i == 0)
    def _():
        ubuf_ref[0:SUBLANES] = uinit_ref[...]

    @pl.when(i > 0)
    def _():
        ubuf_ref[0:SUBLANES] = ubuf_ref[tm:tm + SUBLANES]

    u = proj(O_C, O_U) * proj(O_U, O_Q)
    ubuf_ref[SUBLANES:SUBLANES + tm] = u
    cw = cw_ref[...]
    y = (cw[2:3] * u + cw[1:2] * ubuf_ref[SUBLANES - 1:SUBLANES - 1 + tm]
         + cw[0:1] * ubuf_ref[SUBLANES - 2:SUBLANES - 2 + tm])
    co = proj(O_B, O_C) * y
    ss = _dot((co * co).astype(BF16), gmat_ref[...])
    group = D_CONV // CONV_GROUPS
    convn_ref[0] = (co * lax.rsqrt(ss * (1.0 / group) + EPS) * gco_ref[...]).astype(BF16)
    utail_ref[...] = ubuf_ref[tm:tm + SUBLANES]

    cos = tab_ref[:, :LANES]
    sin = tab_ref[:, LANES:]

    qn = _rms(proj(O_Q, O_KV), gq_ref[...]).astype(BF16)
    qa = _dot(qn, wq_ref[:, :N_HEADS * HEAD_W])
    qb = _dot(qn, wq_ref[:, N_HEADS * HEAD_W:])
    for h in range(N_HEADS):
        c0 = h * HEAD_W
        qt_ref[0, c0:c0 + LANES, :] = (qa[:, c0:c0 + LANES] * scale).T.astype(BF16)
        pe = qa[:, c0 + LANES:c0 + HEAD_W] * cos + qb[:, h * LANES:(h + 1) * LANES] * sin
        qt_ref[0, c0 + LANES:c0 + HEAD_W, :] = (pe * scale).T.astype(BF16)

    kvn = _rms(proj(O_KV, O_KPE), gkv_ref[...]).astype(BF16)
    kv = _dot(kvn, wkv_ref[...])
    kn_ref[0] = kv[:, :ATTN_W].astype(BF16)
    vt_ref[0, 0] = kv[:, ATTN_W:].T.astype(BF16)
    kpe_ref[0] = (proj(O_KPE, O_KROT) * cos + proj(O_KROT, W1_COLS) * sin).astype(BF16)


def _mix_in(x3, uinit, tab, w, *, tm):
    b, l, d = x3.shape
    nt = l // tm
    kern = functools.partial(_mix_in_kernel, tm=tm, scale=float((QK_NOPE + QK_ROPE) ** -0.5))
    row = lambda width: pl.BlockSpec((1, tm, width), lambda bi, i: (bi, i, 0))
    return pl.pallas_call(
        kern,
        grid=(b, nt),
        in_specs=[
            row(d),
            _resident((1, d)),
            _resident((d, W1_COLS)),
            _resident((SUBLANES, D_CONV)),
            _resident((1, Q_LORA)),
            _resident((Q_LORA, N_HEADS * (HEAD_W + LANES))),
            _resident((1, KV_LORA)),
            _resident((KV_LORA, 2 * ATTN_W)),
            _resident((1, D_CONV)),
            _resident((D_CONV, D_CONV)),
            pl.BlockSpec((tm, 2 * LANES), lambda bi, i: (i, 0)),
            _resident((SUBLANES, D_CONV)),
        ],
        out_specs=[
            row(D_CONV),
            pl.BlockSpec((1, N_HEADS * HEAD_W, tm), lambda bi, i: (bi, 0, i)),
            row(ATTN_W), row(LANES),
            pl.BlockSpec((1, 1, ATTN_W, tm), lambda bi, i: (bi, i, 0, 0)),
            pl.BlockSpec((SUBLANES, D_CONV), lambda bi, i: (bi * nt + i, 0)),
        ],
        out_shape=[
            jax.ShapeDtypeStruct((b, l, D_CONV), BF16),
            jax.ShapeDtypeStruct((b, N_HEADS * HEAD_W, l), BF16),
            jax.ShapeDtypeStruct((b, l, ATTN_W), BF16),
            jax.ShapeDtypeStruct((b, l, LANES), BF16),
            jax.ShapeDtypeStruct((b, nt, ATTN_W, tm), BF16),
            jax.ShapeDtypeStruct((b * nt * SUBLANES, D_CONV), F32),
        ],
        scratch_shapes=[pltpu.VMEM((tm + SUBLANES, D_CONV), F32)],
        compiler_params=pltpu.CompilerParams(
            dimension_semantics=("arbitrary", "arbitrary"), vmem_limit_bytes=VMEM_LIMIT),
        name="mix_in",
    )(x3, w["g_mix"], w["w1"], w["conv_w"], w["g_q"], w["wq"], w["g_kv"], w["wkv"], w["g_conv_out"],
      w["gmat"], tab, uinit)


def _attn_kernel(qt_ref, kn_ref, kpe_ref, vt_ref, mkn_ref, mkpe_ref, mvt_ref, g_ref, o_ref,
                 m_ref, l_ref, acc_ref, *, tq):
    qi = pl.program_id(2)
    qt = qt_ref[0]
    tv = vt_ref.shape[3]

    km = jnp.concatenate([mkn_ref[0], mkpe_ref[0]], axis=-1)
    s = _dot(km, qt)
    row = lax.broadcasted_iota(jnp.int32, s.shape, 0)
    s = jnp.where(row >= META_ROWS - N_META, s, -jnp.inf)
    m0 = jnp.max(s, axis=0, keepdims=True)
    p = jnp.exp(s - m0)
    m_ref[...] = m0
    l_ref[...] = jnp.sum(p, axis=0, keepdims=True)
    acc_ref[...] = _dot(mvt_ref[0, 0], p.astype(BF16))

    def step(kb, diagonal):
        off = pl.multiple_of(kb * tq, tq)
        k = jnp.concatenate([kn_ref[0, pl.ds(off, tq), :], kpe_ref[0, pl.ds(off, tq), :]], axis=-1)
        s = _dot(k, qt)
        if diagonal:
            r = lax.broadcasted_iota(jnp.int32, s.shape, 0)
            c = lax.broadcasted_iota(jnp.int32, s.shape, 1)
            s = jnp.where(r <= c, s, -jnp.inf)
        m_prev = m_ref[...]
        m_new = jnp.maximum(m_prev, jnp.max(s, axis=0, keepdims=True))
        alpha = jnp.exp(m_prev - m_new)
        p = jnp.exp(s - m_new)
        l_ref[...] = alpha * l_ref[...] + jnp.sum(p, axis=0, keepdims=True)
        pb = p.astype(BF16)
        pv = _dot(vt_ref[0, kb * (tq // tv)], pb[:tv])
        for c in range(1, tq // tv):
            pv = pv + _dot(vt_ref[0, kb * (tq // tv) + c], pb[c * tv:(c + 1) * tv])
        acc_ref[...] = alpha * acc_ref[...] + pv
        m_ref[...] = m_new

    def body(kb, carry):
        step(kb, False)
        return carry

    lax.fori_loop(0, qi, body, 0)
    step(qi, True)

    o = (acc_ref[...] / l_ref[...]).T
    o_ref[0] = _rms(o, g_ref[...]).astype(BF16)


def _attention(qt, kn, kpe, vt, mkn, mkpe, mvt, g_attn, *, tq):
    b, l, _ = kn.shape
    _, nt, _, tv = vt.shape
    nq = l // tq
    kern = functools.partial(_attn_kernel, tq=tq)
    return pl.pallas_call(
        kern,
        grid=(b, N_HEADS, nq),
        in_specs=[
            pl.BlockSpec((1, HEAD_W, tq), lambda bi, h, i: (bi, h, i)),
            pl.BlockSpec((1, l, LANES), lambda bi, h, i: (bi, 0, h)),
            pl.BlockSpec((1, l, LANES), lambda bi, h, i: (bi, 0, 0)),
            pl.BlockSpec((1, nt, V_DIM, tv), lambda bi, h, i: (bi, 0, h, 0)),
            pl.BlockSpec((1, META_ROWS, LANES), lambda bi, h, i: (0, 0, h)),
            pl.BlockSpec((1, META_ROWS, LANES), lambda bi, h, i: (0, 0, 0)),
            pl.BlockSpec((1, 1, V_DIM, META_ROWS), lambda bi, h, i: (0, 0, h, 0)),
            pl.BlockSpec((1, V_DIM), lambda bi, h, i: (0, h)),
        ],
        out_specs=pl.BlockSpec((1, tq, V_DIM), lambda bi, h, i: (bi, i, h)),
        out_shape=jax.ShapeDtypeStruct((b, l, ATTN_W), BF16),
        scratch_shapes=[pltpu.VMEM((1, tq), F32), pltpu.VMEM((1, tq), F32), pltpu.VMEM((V_DIM, tq), F32)],
        compiler_params=pltpu.CompilerParams(
            dimension_semantics=("arbitrary", "arbitrary", "arbitrary"), vmem_limit_bytes=VMEM_LIMIT),
        name="attn",
    )(qt, kn, kpe, vt, mkn, mkpe, mvt, g_attn)


def _mix_out_kernel(convn_ref, attn_ref, x_ref, wo_ref, gffn_ref, wr_ref, br_ref,
                    h1_ref, xn_ref, ri_ref, rf_ref, cnt_ref, carry_ref, *, tm):
    @pl.when(pl.program_id(0) == 0)
    def _():
        carry_ref[...] = jnp.zeros_like(carry_ref)

    h1 = x_ref[...] + _dot(convn_ref[...], wo_ref[:D_CONV]) + _dot(attn_ref[...], wo_ref[D_CONV:])
    h1_ref[...] = h1
    xn = _rms(h1, gffn_ref[...])
    xn_ref[...] = xn

    xh = xn.astype(BF16)
    xl = (xn - xh.astype(F32)).astype(BF16)
    ph = _dot(xh, wr_ref[...])
    logits = ph[:, :LANES] + ph[:, LANES:] + _dot(xl, wr_ref[:, :LANES]) + br_ref[...]

    lane = lax.broadcasted_iota(jnp.int32, (tm, LANES), 1).astype(F32)
    work = logits
    top_v, top_i, onehots = [], [], []
    for _ in range(TOP_K):
        mk = jnp.max(work, axis=-1, keepdims=True)
        ik = jnp.min(jnp.where(work == mk, lane, float(LANES)), axis=-1, keepdims=True)
        oh = lane == ik
        work = jnp.where(oh, -jnp.inf, work)
        top_v.append(mk)
        top_i.append(ik)
        onehots.append(oh)

    ex = [jnp.exp(v - top_v[0]) for v in top_v]
    denom = ex[0] + ex[1] + ex[2] + ex[3]
    gates = [e / denom for e in ex]

    sel = jnp.zeros((tm, LANES), F32)
    for oh in onehots:
        sel = sel + oh.astype(F32)
    r = lax.broadcasted_iota(jnp.int32, (tm, tm), 0)
    c = lax.broadcasted_iota(jnp.int32, (tm, tm), 1)
    tri = jnp.where(c < r, 1.0, 0.0).astype(BF16)
    cum = _dot(tri, sel.astype(BF16)) + carry_ref[...]
    carry_ref[...] = carry_ref[...] + jnp.sum(sel, axis=0, keepdims=True)
    cnt_ref[...] = carry_ref[...]

    ri = jnp.zeros((tm, LANES), F32)
    rf = jnp.zeros((tm, LANES), F32)
    for k in range(TOP_K):
        rank_k = jnp.sum(jnp.where(onehots[k], cum, 0.0), axis=-1, keepdims=True)
        ri = jnp.where(lane == float(k), top_i[k], ri)
        ri = jnp.where(lane == float(TOP_K + k), rank_k, ri)
        rf = jnp.where(lane == float(k), gates[k], rf)
    ri_ref[...] = ri.astype(jnp.int32)
    rf_ref[...] = rf


def _mix_out(convn, attn, x2, w, *, tm):
    t, d = x2.shape
    kern = functools.partial(_mix_out_kernel, tm=tm)
    row = lambda width: pl.BlockSpec((tm, width), lambda i: (i, 0))
    return pl.pallas_call(
        kern,
        grid=(t // tm,),
        in_specs=[
            row(D_CONV), row(ATTN_W), row(d),
            _resident((D_CONV + ATTN_W, d)),
            _resident((1, d)),
            _resident((d, 2 * LANES)),
            _resident((1, LANES)),
        ],
        out_specs=[row(d), row(d), row(LANES), row(LANES), pl.BlockSpec((1, LANES), lambda i: (0, 0))],
        out_shape=[
            jax.ShapeDtypeStruct((t, d), F32),
            jax.ShapeDtypeStruct((t, d), F32),
            jax.ShapeDtypeStruct((t, LANES), jnp.int32),
            jax.ShapeDtypeStruct((t, LANES), F32),
            jax.ShapeDtypeStruct((1, LANES), F32),
        ],
        scratch_shapes=[pltpu.VMEM((1, LANES), F32)],
        compiler_params=pltpu.CompilerParams(
            dimension_semantics=("arbitrary",), vmem_limit_bytes=VMEM_LIMIT),
        name="mix_out",
    )(convn, attn, x2, w["wo"], w["g_ffn"], w["wr"], w["br"])


def _dispatch_kernel(dest_ref, pend_ref, xn_ref, xg_hbm, zbuf_ref, sem, zsem, *, tmd, sb):
    i = pl.program_id(0)

    @pl.when(i == 0)
    def _():
        zbuf_ref[...] = jnp.zeros_like(zbuf_ref)

        def zero_copy(e):
            end = pend_ref[e]
            start = pl.multiple_of(jnp.maximum(end - sb, 0), sb)
            return pltpu.make_async_copy(zbuf_ref, xg_hbm.at[pl.ds(start, sb)], zsem)

        def nonempty(e):
            prev = pend_ref[jnp.maximum(e - 1, 0)]
            return pend_ref[e] > jnp.where(e > 0, prev, 0)

        def start(e, carry):
            @pl.when(nonempty(e))
            def _():
                zero_copy(e).start()
            return carry

        def wait(e, carry):
            @pl.when(nonempty(e))
            def _():
                zero_copy(e).wait()
            return carry

        lax.fori_loop(0, N_EXPERTS, start, 0)
        lax.fori_loop(0, N_EXPERTS, wait, 0)
        _zero_tail(zbuf_ref, xg_hbm, pend_ref[N_EXPERTS - 1], zsem)

    base = i * tmd

    def issue(r, carry):
        for k in range(TOP_K):
            dst = xg_hbm.at[pl.ds(dest_ref[(base + r) * TOP_K + k], 1)]
            pltpu.make_async_copy(xn_ref.at[pl.ds(r, 1)], dst, sem).start()
        return carry

    lax.fori_loop(0, tmd, issue, 0)
    for k in range(TOP_K):
        pltpu.make_async_copy(xn_ref, xg_hbm.at[pl.ds(0, tmd)], sem).wait()


def _dispatch(dest, pad_end, xn, *, rows, tmd, sb):
    t, d = xn.shape
    kern = functools.partial(_dispatch_kernel, tmd=tmd, sb=sb)
    return pl.pallas_call(
        kern,
        grid_spec=pltpu.PrefetchScalarGridSpec(
            num_scalar_prefetch=2,
            grid=(t // tmd,),
            in_specs=[pl.BlockSpec((tmd, d), lambda i, dest, pend: (i, 0))],
            out_specs=pl.BlockSpec(memory_space=pl.ANY),
            scratch_shapes=[pltpu.VMEM((sb, d), F32), pltpu.SemaphoreType.DMA, pltpu.SemaphoreType.DMA],
        ),
        out_shape=jax.ShapeDtypeStruct((rows, d), F32),
        compiler_params=pltpu.CompilerParams(
            dimension_semantics=("arbitrary",), vmem_limit_bytes=VMEM_LIMIT),
        name="dispatch",
    )(dest, pad_end, xn)


def _expert_kernel(ie_ref, irow_ref, inr_ref, tail_ref, xg_hbm, wg_ref, wl_ref, wd_ref, bg_ref, bl_ref, bd_ref, yg_hbm,
                   xst_ref, xb_ref, act_ref, acc_ref, wgb_ref, wlb_ref, wdb_ref, sem_in, sem_out,
                   *, tm, sb, chunks):
    it = pl.program_id(0)
    s = pl.program_id(1)
    nrows = inr_ref[it]
    row0 = irow_ref[it]
    n_up = act_ref.shape[0]
    n_down, _, fc = acc_ref.shape
    last = n_up + n_down - 1

    def piece(r):
        return pl.ds(r * sb, sb)

    def hbm_rows(r):
        return pl.ds(pl.multiple_of(row0 + r * sb, sb), sb)

    def in_copy(r):
        return pltpu.make_async_copy(xg_hbm.at[hbm_rows(r)], xst_ref.at[piece(r)], sem_in.at[r])

    def out_copies(r):
        return [pltpu.make_async_copy(acc_ref.at[n, piece(r)], yg_hbm.at[hbm_rows(r), n * fc:(n + 1) * fc],
                                      sem_out.at[r]) for n in range(n_down)]

    def for_valid_pieces(fn):
        for r in range(tm // sb):
            pl.when(r * sb < nrows)(functools.partial(fn, r))

    def for_valid_chunks(fn):
        for n, (c0, cn) in enumerate(chunks):
            pl.when(c0 < nrows)(functools.partial(fn, pl.ds(c0, cn), n == 0))

    @pl.when((it == 0) & (s == 0))
    def _():
        xb_ref[...] = jnp.zeros_like(xb_ref)

    @pl.when(s == 0)
    def _():
        for_valid_pieces(lambda r: in_copy(r).start())

        def land(r):
            in_copy(r).wait()
            xb_ref[piece(r)] = xst_ref[piece(r)].astype(BF16)

        for_valid_pieces(land)

    def up(rows, cast_weights):
        if cast_weights:
            wgb_ref[...] = wg_ref[0].astype(BF16)
            wlb_ref[...] = wl_ref[0].astype(BF16)
        xs = xb_ref[rows]
        g = jnp.minimum(_dot(xs, wgb_ref[...]) + bg_ref[0], SWIGLU_LIMIT)
        lin = jnp.clip(_dot(xs, wlb_ref[...]) + bl_ref[0], -SWIGLU_LIMIT, SWIGLU_LIMIT)
        act_ref[s, rows] = ((lin + 1.0) * (g * jax.nn.sigmoid(SWIGLU_ALPHA * g))).astype(BF16)

    def down(rows, cast_weights):
        if cast_weights:
            wdb_ref[...] = wd_ref[0].astype(BF16)
        act = jnp.concatenate([act_ref[jj, rows] for jj in range(n_up)], axis=-1)
        acc_ref[s - n_up, rows] = _dot(act, wdb_ref[...]) + bd_ref[0]

    @pl.when(s < n_up)
    def _():
        for_valid_chunks(up)

    @pl.when(s >= n_up)
    def _():
        for_valid_chunks(down)

    @pl.when(s == last)
    def _():
        def start_out(r):
            for cp in out_copies(r):
                cp.start()

        def wait_out(r):
            for cp in out_copies(r):
                cp.wait()

        for_valid_pieces(start_out)
        for_valid_pieces(wait_out)

    @pl.when((it == pl.num_programs(0) - 1) & (s == last))
    def _():
        acc_ref[0, piece(0)] = jnp.zeros((sb, fc), F32)
        _zero_tail(acc_ref.at[0, piece(0)], yg_hbm, tail_ref[0], sem_out.at[0])


def _zero_tail(zeros_vmem, dst_hbm, first_row, sem):
    sb, width = zeros_vmem.shape
    n_blocks = (dst_hbm.shape[0] - first_row) // sb

    def fill(b, carry):
        rows = pl.ds(pl.multiple_of(first_row + b * sb, sb), sb)
        copies = [pltpu.make_async_copy(zeros_vmem, dst_hbm.at[rows, c * width:(c + 1) * width], sem)
                  for c in range(dst_hbm.shape[1] // width)]
        for cp in copies:
            cp.start()
        for cp in copies:
            cp.wait()
        return carry

    lax.fori_loop(0, n_blocks, fill, 0)


def _experts(item_e, item_row, item_nrows, tail, xg, w_gate_up, b_gate_up, w_down, b_down, *, tm, sb, fc):
    rows, d = xg.shape
    n_items = item_e.shape[0]
    dff = w_down.shape[1]
    n_up = dff // fc
    n_down = d // fc
    nsub = tm // sb
    kern = functools.partial(_expert_kernel, tm=tm, sb=sb, chunks=_row_chunks(tm, sb))

    def up_chunk(it, s, inr):
        return jnp.where(inr[it] > 0, jnp.minimum(s, n_up - 1), n_up - 1)

    def down_chunk(it, s, inr):
        return jnp.where(inr[it] > 0, jnp.maximum(s - n_up, 0), n_down - 1)

    return pl.pallas_call(
        kern,
        grid_spec=pltpu.PrefetchScalarGridSpec(
            num_scalar_prefetch=4,
            grid=(n_items, n_up + n_down),
            in_specs=[
                pl.BlockSpec(memory_space=pl.ANY),
                pl.BlockSpec((1, d, fc), lambda it, s, ie, ir, inr, tl: (ie[it], 0, up_chunk(it, s, inr))),
                pl.BlockSpec((1, d, fc), lambda it, s, ie, ir, inr, tl: (ie[it], 0, n_up + up_chunk(it, s, inr))),
                pl.BlockSpec((1, dff, fc), lambda it, s, ie, ir, inr, tl: (ie[it], 0, down_chunk(it, s, inr))),
                pl.BlockSpec((1, 1, fc), lambda it, s, ie, ir, inr, tl: (ie[it], 0, up_chunk(it, s, inr))),
                pl.BlockSpec((1, 1, fc), lambda it, s, ie, ir, inr, tl: (ie[it], 0, n_up + up_chunk(it, s, inr))),
                pl.BlockSpec((1, 1, fc), lambda it, s, ie, ir, inr, tl: (ie[it], 0, down_chunk(it, s, inr))),
            ],
            out_specs=pl.BlockSpec(memory_space=pl.ANY),
            scratch_shapes=[
                pltpu.VMEM((tm, d), F32),
                pltpu.VMEM((tm, d), BF16),
                pltpu.VMEM((n_up, tm, fc), BF16),
                pltpu.VMEM((n_down, tm, fc), F32),
                pltpu.VMEM((d, fc), BF16),
                pltpu.VMEM((d, fc), BF16),
                pltpu.VMEM((dff, fc), BF16),
                pltpu.SemaphoreType.DMA((nsub,)),
                pltpu.SemaphoreType.DMA((nsub,)),
            ],
        ),
        out_shape=jax.ShapeDtypeStruct((rows, d), F32),
        compiler_params=pltpu.CompilerParams(
            dimension_semantics=("arbitrary", "arbitrary"), vmem_limit_bytes=VMEM_LIMIT),
        name="experts",
    )(item_e, item_row, item_nrows, tail, xg, w_gate_up, w_gate_up, w_down,
      b_gate_up, b_gate_up, b_down)


def _combine_kernel(dest_ref, yg_hbm, h1_ref, rf_ref, gfin_ref, o_ref, gbuf_ref, sem, *, tmc):
    base = pl.program_id(0) * tmc

    def issue(r, carry):
        for k in range(TOP_K):
            src = yg_hbm.at[pl.ds(dest_ref[(base + r) * TOP_K + k], 1)]
            pltpu.make_async_copy(src, gbuf_ref.at[k, pl.ds(r, 1)], sem).start()
        return carry

    lax.fori_loop(0, tmc, issue, 0)
    for k in range(TOP_K):
        pltpu.make_async_copy(yg_hbm.at[pl.ds(0, tmc)], gbuf_ref.at[k], sem).wait()

    gates = rf_ref[...]
    y = h1_ref[...]
    for k in range(TOP_K):
        y = y + gates[:, k:k + 1] * gbuf_ref[k]
    o_ref[...] = _rms(y, gfin_ref[...])


def _combine(dest, yg, h1, rf, g_final, *, tmc):
    t, d = h1.shape
    kern = functools.partial(_combine_kernel, tmc=tmc)
    return pl.pallas_call(
        kern,
        grid_spec=pltpu.PrefetchScalarGridSpec(
            num_scalar_prefetch=1,
            grid=(t // tmc,),
            in_specs=[
                pl.BlockSpec(memory_space=pl.ANY),
                pl.BlockSpec((tmc, d), lambda i, dest: (i, 0)),
                pl.BlockSpec((tmc, LANES), lambda i, dest: (i, 0)),
                pl.BlockSpec((1, d), lambda i, dest: (0, 0)),
            ],
            out_specs=pl.BlockSpec((tmc, d), lambda i, dest: (i, 0)),
            scratch_shapes=[pltpu.VMEM((TOP_K, tmc, d), F32), pltpu.SemaphoreType.DMA],
        ),
        out_shape=jax.ShapeDtypeStruct((t, d), F32),
        compiler_params=pltpu.CompilerParams(
            dimension_semantics=("arbitrary",), vmem_limit_bytes=VMEM_LIMIT),
        name="combine",
    )(dest, yg, h1, rf, g_final)


def _rotate_half_cols(w):
    half = w.shape[-1] // 2
    return jnp.concatenate([-w[..., half:], w[..., :half]], axis=-1)


def _prep_weights(g_mix, w_in, conv_w, g_q, w_q_up, g_kv, w_kv_up, g_conv_out, w_out, g_ffn, w_router, b_router):
    d = w_in.shape[0]
    k_rope = w_in[:, O_KPE:O_KPE + QK_ROPE]
    pad = jnp.zeros((d, LANES - QK_ROPE), F32)
    w1 = jnp.concatenate([w_in[:, :O_KPE], k_rope, pad, _rotate_half_cols(k_rope), pad], axis=1).astype(BF16)

    wq = w_q_up.reshape(Q_LORA, N_HEADS, QK_NOPE + QK_ROPE)
    nope, pe = wq[:, :, :QK_NOPE], wq[:, :, QK_NOPE:]
    hpad = jnp.zeros((Q_LORA, N_HEADS, LANES - QK_ROPE), F32)
    wqa = jnp.concatenate([nope, pe, hpad], axis=2).reshape(Q_LORA, N_HEADS * HEAD_W)
    wqb = jnp.concatenate([_rotate_half_cols(pe), hpad], axis=2).reshape(Q_LORA, N_HEADS * LANES)
    wq_all = jnp.concatenate([wqa, wqb], axis=1).astype(BF16)

    wkv = w_kv_up.reshape(KV_LORA, N_HEADS, QK_NOPE + V_DIM)
    wkv2 = jnp.concatenate([wkv[:, :, :QK_NOPE].reshape(KV_LORA, ATTN_W),
                            wkv[:, :, QK_NOPE:].reshape(KV_LORA, ATTN_W)], axis=1).astype(BF16)

    wr_pad = jnp.zeros((d, LANES), F32).at[:, :N_EXPERTS].set(w_router)
    wr_hi = wr_pad.astype(BF16)
    wr_lo = (wr_pad - wr_hi.astype(F32)).astype(BF16)
    br = jnp.full((1, LANES), -1e30, F32).at[0, :N_EXPERTS].set(b_router)

    grp = jnp.arange(D_CONV) // (D_CONV // CONV_GROUPS)
    cw = jnp.zeros((SUBLANES, D_CONV), F32).at[:CONV_WIDTH].set(conv_w)
    return {
        "g_mix": g_mix[None], "w1": w1, "conv_w": cw, "g_q": g_q[None], "wq": wq_all, "g_kv": g_kv[None],
        "wkv": wkv2, "g_conv_out": g_conv_out[None], "gmat": (grp[:, None] == grp[None, :]).astype(BF16),
        "wo": w_out.astype(BF16), "g_ffn": g_ffn[None], "wr": jnp.concatenate([wr_hi, wr_lo], axis=1), "br": br,
    }


def _rope_table(pos):
    half = QK_ROPE // 2
    inv_freq = ROPE_THETA ** (-jnp.arange(half, dtype=F32) / half)
    ang = pos[:, None] * inv_freq[None, :]
    c, s = jnp.cos(ang), jnp.sin(ang)
    z = jnp.zeros((pos.shape[0], LANES - QK_ROPE), F32)
    return jnp.concatenate([c, c, z, s, s, z], axis=1)


def _schedule(counts, eidx, rank, *, tm, sb, n_items):
    padded = (counts + sb - 1) // sb * sb
    pad_end = jnp.cumsum(padded)
    pad_start = pad_end - padded
    dest = (pad_start[eidx] + rank).reshape(-1).astype(jnp.int32)

    per_e = (counts + tm - 1) // tm
    it_end = jnp.cumsum(per_e)
    it_start = it_end - per_e
    ii = jnp.arange(n_items)
    e_of = jnp.minimum(jnp.searchsorted(it_end, ii, side="right"), N_EXPERTS - 1)
    valid = ii < it_end[-1]
    li = ii - it_start[e_of]
    item_row = jnp.where(valid, pad_start[e_of] + li * tm, 0)
    item_nrows = jnp.where(valid, jnp.clip(counts[e_of] - li * tm, 0, tm), 0)
    e_last = e_of[jnp.maximum(it_end[-1] - 1, 0)]
    item_e = jnp.where(valid, e_of, e_last)
    i32 = lambda a: a.astype(jnp.int32)
    return dest, i32(pad_end), i32(item_e), i32(item_row), i32(item_nrows)


def _row_chunks(tm, sb, big=512):
    n_big = (tm - 2 * sb) // big
    chunks = [(c * big, big) for c in range(n_big)]
    chunks += [(r, sb) for r in range(n_big * big, tm, sb)]
    return tuple(chunks)


def _moe_tiles(t):
    sb = 128
    tm = 10 * sb
    a = t * TOP_K
    rows = (a + N_EXPERTS * (sb - 1) + sb - 1) // sb * sb
    n_items = N_EXPERTS + a // tm
    return sb, tm, rows, n_items


def _layer(x, meta_tokens, w, w_gate_up, b_gate_up, w_down, b_down, g_attn_out, g_final, *,
           tm_in, tq, tm_out, tmd, tmc, fc):
    b, seq, d = x.shape
    t = b * seq

    meta_blk = jnp.zeros((1, META_ROWS, d), F32).at[0, META_ROWS - N_META:].set(meta_tokens)
    meta_pos = jnp.maximum(jnp.arange(META_ROWS, dtype=F32) - (META_ROWS - N_META), 0.0)
    zero_tail = jnp.zeros((SUBLANES, D_CONV), F32)
    _, _, mkn, mkpe, mv, u_tail = _mix_in(meta_blk, zero_tail, _rope_table(meta_pos), w, tm=META_ROWS)

    real_pos = jnp.arange(seq, dtype=F32) + N_META
    convn, q, kn, kpe, v, _ = _mix_in(x, u_tail, _rope_table(real_pos), w, tm=tm_in)
    attn = _attention(q, kn, kpe, v, mkn, mkpe, mv, g_attn_out[None], tq=tq)

    h1, xn, ri, rf, cnt = _mix_out(convn.reshape(t, D_CONV), attn.reshape(t, ATTN_W), x.reshape(t, d), w, tm=tm_out)

    sb, tm_e, rows, n_items = _moe_tiles(t)
    counts = cnt[0, :N_EXPERTS].astype(jnp.int32)
    dest, pad_end, item_e, item_row, item_nrows = _schedule(
        counts, ri[:, :TOP_K], ri[:, TOP_K:2 * TOP_K], tm=tm_e, sb=sb, n_items=n_items)

    xg = _dispatch(dest, pad_end, xn, rows=rows, tmd=tmd, sb=sb)
    dff = w_down.shape[1]
    yg = _experts(item_e, item_row, item_nrows, pad_end[N_EXPERTS - 1:], xg, w_gate_up, b_gate_up.reshape(N_EXPERTS, 1, 2 * dff),
                  w_down, b_down.reshape(N_EXPERTS, 1, d), tm=tm_e, sb=sb, fc=fc)
    out = _combine(dest, yg, h1, rf, g_final[None], tmc=tmc)
    return out.reshape(b, seq, d)


def kernel(x, meta_tokens, g_mix, w_in, conv_w, g_q, w_q_up, g_kv, w_kv_up, g_conv_out, g_attn_out, w_out, g_ffn,
           w_router, b_router, w_gate_up, b_gate_up, w_down, b_down, g_final):
    w = _prep_weights(g_mix[0], w_in[0], conv_w[0], g_q[0], w_q_up[0], g_kv[0], w_kv_up[0], g_conv_out[0],
                      w_out[0], g_ffn[0], w_router[0], b_router[0])
    seq = x.shape[1]
    return _layer(x, meta_tokens, w, w_gate_up[0], b_gate_up[0], w_down[0], b_down[0], g_attn_out[0], g_final,
                  tm_in=min(256, seq), tq=min(512, seq), tm_out=min(256, seq), tmd=min(512, seq),
                  tmc=min(128, seq), fc=256)
```

```python
import functools

import jax
import jax.numpy as jnp
import numpy as np
from jax import lax
from jax.experimental import pallas as pl
from jax.experimental.pallas import tpu as pltpu

N_META = 16
EPS = 1e-6
D_CONV = 1024
CONV_GROUPS = 16
CONV_WIDTH = 3
N_HEADS = 8
QK_NOPE = 128
QK_ROPE = 64
V_DIM = 128
Q_LORA = 512
KV_LORA = 256
ROPE_THETA = 10000.0
N_EXPERTS = 32
TOP_K = 4
SWIGLU_LIMIT = 7.0
SWIGLU_ALPHA = 1.702

LANES = 128
SUBLANES = 8
META_ROWS = 128
HEAD_W = 2 * LANES
ATTN_W = N_HEADS * V_DIM
V_EXT = V_DIM + 16
VMEM_LIMIT = 56 * 1024 * 1024

O_B, O_C, O_U = 0, D_CONV, 2 * D_CONV
O_Q = 3 * D_CONV
O_KV = O_Q + Q_LORA
O_KPE = O_KV + KV_LORA
D_IN = O_KPE + QK_ROPE

F32 = jnp.float32
BF16 = jnp.bfloat16


def _rms(x, g):
    return x * lax.rsqrt(jnp.mean(x * x, axis=-1, keepdims=True) + EPS) * g


def _dot(a, b):
    return jnp.dot(a, b, preferred_element_type=F32)


def _resident(shape):
    zeros = (0,) * len(shape)
    return pl.BlockSpec(shape, lambda *_: zeros, pipeline_mode=pl.Buffered(1))


def _mix_in_kernel(x_ref, gmix_ref, w1_ref, wk2_ref, cw_ref, gq_ref, wq_ref, gkv_ref, wkv_ref, gco_ref, gmat_ref,
                   tab_ref, uinit_ref,
                   convn_ref, qt_ref, kn_ref, kpe_ref, vt_ref, utail_ref,
                   ubuf_ref, *, tm, scale):
    i = pl.program_id(1)
    hn = _rms(x_ref[0], gmix_ref[...]).astype(BF16)

    def proj(lo, hi):
        return _dot(hn, w1_ref[:, lo:hi])

    @pl.when(i == 0)
    def _():
        ubuf_ref[0:SUBLANES] = uinit_ref[...]

    @pl.when(i > 0)
    def _():
        ubuf_ref[0:SUBLANES] = ubuf_ref[tm:tm + SUBLANES]

    u = proj(O_C, O_U) * proj(O_U, O_Q)
    ubuf_ref[SUBLANES:SUBLANES + tm] = u
    cw = cw_ref[...]
    y = (cw[2:3] * u + cw[1:2] * ubuf_ref[SUBLANES - 1:SUBLANES - 1 + tm]
         + cw[0:1] * ubuf_ref[SUBLANES - 2:SUBLANES - 2 + tm])
    co = proj(O_B, O_C) * y
    ss = _dot((co * co).astype(BF16), gmat_ref[...])
    group = D_CONV // CONV_GROUPS
    convn_ref[0] = (co * lax.rsqrt(ss * (1.0 / group) + EPS) * gco_ref[...]).astype(BF16)
    utail_ref[...] = ubuf_ref[tm:tm + SUBLANES]

    cos = tab_ref[:, :LANES]
    sin = tab_ref[:, LANES:]

    qn = _rms(proj(O_Q, O_KV), gq_ref[...]).astype(BF16)
    qa = _dot(qn, wq_ref[:, :N_HEADS * HEAD_W])
    qb = _dot(qn, wq_ref[:, N_HEADS * HEAD_W:])
    for h in range(N_HEADS):
        c0 = h * HEAD_W
        qt_ref[0, c0:c0 + LANES, :] = (qa[:, c0:c0 + LANES] * scale).T.astype(BF16)
        pe = qa[:, c0 + LANES:c0 + HEAD_W] * cos + qb[:, h * LANES:(h + 1) * LANES] * sin
        qt_ref[0, c0 + LANES:c0 + HEAD_W, :] = (pe * scale).T.astype(BF16)

    kvn = _rms(proj(O_KV, O_KPE), gkv_ref[...]).astype(BF16)
    kv = _dot(kvn, wkv_ref[...])
    kn_ref[0] = kv[:, :ATTN_W].astype(BF16)
    vt = kv[:, ATTN_W:].T.astype(BF16)
    ones = jnp.ones((V_EXT - V_DIM, tm), BF16)
    for h in range(N_HEADS):
        vt_ref[0, 0, h * V_EXT:h * V_EXT + V_DIM, :] = vt[h * V_DIM:(h + 1) * V_DIM]
        vt_ref[0, 0, h * V_EXT + V_DIM:(h + 1) * V_EXT, :] = ones
    kk = _dot(hn, wk2_ref[...])
    kpe_ref[0] = (kk[:, :LANES] * cos + kk[:, LANES:] * sin).astype(BF16)


def _mix_in(x3, uinit, tab, w, *, tm):
    b, l, d = x3.shape
    nt = l // tm
    kern = functools.partial(_mix_in_kernel, tm=tm, scale=float((QK_NOPE + QK_ROPE) ** -0.5))
    row = lambda width: pl.BlockSpec((1, tm, width), lambda bi, i: (bi, i, 0))
    return pl.pallas_call(
        kern,
        grid=(b, nt),
        in_specs=[
            row(d),
            _resident((1, d)),
            _resident((d, D_IN)),
            _resident((d, 2 * LANES)),
            _resident((SUBLANES, D_CONV)),
            _resident((1, Q_LORA)),
            _resident((Q_LORA, N_HEADS * (HEAD_W + LANES))),
            _resident((1, KV_LORA)),
            _resident((KV_LORA, 2 * ATTN_W)),
            _resident((1, D_CONV)),
            _resident((D_CONV, D_CONV)),
            pl.BlockSpec((tm, 2 * LANES), lambda bi, i: (i, 0)),
            _resident((SUBLANES, D_CONV)),
        ],
        out_specs=[
            row(D_CONV),
            pl.BlockSpec((1, N_HEADS * HEAD_W, tm), lambda bi, i: (bi, 0, i)),
            row(ATTN_W), row(LANES),
            pl.BlockSpec((1, 1, N_HEADS * V_EXT, tm), lambda bi, i: (bi, i, 0, 0)),
            pl.BlockSpec((SUBLANES, D_CONV), lambda bi, i: (bi * nt + i, 0)),
        ],
        out_shape=[
            jax.ShapeDtypeStruct((b, l, D_CONV), BF16),
            jax.ShapeDtypeStruct((b, N_HEADS * HEAD_W, l), BF16),
            jax.ShapeDtypeStruct((b, l, ATTN_W), BF16),
            jax.ShapeDtypeStruct((b, l, LANES), BF16),
            jax.ShapeDtypeStruct((b, nt, N_HEADS * V_EXT, tm), BF16),
            jax.ShapeDtypeStruct((b * nt * SUBLANES, D_CONV), F32),
        ],
        scratch_shapes=[pltpu.VMEM((tm + SUBLANES, D_CONV), F32)],
        compiler_params=pltpu.CompilerParams(
            dimension_semantics=("arbitrary", "arbitrary"), vmem_limit_bytes=VMEM_LIMIT),
        name="mix_in",
    )(x3, w["g_mix"], w["w1"], w["wk2"], w["conv_w"], w["g_q"], w["wq"], w["g_kv"], w["wkv"], w["g_conv_out"],
      w["gmat"], tab, uinit)


def _attn_kernel(qt_ref, kn_ref, kpe_ref, vt_ref, mkn_ref, mkpe_ref, mvt_ref, g_ref, o_ref,
                 m_ref, acc_ref, *, tq, hps):
    qi = pl.program_id(2)
    tv = vt_ref.shape[3]

    def lanes(hh):
        return slice(hh * LANES, (hh + 1) * LANES)

    def vrows(hh):
        return slice(hh * V_EXT, (hh + 1) * V_EXT)

    def qt(hh):
        return qt_ref[0, hh * HEAD_W:(hh + 1) * HEAD_W, :]

    for hh in range(hps):
        km = jnp.concatenate([mkn_ref[0, :, lanes(hh)], mkpe_ref[0]], axis=-1)
        s = _dot(km, qt(hh))
        row = lax.broadcasted_iota(jnp.int32, s.shape, 0)
        s = jnp.where(row >= META_ROWS - N_META, s, -jnp.inf)
        m0 = jnp.max(s, axis=0, keepdims=True)
        m_ref[hh] = m0
        acc_ref[hh] = _dot(mvt_ref[0, 0, vrows(hh), :], jnp.exp((s - m0).astype(BF16)))

    def step(kb, diagonal):
        off = pl.multiple_of(kb * tq, tq)
        kpe = kpe_ref[0, pl.ds(off, tq), :]
        for hh in range(hps):
            k = jnp.concatenate([kn_ref[0, pl.ds(off, tq), lanes(hh)], kpe], axis=-1)
            s = _dot(k, qt(hh))
            if diagonal:
                r = lax.broadcasted_iota(jnp.int32, s.shape, 0)
                c = lax.broadcasted_iota(jnp.int32, s.shape, 1)
                s = jnp.where(r <= c, s, -jnp.inf)
            m_prev = m_ref[hh]
            m_new = jnp.maximum(m_prev, jnp.max(s, axis=0, keepdims=True))
            alpha = jnp.exp(m_prev - m_new)
            pb = jnp.exp((s - m_new).astype(BF16))
            pv = _dot(vt_ref[0, kb * (tq // tv), vrows(hh), :], pb[:tv])
            for c in range(1, tq // tv):
                pv = pv + _dot(vt_ref[0, kb * (tq // tv) + c, vrows(hh), :], pb[c * tv:(c + 1) * tv])
            acc_ref[hh] = alpha * acc_ref[hh] + pv
            m_ref[hh] = m_new

    def body(kb, carry):
        step(kb, False)
        return carry

    lax.fori_loop(0, qi, body, 0)
    step(qi, True)

    for hh in range(hps):
        acc = acc_ref[hh]
        o = (acc[:V_DIM] / acc[V_DIM:V_DIM + 1]).T
        o_ref[0, :, lanes(hh)] = _rms(o, g_ref[:, lanes(hh)]).astype(BF16)


def _attention(qt, kn, kpe, vt, mkn, mkpe, mvt, g_attn, *, tq, hps=2):
    b, l, _ = kn.shape
    _, nt, _, tv = vt.shape
    nq = l // tq
    kern = functools.partial(_attn_kernel, tq=tq, hps=hps)
    return pl.pallas_call(
        kern,
        grid=(b, N_HEADS // hps, nq),
        in_specs=[
            pl.BlockSpec((1, hps * HEAD_W, tq), lambda bi, h, i: (bi, h, i)),
            pl.BlockSpec((1, l, hps * LANES), lambda bi, h, i: (bi, 0, h)),
            pl.BlockSpec((1, l, LANES), lambda bi, h, i: (bi, 0, 0)),
            pl.BlockSpec((1, nt, hps * V_EXT, tv), lambda bi, h, i: (bi, 0, h, 0)),
            pl.BlockSpec((1, META_ROWS, hps * LANES), lambda bi, h, i: (0, 0, h)),
            pl.BlockSpec((1, META_ROWS, LANES), lambda bi, h, i: (0, 0, 0)),
            pl.BlockSpec((1, 1, hps * V_EXT, META_ROWS), lambda bi, h, i: (0, 0, h, 0)),
            pl.BlockSpec((1, hps * V_DIM), lambda bi, h, i: (0, h)),
        ],
        out_specs=pl.BlockSpec((1, tq, hps * V_DIM), lambda bi, h, i: (bi, i, h)),
        out_shape=jax.ShapeDtypeStruct((b, l, ATTN_W), BF16),
        scratch_shapes=[pltpu.VMEM((hps, 1, tq), F32), pltpu.VMEM((hps, V_EXT, tq), F32)],
        compiler_params=pltpu.CompilerParams(
            dimension_semantics=("arbitrary", "arbitrary", "arbitrary"), vmem_limit_bytes=VMEM_LIMIT),
        name="attn",
    )(qt, kn, kpe, vt, mkn, mkpe, mvt, g_attn)


def _mix_out_kernel(convn_ref, attn_ref, x_ref, wo_ref, gffn_ref, wr_ref, br_ref,
                    h1_ref, xn_ref, ri_ref, rf_ref, cnt_ref, carry_ref, *, tm):
    @pl.when(pl.program_id(0) == 0)
    def _():
        carry_ref[...] = jnp.zeros_like(carry_ref)

    h1 = x_ref[...] + _dot(convn_ref[...], wo_ref[:D_CONV]) + _dot(attn_ref[...], wo_ref[D_CONV:])
    h1_ref[...] = h1
    xn = _rms(h1, gffn_ref[...])
    xn_ref[...] = xn

    xh = xn.astype(BF16)
    xl = (xn - xh.astype(F32)).astype(BF16)
    ph = _dot(xh, wr_ref[...])
    logits = ph[:, :LANES] + ph[:, LANES:] + _dot(xl, wr_ref[:, :LANES]) + br_ref[...]

    lane = lax.broadcasted_iota(jnp.int32, (tm, LANES), 1).astype(F32)
    work = logits
    top_v, top_i, onehots = [], [], []
    for _ in range(TOP_K):
        mk = jnp.max(work, axis=-1, keepdims=True)
        ik = jnp.min(jnp.where(work == mk, lane, float(LANES)), axis=-1, keepdims=True)
        oh = lane == ik
        work = jnp.where(oh, -jnp.inf, work)
        top_v.append(mk)
        top_i.append(ik)
        onehots.append(oh)

    ex = [jnp.exp(v - top_v[0]) for v in top_v]
    denom = ex[0] + ex[1] + ex[2] + ex[3]
    gates = [e / denom for e in ex]

    sel = jnp.zeros((tm, LANES), F32)
    for oh in onehots:
        sel = sel + oh.astype(F32)
    r = lax.broadcasted_iota(jnp.int32, (tm, tm), 0)
    c = lax.broadcasted_iota(jnp.int32, (tm, tm), 1)
    tri = jnp.where(c < r, 1.0, 0.0).astype(BF16)
    cum = _dot(tri, sel.astype(BF16)) + carry_ref[...]
    carry_ref[...] = carry_ref[...] + jnp.sum(sel, axis=0, keepdims=True)
    cnt_ref[...] = carry_ref[...]

    ri = jnp.zeros((tm, LANES), F32)
    rf = jnp.zeros((tm, LANES), F32)
    for k in range(TOP_K):
        rank_k = jnp.sum(jnp.where(onehots[k], cum, 0.0), axis=-1, keepdims=True)
        ri = jnp.where(lane == float(k), top_i[k], ri)
        ri = jnp.where(lane == float(TOP_K + k), rank_k, ri)
        rf = jnp.where(lane == float(k), gates[k], rf)
    ri_ref[...] = ri.T[:2 * TOP_K].astype(jnp.int32)
    rf_ref[...] = rf


def _mix_out(convn, attn, x2, w, *, tm):
    t, d = x2.shape
    kern = functools.partial(_mix_out_kernel, tm=tm)
    row = lambda width: pl.BlockSpec((tm, width), lambda i: (i, 0))
    return pl.pallas_call(
        kern,
        grid=(t // tm,),
        in_specs=[
            row(D_CONV), row(ATTN_W), row(d),
            _resident((D_CONV + ATTN_W, d)),
            _resident((1, d)),
            _resident((d, 2 * LANES)),
            _resident((1, LANES)),
        ],
        out_specs=[row(d), row(d), pl.BlockSpec((2 * TOP_K, tm), lambda i: (0, i)), row(LANES),
                   pl.BlockSpec((1, LANES), lambda i: (0, 0))],
        out_shape=[
            jax.ShapeDtypeStruct((t, d), F32),
            jax.ShapeDtypeStruct((t, d), F32),
            jax.ShapeDtypeStruct((2 * TOP_K, t), jnp.int32),
            jax.ShapeDtypeStruct((t, LANES), F32),
            jax.ShapeDtypeStruct((1, LANES), F32),
        ],
        scratch_shapes=[pltpu.VMEM((1, LANES), F32)],
        compiler_params=pltpu.CompilerParams(
            dimension_semantics=("arbitrary",), vmem_limit_bytes=VMEM_LIMIT),
        name="mix_out",
    )(convn, attn, x2, w["wo"], w["g_ffn"], w["wr"], w["br"])


def _dispatch_kernel(dest_ref, pend_ref, xn_ref, xg_hbm, zbuf_ref, sem, zsem, *, tmd, sb):
    i = pl.program_id(0)

    @pl.when(i == 0)
    def _():
        zbuf_ref[...] = jnp.zeros_like(zbuf_ref)

        def zero_copy(e):
            end = pend_ref[e]
            start = pl.multiple_of(jnp.maximum(end - sb, 0), sb)
            return pltpu.make_async_copy(zbuf_ref, xg_hbm.at[pl.ds(start, sb)], zsem)

        def nonempty(e):
            prev = pend_ref[jnp.maximum(e - 1, 0)]
            return pend_ref[e] > jnp.where(e > 0, prev, 0)

        def start(e, carry):
            @pl.when(nonempty(e))
            def _():
                zero_copy(e).start()
            return carry

        def wait(e, carry):
            @pl.when(nonempty(e))
            def _():
                zero_copy(e).wait()
            return carry

        lax.fori_loop(0, N_EXPERTS, start, 0)
        lax.fori_loop(0, N_EXPERTS, wait, 0)
        _zero_tail(zbuf_ref, xg_hbm, pend_ref[N_EXPERTS - 1], zsem)

    base = i * tmd
    n_tok = pl.num_programs(0) * tmd

    def issue(r, carry):
        for k in range(TOP_K):
            dst = xg_hbm.at[pl.ds(dest_ref[k * n_tok + base + r], 1)]
            pltpu.make_async_copy(xn_ref.at[pl.ds(r, 1)], dst, sem).start()
        return carry

    lax.fori_loop(0, tmd, issue, 0)
    for k in range(TOP_K):
        pltpu.make_async_copy(xn_ref, xg_hbm.at[pl.ds(0, tmd)], sem).wait()


def _dispatch(dest, pad_end, xn, *, rows, tmd, sb):
    t, d = xn.shape
    kern = functools.partial(_dispatch_kernel, tmd=tmd, sb=sb)
    return pl.pallas_call(
        kern,
        grid_spec=pltpu.PrefetchScalarGridSpec(
            num_scalar_prefetch=2,
            grid=(t // tmd,),
            in_specs=[pl.BlockSpec((tmd, d), lambda i, dest, pend: (i, 0))],
            out_specs=pl.BlockSpec(memory_space=pl.ANY),
            scratch_shapes=[pltpu.VMEM((sb, d), F32), pltpu.SemaphoreType.DMA, pltpu.SemaphoreType.DMA],
        ),
        out_shape=jax.ShapeDtypeStruct((rows, d), F32),
        compiler_params=pltpu.CompilerParams(
            dimension_semantics=("arbitrary",), vmem_limit_bytes=VMEM_LIMIT),
        name="dispatch",
    )(dest, pad_end, xn)


def _expert_kernel(ie_ref, irow_ref, inr_ref, tail_ref, xg_hbm, wg_ref, wl_ref, wd_ref, bg_ref, bl_ref, bd_ref, yg_hbm,
                   xst_ref, xb_ref, act_ref, acc_ref, wgb_ref, wlb_ref, wdb_ref, sem_in, sem_out,
                   *, tm, sb, chunks):
    it = pl.program_id(0)
    s = pl.program_id(1)
    n_it = pl.num_programs(0)
    nrows = inr_ref[it]
    n_up = act_ref.shape[0]
    n_down, _, fc = acc_ref.shape
    last = n_up + n_down - 1
    prev_it = jnp.maximum(it - 1, 0)
    next_it = jnp.minimum(it + 1, n_it - 1)

    def piece(r):
        return pl.ds(r * sb, sb)

    def hbm_rows(item, r):
        return pl.ds(pl.multiple_of(irow_ref[item] + r * sb, sb), sb)

    def in_copy(item, r):
        return pltpu.make_async_copy(xg_hbm.at[hbm_rows(item, r)], xst_ref.at[piece(r)], sem_in.at[r])

    def out_copies(item, r):
        return [pltpu.make_async_copy(acc_ref.at[n, piece(r)], yg_hbm.at[hbm_rows(item, r), n * fc:(n + 1) * fc],
                                      sem_out.at[r]) for n in range(n_down)]

    def for_pieces_of(item, fn):
        for r in range(tm // sb):
            pl.when(r * sb < inr_ref[item])(functools.partial(fn, item, r))

    def for_valid_chunks(fn):
        for n, (c0, cn) in enumerate(chunks):
            pl.when(c0 < nrows)(functools.partial(fn, pl.ds(c0, cn), n == 0))

    def start_in(item, r):
        in_copy(item, r).start()

    def land_in(item, r):
        in_copy(item, r).wait()
        xb_ref[piece(r)] = xst_ref[piece(r)].astype(BF16)

    def start_out(item, r):
        for cp in out_copies(item, r):
            cp.start()

    def wait_out(item, r):
        for cp in out_copies(item, r):
            cp.wait()

    @pl.when((it == 0) & (s == 0))
    def _():
        xb_ref[...] = jnp.zeros_like(xb_ref)
        for_pieces_of(it, start_in)

    @pl.when(s == 0)
    def _():
        for_pieces_of(it, land_in)

    @pl.when((s == 1) & (it + 1 < n_it))
    def _():
        for_pieces_of(next_it, start_in)

    @pl.when((s == n_up) & (it > 0))
    def _():
        for_pieces_of(prev_it, wait_out)

    def up(rows, cast_weights):
        if cast_weights:
            wgb_ref[...] = wg_ref[0].astype(BF16)
            wlb_ref[...] = wl_ref[0].astype(BF16)
        xs = xb_ref[rows]
        g = jnp.minimum(_dot(xs, wgb_ref[...]) + bg_ref[0], SWIGLU_LIMIT)
        lin = jnp.clip(_dot(xs, wlb_ref[...]) + bl_ref[0], -SWIGLU_LIMIT, SWIGLU_LIMIT)
        act_ref[s, rows] = ((lin + 1.0) * (g * jax.nn.sigmoid(SWIGLU_ALPHA * g))).astype(BF16)

    def down(rows, cast_weights):
        if cast_weights:
            wdb_ref[...] = wd_ref[0].astype(BF16)
        act = jnp.concatenate([act_ref[jj, rows] for jj in range(n_up)], axis=-1)
        acc_ref[s - n_up, rows] = _dot(act, wdb_ref[...]) + bd_ref[0]

    @pl.when(s < n_up)
    def _():
        for_valid_chunks(up)

    @pl.when(s >= n_up)
    def _():
        for_valid_chunks(down)

    @pl.when(s == last)
    def _():
        for_pieces_of(it, start_out)

    @pl.when((it == n_it - 1) & (s == last))
    def _():
        for_pieces_of(it, wait_out)
        acc_ref[0, piece(0)] = jnp.zeros((sb, fc), F32)
        _zero_tail(acc_ref.at[0, piece(0)], yg_hbm, tail_ref[0], sem_out.at[0])


def _zero_tail(zeros_vmem, dst_hbm, first_row, sem):
    sb, width = zeros_vmem.shape
    n_blocks = (dst_hbm.shape[0] - first_row) // sb

    def fill(b, carry):
        rows = pl.ds(pl.multiple_of(first_row + b * sb, sb), sb)
        copies = [pltpu.make_async_copy(zeros_vmem, dst_hbm.at[rows, c * width:(c + 1) * width], sem)
                  for c in range(dst_hbm.shape[1] // width)]
        for cp in copies:
            cp.start()
        for cp in copies:
            cp.wait()
        return carry

    lax.fori_loop(0, n_blocks, fill, 0)


def _experts(item_e, item_row, item_nrows, tail, xg, w_gate_up, b_gate_up, w_down, b_down, *, tm, sb, fc):
    rows, d = xg.shape
    n_items = item_e.shape[0]
    dff = w_down.shape[1]
    n_up = dff // fc
    n_down = d // fc
    nsub = tm // sb
    kern = functools.partial(_expert_kernel, tm=tm, sb=sb, chunks=_row_chunks(tm, sb))

    def up_chunk(it, s, inr):
        return jnp.where(inr[it] > 0, jnp.minimum(s, n_up - 1), n_up - 1)

    def down_chunk(it, s, inr):
        return jnp.where(inr[it] > 0, jnp.maximum(s - n_up, 0), n_down - 1)

    return pl.pallas_call(
        kern,
        grid_spec=pltpu.PrefetchScalarGridSpec(
            num_scalar_prefetch=4,
            grid=(n_items, n_up + n_down),
            in_specs=[
                pl.BlockSpec(memory_space=pl.ANY),
                pl.BlockSpec((1, d, fc), lambda it, s, ie, ir, inr, tl: (ie[it], 0, up_chunk(it, s, inr))),
                pl.BlockSpec((1, d, fc), lambda it, s, ie, ir, inr, tl: (ie[it], 0, n_up + up_chunk(it, s, inr))),
                pl.BlockSpec((1, dff, fc), lambda it, s, ie, ir, inr, tl: (ie[it], 0, down_chunk(it, s, inr))),
                pl.BlockSpec((1, 1, fc), lambda it, s, ie, ir, inr, tl: (ie[it], 0, up_chunk(it, s, inr))),
                pl.BlockSpec((1, 1, fc), lambda it, s, ie, ir, inr, tl: (ie[it], 0, n_up + up_chunk(it, s, inr))),
                pl.BlockSpec((1, 1, fc), lambda it, s, ie, ir, inr, tl: (ie[it], 0, down_chunk(it, s, inr))),
            ],
            out_specs=pl.BlockSpec(memory_space=pl.ANY),
            scratch_shapes=[
                pltpu.VMEM((tm, d), F32),
                pltpu.VMEM((tm, d), BF16),
                pltpu.VMEM((n_up, tm, fc), BF16),
                pltpu.VMEM((n_down, tm, fc), F32),
                pltpu.VMEM((d, fc), BF16),
                pltpu.VMEM((d, fc), BF16),
                pltpu.VMEM((dff, fc), BF16),
                pltpu.SemaphoreType.DMA((nsub,)),
                pltpu.SemaphoreType.DMA((nsub,)),
            ],
        ),
        out_shape=jax.ShapeDtypeStruct((rows, d), F32),
        compiler_params=pltpu.CompilerParams(
            dimension_semantics=("arbitrary", "arbitrary"), vmem_limit_bytes=VMEM_LIMIT),
        name="experts",
    )(item_e, item_row, item_nrows, tail, xg, w_gate_up, w_gate_up, w_down,
      b_gate_up, b_gate_up, b_down)


def _combine_kernel(dest_ref, yg_hbm, h1_ref, rf_ref, gfin_ref, o_ref, gbuf_ref, sem, *, tmc):
    base = pl.program_id(0) * tmc
    n_tok = pl.num_programs(0) * tmc

    def issue(r, carry):
        for k in range(TOP_K):
            src = yg_hbm.at[pl.ds(dest_ref[k * n_tok + base + r], 1)]
            pltpu.make_async_copy(src, gbuf_ref.at[k, pl.ds(r, 1)], sem).start()
        return carry

    lax.fori_loop(0, tmc, issue, 0)
    for k in range(TOP_K):
        pltpu.make_async_copy(yg_hbm.at[pl.ds(0, tmc)], gbuf_ref.at[k], sem).wait()

    gates = rf_ref[...]
    y = h1_ref[...]
    for k in range(TOP_K):
        y = y + gates[:, k:k + 1] * gbuf_ref[k]
    o_ref[...] = _rms(y, gfin_ref[...])


def _combine(dest, yg, h1, rf, g_final, *, tmc):
    t, d = h1.shape
    kern = functools.partial(_combine_kernel, tmc=tmc)
    return pl.pallas_call(
        kern,
        grid_spec=pltpu.PrefetchScalarGridSpec(
            num_scalar_prefetch=1,
            grid=(t // tmc,),
            in_specs=[
                pl.BlockSpec(memory_space=pl.ANY),
                pl.BlockSpec((tmc, d), lambda i, dest: (i, 0)),
                pl.BlockSpec((tmc, LANES), lambda i, dest: (i, 0)),
                pl.BlockSpec((1, d), lambda i, dest: (0, 0)),
            ],
            out_specs=pl.BlockSpec((tmc, d), lambda i, dest: (i, 0)),
            scratch_shapes=[pltpu.VMEM((TOP_K, tmc, d), F32), pltpu.SemaphoreType.DMA],
        ),
        out_shape=jax.ShapeDtypeStruct((t, d), F32),
        compiler_params=pltpu.CompilerParams(
            dimension_semantics=("arbitrary",), vmem_limit_bytes=VMEM_LIMIT),
        name="combine",
    )(dest, yg, h1, rf, g_final)


def _rotate_half_cols(w):
    half = w.shape[-1] // 2
    return jnp.concatenate([-w[..., half:], w[..., :half]], axis=-1)


def _prep_weights(g_mix, w_in, conv_w, g_q, w_q_up, g_kv, w_kv_up, g_conv_out, w_out, g_ffn, w_router, b_router):
    d = w_in.shape[0]
    k_rope = w_in[:, O_KPE:O_KPE + QK_ROPE]
    pad = jnp.zeros((d, LANES - QK_ROPE), F32)
    wk2 = jnp.concatenate([k_rope, pad, _rotate_half_cols(k_rope), pad], axis=1).astype(BF16)

    wq = w_q_up.reshape(Q_LORA, N_HEADS, QK_NOPE + QK_ROPE)
    nope, pe = wq[:, :, :QK_NOPE], wq[:, :, QK_NOPE:]
    hpad = jnp.zeros((Q_LORA, N_HEADS, LANES - QK_ROPE), F32)
    wqa = jnp.concatenate([nope, pe, hpad], axis=2).reshape(Q_LORA, N_HEADS * HEAD_W)
    wqb = jnp.concatenate([_rotate_half_cols(pe), hpad], axis=2).reshape(Q_LORA, N_HEADS * LANES)
    wq_all = jnp.concatenate([wqa, wqb], axis=1).astype(BF16)

    wkv = w_kv_up.reshape(KV_LORA, N_HEADS, QK_NOPE + V_DIM)
    wkv2 = jnp.concatenate([wkv[:, :, :QK_NOPE].reshape(KV_LORA, ATTN_W),
                            wkv[:, :, QK_NOPE:].reshape(KV_LORA, ATTN_W)], axis=1).astype(BF16)

    wr_pad = jnp.zeros((d, LANES), F32).at[:, :N_EXPERTS].set(w_router)
    wr_hi = wr_pad.astype(BF16)
    wr_lo = (wr_pad - wr_hi.astype(F32)).astype(BF16)
    br = jnp.full((1, LANES), -1e30, F32).at[0, :N_EXPERTS].set(b_router)

    grp = jnp.arange(D_CONV) // (D_CONV // CONV_GROUPS)
    cw = jnp.zeros((SUBLANES, D_CONV), F32).at[:CONV_WIDTH].set(conv_w)
    return {
        "g_mix": g_mix[None], "w1": w_in.astype(BF16), "wk2": wk2, "conv_w": cw, "g_q": g_q[None], "wq": wq_all, "g_kv": g_kv[None],
        "wkv": wkv2, "g_conv_out": g_conv_out[None], "gmat": (grp[:, None] == grp[None, :]).astype(BF16),
        "wo": w_out.astype(BF16), "g_ffn": g_ffn[None], "wr": jnp.concatenate([wr_hi, wr_lo], axis=1), "br": br,
    }


def _rope_table(pos):
    half = QK_ROPE // 2
    inv_freq = np.float32(ROPE_THETA) ** (-np.arange(half, dtype=np.float32) / np.float32(half))
    ang = (np.asarray(pos, np.float32)[:, None] * inv_freq[None, :]).astype(np.float32)
    c, s = np.cos(ang), np.sin(ang)
    z = np.zeros((ang.shape[0], LANES - QK_ROPE), np.float32)
    return jnp.asarray(np.concatenate([c, c, z, s, s, z], axis=1), dtype=F32)


def _schedule(counts, eidx, rank, *, tm, sb, n_items):
    padded = (counts + sb - 1) // sb * sb
    pad_end = jnp.cumsum(padded)
    pad_start = pad_end - padded
    dest = (pad_start[eidx] + rank).reshape(-1).astype(jnp.int32)

    per_e = (counts + tm - 1) // tm
    it_end = jnp.cumsum(per_e)
    it_start = it_end - per_e
    ii = jnp.arange(n_items)
    e_of = jnp.minimum(jnp.sum(it_end[None, :] <= ii[:, None], axis=1), N_EXPERTS - 1)
    valid = ii < it_end[-1]
    li = ii - it_start[e_of]
    item_row = jnp.where(valid, pad_start[e_of] + li * tm, 0)
    item_nrows = jnp.where(valid, jnp.clip(counts[e_of] - li * tm, 0, tm), 0)
    e_last = e_of[jnp.maximum(it_end[-1] - 1, 0)]
    item_e = jnp.where(valid, e_of, e_last)
    i32 = lambda a: a.astype(jnp.int32)
    return dest, i32(pad_end), i32(item_e), i32(item_row), i32(item_nrows)


def _row_chunks(tm, sb, big=512):
    n_big = (tm - 2 * sb) // big
    chunks = [(c * big, big) for c in range(n_big)]
    chunks += [(r, sb) for r in range(n_big * big, tm, sb)]
    return tuple(chunks)


def _moe_tiles(t):
    sb = 128
    tm = 10 * sb
    a = t * TOP_K
    rows = (a + N_EXPERTS * (sb - 1) + sb - 1) // sb * sb
    n_items = N_EXPERTS + a // tm
    return sb, tm, rows, n_items


def _layer(x, meta_tokens, w, w_gate_up, b_gate_up, w_down, b_down, g_attn_out, g_final, *,
           tm_in, tq, tm_out, tmd, tmc, fc):
    b, seq, d = x.shape
    t = b * seq

    meta_blk = jnp.zeros((1, META_ROWS, d), F32).at[0, META_ROWS - N_META:].set(meta_tokens)
    meta_pos = np.maximum(np.arange(META_ROWS) - (META_ROWS - N_META), 0)
    zero_tail = jnp.zeros((SUBLANES, D_CONV), F32)
    _, _, mkn, mkpe, mv, u_tail = _mix_in(meta_blk, zero_tail, _rope_table(meta_pos), w, tm=META_ROWS)

    real_pos = np.arange(seq) + N_META
    convn, q, kn, kpe, v, _ = _mix_in(x, u_tail, _rope_table(real_pos), w, tm=tm_in)
    attn = _attention(q, kn, kpe, v, mkn, mkpe, mv, g_attn_out[None], tq=tq)

    h1, xn, ri, rf, cnt = _mix_out(convn.reshape(t, D_CONV), attn.reshape(t, ATTN_W), x.reshape(t, d), w, tm=tm_out)

    sb, tm_e, rows, n_items = _moe_tiles(t)
    counts = cnt[0, :N_EXPERTS].astype(jnp.int32)
    dest, pad_end, item_e, item_row, item_nrows = _schedule(
        counts, ri[:TOP_K], ri[TOP_K:], tm=tm_e, sb=sb, n_items=n_items)

    xg = _dispatch(dest, pad_end, xn, rows=rows, tmd=tmd, sb=sb)
    dff = w_down.shape[1]
    yg = _experts(item_e, item_row, item_nrows, pad_end[N_EXPERTS - 1:], xg, w_gate_up, b_gate_up.reshape(N_EXPERTS, 1, 2 * dff),
                  w_down, b_down.reshape(N_EXPERTS, 1, d), tm=tm_e, sb=sb, fc=fc)
    out = _combine(dest, yg, h1, rf, g_final[None], tmc=tmc)
    return out.reshape(b, seq, d)


def kernel(x, meta_tokens, g_mix, w_in, conv_w, g_q, w_q_up, g_kv, w_kv_up, g_conv_out, g_attn_out, w_out, g_ffn,
           w_router, b_router, w_gate_up, b_gate_up, w_down, b_down, g_final):
    w = _prep_weights(g_mix[0], w_in[0], conv_w[0], g_q[0], w_q_up[0], g_kv[0], w_kv_up[0], g_conv_out[0],
                      w_out[0], g_ffn[0], w_router[0], b_router[0])
    seq = x.shape[1]
    return _layer(x, meta_tokens, w, w_gate_up[0], b_gate_up[0], w_down[0], b_down[0], g_attn_out[0], g_final,
                  tm_in=min(256, seq), tq=min(512, seq), tm_out=min(256, seq), tmd=min(512, seq),
                  tmc=min(128, seq), fc=256)
```

```python
import functools

import jax
import jax.numpy as jnp
import numpy as np
from jax import lax
from jax.experimental import pallas as pl
from jax.experimental.pallas import tpu as pltpu

N_META = 16
EPS = 1e-6
D_CONV = 1024
CONV_GROUPS = 16
CONV_WIDTH = 3
N_HEADS = 8
QK_NOPE = 128
QK_ROPE = 64
V_DIM = 128
Q_LORA = 512
KV_LORA = 256
ROPE_THETA = 10000.0
N_EXPERTS = 32
TOP_K = 4
SWIGLU_LIMIT = 7.0
SWIGLU_ALPHA = 1.702

LANES = 128
SUBLANES = 8
META_ROWS = 128
HEAD_W = 2 * LANES
ATTN_W = N_HEADS * V_DIM
V_EXT = V_DIM + 16
VMEM_LIMIT = 56 * 1024 * 1024

O_B, O_C, O_U = 0, D_CONV, 2 * D_CONV
O_Q = 3 * D_CONV
O_KV = O_Q + Q_LORA
O_KPE = O_KV + KV_LORA
D_IN = O_KPE + QK_ROPE

F32 = jnp.float32
BF16 = jnp.bfloat16


def _rms(x, g):
    return x * lax.rsqrt(jnp.mean(x * x, axis=-1, keepdims=True) + EPS) * g


def _dot(a, b):
    return jnp.dot(a, b, preferred_element_type=F32)


def _resident(shape):
    zeros = (0,) * len(shape)
    return pl.BlockSpec(shape, lambda *_: zeros, pipeline_mode=pl.Buffered(1))


def _mix_in_kernel(x_ref, gmix_ref, w1_ref, wk2_ref, cw_ref, gq_ref, wq_ref, gkv_ref, wkv_ref, gco_ref, gmat_ref,
                   tab_ref, uinit_ref,
                   convn_ref, qt_ref, kn_ref, kpe_ref, vt_ref, utail_ref,
                   ubuf_ref, *, tm, scale):
    i = pl.program_id(1)
    hn = _rms(x_ref[0], gmix_ref[...]).astype(BF16)

    def proj(lo, hi):
        return _dot(hn, w1_ref[:, lo:hi])

    @pl.when(i == 0)
    def _():
        ubuf_ref[0:SUBLANES] = uinit_ref[...]

    @pl.when(i > 0)
    def _():
        ubuf_ref[0:SUBLANES] = ubuf_ref[tm:tm + SUBLANES]

    u = proj(O_C, O_U) * proj(O_U, O_Q)
    ubuf_ref[SUBLANES:SUBLANES + tm] = u
    cw = cw_ref[...]
    y = (cw[2:3] * u + cw[1:2] * ubuf_ref[SUBLANES - 1:SUBLANES - 1 + tm]
         + cw[0:1] * ubuf_ref[SUBLANES - 2:SUBLANES - 2 + tm])
    co = proj(O_B, O_C) * y
    ss = _dot((co * co).astype(BF16), gmat_ref[...])
    group = D_CONV // CONV_GROUPS
    convn_ref[0] = (co * lax.rsqrt(ss * (1.0 / group) + EPS) * gco_ref[...]).astype(BF16)
    utail_ref[...] = ubuf_ref[tm:tm + SUBLANES]

    cos = tab_ref[:, :LANES]
    sin = tab_ref[:, LANES:]

    qn = _rms(proj(O_Q, O_KV), gq_ref[...]).astype(BF16)
    qa = _dot(qn, wq_ref[:, :N_HEADS * HEAD_W])
    qb = _dot(qn, wq_ref[:, N_HEADS * HEAD_W:])
    for h in range(N_HEADS):
        c0 = h * HEAD_W
        qt_ref[0, c0:c0 + LANES, :] = (qa[:, c0:c0 + LANES] * scale).T.astype(BF16)
        pe = qa[:, c0 + LANES:c0 + HEAD_W] * cos + qb[:, h * LANES:(h + 1) * LANES] * sin
        qt_ref[0, c0 + LANES:c0 + HEAD_W, :] = (pe * scale).T.astype(BF16)

    kvn = _rms(proj(O_KV, O_KPE), gkv_ref[...]).astype(BF16)
    kv = _dot(kvn, wkv_ref[...])
    kn_ref[0] = kv[:, :ATTN_W].astype(BF16)
    vt = kv[:, ATTN_W:].T.astype(BF16)
    ones = jnp.ones((V_EXT - V_DIM, tm), BF16)
    for h in range(N_HEADS):
        vt_ref[0, 0, h * V_EXT:h * V_EXT + V_DIM, :] = vt[h * V_DIM:(h + 1) * V_DIM]
        vt_ref[0, 0, h * V_EXT + V_DIM:(h + 1) * V_EXT, :] = ones
    kk = _dot(hn, wk2_ref[...])
    kpe_ref[0] = (kk[:, :LANES] * cos + kk[:, LANES:] * sin).astype(BF16)


def _mix_in(x3, uinit, tab, w, *, tm):
    b, l, d = x3.shape
    nt = l // tm
    kern = functools.partial(_mix_in_kernel, tm=tm, scale=float((QK_NOPE + QK_ROPE) ** -0.5))
    row = lambda width: pl.BlockSpec((1, tm, width), lambda bi, i: (bi, i, 0))
    return pl.pallas_call(
        kern,
        grid=(b, nt),
        in_specs=[
            row(d),
            _resident((1, d)),
            _resident((d, D_IN)),
            _resident((d, 2 * LANES)),
            _resident((SUBLANES, D_CONV)),
            _resident((1, Q_LORA)),
            _resident((Q_LORA, N_HEADS * (HEAD_W + LANES))),
            _resident((1, KV_LORA)),
            _resident((KV_LORA, 2 * ATTN_W)),
            _resident((1, D_CONV)),
            _resident((D_CONV, D_CONV)),
            pl.BlockSpec((tm, 2 * LANES), lambda bi, i: (i, 0)),
            _resident((SUBLANES, D_CONV)),
        ],
        out_specs=[
            row(D_CONV),
            pl.BlockSpec((1, N_HEADS * HEAD_W, tm), lambda bi, i: (bi, 0, i)),
            row(ATTN_W), row(LANES),
            pl.BlockSpec((1, 1, N_HEADS * V_EXT, tm), lambda bi, i: (bi, i, 0, 0)),
            pl.BlockSpec((SUBLANES, D_CONV), lambda bi, i: (bi * nt + i, 0)),
        ],
        out_shape=[
            jax.ShapeDtypeStruct((b, l, D_CONV), BF16),
            jax.ShapeDtypeStruct((b, N_HEADS * HEAD_W, l), BF16),
            jax.ShapeDtypeStruct((b, l, ATTN_W), BF16),
            jax.ShapeDtypeStruct((b, l, LANES), BF16),
            jax.ShapeDtypeStruct((b, nt, N_HEADS * V_EXT, tm), BF16),
            jax.ShapeDtypeStruct((b * nt * SUBLANES, D_CONV), F32),
        ],
        scratch_shapes=[pltpu.VMEM((tm + SUBLANES, D_CONV), F32)],
        compiler_params=pltpu.CompilerParams(
            dimension_semantics=("arbitrary", "arbitrary"), vmem_limit_bytes=VMEM_LIMIT),
        name="mix_in",
    )(x3, w["g_mix"], w["w1"], w["wk2"], w["conv_w"], w["g_q"], w["wq"], w["g_kv"], w["wkv"], w["g_conv_out"],
      w["gmat"], tab, uinit)


def _attn_kernel(qt_ref, kn_ref, kpe_ref, vt_ref, mkn_ref, mkpe_ref, mvt_ref, g_ref, o_ref,
                 m_ref, acc_ref, *, tq, hps):
    qi = pl.program_id(2)
    tv = vt_ref.shape[3]

    def lanes(hh):
        return slice(hh * LANES, (hh + 1) * LANES)

    def vrows(hh):
        return slice(hh * V_EXT, (hh + 1) * V_EXT)

    def qt(hh):
        return qt_ref[0, hh * HEAD_W:(hh + 1) * HEAD_W, :]

    for hh in range(hps):
        km = jnp.concatenate([mkn_ref[0, :, lanes(hh)], mkpe_ref[0]], axis=-1)
        s = _dot(km, qt(hh))
        row = lax.broadcasted_iota(jnp.int32, s.shape, 0)
        s = jnp.where(row >= META_ROWS - N_META, s, -jnp.inf)
        m0 = jnp.max(s, axis=0, keepdims=True)
        m_ref[hh] = m0
        acc_ref[hh] = _dot(mvt_ref[0, 0, vrows(hh), :], jnp.exp((s - m0).astype(BF16)))

    def step(kb, diagonal):
        off = pl.multiple_of(kb * tq, tq)
        kpe = kpe_ref[0, pl.ds(off, tq), :]
        for hh in range(hps):
            k = jnp.concatenate([kn_ref[0, pl.ds(off, tq), lanes(hh)], kpe], axis=-1)
            s = _dot(k, qt(hh))
            if diagonal:
                r = lax.broadcasted_iota(jnp.int32, s.shape, 0)
                c = lax.broadcasted_iota(jnp.int32, s.shape, 1)
                s = jnp.where(r <= c, s, -jnp.inf)
            m_prev = m_ref[hh]
            m_new = jnp.maximum(m_prev, jnp.max(s, axis=0, keepdims=True))
            alpha = jnp.exp(m_prev - m_new)
            pb = jnp.exp((s - m_new).astype(BF16))
            pv = _dot(vt_ref[0, kb * (tq // tv), vrows(hh), :], pb[:tv])
            for c in range(1, tq // tv):
                pv = pv + _dot(vt_ref[0, kb * (tq // tv) + c, vrows(hh), :], pb[c * tv:(c + 1) * tv])
            acc_ref[hh] = alpha * acc_ref[hh] + pv
            m_ref[hh] = m_new

    def body(kb, carry):
        step(kb, False)
        return carry

    lax.fori_loop(0, qi, body, 0)
    step(qi, True)

    for hh in range(hps):
        acc = acc_ref[hh]
        o = (acc[:V_DIM] / acc[V_DIM:V_DIM + 1]).T
        o_ref[0, :, lanes(hh)] = _rms(o, g_ref[:, lanes(hh)]).astype(BF16)


def _attention(qt, kn, kpe, vt, mkn, mkpe, mvt, g_attn, *, tq, hps=2):
    b, l, _ = kn.shape
    _, nt, _, tv = vt.shape
    nq = l // tq
    kern = functools.partial(_attn_kernel, tq=tq, hps=hps)
    return pl.pallas_call(
        kern,
        grid=(b, N_HEADS // hps, nq),
        in_specs=[
            pl.BlockSpec((1, hps * HEAD_W, tq), lambda bi, h, i: (bi, h, i)),
            pl.BlockSpec((1, l, hps * LANES), lambda bi, h, i: (bi, 0, h)),
            pl.BlockSpec((1, l, LANES), lambda bi, h, i: (bi, 0, 0)),
            pl.BlockSpec((1, nt, hps * V_EXT, tv), lambda bi, h, i: (bi, 0, h, 0)),
            pl.BlockSpec((1, META_ROWS, hps * LANES), lambda bi, h, i: (0, 0, h)),
            pl.BlockSpec((1, META_ROWS, LANES), lambda bi, h, i: (0, 0, 0)),
            pl.BlockSpec((1, 1, hps * V_EXT, META_ROWS), lambda bi, h, i: (0, 0, h, 0)),
            pl.BlockSpec((1, hps * V_DIM), lambda bi, h, i: (0, h)),
        ],
        out_specs=pl.BlockSpec((1, tq, hps * V_DIM), lambda bi, h, i: (bi, i, h)),
        out_shape=jax.ShapeDtypeStruct((b, l, ATTN_W), BF16),
        scratch_shapes=[pltpu.VMEM((hps, 1, tq), F32), pltpu.VMEM((hps, V_EXT, tq), F32)],
        compiler_params=pltpu.CompilerParams(
            dimension_semantics=("arbitrary", "arbitrary", "arbitrary"), vmem_limit_bytes=VMEM_LIMIT),
        name="attn",
    )(qt, kn, kpe, vt, mkn, mkpe, mvt, g_attn)


def _mix_out_kernel(convn_ref, attn_ref, x_ref, wo_ref, gffn_ref, wr_ref, br_ref,
                    h1_ref, xn_ref, ri_ref, rf_ref, cnt_ref, carry_ref, *, tm):
    @pl.when(pl.program_id(0) == 0)
    def _():
        carry_ref[...] = jnp.zeros_like(carry_ref)

    h1 = x_ref[...] + _dot(convn_ref[...], wo_ref[:D_CONV]) + _dot(attn_ref[...], wo_ref[D_CONV:])
    h1_ref[...] = h1
    xn = _rms(h1, gffn_ref[...])
    xn_ref[...] = xn

    xh = xn.astype(BF16)
    xl = (xn - xh.astype(F32)).astype(BF16)
    ph = _dot(xh, wr_ref[...])
    logits = ph[:, :LANES] + ph[:, LANES:] + _dot(xl, wr_ref[:, :LANES]) + br_ref[...]

    lane = lax.broadcasted_iota(jnp.int32, (tm, LANES), 1).astype(F32)
    work = logits
    top_v, top_i, onehots = [], [], []
    for _ in range(TOP_K):
        mk = jnp.max(work, axis=-1, keepdims=True)
        ik = jnp.min(jnp.where(work == mk, lane, float(LANES)), axis=-1, keepdims=True)
        oh = lane == ik
        work = jnp.where(oh, -jnp.inf, work)
        top_v.append(mk)
        top_i.append(ik)
        onehots.append(oh)

    ex = [jnp.exp(v - top_v[0]) for v in top_v]
    denom = ex[0] + ex[1] + ex[2] + ex[3]
    gates = [e / denom for e in ex]

    sel = jnp.zeros((tm, LANES), F32)
    for oh in onehots:
        sel = sel + oh.astype(F32)
    r = lax.broadcasted_iota(jnp.int32, (tm, tm), 0)
    c = lax.broadcasted_iota(jnp.int32, (tm, tm), 1)
    tri = jnp.where(c < r, 1.0, 0.0).astype(BF16)
    cum = _dot(tri, sel.astype(BF16)) + carry_ref[...]
    carry_ref[...] = carry_ref[...] + jnp.sum(sel, axis=0, keepdims=True)
    cnt_ref[...] = carry_ref[...]

    ri = jnp.zeros((tm, LANES), F32)
    rf = jnp.zeros((tm, LANES), F32)
    for k in range(TOP_K):
        rank_k = jnp.sum(jnp.where(onehots[k], cum, 0.0), axis=-1, keepdims=True)
        ri = jnp.where(lane == float(k), top_i[k], ri)
        ri = jnp.where(lane == float(TOP_K + k), rank_k, ri)
        rf = jnp.where(lane == float(k), gates[k], rf)
    ri_ref[...] = ri.T[:2 * TOP_K].astype(jnp.int32)
    rf_ref[...] = rf


def _mix_out(convn, attn, x2, w, *, tm):
    t, d = x2.shape
    kern = functools.partial(_mix_out_kernel, tm=tm)
    row = lambda width: pl.BlockSpec((tm, width), lambda i: (i, 0))
    return pl.pallas_call(
        kern,
        grid=(t // tm,),
        in_specs=[
            row(D_CONV), row(ATTN_W), row(d),
            _resident((D_CONV + ATTN_W, d)),
            _resident((1, d)),
            _resident((d, 2 * LANES)),
            _resident((1, LANES)),
        ],
        out_specs=[row(d), row(d), pl.BlockSpec((2 * TOP_K, tm), lambda i: (0, i)), row(LANES),
                   pl.BlockSpec((1, LANES), lambda i: (0, 0))],
        out_shape=[
            jax.ShapeDtypeStruct((t, d), F32),
            jax.ShapeDtypeStruct((t, d), F32),
            jax.ShapeDtypeStruct((2 * TOP_K, t), jnp.int32),
            jax.ShapeDtypeStruct((t, LANES), F32),
            jax.ShapeDtypeStruct((1, LANES), F32),
        ],
        scratch_shapes=[pltpu.VMEM((1, LANES), F32)],
        compiler_params=pltpu.CompilerParams(
            dimension_semantics=("arbitrary",), vmem_limit_bytes=VMEM_LIMIT),
        name="mix_out",
    )(convn, attn, x2, w["wo"], w["g_ffn"], w["wr"], w["br"])


def _dispatch_kernel(dest_ref, pend_ref, xn_ref, xg_hbm, zbuf_ref, sem, zsem, *, tmd, sb):
    i = pl.program_id(0)

    @pl.when(i == 0)
    def _():
        zbuf_ref[...] = jnp.zeros_like(zbuf_ref)

        def zero_copy(e):
            end = pend_ref[e]
            start = pl.multiple_of(jnp.maximum(end - sb, 0), sb)
            return pltpu.make_async_copy(zbuf_ref, xg_hbm.at[pl.ds(start, sb)], zsem)

        def nonempty(e):
            prev = pend_ref[jnp.maximum(e - 1, 0)]
            return pend_ref[e] > jnp.where(e > 0, prev, 0)

        def start(e, carry):
            @pl.when(nonempty(e))
            def _():
                zero_copy(e).start()
            return carry

        def wait(e, carry):
            @pl.when(nonempty(e))
            def _():
                zero_copy(e).wait()
            return carry

        lax.fori_loop(0, N_EXPERTS, start, 0)
        lax.fori_loop(0, N_EXPERTS, wait, 0)
        _zero_tail(zbuf_ref, xg_hbm, pend_ref[N_EXPERTS - 1], zsem)

    base = i * tmd
    n_tok = pl.num_programs(0) * tmd

    def issue(r, carry):
        for k in range(TOP_K):
            dst = xg_hbm.at[pl.ds(dest_ref[k * n_tok + base + r], 1)]
            pltpu.make_async_copy(xn_ref.at[pl.ds(r, 1)], dst, sem).start()
        return carry

    lax.fori_loop(0, tmd, issue, 0)
    for k in range(TOP_K):
        pltpu.make_async_copy(xn_ref, xg_hbm.at[pl.ds(0, tmd)], sem).wait()


def _dispatch(dest, pad_end, xn, *, rows, tmd, sb):
    t, d = xn.shape
    kern = functools.partial(_dispatch_kernel, tmd=tmd, sb=sb)
    return pl.pallas_call(
        kern,
        grid_spec=pltpu.PrefetchScalarGridSpec(
            num_scalar_prefetch=2,
            grid=(t // tmd,),
            in_specs=[pl.BlockSpec((tmd, d), lambda i, dest, pend: (i, 0))],
            out_specs=pl.BlockSpec(memory_space=pl.ANY),
            scratch_shapes=[pltpu.VMEM((sb, d), F32), pltpu.SemaphoreType.DMA, pltpu.SemaphoreType.DMA],
        ),
        out_shape=jax.ShapeDtypeStruct((rows, d), F32),
        compiler_params=pltpu.CompilerParams(
            dimension_semantics=("arbitrary",), vmem_limit_bytes=VMEM_LIMIT),
        name="dispatch",
    )(dest, pad_end, xn)


def _expert_kernel(ie_ref, irow_ref, inr_ref, tail_ref, xg_hbm, wg_ref, wl_ref, wd_ref, bg_ref, bl_ref, bd_ref, yg_hbm,
                   xst_ref, xb_ref, act_ref, acc_ref, wgb_ref, wlb_ref, wdb_ref, sem_in, sem_out,
                   *, tm, sb, chunks):
    it = pl.program_id(0)
    s = pl.program_id(1)
    n_it = pl.num_programs(0)
    nrows = inr_ref[it]
    n_up = act_ref.shape[0]
    n_down, _, fc = acc_ref.shape
    last = n_up + n_down - 1
    prev_it = jnp.maximum(it - 1, 0)
    next_it = jnp.minimum(it + 1, n_it - 1)

    def piece(r):
        return pl.ds(r * sb, sb)

    def hbm_rows(item, r):
        return pl.ds(pl.multiple_of(irow_ref[item] + r * sb, sb), sb)

    def in_copy(item, r):
        return pltpu.make_async_copy(xg_hbm.at[hbm_rows(item, r)], xst_ref.at[piece(r)], sem_in.at[r])

    def out_copies(item, r):
        return [pltpu.make_async_copy(acc_ref.at[n, piece(r)], yg_hbm.at[hbm_rows(item, r), n * fc:(n + 1) * fc],
                                      sem_out.at[r]) for n in range(n_down)]

    def for_pieces_of(item, fn):
        for r in range(tm // sb):
            pl.when(r * sb < inr_ref[item])(functools.partial(fn, item, r))

    def for_valid_chunks(fn):
        for n, (c0, cn) in enumerate(chunks):
            pl.when(c0 < nrows)(functools.partial(fn, pl.ds(c0, cn), n == 0))

    def start_in(item, r):
        in_copy(item, r).start()

    def land_in(item, r):
        in_copy(item, r).wait()
        xb_ref[piece(r)] = xst_ref[piece(r)].astype(BF16)

    def start_out(item, r):
        for cp in out_copies(item, r):
            cp.start()

    def wait_out(item, r):
        for cp in out_copies(item, r):
            cp.wait()

    @pl.when((it == 0) & (s == 0))
    def _():
        xb_ref[...] = jnp.zeros_like(xb_ref)
        for_pieces_of(it, start_in)

    @pl.when(s == 0)
    def _():
        for_pieces_of(it, land_in)

    @pl.when((s == 1) & (it + 1 < n_it))
    def _():
        for_pieces_of(next_it, start_in)

    @pl.when((s == n_up) & (it > 0))
    def _():
        for_pieces_of(prev_it, wait_out)

    def up(rows, cast_weights):
        if cast_weights:
            wgb_ref[...] = wg_ref[0].astype(BF16)
            wlb_ref[...] = wl_ref[0].astype(BF16)
        xs = xb_ref[rows]
        g = jnp.minimum(_dot(xs, wgb_ref[...]) + bg_ref[0], SWIGLU_LIMIT)
        lin = jnp.clip(_dot(xs, wlb_ref[...]) + bl_ref[0], -SWIGLU_LIMIT, SWIGLU_LIMIT)
        act_ref[s, rows] = ((lin + 1.0) * (g * jax.nn.sigmoid(SWIGLU_ALPHA * g))).astype(BF16)

    def down(rows, cast_weights):
        if cast_weights:
            wdb_ref[...] = wd_ref[0].astype(BF16)
        act = jnp.concatenate([act_ref[jj, rows] for jj in range(n_up)], axis=-1)
        acc_ref[s - n_up, rows] = _dot(act, wdb_ref[...]) + bd_ref[0]

    @pl.when(s < n_up)
    def _():
        for_valid_chunks(up)

    @pl.when(s >= n_up)
    def _():
        for_valid_chunks(down)

    @pl.when(s == last)
    def _():
        for_pieces_of(it, start_out)

    @pl.when((it == n_it - 1) & (s == last))
    def _():
        for_pieces_of(it, wait_out)
        acc_ref[0, piece(0)] = jnp.zeros((sb, fc), F32)
        _zero_tail(acc_ref.at[0, piece(0)], yg_hbm, tail_ref[0], sem_out.at[0])


def _zero_tail(zeros_vmem, dst_hbm, first_row, sem):
    sb, width = zeros_vmem.shape
    n_blocks = (dst_hbm.shape[0] - first_row) // sb

    def fill(b, carry):
        rows = pl.ds(pl.multiple_of(first_row + b * sb, sb), sb)
        copies = [pltpu.make_async_copy(zeros_vmem, dst_hbm.at[rows, c * width:(c + 1) * width], sem)
                  for c in range(dst_hbm.shape[1] // width)]
        for cp in copies:
            cp.start()
        for cp in copies:
            cp.wait()
        return carry

    lax.fori_loop(0, n_blocks, fill, 0)


def _experts(n_used, item_e, item_row, item_nrows, tail, xg, w_gate_up, b_gate_up, w_down, b_down, *, tm, sb, fc):
    rows, d = xg.shape
    n_items = item_e.shape[0]
    dff = w_down.shape[1]
    n_up = dff // fc
    n_down = d // fc
    nsub = tm // sb
    kern = functools.partial(_expert_kernel, tm=tm, sb=sb, chunks=_row_chunks(tm, sb))

    def up_chunk(it, s, inr):
        return jnp.where(inr[it] > 0, jnp.minimum(s, n_up - 1), n_up - 1)

    def down_chunk(it, s, inr):
        return jnp.where(inr[it] > 0, jnp.maximum(s - n_up, 0), n_down - 1)

    return pl.pallas_call(
        kern,
        grid_spec=pltpu.PrefetchScalarGridSpec(
            num_scalar_prefetch=4,
            grid=(n_used, n_up + n_down),
            in_specs=[
                pl.BlockSpec(memory_space=pl.ANY),
                pl.BlockSpec((1, d, fc), lambda it, s, ie, ir, inr, tl: (ie[it], 0, up_chunk(it, s, inr))),
                pl.BlockSpec((1, d, fc), lambda it, s, ie, ir, inr, tl: (ie[it], 0, n_up + up_chunk(it, s, inr))),
                pl.BlockSpec((1, dff, fc), lambda it, s, ie, ir, inr, tl: (ie[it], 0, down_chunk(it, s, inr))),
                pl.BlockSpec((1, 1, fc), lambda it, s, ie, ir, inr, tl: (ie[it], 0, up_chunk(it, s, inr))),
                pl.BlockSpec((1, 1, fc), lambda it, s, ie, ir, inr, tl: (ie[it], 0, n_up + up_chunk(it, s, inr))),
                pl.BlockSpec((1, 1, fc), lambda it, s, ie, ir, inr, tl: (ie[it], 0, down_chunk(it, s, inr))),
            ],
            out_specs=pl.BlockSpec(memory_space=pl.ANY),
            scratch_shapes=[
                pltpu.VMEM((tm, d), F32),
                pltpu.VMEM((tm, d), BF16),
                pltpu.VMEM((n_up, tm, fc), BF16),
                pltpu.VMEM((n_down, tm, fc), F32),
                pltpu.VMEM((d, fc), BF16),
                pltpu.VMEM((d, fc), BF16),
                pltpu.VMEM((dff, fc), BF16),
                pltpu.SemaphoreType.DMA((nsub,)),
                pltpu.SemaphoreType.DMA((nsub,)),
            ],
        ),
        out_shape=jax.ShapeDtypeStruct((rows, d), F32),
        compiler_params=pltpu.CompilerParams(
            dimension_semantics=("arbitrary", "arbitrary"), vmem_limit_bytes=VMEM_LIMIT),
        name="experts",
    )(item_e, item_row, item_nrows, tail, xg, w_gate_up, w_gate_up, w_down,
      b_gate_up, b_gate_up, b_down)


def _combine_kernel(dest_ref, yg_hbm, h1_ref, rf_ref, gfin_ref, o_ref, gbuf_ref, sem, *, tmc):
    base = pl.program_id(0) * tmc
    n_tok = pl.num_programs(0) * tmc

    def issue(r, carry):
        for k in range(TOP_K):
            src = yg_hbm.at[pl.ds(dest_ref[k * n_tok + base + r], 1)]
            pltpu.make_async_copy(src, gbuf_ref.at[k, pl.ds(r, 1)], sem).start()
        return carry

    lax.fori_loop(0, tmc, issue, 0)
    for k in range(TOP_K):
        pltpu.make_async_copy(yg_hbm.at[pl.ds(0, tmc)], gbuf_ref.at[k], sem).wait()

    gates = rf_ref[...]
    y = h1_ref[...]
    for k in range(TOP_K):
        y = y + gates[:, k:k + 1] * gbuf_ref[k]
    o_ref[...] = _rms(y, gfin_ref[...])


def _combine(dest, yg, h1, rf, g_final, *, tmc):
    t, d = h1.shape
    kern = functools.partial(_combine_kernel, tmc=tmc)
    return pl.pallas_call(
        kern,
        grid_spec=pltpu.PrefetchScalarGridSpec(
            num_scalar_prefetch=1,
            grid=(t // tmc,),
            in_specs=[
                pl.BlockSpec(memory_space=pl.ANY),
                pl.BlockSpec((tmc, d), lambda i, dest: (i, 0)),
                pl.BlockSpec((tmc, LANES), lambda i, dest: (i, 0)),
                pl.BlockSpec((1, d), lambda i, dest: (0, 0)),
            ],
            out_specs=pl.BlockSpec((tmc, d), lambda i, dest: (i, 0)),
            scratch_shapes=[pltpu.VMEM((TOP_K, tmc, d), F32), pltpu.SemaphoreType.DMA],
        ),
        out_shape=jax.ShapeDtypeStruct((t, d), F32),
        compiler_params=pltpu.CompilerParams(
            dimension_semantics=("arbitrary",), vmem_limit_bytes=VMEM_LIMIT),
        name="combine",
    )(dest, yg, h1, rf, g_final)


def _rotate_half_cols(w):
    half = w.shape[-1] // 2
    return jnp.concatenate([-w[..., half:], w[..., :half]], axis=-1)


def _prep_weights(g_mix, w_in, conv_w, g_q, w_q_up, g_kv, w_kv_up, g_conv_out, w_out, g_ffn, w_router, b_router):
    d = w_in.shape[0]
    k_rope = w_in[:, O_KPE:O_KPE + QK_ROPE]
    pad = jnp.zeros((d, LANES - QK_ROPE), F32)
    wk2 = jnp.concatenate([k_rope, pad, _rotate_half_cols(k_rope), pad], axis=1).astype(BF16)

    wq = w_q_up.reshape(Q_LORA, N_HEADS, QK_NOPE + QK_ROPE)
    nope, pe = wq[:, :, :QK_NOPE], wq[:, :, QK_NOPE:]
    hpad = jnp.zeros((Q_LORA, N_HEADS, LANES - QK_ROPE), F32)
    wqa = jnp.concatenate([nope, pe, hpad], axis=2).reshape(Q_LORA, N_HEADS * HEAD_W)
    wqb = jnp.concatenate([_rotate_half_cols(pe), hpad], axis=2).reshape(Q_LORA, N_HEADS * LANES)
    wq_all = jnp.concatenate([wqa, wqb], axis=1).astype(BF16)

    wkv = w_kv_up.reshape(KV_LORA, N_HEADS, QK_NOPE + V_DIM)
    wkv2 = jnp.concatenate([wkv[:, :, :QK_NOPE].reshape(KV_LORA, ATTN_W),
                            wkv[:, :, QK_NOPE:].reshape(KV_LORA, ATTN_W)], axis=1).astype(BF16)

    wr_pad = jnp.zeros((d, LANES), F32).at[:, :N_EXPERTS].set(w_router)
    wr_hi = wr_pad.astype(BF16)
    wr_lo = (wr_pad - wr_hi.astype(F32)).astype(BF16)
    br = jnp.full((1, LANES), -1e30, F32).at[0, :N_EXPERTS].set(b_router)

    grp = jnp.arange(D_CONV) // (D_CONV // CONV_GROUPS)
    cw = jnp.zeros((SUBLANES, D_CONV), F32).at[:CONV_WIDTH].set(conv_w)
    return {
        "g_mix": g_mix[None], "w1": w_in.astype(BF16), "wk2": wk2, "conv_w": cw, "g_q": g_q[None], "wq": wq_all, "g_kv": g_kv[None],
        "wkv": wkv2, "g_conv_out": g_conv_out[None], "gmat": (grp[:, None] == grp[None, :]).astype(BF16),
        "wo": w_out.astype(BF16), "g_ffn": g_ffn[None], "wr": jnp.concatenate([wr_hi, wr_lo], axis=1), "br": br,
    }


def _rope_table(pos):
    half = QK_ROPE // 2
    inv_freq = np.float32(ROPE_THETA) ** (-np.arange(half, dtype=np.float32) / np.float32(half))
    ang = (np.asarray(pos, np.float32)[:, None] * inv_freq[None, :]).astype(np.float32)
    c, s = np.cos(ang), np.sin(ang)
    z = np.zeros((ang.shape[0], LANES - QK_ROPE), np.float32)
    return jnp.asarray(np.concatenate([c, c, z, s, s, z], axis=1), dtype=F32)


def _schedule(counts, eidx, rank, *, tm, sb, n_items):
    padded = (counts + sb - 1) // sb * sb
    pad_end = jnp.cumsum(padded)
    pad_start = pad_end - padded
    start_of = jnp.zeros_like(eidx)
    for e in range(N_EXPERTS):
        start_of = jnp.where(eidx == e, pad_start[e], start_of)
    dest = (start_of + rank).reshape(-1).astype(jnp.int32)

    per_e = (counts + tm - 1) // tm
    it_end = jnp.cumsum(per_e)
    it_start = it_end - per_e
    ii = jnp.arange(n_items)
    e_of = jnp.minimum(jnp.sum(it_end[None, :] <= ii[:, None], axis=1), N_EXPERTS - 1)
    valid = ii < it_end[-1]
    li = ii - it_start[e_of]
    item_row = jnp.where(valid, pad_start[e_of] + li * tm, 0)
    item_nrows = jnp.where(valid, jnp.clip(counts[e_of] - li * tm, 0, tm), 0)
    e_last = e_of[jnp.maximum(it_end[-1] - 1, 0)]
    item_e = jnp.where(valid, e_of, e_last)
    i32 = lambda a: a.astype(jnp.int32)
    return dest, i32(pad_end), i32(item_e), i32(item_row), i32(item_nrows), i32(it_end[-1])


def _row_chunks(tm, sb, big=512):
    n_big = (tm - 2 * sb) // big
    chunks = [(c * big, big) for c in range(n_big)]
    chunks += [(r, sb) for r in range(n_big * big, tm, sb)]
    return tuple(chunks)


def _moe_tiles(t):
    sb = 128
    tm = 10 * sb
    a = t * TOP_K
    rows = (a + N_EXPERTS * (sb - 1) + sb - 1) // sb * sb
    n_items = N_EXPERTS + a // tm
    return sb, tm, rows, n_items


def _layer(x, meta_tokens, w, w_gate_up, b_gate_up, w_down, b_down, g_attn_out, g_final, *,
           tm_in, tq, tm_out, tmd, tmc, fc):
    b, seq, d = x.shape
    t = b * seq

    meta_blk = jnp.zeros((1, META_ROWS, d), F32).at[0, META_ROWS - N_META:].set(meta_tokens)
    meta_pos = np.maximum(np.arange(META_ROWS) - (META_ROWS - N_META), 0)
    zero_tail = jnp.zeros((SUBLANES, D_CONV), F32)
    _, _, mkn, mkpe, mv, u_tail = _mix_in(meta_blk, zero_tail, _rope_table(meta_pos), w, tm=META_ROWS)

    real_pos = np.arange(seq) + N_META
    convn, q, kn, kpe, v, _ = _mix_in(x, u_tail, _rope_table(real_pos), w, tm=tm_in)
    attn = _attention(q, kn, kpe, v, mkn, mkpe, mv, g_attn_out[None], tq=tq)

    h1, xn, ri, rf, cnt = _mix_out(convn.reshape(t, D_CONV), attn.reshape(t, ATTN_W), x.reshape(t, d), w, tm=tm_out)

    sb, tm_e, rows, n_items = _moe_tiles(t)
    counts = cnt[0, :N_EXPERTS].astype(jnp.int32)
    dest, pad_end, item_e, item_row, item_nrows, n_used = _schedule(
        counts, ri[:TOP_K], ri[TOP_K:], tm=tm_e, sb=sb, n_items=n_items)

    xg = _dispatch(dest, pad_end, xn, rows=rows, tmd=tmd, sb=sb)
    dff = w_down.shape[1]
    yg = _experts(n_used, item_e, item_row, item_nrows, pad_end[N_EXPERTS - 1:], xg, w_gate_up, b_gate_up.reshape(N_EXPERTS, 1, 2 * dff),
                  w_down, b_down.reshape(N_EXPERTS, 1, d), tm=tm_e, sb=sb, fc=fc)
    out = _combine(dest, yg, h1, rf, g_final[None], tmc=tmc)
    return out.reshape(b, seq, d)


def kernel(x, meta_tokens, g_mix, w_in, conv_w, g_q, w_q_up, g_kv, w_kv_up, g_conv_out, g_attn_out, w_out, g_ffn,
           w_router, b_router, w_gate_up, b_gate_up, w_down, b_down, g_final):
    w = _prep_weights(g_mix[0], w_in[0], conv_w[0], g_q[0], w_q_up[0], g_kv[0], w_kv_up[0], g_conv_out[0],
                      w_out[0], g_ffn[0], w_router[0], b_router[0])
    seq = x.shape[1]
    return _layer(x, meta_tokens, w, w_gate_up[0], b_gate_up[0], w_down[0], b_down[0], g_attn_out[0], g_final,
                  tm_in=min(256, seq), tq=min(512, seq), tm_out=min(256, seq), tmd=min(512, seq),
                  tmc=min(128, seq), fc=256)
```

```python
import functools

import jax
import jax.numpy as jnp
import numpy as np
from jax import lax
from jax.experimental import pallas as pl
from jax.experimental.pallas import tpu as pltpu

N_META = 16
EPS = 1e-6
D_CONV = 1024
CONV_GROUPS = 16
CONV_WIDTH = 3
N_HEADS = 8
QK_NOPE = 128
QK_ROPE = 64
V_DIM = 128
Q_LORA = 512
KV_LORA = 256
ROPE_THETA = 10000.0
N_EXPERTS = 32
TOP_K = 4
SWIGLU_LIMIT = 7.0
SWIGLU_ALPHA = 1.702

LANES = 128
SUBLANES = 8
META_ROWS = 128
HEAD_W = 2 * LANES
ATTN_W = N_HEADS * V_DIM
V_EXT = V_DIM + 16
VMEM_LIMIT = 56 * 1024 * 1024

O_B, O_C, O_U = 0, D_CONV, 2 * D_CONV
O_Q = 3 * D_CONV
O_KV = O_Q + Q_LORA
O_KPE = O_KV + KV_LORA
D_IN = O_KPE + QK_ROPE

F32 = jnp.float32
BF16 = jnp.bfloat16


def _rms(x, g):
    return x * lax.rsqrt(jnp.mean(x * x, axis=-1, keepdims=True) + EPS) * g


def _dot(a, b):
    return jnp.dot(a, b, preferred_element_type=F32)


def _resident(shape):
    zeros = (0,) * len(shape)
    return pl.BlockSpec(shape, lambda *_: zeros, pipeline_mode=pl.Buffered(1))


def _mix_in_kernel(x_ref, gmix_ref, w1_ref, wk2_ref, cw_ref, gq_ref, wq_ref, gkv_ref, wkv_ref, gco_ref, gmat_ref,
                   tab_ref, uinit_ref,
                   convn_ref, qt_ref, kn_ref, kpe_ref, vt_ref, utail_ref,
                   ubuf_ref, *, tm, scale):
    i = pl.program_id(1)
    hn = _rms(x_ref[0], gmix_ref[...]).astype(BF16)

    def proj(lo, hi):
        return _dot(hn, w1_ref[:, lo:hi])

    @pl.when(i == 0)
    def _():
        ubuf_ref[0:SUBLANES] = uinit_ref[...]

    @pl.when(i > 0)
    def _():
        ubuf_ref[0:SUBLANES] = ubuf_ref[tm:tm + SUBLANES]

    u = proj(O_C, O_U) * proj(O_U, O_Q)
    ubuf_ref[SUBLANES:SUBLANES + tm] = u
    cw = cw_ref[...]
    y = (cw[2:3] * u + cw[1:2] * ubuf_ref[SUBLANES - 1:SUBLANES - 1 + tm]
         + cw[0:1] * ubuf_ref[SUBLANES - 2:SUBLANES - 2 + tm])
    co = proj(O_B, O_C) * y
    ss = _dot((co * co).astype(BF16), gmat_ref[...])
    group = D_CONV // CONV_GROUPS
    convn_ref[0] = (co * lax.rsqrt(ss * (1.0 / group) + EPS) * gco_ref[...]).astype(BF16)
    utail_ref[...] = ubuf_ref[tm:tm + SUBLANES]

    cos = tab_ref[:, :LANES]
    sin = tab_ref[:, LANES:]

    qn = _rms(proj(O_Q, O_KV), gq_ref[...]).astype(BF16)
    qa = _dot(qn, wq_ref[:, :N_HEADS * HEAD_W])
    qb = _dot(qn, wq_ref[:, N_HEADS * HEAD_W:])
    for h in range(N_HEADS):
        c0 = h * HEAD_W
        qt_ref[0, c0:c0 + LANES, :] = (qa[:, c0:c0 + LANES] * scale).T.astype(BF16)
        pe = qa[:, c0 + LANES:c0 + HEAD_W] * cos + qb[:, h * LANES:(h + 1) * LANES] * sin
        qt_ref[0, c0 + LANES:c0 + HEAD_W, :] = (pe * scale).T.astype(BF16)

    kvn = _rms(proj(O_KV, O_KPE), gkv_ref[...]).astype(BF16)
    kv = _dot(kvn, wkv_ref[...])
    kn_ref[0] = kv[:, :ATTN_W].astype(BF16)
    vt = kv[:, ATTN_W:].T.astype(BF16)
    ones = jnp.ones((V_EXT - V_DIM, tm), BF16)
    for h in range(N_HEADS):
        vt_ref[0, 0, h * V_EXT:h * V_EXT + V_DIM, :] = vt[h * V_DIM:(h + 1) * V_DIM]
        vt_ref[0, 0, h * V_EXT + V_DIM:(h + 1) * V_EXT, :] = ones
    kk = _dot(hn, wk2_ref[...])
    kpe_ref[0] = (kk[:, :LANES] * cos + kk[:, LANES:] * sin).astype(BF16)


def _mix_in(x3, uinit, tab, w, *, tm):
    b, l, d = x3.shape
    nt = l // tm
    kern = functools.partial(_mix_in_kernel, tm=tm, scale=float((QK_NOPE + QK_ROPE) ** -0.5))
    row = lambda width: pl.BlockSpec((1, tm, width), lambda bi, i: (bi, i, 0))
    return pl.pallas_call(
        kern,
        grid=(b, nt),
        in_specs=[
            row(d),
            _resident((1, d)),
            _resident((d, D_IN)),
            _resident((d, 2 * LANES)),
            _resident((SUBLANES, D_CONV)),
            _resident((1, Q_LORA)),
            _resident((Q_LORA, N_HEADS * (HEAD_W + LANES))),
            _resident((1, KV_LORA)),
            _resident((KV_LORA, 2 * ATTN_W)),
            _resident((1, D_CONV)),
            _resident((D_CONV, D_CONV)),
            pl.BlockSpec((tm, 2 * LANES), lambda bi, i: (i, 0)),
            _resident((SUBLANES, D_CONV)),
        ],
        out_specs=[
            row(D_CONV),
            pl.BlockSpec((1, N_HEADS * HEAD_W, tm), lambda bi, i: (bi, 0, i)),
            row(ATTN_W), row(LANES),
            pl.BlockSpec((1, 1, N_HEADS * V_EXT, tm), lambda bi, i: (bi, i, 0, 0)),
            pl.BlockSpec((SUBLANES, D_CONV), lambda bi, i: (bi * nt + i, 0)),
        ],
        out_shape=[
            jax.ShapeDtypeStruct((b, l, D_CONV), BF16),
            jax.ShapeDtypeStruct((b, N_HEADS * HEAD_W, l), BF16),
            jax.ShapeDtypeStruct((b, l, ATTN_W), BF16),
            jax.ShapeDtypeStruct((b, l, LANES), BF16),
            jax.ShapeDtypeStruct((b, nt, N_HEADS * V_EXT, tm), BF16),
            jax.ShapeDtypeStruct((b * nt * SUBLANES, D_CONV), F32),
        ],
        scratch_shapes=[pltpu.VMEM((tm + SUBLANES, D_CONV), F32)],
        compiler_params=pltpu.CompilerParams(
            dimension_semantics=("arbitrary", "arbitrary"), vmem_limit_bytes=VMEM_LIMIT),
        name="mix_in",
    )(x3, w["g_mix"], w["w1"], w["wk2"], w["conv_w"], w["g_q"], w["wq"], w["g_kv"], w["wkv"], w["g_conv_out"],
      w["gmat"], tab, uinit)


def _attn_kernel(qt_ref, kn_ref, kpe_ref, vt_ref, mkn_ref, mkpe_ref, mvt_ref, g_ref, o_ref,
                 m_ref, acc_ref, *, tq, hps):
    qi = pl.program_id(2)
    tv = vt_ref.shape[3]

    def lanes(hh):
        return slice(hh * LANES, (hh + 1) * LANES)

    def vrows(hh):
        return slice(hh * V_EXT, (hh + 1) * V_EXT)

    def qt(hh):
        return qt_ref[0, hh * HEAD_W:(hh + 1) * HEAD_W, :]

    for hh in range(hps):
        km = jnp.concatenate([mkn_ref[0, :, lanes(hh)], mkpe_ref[0]], axis=-1)
        s = _dot(km, qt(hh))
        row = lax.broadcasted_iota(jnp.int32, s.shape, 0)
        s = jnp.where(row >= META_ROWS - N_META, s, -jnp.inf)
        m0 = jnp.max(s, axis=0, keepdims=True)
        m_ref[hh] = m0
        acc_ref[hh] = _dot(mvt_ref[0, 0, vrows(hh), :], jnp.exp((s - m0).astype(BF16)))

    def step(kb, diagonal):
        off = pl.multiple_of(kb * tq, tq)
        kpe = kpe_ref[0, pl.ds(off, tq), :]
        scores = []
        for hh in range(hps):
            k = jnp.concatenate([kn_ref[0, pl.ds(off, tq), lanes(hh)], kpe], axis=-1)
            s = _dot(k, qt(hh))
            if diagonal:
                r = lax.broadcasted_iota(jnp.int32, s.shape, 0)
                c = lax.broadcasted_iota(jnp.int32, s.shape, 1)
                s = jnp.where(r <= c, s, -jnp.inf)
            scores.append(s)
        probs, alphas = [], []
        for hh in range(hps):
            m_prev = m_ref[hh]
            m_new = jnp.maximum(m_prev, jnp.max(scores[hh], axis=0, keepdims=True))
            alphas.append(jnp.exp(m_prev - m_new))
            probs.append(jnp.exp((scores[hh] - m_new).astype(BF16)))
            m_ref[hh] = m_new
        for hh in range(hps):
            pb = probs[hh]
            pv = _dot(vt_ref[0, kb * (tq // tv), vrows(hh), :], pb[:tv])
            for c in range(1, tq // tv):
                pv = pv + _dot(vt_ref[0, kb * (tq // tv) + c, vrows(hh), :], pb[c * tv:(c + 1) * tv])
            acc_ref[hh] = alphas[hh] * acc_ref[hh] + pv

    def body(kb, carry):
        step(kb, False)
        return carry

    lax.fori_loop(0, qi, body, 0)
    step(qi, True)

    for hh in range(hps):
        acc = acc_ref[hh]
        o = (acc[:V_DIM] / acc[V_DIM:V_DIM + 1]).T
        o_ref[0, :, lanes(hh)] = _rms(o, g_ref[:, lanes(hh)]).astype(BF16)


def _attention(qt, kn, kpe, vt, mkn, mkpe, mvt, g_attn, *, tq, hps=4):
    b, l, _ = kn.shape
    _, nt, _, tv = vt.shape
    nq = l // tq
    kern = functools.partial(_attn_kernel, tq=tq, hps=hps)
    return pl.pallas_call(
        kern,
        grid=(b, N_HEADS // hps, nq),
        in_specs=[
            pl.BlockSpec((1, hps * HEAD_W, tq), lambda bi, h, i: (bi, h, i)),
            pl.BlockSpec((1, l, hps * LANES), lambda bi, h, i: (bi, 0, h)),
            pl.BlockSpec((1, l, LANES), lambda bi, h, i: (bi, 0, 0)),
            pl.BlockSpec((1, nt, hps * V_EXT, tv), lambda bi, h, i: (bi, 0, h, 0)),
            pl.BlockSpec((1, META_ROWS, hps * LANES), lambda bi, h, i: (0, 0, h)),
            pl.BlockSpec((1, META_ROWS, LANES), lambda bi, h, i: (0, 0, 0)),
            pl.BlockSpec((1, 1, hps * V_EXT, META_ROWS), lambda bi, h, i: (0, 0, h, 0)),
            pl.BlockSpec((1, hps * V_DIM), lambda bi, h, i: (0, h)),
        ],
        out_specs=pl.BlockSpec((1, tq, hps * V_DIM), lambda bi, h, i: (bi, i, h)),
        out_shape=jax.ShapeDtypeStruct((b, l, ATTN_W), BF16),
        scratch_shapes=[pltpu.VMEM((hps, 1, tq), F32), pltpu.VMEM((hps, V_EXT, tq), F32)],
        compiler_params=pltpu.CompilerParams(
            dimension_semantics=("arbitrary", "arbitrary", "arbitrary"), vmem_limit_bytes=VMEM_LIMIT),
        name="attn",
    )(qt, kn, kpe, vt, mkn, mkpe, mvt, g_attn)


def _mix_out_kernel(convn_ref, attn_ref, x_ref, wo_ref, gffn_ref, wr_ref, br_ref,
                    h1_ref, xn_ref, ri_ref, rf_ref, cnt_ref, carry_ref, *, tm):
    @pl.when(pl.program_id(0) == 0)
    def _():
        carry_ref[...] = jnp.zeros_like(carry_ref)

    h1 = x_ref[...] + _dot(convn_ref[...], wo_ref[:D_CONV]) + _dot(attn_ref[...], wo_ref[D_CONV:])
    h1_ref[...] = h1
    xn = _rms(h1, gffn_ref[...])
    xn_ref[...] = xn

    xh = xn.astype(BF16)
    xl = (xn - xh.astype(F32)).astype(BF16)
    ph = _dot(xh, wr_ref[...])
    logits = ph[:, :LANES] + ph[:, LANES:] + _dot(xl, wr_ref[:, :LANES]) + br_ref[...]

    lane = lax.broadcasted_iota(jnp.int32, (tm, LANES), 1).astype(F32)
    work = logits
    top_v, top_i, onehots = [], [], []
    for _ in range(TOP_K):
        mk = jnp.max(work, axis=-1, keepdims=True)
        ik = jnp.min(jnp.where(work == mk, lane, float(LANES)), axis=-1, keepdims=True)
        oh = lane == ik
        work = jnp.where(oh, -jnp.inf, work)
        top_v.append(mk)
        top_i.append(ik)
        onehots.append(oh)

    ex = [jnp.exp(v - top_v[0]) for v in top_v]
    denom = ex[0] + ex[1] + ex[2] + ex[3]
    gates = [e / denom for e in ex]

    sel = jnp.zeros((tm, LANES), F32)
    for oh in onehots:
        sel = sel + oh.astype(F32)
    r = lax.broadcasted_iota(jnp.int32, (tm, tm), 0)
    c = lax.broadcasted_iota(jnp.int32, (tm, tm), 1)
    tri = jnp.where(c < r, 1.0, 0.0).astype(BF16)
    cum = _dot(tri, sel.astype(BF16)) + carry_ref[...]
    carry_ref[...] = carry_ref[...] + jnp.sum(sel, axis=0, keepdims=True)
    cnt_ref[...] = carry_ref[...]

    ri = jnp.zeros((tm, LANES), F32)
    rf = jnp.zeros((tm, LANES), F32)
    for k in range(TOP_K):
        rank_k = jnp.sum(jnp.where(onehots[k], cum, 0.0), axis=-1, keepdims=True)
        ri = jnp.where(lane == float(k), top_i[k], ri)
        ri = jnp.where(lane == float(TOP_K + k), rank_k, ri)
        rf = jnp.where(lane == float(k), gates[k], rf)
    ri_ref[...] = ri.T[:2 * TOP_K].astype(jnp.int32)
    rf_ref[...] = rf


def _mix_out(convn, attn, x2, w, *, tm):
    t, d = x2.shape
    kern = functools.partial(_mix_out_kernel, tm=tm)
    row = lambda width: pl.BlockSpec((tm, width), lambda i: (i, 0))
    return pl.pallas_call(
        kern,
        grid=(t // tm,),
        in_specs=[
            row(D_CONV), row(ATTN_W), row(d),
            _resident((D_CONV + ATTN_W, d)),
            _resident((1, d)),
            _resident((d, 2 * LANES)),
            _resident((1, LANES)),
        ],
        out_specs=[row(d), row(d), pl.BlockSpec((2 * TOP_K, tm), lambda i: (0, i)), row(LANES),
                   pl.BlockSpec((1, LANES), lambda i: (0, 0))],
        out_shape=[
            jax.ShapeDtypeStruct((t, d), F32),
            jax.ShapeDtypeStruct((t, d), F32),
            jax.ShapeDtypeStruct((2 * TOP_K, t), jnp.int32),
            jax.ShapeDtypeStruct((t, LANES), F32),
            jax.ShapeDtypeStruct((1, LANES), F32),
        ],
        scratch_shapes=[pltpu.VMEM((1, LANES), F32)],
        compiler_params=pltpu.CompilerParams(
            dimension_semantics=("arbitrary",), vmem_limit_bytes=VMEM_LIMIT),
        name="mix_out",
    )(convn, attn, x2, w["wo"], w["g_ffn"], w["wr"], w["br"])


def _dispatch_kernel(dest_ref, pend_ref, xn_ref, xg_hbm, zbuf_ref, sem, zsem, *, tmd, sb):
    i = pl.program_id(0)

    @pl.when(i == 0)
    def _():
        zbuf_ref[...] = jnp.zeros_like(zbuf_ref)

        def zero_copy(e):
            end = pend_ref[e]
            start = pl.multiple_of(jnp.maximum(end - sb, 0), sb)
            return pltpu.make_async_copy(zbuf_ref, xg_hbm.at[pl.ds(start, sb)], zsem)

        def nonempty(e):
            prev = pend_ref[jnp.maximum(e - 1, 0)]
            return pend_ref[e] > jnp.where(e > 0, prev, 0)

        def start(e, carry):
            @pl.when(nonempty(e))
            def _():
                zero_copy(e).start()
            return carry

        def wait(e, carry):
            @pl.when(nonempty(e))
            def _():
                zero_copy(e).wait()
            return carry

        lax.fori_loop(0, N_EXPERTS, start, 0)
        lax.fori_loop(0, N_EXPERTS, wait, 0)
        _zero_tail(zbuf_ref, xg_hbm, pend_ref[N_EXPERTS - 1], zsem)

    base = i * tmd
    n_tok = pl.num_programs(0) * tmd

    def issue(r, carry):
        for k in range(TOP_K):
            dst = xg_hbm.at[pl.ds(dest_ref[k * n_tok + base + r], 1)]
            pltpu.make_async_copy(xn_ref.at[pl.ds(r, 1)], dst, sem).start()
        return carry

    lax.fori_loop(0, tmd, issue, 0)
    for k in range(TOP_K):
        pltpu.make_async_copy(xn_ref, xg_hbm.at[pl.ds(0, tmd)], sem).wait()


def _dispatch(dest, pad_end, xn, *, rows, tmd, sb):
    t, d = xn.shape
    kern = functools.partial(_dispatch_kernel, tmd=tmd, sb=sb)
    return pl.pallas_call(
        kern,
        grid_spec=pltpu.PrefetchScalarGridSpec(
            num_scalar_prefetch=2,
            grid=(t // tmd,),
            in_specs=[pl.BlockSpec((tmd, d), lambda i, dest, pend: (i, 0))],
            out_specs=pl.BlockSpec(memory_space=pl.ANY),
            scratch_shapes=[pltpu.VMEM((sb, d), F32), pltpu.SemaphoreType.DMA, pltpu.SemaphoreType.DMA],
        ),
        out_shape=jax.ShapeDtypeStruct((rows, d), F32),
        compiler_params=pltpu.CompilerParams(
            dimension_semantics=("arbitrary",), vmem_limit_bytes=VMEM_LIMIT),
        name="dispatch",
    )(dest, pad_end, xn)


def _expert_kernel(ie_ref, irow_ref, inr_ref, tail_ref, xg_hbm, wg_ref, wl_ref, wd_ref, bg_ref, bl_ref, bd_ref, yg_hbm,
                   xst_ref, xb_ref, act_ref, acc_ref, wgb_ref, wlb_ref, wdb_ref, sem_in, sem_out,
                   *, tm, sb, chunks):
    it = pl.program_id(0)
    s = pl.program_id(1)
    n_it = pl.num_programs(0)
    nrows = inr_ref[it]
    n_up = act_ref.shape[0]
    n_down, _, fc = acc_ref.shape
    last = n_up + n_down - 1
    prev_it = jnp.maximum(it - 1, 0)
    next_it = jnp.minimum(it + 1, n_it - 1)

    def piece(r):
        return pl.ds(r * sb, sb)

    def hbm_rows(item, r):
        return pl.ds(pl.multiple_of(irow_ref[item] + r * sb, sb), sb)

    def in_copy(item, r):
        return pltpu.make_async_copy(xg_hbm.at[hbm_rows(item, r)], xst_ref.at[piece(r)], sem_in.at[r])

    def out_copies(item, r):
        return [pltpu.make_async_copy(acc_ref.at[n, piece(r)], yg_hbm.at[hbm_rows(item, r), n * fc:(n + 1) * fc],
                                      sem_out.at[r]) for n in range(n_down)]

    def for_pieces_of(item, fn):
        for r in range(tm // sb):
            pl.when(r * sb < inr_ref[item])(functools.partial(fn, item, r))

    def for_valid_chunks(fn):
        for n, (c0, cn) in enumerate(chunks):
            pl.when(c0 < nrows)(functools.partial(fn, pl.ds(c0, cn), n == 0))

    def start_in(item, r):
        in_copy(item, r).start()

    def land_in(item, r):
        in_copy(item, r).wait()
        xb_ref[piece(r)] = xst_ref[piece(r)].astype(BF16)

    def start_out(item, r):
        for cp in out_copies(item, r):
            cp.start()

    def wait_out(item, r):
        for cp in out_copies(item, r):
            cp.wait()

    @pl.when((it == 0) & (s == 0))
    def _():
        xb_ref[...] = jnp.zeros_like(xb_ref)
        for_pieces_of(it, start_in)

    @pl.when(s == 0)
    def _():
        for_pieces_of(it, land_in)

    @pl.when((s == 1) & (it + 1 < n_it))
    def _():
        for_pieces_of(next_it, start_in)

    @pl.when((s == n_up) & (it > 0))
    def _():
        for_pieces_of(prev_it, wait_out)

    def up(rows, cast_weights):
        if cast_weights:
            wgb_ref[...] = wg_ref[0].astype(BF16)
            wlb_ref[...] = wl_ref[0].astype(BF16)
        xs = xb_ref[rows]
        g = jnp.minimum(_dot(xs, wgb_ref[...]) + bg_ref[0], SWIGLU_LIMIT)
        lin = jnp.clip(_dot(xs, wlb_ref[...]) + bl_ref[0], -SWIGLU_LIMIT, SWIGLU_LIMIT)
        act_ref[s, rows] = ((lin + 1.0) * (g * jax.nn.sigmoid(SWIGLU_ALPHA * g))).astype(BF16)

    def down(rows, cast_weights):
        if cast_weights:
            wdb_ref[...] = wd_ref[0].astype(BF16)
        act = jnp.concatenate([act_ref[jj, rows] for jj in range(n_up)], axis=-1)
        acc_ref[s - n_up, rows] = _dot(act, wdb_ref[...]) + bd_ref[0]

    @pl.when(s < n_up)
    def _():
        for_valid_chunks(up)

    @pl.when(s >= n_up)
    def _():
        for_valid_chunks(down)

    @pl.when(s == last)
    def _():
        for_pieces_of(it, start_out)

    @pl.when((it == n_it - 1) & (s == last))
    def _():
        for_pieces_of(it, wait_out)
        acc_ref[0, piece(0)] = jnp.zeros((sb, fc), F32)
        _zero_tail(acc_ref.at[0, piece(0)], yg_hbm, tail_ref[0], sem_out.at[0])


def _zero_tail(zeros_vmem, dst_hbm, first_row, sem):
    sb, width = zeros_vmem.shape
    n_blocks = (dst_hbm.shape[0] - first_row) // sb

    def fill(b, carry):
        rows = pl.ds(pl.multiple_of(first_row + b * sb, sb), sb)
        copies = [pltpu.make_async_copy(zeros_vmem, dst_hbm.at[rows, c * width:(c + 1) * width], sem)
                  for c in range(dst_hbm.shape[1] // width)]
        for cp in copies:
            cp.start()
        for cp in copies:
            cp.wait()
        return carry

    lax.fori_loop(0, n_blocks, fill, 0)


def _experts(n_used, item_e, item_row, item_nrows, tail, xg, w_gate_up, b_gate_up, w_down, b_down, *, tm, sb, fc):
    rows, d = xg.shape
    dff = w_down.shape[1]
    n_up = dff // fc
    n_down = d // fc
    nsub = tm // sb
    kern = functools.partial(_expert_kernel, tm=tm, sb=sb, chunks=_row_chunks(tm, sb))

    def up_block(offset):
        def index_map(it, s, ie, ir, inr, tl):
            ahead = s >= n_up
            nxt = jnp.minimum(it + 1, tl[1] - 1)
            return jnp.where(ahead, ie[nxt], ie[it]), 0, offset + jnp.where(ahead, 0, s)
        return index_map

    def down_block(it, s, ie, ir, inr, tl):
        return ie[it], 0, jnp.maximum(s - n_up, 0)

    return pl.pallas_call(
        kern,
        grid_spec=pltpu.PrefetchScalarGridSpec(
            num_scalar_prefetch=4,
            grid=(n_used, n_up + n_down),
            in_specs=[
                pl.BlockSpec(memory_space=pl.ANY),
                pl.BlockSpec((1, d, fc), up_block(0)),
                pl.BlockSpec((1, d, fc), up_block(n_up)),
                pl.BlockSpec((1, dff, fc), down_block),
                pl.BlockSpec((1, 1, fc), up_block(0)),
                pl.BlockSpec((1, 1, fc), up_block(n_up)),
                pl.BlockSpec((1, 1, fc), down_block),
            ],
            out_specs=pl.BlockSpec(memory_space=pl.ANY),
            scratch_shapes=[
                pltpu.VMEM((tm, d), F32),
                pltpu.VMEM((tm, d), BF16),
                pltpu.VMEM((n_up, tm, fc), BF16),
                pltpu.VMEM((n_down, tm, fc), F32),
                pltpu.VMEM((d, fc), BF16),
                pltpu.VMEM((d, fc), BF16),
                pltpu.VMEM((dff, fc), BF16),
                pltpu.SemaphoreType.DMA((nsub,)),
                pltpu.SemaphoreType.DMA((nsub,)),
            ],
        ),
        out_shape=jax.ShapeDtypeStruct((rows, d), F32),
        compiler_params=pltpu.CompilerParams(
            dimension_semantics=("arbitrary", "arbitrary"), vmem_limit_bytes=VMEM_LIMIT),
        name="experts",
    )(item_e, item_row, item_nrows, tail, xg, w_gate_up, w_gate_up, w_down,
      b_gate_up, b_gate_up, b_down)


def _combine_kernel(dest_ref, yg_hbm, h1_ref, rf_ref, gfin_ref, o_ref, gbuf_ref, sem, *, tmc):
    i = pl.program_id(0)
    n_tiles = pl.num_programs(0)
    n_tok = n_tiles * tmc

    def issue_tile(tile):
        buf = tile % 2
        base = tile * tmc

        def issue(r, carry):
            for k in range(TOP_K):
                src = yg_hbm.at[pl.ds(dest_ref[k * n_tok + base + r], 1)]
                pltpu.make_async_copy(src, gbuf_ref.at[buf, k, pl.ds(r, 1)], sem.at[buf]).start()
            return carry

        lax.fori_loop(0, tmc, issue, 0)

    @pl.when(i == 0)
    def _():
        issue_tile(i)

    @pl.when(i + 1 < n_tiles)
    def _():
        issue_tile(i + 1)

    buf = i % 2
    for k in range(TOP_K):
        pltpu.make_async_copy(yg_hbm.at[pl.ds(0, tmc)], gbuf_ref.at[buf, k], sem.at[buf]).wait()

    gates = rf_ref[...]
    y = h1_ref[...]
    for k in range(TOP_K):
        y = y + gates[:, k:k + 1] * gbuf_ref[buf, k]
    o_ref[...] = _rms(y, gfin_ref[...])


def _combine(dest, yg, h1, rf, g_final, *, tmc):
    t, d = h1.shape
    kern = functools.partial(_combine_kernel, tmc=tmc)
    return pl.pallas_call(
        kern,
        grid_spec=pltpu.PrefetchScalarGridSpec(
            num_scalar_prefetch=1,
            grid=(t // tmc,),
            in_specs=[
                pl.BlockSpec(memory_space=pl.ANY),
                pl.BlockSpec((tmc, d), lambda i, dest: (i, 0)),
                pl.BlockSpec((tmc, LANES), lambda i, dest: (i, 0)),
                pl.BlockSpec((1, d), lambda i, dest: (0, 0)),
            ],
            out_specs=pl.BlockSpec((tmc, d), lambda i, dest: (i, 0)),
            scratch_shapes=[pltpu.VMEM((2, TOP_K, tmc, d), F32), pltpu.SemaphoreType.DMA((2,))],
        ),
        out_shape=jax.ShapeDtypeStruct((t, d), F32),
        compiler_params=pltpu.CompilerParams(
            dimension_semantics=("arbitrary",), vmem_limit_bytes=VMEM_LIMIT),
        name="combine",
    )(dest, yg, h1, rf, g_final)


def _rotate_half_cols(w):
    half = w.shape[-1] // 2
    return jnp.concatenate([-w[..., half:], w[..., :half]], axis=-1)


def _prep_weights(g_mix, w_in, conv_w, g_q, w_q_up, g_kv, w_kv_up, g_conv_out, w_out, g_ffn, w_router, b_router):
    d = w_in.shape[0]
    k_rope = w_in[:, O_KPE:O_KPE + QK_ROPE]
    pad = jnp.zeros((d, LANES - QK_ROPE), F32)
    wk2 = jnp.concatenate([k_rope, pad, _rotate_half_cols(k_rope), pad], axis=1).astype(BF16)

    wq = w_q_up.reshape(Q_LORA, N_HEADS, QK_NOPE + QK_ROPE)
    nope, pe = wq[:, :, :QK_NOPE], wq[:, :, QK_NOPE:]
    hpad = jnp.zeros((Q_LORA, N_HEADS, LANES - QK_ROPE), F32)
    wqa = jnp.concatenate([nope, pe, hpad], axis=2).reshape(Q_LORA, N_HEADS * HEAD_W)
    wqb = jnp.concatenate([_rotate_half_cols(pe), hpad], axis=2).reshape(Q_LORA, N_HEADS * LANES)
    wq_all = jnp.concatenate([wqa, wqb], axis=1).astype(BF16)

    wkv = w_kv_up.reshape(KV_LORA, N_HEADS, QK_NOPE + V_DIM)
    wkv2 = jnp.concatenate([wkv[:, :, :QK_NOPE].reshape(KV_LORA, ATTN_W),
                            wkv[:, :, QK_NOPE:].reshape(KV_LORA, ATTN_W)], axis=1).astype(BF16)

    wr_pad = jnp.zeros((d, LANES), F32).at[:, :N_EXPERTS].set(w_router)
    wr_hi = wr_pad.astype(BF16)
    wr_lo = (wr_pad - wr_hi.astype(F32)).astype(BF16)
    br = jnp.full((1, LANES), -1e30, F32).at[0, :N_EXPERTS].set(b_router)

    grp = jnp.arange(D_CONV) // (D_CONV // CONV_GROUPS)
    cw = jnp.zeros((SUBLANES, D_CONV), F32).at[:CONV_WIDTH].set(conv_w)
    return {
        "g_mix": g_mix[None], "w1": w_in.astype(BF16), "wk2": wk2, "conv_w": cw, "g_q": g_q[None], "wq": wq_all, "g_kv": g_kv[None],
        "wkv": wkv2, "g_conv_out": g_conv_out[None], "gmat": (grp[:, None] == grp[None, :]).astype(BF16),
        "wo": w_out.astype(BF16), "g_ffn": g_ffn[None], "wr": jnp.concatenate([wr_hi, wr_lo], axis=1), "br": br,
    }


def _rope_table(pos):
    half = QK_ROPE // 2
    inv_freq = np.float32(ROPE_THETA) ** (-np.arange(half, dtype=np.float32) / np.float32(half))
    ang = (np.asarray(pos, np.float32)[:, None] * inv_freq[None, :]).astype(np.float32)
    c, s = np.cos(ang), np.sin(ang)
    z = np.zeros((ang.shape[0], LANES - QK_ROPE), np.float32)
    return jnp.asarray(np.concatenate([c, c, z, s, s, z], axis=1), dtype=F32)


def _schedule(counts, eidx, rank, *, tm, sb, n_items):
    padded = (counts + sb - 1) // sb * sb
    pad_end = jnp.cumsum(padded)
    pad_start = pad_end - padded
    start_of = jnp.zeros_like(eidx)
    for e in range(N_EXPERTS):
        start_of = jnp.where(eidx == e, pad_start[e], start_of)
    dest = (start_of + rank).reshape(-1).astype(jnp.int32)

    per_e = (counts + tm - 1) // tm
    it_end = jnp.cumsum(per_e)
    it_start = it_end - per_e
    ii = jnp.arange(n_items)
    e_of = jnp.minimum(jnp.sum(it_end[None, :] <= ii[:, None], axis=1), N_EXPERTS - 1)
    valid = ii < it_end[-1]
    li = ii - it_start[e_of]
    item_row = jnp.where(valid, pad_start[e_of] + li * tm, 0)
    item_nrows = jnp.where(valid, jnp.clip(counts[e_of] - li * tm, 0, tm), 0)
    e_last = e_of[jnp.maximum(it_end[-1] - 1, 0)]
    item_e = jnp.where(valid, e_of, e_last)
    i32 = lambda a: a.astype(jnp.int32)
    return dest, i32(pad_end), i32(item_e), i32(item_row), i32(item_nrows), i32(it_end[-1])


def _row_chunks(tm, sb, big=512):
    n_big = (tm - 2 * sb) // big
    chunks = [(c * big, big) for c in range(n_big)]
    chunks += [(r, sb) for r in range(n_big * big, tm, sb)]
    return tuple(chunks)


def _moe_tiles(t):
    sb = 128
    tm = 10 * sb
    a = t * TOP_K
    rows = (a + N_EXPERTS * (sb - 1) + sb - 1) // sb * sb
    n_items = N_EXPERTS + a // tm
    return sb, tm, rows, n_items


def _layer(x, meta_tokens, w, w_gate_up, b_gate_up, w_down, b_down, g_attn_out, g_final, *,
           tm_in, tq, tm_out, tmd, tmc, fc):
    b, seq, d = x.shape
    t = b * seq

    meta_blk = jnp.zeros((1, META_ROWS, d), F32).at[0, META_ROWS - N_META:].set(meta_tokens)
    meta_pos = np.maximum(np.arange(META_ROWS) - (META_ROWS - N_META), 0)
    zero_tail = jnp.zeros((SUBLANES, D_CONV), F32)
    _, _, mkn, mkpe, mv, u_tail = _mix_in(meta_blk, zero_tail, _rope_table(meta_pos), w, tm=META_ROWS)

    real_pos = np.arange(seq) + N_META
    convn, q, kn, kpe, v, _ = _mix_in(x, u_tail, _rope_table(real_pos), w, tm=tm_in)
    attn = _attention(q, kn, kpe, v, mkn, mkpe, mv, g_attn_out[None], tq=tq)

    h1, xn, ri, rf, cnt = _mix_out(convn.reshape(t, D_CONV), attn.reshape(t, ATTN_W), x.reshape(t, d), w, tm=tm_out)

    sb, tm_e, rows, n_items = _moe_tiles(t)
    counts = cnt[0, :N_EXPERTS].astype(jnp.int32)
    dest, pad_end, item_e, item_row, item_nrows, n_used = _schedule(
        counts, ri[:TOP_K], ri[TOP_K:], tm=tm_e, sb=sb, n_items=n_items)

    xg = _dispatch(dest, pad_end, xn, rows=rows, tmd=tmd, sb=sb)
    dff = w_down.shape[1]
    tail = jnp.stack([pad_end[N_EXPERTS - 1], n_used])
    yg = _experts(n_used, item_e, item_row, item_nrows, tail, xg, w_gate_up, b_gate_up.reshape(N_EXPERTS, 1, 2 * dff),
                  w_down, b_down.reshape(N_EXPERTS, 1, d), tm=tm_e, sb=sb, fc=fc)
    out = _combine(dest, yg, h1, rf, g_final[None], tmc=tmc)
    return out.reshape(b, seq, d)


def kernel(x, meta_tokens, g_mix, w_in, conv_w, g_q, w_q_up, g_kv, w_kv_up, g_conv_out, g_attn_out, w_out, g_ffn,
           w_router, b_router, w_gate_up, b_gate_up, w_down, b_down, g_final):
    w = _prep_weights(g_mix[0], w_in[0], conv_w[0], g_q[0], w_q_up[0], g_kv[0], w_kv_up[0], g_conv_out[0],
                      w_out[0], g_ffn[0], w_router[0], b_router[0])
    seq = x.shape[1]
    return _layer(x, meta_tokens, w, w_gate_up[0], b_gate_up[0], w_down[0], b_down[0], g_attn_out[0], g_final,
                  tm_in=min(256, seq), tq=min(512, seq), tm_out=min(256, seq), tmd=min(512, seq),
                  tmc=min(128, seq), fc=256)
```

```python
import functools

import jax
import jax.numpy as jnp
import numpy as np
from jax import lax
from jax.experimental import pallas as pl
from jax.experimental.pallas import tpu as pltpu

N_META = 16
EPS = 1e-6
D_CONV = 1024
CONV_GROUPS = 16
CONV_WIDTH = 3
N_HEADS = 8
QK_NOPE = 128
QK_ROPE = 64
V_DIM = 128
Q_LORA = 512
KV_LORA = 256
ROPE_THETA = 10000.0
N_EXPERTS = 32
TOP_K = 4
SWIGLU_LIMIT = 7.0
SWIGLU_ALPHA = 1.702

LANES = 128
SUBLANES = 8
META_ROWS = 128
HEAD_W = 2 * LANES
ATTN_W = N_HEADS * V_DIM
V_EXT = V_DIM + 16
VMEM_LIMIT = 56 * 1024 * 1024

O_B, O_C, O_U = 0, D_CONV, 2 * D_CONV
O_Q = 3 * D_CONV
O_KV = O_Q + Q_LORA
O_KPE = O_KV + KV_LORA
D_IN = O_KPE + QK_ROPE

F32 = jnp.float32
BF16 = jnp.bfloat16


def _rms(x, g):
    return x * lax.rsqrt(jnp.mean(x * x, axis=-1, keepdims=True) + EPS) * g


def _dot(a, b):
    return jnp.dot(a, b, preferred_element_type=F32)


def _resident(shape):
    zeros = (0,) * len(shape)
    return pl.BlockSpec(shape, lambda *_: zeros, pipeline_mode=pl.Buffered(1))


def _mix_in_kernel(x_ref, gmix_ref, w1_ref, wk2_ref, cw_ref, gq_ref, wq_ref, gkv_ref, wkv_ref, gco_ref, gmat_ref,
                   tab_ref, uinit_ref,
                   convn_ref, qt_ref, kn_ref, kpe_ref, vt_ref, utail_ref,
                   ubuf_ref, *, tm, scale):
    i = pl.program_id(1)
    hn = _rms(x_ref[0], gmix_ref[...]).astype(BF16)

    def proj(lo, hi):
        return _dot(hn, w1_ref[:, lo:hi])

    @pl.when(i == 0)
    def _():
        ubuf_ref[0:SUBLANES] = uinit_ref[...]

    @pl.when(i > 0)
    def _():
        ubuf_ref[0:SUBLANES] = ubuf_ref[tm:tm + SUBLANES]

    u = proj(O_C, O_U) * proj(O_U, O_Q)
    ubuf_ref[SUBLANES:SUBLANES + tm] = u
    cw = cw_ref[...]
    y = (cw[2:3] * u + cw[1:2] * ubuf_ref[SUBLANES - 1:SUBLANES - 1 + tm]
         + cw[0:1] * ubuf_ref[SUBLANES - 2:SUBLANES - 2 + tm])
    co = proj(O_B, O_C) * y
    ss = _dot((co * co).astype(BF16), gmat_ref[...])
    group = D_CONV // CONV_GROUPS
    convn_ref[0] = (co * lax.rsqrt(ss * (1.0 / group) + EPS) * gco_ref[...]).astype(BF16)
    utail_ref[...] = ubuf_ref[tm:tm + SUBLANES]

    cos = tab_ref[:, :LANES]
    sin = tab_ref[:, LANES:]

    qn = _rms(proj(O_Q, O_KV), gq_ref[...]).astype(BF16)
    qa = _dot(qn, wq_ref[:, :N_HEADS * HEAD_W])
    qb = _dot(qn, wq_ref[:, N_HEADS * HEAD_W:])
    for h in range(N_HEADS):
        c0 = h * HEAD_W
        qt_ref[0, c0:c0 + LANES, :] = (qa[:, c0:c0 + LANES] * scale).T.astype(BF16)
        pe = qa[:, c0 + LANES:c0 + HEAD_W] * cos + qb[:, h * LANES:(h + 1) * LANES] * sin
        qt_ref[0, c0 + LANES:c0 + HEAD_W, :] = (pe * scale).T.astype(BF16)

    kvn = _rms(proj(O_KV, O_KPE), gkv_ref[...]).astype(BF16)
    kv = _dot(kvn, wkv_ref[...])
    kn_ref[0] = kv[:, :ATTN_W].astype(BF16)
    vt = kv[:, ATTN_W:].T.astype(BF16)
    ones = jnp.ones((V_EXT - V_DIM, tm), BF16)
    for h in range(N_HEADS):
        vt_ref[0, 0, h * V_EXT:h * V_EXT + V_DIM, :] = vt[h * V_DIM:(h + 1) * V_DIM]
        vt_ref[0, 0, h * V_EXT + V_DIM:(h + 1) * V_EXT, :] = ones
    kk = _dot(hn, wk2_ref[...])
    kpe_ref[0] = (kk[:, :LANES] * cos + kk[:, LANES:] * sin).astype(BF16)


def _mix_in(x3, uinit, tab, w, *, tm):
    b, l, d = x3.shape
    nt = l // tm
    kern = functools.partial(_mix_in_kernel, tm=tm, scale=float((QK_NOPE + QK_ROPE) ** -0.5))
    row = lambda width: pl.BlockSpec((1, tm, width), lambda bi, i: (bi, i, 0))
    return pl.pallas_call(
        kern,
        grid=(b, nt),
        in_specs=[
            row(d),
            _resident((1, d)),
            _resident((d, D_IN)),
            _resident((d, 2 * LANES)),
            _resident((SUBLANES, D_CONV)),
            _resident((1, Q_LORA)),
            _resident((Q_LORA, N_HEADS * (HEAD_W + LANES))),
            _resident((1, KV_LORA)),
            _resident((KV_LORA, 2 * ATTN_W)),
            _resident((1, D_CONV)),
            _resident((D_CONV, D_CONV)),
            pl.BlockSpec((tm, 2 * LANES), lambda bi, i: (i, 0)),
            _resident((SUBLANES, D_CONV)),
        ],
        out_specs=[
            row(D_CONV),
            pl.BlockSpec((1, N_HEADS * HEAD_W, tm), lambda bi, i: (bi, 0, i)),
            row(ATTN_W), row(LANES),
            pl.BlockSpec((1, 1, N_HEADS * V_EXT, tm), lambda bi, i: (bi, i, 0, 0)),
            pl.BlockSpec((SUBLANES, D_CONV), lambda bi, i: (bi * nt + i, 0)),
        ],
        out_shape=[
            jax.ShapeDtypeStruct((b, l, D_CONV), BF16),
            jax.ShapeDtypeStruct((b, N_HEADS * HEAD_W, l), BF16),
            jax.ShapeDtypeStruct((b, l, ATTN_W), BF16),
            jax.ShapeDtypeStruct((b, l, LANES), BF16),
            jax.ShapeDtypeStruct((b, nt, N_HEADS * V_EXT, tm), BF16),
            jax.ShapeDtypeStruct((b * nt * SUBLANES, D_CONV), F32),
        ],
        scratch_shapes=[pltpu.VMEM((tm + SUBLANES, D_CONV), F32)],
        compiler_params=pltpu.CompilerParams(
            dimension_semantics=("arbitrary", "arbitrary"), vmem_limit_bytes=VMEM_LIMIT),
        name="mix_in",
    )(x3, w["g_mix"], w["w1"], w["wk2"], w["conv_w"], w["g_q"], w["wq"], w["g_kv"], w["wkv"], w["g_conv_out"],
      w["gmat"], tab, uinit)


def _attn_kernel(qt_ref, kn_ref, kpe_ref, vt_ref, mkn_ref, mkpe_ref, mvt_ref, g_ref, o_ref,
                 m_ref, acc_ref, *, tq, hps):
    qi = pl.program_id(2)
    tv = vt_ref.shape[3]

    def lanes(hh):
        return slice(hh * LANES, (hh + 1) * LANES)

    def vrows(hh):
        return slice(hh * V_EXT, (hh + 1) * V_EXT)

    def qt(hh):
        return qt_ref[0, hh * HEAD_W:(hh + 1) * HEAD_W, :]

    for hh in range(hps):
        km = jnp.concatenate([mkn_ref[0, :, lanes(hh)], mkpe_ref[0]], axis=-1)
        s = _dot(km, qt(hh))
        row = lax.broadcasted_iota(jnp.int32, s.shape, 0)
        s = jnp.where(row >= META_ROWS - N_META, s, -jnp.inf)
        m0 = jnp.max(s, axis=0, keepdims=True)
        m_ref[hh] = m0
        acc_ref[hh] = _dot(mvt_ref[0, 0, vrows(hh), :], jnp.exp((s - m0).astype(BF16)))

    def step(kb, diagonal):
        off = pl.multiple_of(kb * tq, tq)
        kpe = kpe_ref[0, pl.ds(off, tq), :]
        scores = []
        for hh in range(hps):
            k = jnp.concatenate([kn_ref[0, pl.ds(off, tq), lanes(hh)], kpe], axis=-1)
            s = _dot(k, qt(hh))
            if diagonal:
                r = lax.broadcasted_iota(jnp.int32, s.shape, 0)
                c = lax.broadcasted_iota(jnp.int32, s.shape, 1)
                s = jnp.where(r <= c, s, -jnp.inf)
            scores.append(s)
        probs, alphas = [], []
        for hh in range(hps):
            m_prev = m_ref[hh]
            m_new = jnp.maximum(m_prev, jnp.max(scores[hh], axis=0, keepdims=True))
            alphas.append(jnp.exp(m_prev - m_new))
            probs.append(jnp.exp((scores[hh] - m_new).astype(BF16)))
            m_ref[hh] = m_new
        for hh in range(hps):
            pb = probs[hh]
            pv = _dot(vt_ref[0, kb * (tq // tv), vrows(hh), :], pb[:tv])
            for c in range(1, tq // tv):
                pv = pv + _dot(vt_ref[0, kb * (tq // tv) + c, vrows(hh), :], pb[c * tv:(c + 1) * tv])
            acc_ref[hh] = alphas[hh] * acc_ref[hh] + pv

    def body(kb, carry):
        step(kb, False)
        return carry

    lax.fori_loop(0, qi, body, 0)
    step(qi, True)

    for hh in range(hps):
        acc = acc_ref[hh]
        o = (acc[:V_DIM] / acc[V_DIM:V_DIM + 1]).T
        o_ref[0, :, lanes(hh)] = _rms(o, g_ref[:, lanes(hh)]).astype(BF16)


def _attention(qt, kn, kpe, vt, mkn, mkpe, mvt, g_attn, *, tq, hps=4):
    b, l, _ = kn.shape
    _, nt, _, tv = vt.shape
    nq = l // tq
    kern = functools.partial(_attn_kernel, tq=tq, hps=hps)
    return pl.pallas_call(
        kern,
        grid=(b, N_HEADS // hps, nq),
        in_specs=[
            pl.BlockSpec((1, hps * HEAD_W, tq), lambda bi, h, i: (bi, h, i)),
            pl.BlockSpec((1, l, hps * LANES), lambda bi, h, i: (bi, 0, h)),
            pl.BlockSpec((1, l, LANES), lambda bi, h, i: (bi, 0, 0)),
            pl.BlockSpec((1, nt, hps * V_EXT, tv), lambda bi, h, i: (bi, 0, h, 0)),
            pl.BlockSpec((1, META_ROWS, hps * LANES), lambda bi, h, i: (0, 0, h)),
            pl.BlockSpec((1, META_ROWS, LANES), lambda bi, h, i: (0, 0, 0)),
            pl.BlockSpec((1, 1, hps * V_EXT, META_ROWS), lambda bi, h, i: (0, 0, h, 0)),
            pl.BlockSpec((1, hps * V_DIM), lambda bi, h, i: (0, h)),
        ],
        out_specs=pl.BlockSpec((1, tq, hps * V_DIM), lambda bi, h, i: (bi, i, h)),
        out_shape=jax.ShapeDtypeStruct((b, l, ATTN_W), BF16),
        scratch_shapes=[pltpu.VMEM((hps, 1, tq), F32), pltpu.VMEM((hps, V_EXT, tq), F32)],
        compiler_params=pltpu.CompilerParams(
            dimension_semantics=("arbitrary", "arbitrary", "arbitrary"), vmem_limit_bytes=VMEM_LIMIT),
        name="attn",
    )(qt, kn, kpe, vt, mkn, mkpe, mvt, g_attn)


def _mix_out_kernel(convn_ref, attn_ref, x_ref, wo_ref, gffn_ref, wr_ref, br_ref,
                    h1_ref, xn_ref, ri_ref, rf_ref, cnt_ref, carry_ref, *, tm):
    @pl.when(pl.program_id(0) == 0)
    def _():
        carry_ref[...] = jnp.zeros_like(carry_ref)

    h1 = x_ref[...] + _dot(convn_ref[...], wo_ref[:D_CONV]) + _dot(attn_ref[...], wo_ref[D_CONV:])
    h1_ref[...] = h1
    xn = _rms(h1, gffn_ref[...])
    xn_ref[...] = xn

    xh = xn.astype(BF16)
    xl = (xn - xh.astype(F32)).astype(BF16)
    ph = _dot(xh, wr_ref[...])
    logits = ph[:, :LANES] + ph[:, LANES:] + _dot(xl, wr_ref[:, :LANES]) + br_ref[...]

    lane = lax.broadcasted_iota(jnp.int32, (tm, LANES), 1).astype(F32)
    work = logits
    top_v, top_i, onehots = [], [], []
    for _ in range(TOP_K):
        mk = jnp.max(work, axis=-1, keepdims=True)
        ik = jnp.min(jnp.where(work == mk, lane, float(LANES)), axis=-1, keepdims=True)
        oh = lane == ik
        work = jnp.where(oh, -jnp.inf, work)
        top_v.append(mk)
        top_i.append(ik)
        onehots.append(oh)

    ex = [jnp.exp(v - top_v[0]) for v in top_v]
    denom = ex[0] + ex[1] + ex[2] + ex[3]
    gates = [e / denom for e in ex]

    sel = jnp.zeros((tm, LANES), F32)
    for oh in onehots:
        sel = sel + oh.astype(F32)
    r = lax.broadcasted_iota(jnp.int32, (tm, tm), 0)
    c = lax.broadcasted_iota(jnp.int32, (tm, tm), 1)
    tri = jnp.where(c < r, 1.0, 0.0).astype(BF16)
    cum = _dot(tri, sel.astype(BF16)) + carry_ref[...]
    carry_ref[...] = carry_ref[...] + jnp.sum(sel, axis=0, keepdims=True)
    cnt_ref[...] = carry_ref[...]

    ri = jnp.zeros((tm, LANES), F32)
    rf = jnp.zeros((tm, LANES), F32)
    for k in range(TOP_K):
        rank_k = jnp.sum(jnp.where(onehots[k], cum, 0.0), axis=-1, keepdims=True)
        ri = jnp.where(lane == float(k), top_i[k], ri)
        ri = jnp.where(lane == float(TOP_K + k), rank_k, ri)
        rf = jnp.where(lane == float(k), gates[k], rf)
    ri_ref[...] = ri.T[:2 * TOP_K].astype(jnp.int32)
    rf_ref[...] = rf


def _mix_out(convn, attn, x2, w, *, tm):
    t, d = x2.shape
    kern = functools.partial(_mix_out_kernel, tm=tm)
    row = lambda width: pl.BlockSpec((tm, width), lambda i: (i, 0))
    return pl.pallas_call(
        kern,
        grid=(t // tm,),
        in_specs=[
            row(D_CONV), row(ATTN_W), row(d),
            _resident((D_CONV + ATTN_W, d)),
            _resident((1, d)),
            _resident((d, 2 * LANES)),
            _resident((1, LANES)),
        ],
        out_specs=[row(d), row(d), pl.BlockSpec((2 * TOP_K, tm), lambda i: (0, i)), row(LANES),
                   pl.BlockSpec((1, LANES), lambda i: (0, 0))],
        out_shape=[
            jax.ShapeDtypeStruct((t, d), F32),
            jax.ShapeDtypeStruct((t, d), F32),
            jax.ShapeDtypeStruct((2 * TOP_K, t), jnp.int32),
            jax.ShapeDtypeStruct((t, LANES), F32),
            jax.ShapeDtypeStruct((1, LANES), F32),
        ],
        scratch_shapes=[pltpu.VMEM((1, LANES), F32)],
        compiler_params=pltpu.CompilerParams(
            dimension_semantics=("arbitrary",), vmem_limit_bytes=VMEM_LIMIT),
        name="mix_out",
    )(convn, attn, x2, w["wo"], w["g_ffn"], w["wr"], w["br"])


def _dispatch_kernel(dest_ref, pend_ref, xn_ref, xg_hbm, zbuf_ref, sem, zsem, *, tmd, sb):
    i = pl.program_id(0)

    @pl.when(i == 0)
    def _():
        zbuf_ref[...] = jnp.zeros_like(zbuf_ref)

        def zero_copy(e):
            end = pend_ref[e]
            start = pl.multiple_of(jnp.maximum(end - sb, 0), sb)
            return pltpu.make_async_copy(zbuf_ref, xg_hbm.at[pl.ds(start, sb)], zsem)

        def nonempty(e):
            prev = pend_ref[jnp.maximum(e - 1, 0)]
            return pend_ref[e] > jnp.where(e > 0, prev, 0)

        def start(e, carry):
            @pl.when(nonempty(e))
            def _():
                zero_copy(e).start()
            return carry

        def wait(e, carry):
            @pl.when(nonempty(e))
            def _():
                zero_copy(e).wait()
            return carry

        lax.fori_loop(0, N_EXPERTS, start, 0)
        lax.fori_loop(0, N_EXPERTS, wait, 0)
        _zero_tail(zbuf_ref, xg_hbm, pend_ref[N_EXPERTS - 1], zsem)

    base = i * tmd
    n_tok = pl.num_programs(0) * tmd

    def issue(r, carry):
        for k in range(TOP_K):
            dst = xg_hbm.at[pl.ds(dest_ref[k * n_tok + base + r], 1)]
            pltpu.make_async_copy(xn_ref.at[pl.ds(r, 1)], dst, sem).start()
        return carry

    lax.fori_loop(0, tmd, issue, 0)
    for k in range(TOP_K):
        pltpu.make_async_copy(xn_ref, xg_hbm.at[pl.ds(0, tmd)], sem).wait()


def _dispatch(dest, pad_end, xn, *, rows, tmd, sb):
    t, d = xn.shape
    kern = functools.partial(_dispatch_kernel, tmd=tmd, sb=sb)
    return pl.pallas_call(
        kern,
        grid_spec=pltpu.PrefetchScalarGridSpec(
            num_scalar_prefetch=2,
            grid=(t // tmd,),
            in_specs=[pl.BlockSpec((tmd, d), lambda i, dest, pend: (i, 0))],
            out_specs=pl.BlockSpec(memory_space=pl.ANY),
            scratch_shapes=[pltpu.VMEM((sb, d), F32), pltpu.SemaphoreType.DMA, pltpu.SemaphoreType.DMA],
        ),
        out_shape=jax.ShapeDtypeStruct((rows, d), F32),
        compiler_params=pltpu.CompilerParams(
            dimension_semantics=("arbitrary",), vmem_limit_bytes=VMEM_LIMIT),
        name="dispatch",
    )(dest, pad_end, xn)


def _expert_kernel(ie_ref, irow_ref, inr_ref, tail_ref, xg_hbm, wg_ref, wl_ref, wd_ref, bg_ref, bl_ref, bd_ref, yg_hbm,
                   xst_ref, xb_ref, act_ref, acc_ref, wgb_ref, wlb_ref, wdb_ref, sem_in, sem_out,
                   *, tm, sb, row_sizes):
    it = pl.program_id(0)
    s = pl.program_id(1)
    n_it = pl.num_programs(0)
    nrows = inr_ref[it]
    n_up = act_ref.shape[0]
    n_down, _, fc = acc_ref.shape
    last = n_up + n_down - 1
    prev_it = jnp.maximum(it - 1, 0)
    next_it = jnp.minimum(it + 1, n_it - 1)

    def piece(r):
        return pl.ds(r * sb, sb)

    def hbm_rows(item, r):
        return pl.ds(pl.multiple_of(irow_ref[item] + r * sb, sb), sb)

    def in_copy(item, r):
        return pltpu.make_async_copy(xg_hbm.at[hbm_rows(item, r)], xst_ref.at[piece(r)], sem_in.at[r])

    def out_copies(item, r):
        return [pltpu.make_async_copy(acc_ref.at[n, piece(r)], yg_hbm.at[hbm_rows(item, r), n * fc:(n + 1) * fc],
                                      sem_out.at[r]) for n in range(n_down)]

    def for_pieces_of(item, fn):
        for r in range(tm // sb):
            pl.when(r * sb < inr_ref[item])(functools.partial(fn, item, r))

    def for_valid_chunks(fn):
        for lo, size in zip((0,) + row_sizes[:-1], row_sizes):
            pl.when((lo < nrows) & (nrows <= size))(functools.partial(fn, pl.ds(0, size), True))

    def start_in(item, r):
        in_copy(item, r).start()

    def land_in(item, r):
        in_copy(item, r).wait()
        xb_ref[piece(r)] = xst_ref[piece(r)].astype(BF16)

    def start_out(item, r):
        for cp in out_copies(item, r):
            cp.start()

    def wait_out(item, r):
        for cp in out_copies(item, r):
            cp.wait()

    @pl.when((it == 0) & (s == 0))
    def _():
        xb_ref[...] = jnp.zeros_like(xb_ref)
        for_pieces_of(it, start_in)

    @pl.when(s == 0)
    def _():
        for_pieces_of(it, land_in)

    @pl.when((s == 1) & (it + 1 < n_it))
    def _():
        for_pieces_of(next_it, start_in)

    @pl.when((s == n_up) & (it > 0))
    def _():
        for_pieces_of(prev_it, wait_out)

    def up(rows, cast_weights):
        if cast_weights:
            wgb_ref[...] = wg_ref[0].astype(BF16)
            wlb_ref[...] = wl_ref[0].astype(BF16)
        xs = xb_ref[rows]
        g = jnp.minimum(_dot(xs, wgb_ref[...]) + bg_ref[0], SWIGLU_LIMIT)
        lin = jnp.clip(_dot(xs, wlb_ref[...]) + bl_ref[0], -SWIGLU_LIMIT, SWIGLU_LIMIT)
        act_ref[s, rows] = ((lin + 1.0) * (g * jax.nn.sigmoid(SWIGLU_ALPHA * g))).astype(BF16)

    def down(rows, cast_weights):
        if cast_weights:
            wdb_ref[...] = wd_ref[0].astype(BF16)
        act = jnp.concatenate([act_ref[jj, rows] for jj in range(n_up)], axis=-1)
        acc_ref[s - n_up, rows] = _dot(act, wdb_ref[...]) + bd_ref[0]

    @pl.when(s < n_up)
    def _():
        for_valid_chunks(up)

    @pl.when(s >= n_up)
    def _():
        for_valid_chunks(down)

    @pl.when(s == last)
    def _():
        for_pieces_of(it, start_out)

    @pl.when((it == n_it - 1) & (s == last))
    def _():
        for_pieces_of(it, wait_out)
        acc_ref[0, piece(0)] = jnp.zeros((sb, fc), F32)
        _zero_tail(acc_ref.at[0, piece(0)], yg_hbm, tail_ref[0], sem_out.at[0])


def _zero_tail(zeros_vmem, dst_hbm, first_row, sem):
    sb, width = zeros_vmem.shape
    n_blocks = (dst_hbm.shape[0] - first_row) // sb

    def fill(b, carry):
        rows = pl.ds(pl.multiple_of(first_row + b * sb, sb), sb)
        copies = [pltpu.make_async_copy(zeros_vmem, dst_hbm.at[rows, c * width:(c + 1) * width], sem)
                  for c in range(dst_hbm.shape[1] // width)]
        for cp in copies:
            cp.start()
        for cp in copies:
            cp.wait()
        return carry

    lax.fori_loop(0, n_blocks, fill, 0)


def _experts(n_used, item_e, item_row, item_nrows, tail, xg, w_gate_up, b_gate_up, w_down, b_down, *, tm, sb, fc):
    rows, d = xg.shape
    dff = w_down.shape[1]
    n_up = dff // fc
    n_down = d // fc
    nsub = tm // sb
    kern = functools.partial(_expert_kernel, tm=tm, sb=sb, row_sizes=_row_sizes(tm, sb))

    def up_block(offset):
        def index_map(it, s, ie, ir, inr, tl):
            ahead = s >= n_up
            nxt = jnp.minimum(it + 1, tl[1] - 1)
            return jnp.where(ahead, ie[nxt], ie[it]), 0, offset + jnp.where(ahead, 0, s)
        return index_map

    def down_block(it, s, ie, ir, inr, tl):
        return ie[it], 0, jnp.maximum(s - n_up, 0)

    return pl.pallas_call(
        kern,
        grid_spec=pltpu.PrefetchScalarGridSpec(
            num_scalar_prefetch=4,
            grid=(n_used, n_up + n_down),
            in_specs=[
                pl.BlockSpec(memory_space=pl.ANY),
                pl.BlockSpec((1, d, fc), up_block(0)),
                pl.BlockSpec((1, d, fc), up_block(n_up)),
                pl.BlockSpec((1, dff, fc), down_block),
                pl.BlockSpec((1, 1, fc), up_block(0)),
                pl.BlockSpec((1, 1, fc), up_block(n_up)),
                pl.BlockSpec((1, 1, fc), down_block),
            ],
            out_specs=pl.BlockSpec(memory_space=pl.ANY),
            scratch_shapes=[
                pltpu.VMEM((tm, d), F32),
                pltpu.VMEM((tm, d), BF16),
                pltpu.VMEM((n_up, tm, fc), BF16),
                pltpu.VMEM((n_down, tm, fc), F32),
                pltpu.VMEM((d, fc), BF16),
                pltpu.VMEM((d, fc), BF16),
                pltpu.VMEM((dff, fc), BF16),
                pltpu.SemaphoreType.DMA((nsub,)),
                pltpu.SemaphoreType.DMA((nsub,)),
            ],
        ),
        out_shape=jax.ShapeDtypeStruct((rows, d), F32),
        compiler_params=pltpu.CompilerParams(
            dimension_semantics=("arbitrary", "arbitrary"), vmem_limit_bytes=VMEM_LIMIT),
        name="experts",
    )(item_e, item_row, item_nrows, tail, xg, w_gate_up, w_gate_up, w_down,
      b_gate_up, b_gate_up, b_down)


def _combine_kernel(dest_ref, yg_hbm, h1_ref, rf_ref, gfin_ref, o_ref, gbuf_ref, sem, *, tmc):
    i = pl.program_id(0)
    n_tiles = pl.num_programs(0)
    n_tok = n_tiles * tmc

    def issue_tile(tile):
        buf = tile % 2
        base = tile * tmc

        def issue(r, carry):
            for k in range(TOP_K):
                src = yg_hbm.at[pl.ds(dest_ref[k * n_tok + base + r], 1)]
                pltpu.make_async_copy(src, gbuf_ref.at[buf, k, pl.ds(r, 1)], sem.at[buf]).start()
            return carry

        lax.fori_loop(0, tmc, issue, 0)

    @pl.when(i == 0)
    def _():
        issue_tile(i)

    @pl.when(i + 1 < n_tiles)
    def _():
        issue_tile(i + 1)

    buf = i % 2
    for k in range(TOP_K):
        pltpu.make_async_copy(yg_hbm.at[pl.ds(0, tmc)], gbuf_ref.at[buf, k], sem.at[buf]).wait()

    gates = rf_ref[...]
    y = h1_ref[...]
    for k in range(TOP_K):
        y = y + gates[:, k:k + 1] * gbuf_ref[buf, k]
    o_ref[...] = _rms(y, gfin_ref[...])


def _combine(dest, yg, h1, rf, g_final, *, tmc):
    t, d = h1.shape
    kern = functools.partial(_combine_kernel, tmc=tmc)
    return pl.pallas_call(
        kern,
        grid_spec=pltpu.PrefetchScalarGridSpec(
            num_scalar_prefetch=1,
            grid=(t // tmc,),
            in_specs=[
                pl.BlockSpec(memory_space=pl.ANY),
                pl.BlockSpec((tmc, d), lambda i, dest: (i, 0)),
                pl.BlockSpec((tmc, LANES), lambda i, dest: (i, 0)),
                pl.BlockSpec((1, d), lambda i, dest: (0, 0)),
            ],
            out_specs=pl.BlockSpec((tmc, d), lambda i, dest: (i, 0)),
            scratch_shapes=[pltpu.VMEM((2, TOP_K, tmc, d), F32), pltpu.SemaphoreType.DMA((2,))],
        ),
        out_shape=jax.ShapeDtypeStruct((t, d), F32),
        compiler_params=pltpu.CompilerParams(
            dimension_semantics=("arbitrary",), vmem_limit_bytes=VMEM_LIMIT),
        name="combine",
    )(dest, yg, h1, rf, g_final)


def _rotate_half_cols(w):
    half = w.shape[-1] // 2
    return jnp.concatenate([-w[..., half:], w[..., :half]], axis=-1)


def _prep_weights(g_mix, w_in, conv_w, g_q, w_q_up, g_kv, w_kv_up, g_conv_out, w_out, g_ffn, w_router, b_router):
    d = w_in.shape[0]
    k_rope = w_in[:, O_KPE:O_KPE + QK_ROPE]
    pad = jnp.zeros((d, LANES - QK_ROPE), F32)
    wk2 = jnp.concatenate([k_rope, pad, _rotate_half_cols(k_rope), pad], axis=1).astype(BF16)

    wq = w_q_up.reshape(Q_LORA, N_HEADS, QK_NOPE + QK_ROPE)
    nope, pe = wq[:, :, :QK_NOPE], wq[:, :, QK_NOPE:]
    hpad = jnp.zeros((Q_LORA, N_HEADS, LANES - QK_ROPE), F32)
    wqa = jnp.concatenate([nope, pe, hpad], axis=2).reshape(Q_LORA, N_HEADS * HEAD_W)
    wqb = jnp.concatenate([_rotate_half_cols(pe), hpad], axis=2).reshape(Q_LORA, N_HEADS * LANES)
    wq_all = jnp.concatenate([wqa, wqb], axis=1).astype(BF16)

    wkv = w_kv_up.reshape(KV_LORA, N_HEADS, QK_NOPE + V_DIM)
    wkv2 = jnp.concatenate([wkv[:, :, :QK_NOPE].reshape(KV_LORA, ATTN_W),
                            wkv[:, :, QK_NOPE:].reshape(KV_LORA, ATTN_W)], axis=1).astype(BF16)

    wr_pad = jnp.zeros((d, LANES), F32).at[:, :N_EXPERTS].set(w_router)
    wr_hi = wr_pad.astype(BF16)
    wr_lo = (wr_pad - wr_hi.astype(F32)).astype(BF16)
    br = jnp.full((1, LANES), -1e30, F32).at[0, :N_EXPERTS].set(b_router)

    grp = jnp.arange(D_CONV) // (D_CONV // CONV_GROUPS)
    cw = jnp.zeros((SUBLANES, D_CONV), F32).at[:CONV_WIDTH].set(conv_w)
    return {
        "g_mix": g_mix[None], "w1": w_in.astype(BF16), "wk2": wk2, "conv_w": cw, "g_q": g_q[None], "wq": wq_all, "g_kv": g_kv[None],
        "wkv": wkv2, "g_conv_out": g_conv_out[None], "gmat": (grp[:, None] == grp[None, :]).astype(BF16),
        "wo": w_out.astype(BF16), "g_ffn": g_ffn[None], "wr": jnp.concatenate([wr_hi, wr_lo], axis=1), "br": br,
    }


def _rope_table(pos):
    half = QK_ROPE // 2
    inv_freq = np.float32(ROPE_THETA) ** (-np.arange(half, dtype=np.float32) / np.float32(half))
    ang = (np.asarray(pos, np.float32)[:, None] * inv_freq[None, :]).astype(np.float32)
    c, s = np.cos(ang), np.sin(ang)
    z = np.zeros((ang.shape[0], LANES - QK_ROPE), np.float32)
    return jnp.asarray(np.concatenate([c, c, z, s, s, z], axis=1), dtype=F32)


def _schedule(counts, eidx, rank, *, tm, sb, n_items):
    padded = (counts + sb - 1) // sb * sb
    pad_end = jnp.cumsum(padded)
    pad_start = pad_end - padded
    start_of = jnp.zeros_like(eidx)
    for e in range(N_EXPERTS):
        start_of = jnp.where(eidx == e, pad_start[e], start_of)
    dest = (start_of + rank).reshape(-1).astype(jnp.int32)

    per_e = (counts + tm - 1) // tm
    it_end = jnp.cumsum(per_e)
    it_start = it_end - per_e
    ii = jnp.arange(n_items)
    e_of = jnp.minimum(jnp.sum(it_end[None, :] <= ii[:, None], axis=1), N_EXPERTS - 1)
    valid = ii < it_end[-1]
    li = ii - it_start[e_of]
    item_row = jnp.where(valid, pad_start[e_of] + li * tm, 0)
    item_nrows = jnp.where(valid, jnp.clip(counts[e_of] - li * tm, 0, tm), 0)
    e_last = e_of[jnp.maximum(it_end[-1] - 1, 0)]
    item_e = jnp.where(valid, e_of, e_last)
    i32 = lambda a: a.astype(jnp.int32)
    return dest, i32(pad_end), i32(item_e), i32(item_row), i32(item_nrows), i32(it_end[-1])


def _row_sizes(tm, sb):
    fine = [tm - k * sb for k in (2, 1, 0) if tm - k * sb > 0]
    coarse = [s for s in (tm // 4 // sb * sb, tm // 2 // sb * sb) if 0 < s < fine[0]]
    return tuple(sorted(set(coarse + fine)))


def _moe_tiles(t):
    sb = 128
    tm = 10 * sb
    a = t * TOP_K
    rows = (a + N_EXPERTS * (sb - 1) + sb - 1) // sb * sb
    n_items = N_EXPERTS + a // tm
    return sb, tm, rows, n_items


def _layer(x, meta_tokens, w, w_gate_up, b_gate_up, w_down, b_down, g_attn_out, g_final, *,
           tm_in, tq, tm_out, tmd, tmc, fc):
    b, seq, d = x.shape
    t = b * seq

    meta_blk = jnp.zeros((1, META_ROWS, d), F32).at[0, META_ROWS - N_META:].set(meta_tokens)
    meta_pos = np.maximum(np.arange(META_ROWS) - (META_ROWS - N_META), 0)
    zero_tail = jnp.zeros((SUBLANES, D_CONV), F32)
    _, _, mkn, mkpe, mv, u_tail = _mix_in(meta_blk, zero_tail, _rope_table(meta_pos), w, tm=META_ROWS)

    real_pos = np.arange(seq) + N_META
    convn, q, kn, kpe, v, _ = _mix_in(x, u_tail, _rope_table(real_pos), w, tm=tm_in)
    attn = _attention(q, kn, kpe, v, mkn, mkpe, mv, g_attn_out[None], tq=tq)

    h1, xn, ri, rf, cnt = _mix_out(convn.reshape(t, D_CONV), attn.reshape(t, ATTN_W), x.reshape(t, d), w, tm=tm_out)

    sb, tm_e, rows, n_items = _moe_tiles(t)
    counts = cnt[0, :N_EXPERTS].astype(jnp.int32)
    dest, pad_end, item_e, item_row, item_nrows, n_used = _schedule(
        counts, ri[:TOP_K], ri[TOP_K:], tm=tm_e, sb=sb, n_items=n_items)

    xg = _dispatch(dest, pad_end, xn, rows=rows, tmd=tmd, sb=sb)
    dff = w_down.shape[1]
    tail = jnp.stack([pad_end[N_EXPERTS - 1], n_used])
    yg = _experts(n_used, item_e, item_row, item_nrows, tail, xg, w_gate_up, b_gate_up.reshape(N_EXPERTS, 1, 2 * dff),
                  w_down, b_down.reshape(N_EXPERTS, 1, d), tm=tm_e, sb=sb, fc=fc)
    out = _combine(dest, yg, h1, rf, g_final[None], tmc=tmc)
    return out.reshape(b, seq, d)


def kernel(x, meta_tokens, g_mix, w_in, conv_w, g_q, w_q_up, g_kv, w_kv_up, g_conv_out, g_attn_out, w_out, g_ffn,
           w_router, b_router, w_gate_up, b_gate_up, w_down, b_down, g_final):
    w = _prep_weights(g_mix[0], w_in[0], conv_w[0], g_q[0], w_q_up[0], g_kv[0], w_kv_up[0], g_conv_out[0],
                      w_out[0], g_ffn[0], w_router[0], b_router[0])
    seq = x.shape[1]
    return _layer(x, meta_tokens, w, w_gate_up[0], b_gate_up[0], w_down[0], b_down[0], g_attn_out[0], g_final,
                  tm_in=min(512, seq), tq=min(512, seq), tm_out=min(512, seq), tmd=min(512, seq),
                  tmc=min(128, seq), fc=256)
```

```python
import functools

import jax
import jax.numpy as jnp
import numpy as np
from jax import lax
from jax.experimental import pallas as pl
from jax.experimental.pallas import tpu as pltpu

N_META = 16
EPS = 1e-6
D_CONV = 1024
CONV_GROUPS = 16
CONV_WIDTH = 3
N_HEADS = 8
QK_NOPE = 128
QK_ROPE = 64
V_DIM = 128
Q_LORA = 512
KV_LORA = 256
ROPE_THETA = 10000.0
N_EXPERTS = 32
TOP_K = 4
SWIGLU_LIMIT = 7.0
SWIGLU_ALPHA = 1.702

LANES = 128
SUBLANES = 8
META_ROWS = 128
HEAD_W = 2 * LANES
ATTN_W = N_HEADS * V_DIM
V_EXT = V_DIM + 16
VMEM_LIMIT = 56 * 1024 * 1024

O_B, O_C, O_U = 0, D_CONV, 2 * D_CONV
O_Q = 3 * D_CONV
O_KV = O_Q + Q_LORA
O_KPE = O_KV + KV_LORA
D_IN = O_KPE + QK_ROPE

F32 = jnp.float32
BF16 = jnp.bfloat16


def _rms(x, g):
    return x * lax.rsqrt(jnp.mean(x * x, axis=-1, keepdims=True) + EPS) * g


def _dot(a, b):
    return jnp.dot(a, b, preferred_element_type=F32)


def _resident(shape):
    zeros = (0,) * len(shape)
    return pl.BlockSpec(shape, lambda *_: zeros, pipeline_mode=pl.Buffered(1))


def _mix_in_kernel(x_ref, gmix_ref, w1_ref, wk2_ref, cw_ref, gq_ref, wq_ref, gkv_ref, wkv_ref, gco_ref, gmat_ref,
                   tab_ref, uinit_ref,
                   convn_ref, qt_ref, kn_ref, kpe_ref, vt_ref, utail_ref,
                   ubuf_ref, *, tm, scale):
    i = pl.program_id(1)
    hn = _rms(x_ref[0], gmix_ref[...]).astype(BF16)

    def proj(lo, hi):
        return _dot(hn, w1_ref[:, lo:hi])

    @pl.when(i == 0)
    def _():
        ubuf_ref[0:SUBLANES] = uinit_ref[...]

    @pl.when(i > 0)
    def _():
        ubuf_ref[0:SUBLANES] = ubuf_ref[tm:tm + SUBLANES]

    u = proj(O_C, O_U) * proj(O_U, O_Q)
    ubuf_ref[SUBLANES:SUBLANES + tm] = u
    cw = cw_ref[...]
    y = (cw[2:3] * u + cw[1:2] * ubuf_ref[SUBLANES - 1:SUBLANES - 1 + tm]
         + cw[0:1] * ubuf_ref[SUBLANES - 2:SUBLANES - 2 + tm])
    co = proj(O_B, O_C) * y
    ss = _dot((co * co).astype(BF16), gmat_ref[...])
    group = D_CONV // CONV_GROUPS
    convn_ref[0] = (co * lax.rsqrt(ss * (1.0 / group) + EPS) * gco_ref[...]).astype(BF16)
    utail_ref[...] = ubuf_ref[tm:tm + SUBLANES]

    cos = tab_ref[:, :LANES]
    sin = tab_ref[:, LANES:]

    qn = _rms(proj(O_Q, O_KV), gq_ref[...]).astype(BF16)
    qa = _dot(qn, wq_ref[:, :N_HEADS * HEAD_W])
    qb = _dot(qn, wq_ref[:, N_HEADS * HEAD_W:])
    for h in range(N_HEADS):
        c0 = h * HEAD_W
        qt_ref[0, c0:c0 + LANES, :] = (qa[:, c0:c0 + LANES] * scale).T.astype(BF16)
        pe = qa[:, c0 + LANES:c0 + HEAD_W] * cos + qb[:, h * LANES:(h + 1) * LANES] * sin
        qt_ref[0, c0 + LANES:c0 + HEAD_W, :] = (pe * scale).T.astype(BF16)

    kvn = _rms(proj(O_KV, O_KPE), gkv_ref[...]).astype(BF16)
    kv = _dot(kvn, wkv_ref[...])
    kn_ref[0] = kv[:, :ATTN_W].astype(BF16)
    vt = kv[:, ATTN_W:].T.astype(BF16)
    ones = jnp.ones((V_EXT - V_DIM, tm), BF16)
    for h in range(N_HEADS):
        vt_ref[0, 0, h * V_EXT:h * V_EXT + V_DIM, :] = vt[h * V_DIM:(h + 1) * V_DIM]
        vt_ref[0, 0, h * V_EXT + V_DIM:(h + 1) * V_EXT, :] = ones
    kk = _dot(hn, wk2_ref[...])
    kpe_ref[0] = (kk[:, :LANES] * cos + kk[:, LANES:] * sin).astype(BF16)


def _mix_in(x3, uinit, tab, w, *, tm):
    b, l, d = x3.shape
    nt = l // tm
    kern = functools.partial(_mix_in_kernel, tm=tm, scale=float((QK_NOPE + QK_ROPE) ** -0.5))
    row = lambda width: pl.BlockSpec((1, tm, width), lambda bi, i: (bi, i, 0))
    return pl.pallas_call(
        kern,
        grid=(b, nt),
        in_specs=[
            row(d),
            _resident((1, d)),
            _resident((d, D_IN)),
            _resident((d, 2 * LANES)),
            _resident((SUBLANES, D_CONV)),
            _resident((1, Q_LORA)),
            _resident((Q_LORA, N_HEADS * (HEAD_W + LANES))),
            _resident((1, KV_LORA)),
            _resident((KV_LORA, 2 * ATTN_W)),
            _resident((1, D_CONV)),
            _resident((D_CONV, D_CONV)),
            pl.BlockSpec((tm, 2 * LANES), lambda bi, i: (i, 0)),
            _resident((SUBLANES, D_CONV)),
        ],
        out_specs=[
            row(D_CONV),
            pl.BlockSpec((1, N_HEADS * HEAD_W, tm), lambda bi, i: (bi, 0, i)),
            row(ATTN_W), row(LANES),
            pl.BlockSpec((1, 1, N_HEADS * V_EXT, tm), lambda bi, i: (bi, i, 0, 0)),
            pl.BlockSpec((SUBLANES, D_CONV), lambda bi, i: (bi * nt + i, 0)),
        ],
        out_shape=[
            jax.ShapeDtypeStruct((b, l, D_CONV), BF16),
            jax.ShapeDtypeStruct((b, N_HEADS * HEAD_W, l), BF16),
            jax.ShapeDtypeStruct((b, l, ATTN_W), BF16),
            jax.ShapeDtypeStruct((b, l, LANES), BF16),
            jax.ShapeDtypeStruct((b, nt, N_HEADS * V_EXT, tm), BF16),
            jax.ShapeDtypeStruct((b * nt * SUBLANES, D_CONV), F32),
        ],
        scratch_shapes=[pltpu.VMEM((tm + SUBLANES, D_CONV), F32)],
        compiler_params=pltpu.CompilerParams(
            dimension_semantics=("arbitrary", "arbitrary"), vmem_limit_bytes=VMEM_LIMIT),
        name="mix_in",
    )(x3, w["g_mix"], w["w1"], w["wk2"], w["conv_w"], w["g_q"], w["wq"], w["g_kv"], w["wkv"], w["g_conv_out"],
      w["gmat"], tab, uinit)


def _attn_kernel(qt_ref, kn_ref, kpe_ref, vt_ref, mkn_ref, mkpe_ref, mvt_ref, g_ref, o_ref,
                 m_ref, acc_ref, *, tq, hps):
    qi = pl.program_id(2)
    tv = vt_ref.shape[3]

    def lanes(hh):
        return slice(hh * LANES, (hh + 1) * LANES)

    def vrows(hh):
        return slice(hh * V_EXT, (hh + 1) * V_EXT)

    def qt(hh):
        return qt_ref[0, hh * HEAD_W:(hh + 1) * HEAD_W, :]

    for hh in range(hps):
        km = jnp.concatenate([mkn_ref[0, :, lanes(hh)], mkpe_ref[0]], axis=-1)
        s = _dot(km, qt(hh))
        row = lax.broadcasted_iota(jnp.int32, s.shape, 0)
        s = jnp.where(row >= META_ROWS - N_META, s, -jnp.inf)
        m0 = jnp.max(s, axis=0, keepdims=True)
        m_ref[hh] = m0
        acc_ref[hh] = _dot(mvt_ref[0, 0, vrows(hh), :], jnp.exp((s - m0).astype(BF16)))

    def step(kb, diagonal):
        off = pl.multiple_of(kb * tq, tq)
        kpe = kpe_ref[0, pl.ds(off, tq), :]
        scores = []
        for hh in range(hps):
            k = jnp.concatenate([kn_ref[0, pl.ds(off, tq), lanes(hh)], kpe], axis=-1)
            s = _dot(k, qt(hh))
            if diagonal:
                r = lax.broadcasted_iota(jnp.int32, s.shape, 0)
                c = lax.broadcasted_iota(jnp.int32, s.shape, 1)
                s = jnp.where(r <= c, s, -jnp.inf)
            scores.append(s)
        probs, alphas = [], []
        for hh in range(hps):
            m_prev = m_ref[hh]
            m_new = jnp.maximum(m_prev, jnp.max(scores[hh], axis=0, keepdims=True))
            alphas.append(jnp.exp(m_prev - m_new))
            probs.append(jnp.exp((scores[hh] - m_new).astype(BF16)))
            m_ref[hh] = m_new
        for hh in range(hps):
            pb = probs[hh]
            pv = _dot(vt_ref[0, kb * (tq // tv), vrows(hh), :], pb[:tv])
            for c in range(1, tq // tv):
                pv = pv + _dot(vt_ref[0, kb * (tq // tv) + c, vrows(hh), :], pb[c * tv:(c + 1) * tv])
            acc_ref[hh] = alphas[hh] * acc_ref[hh] + pv

    def body(kb, carry):
        step(kb, False)
        return carry

    lax.fori_loop(0, qi, body, 0)
    step(qi, True)

    for hh in range(hps):
        acc = acc_ref[hh]
        o = (acc[:V_DIM] / acc[V_DIM:V_DIM + 1]).T
        o_ref[0, :, lanes(hh)] = _rms(o, g_ref[:, lanes(hh)]).astype(BF16)


def _attention(qt, kn, kpe, vt, mkn, mkpe, mvt, g_attn, *, tq, hps=4):
    b, l, _ = kn.shape
    _, nt, _, tv = vt.shape
    nq = l // tq
    kern = functools.partial(_attn_kernel, tq=tq, hps=hps)
    return pl.pallas_call(
        kern,
        grid=(b, N_HEADS // hps, nq),
        in_specs=[
            pl.BlockSpec((1, hps * HEAD_W, tq), lambda bi, h, i: (bi, h, i)),
            pl.BlockSpec((1, l, hps * LANES), lambda bi, h, i: (bi, 0, h)),
            pl.BlockSpec((1, l, LANES), lambda bi, h, i: (bi, 0, 0)),
            pl.BlockSpec((1, nt, hps * V_EXT, tv), lambda bi, h, i: (bi, 0, h, 0)),
            pl.BlockSpec((1, META_ROWS, hps * LANES), lambda bi, h, i: (0, 0, h)),
            pl.BlockSpec((1, META_ROWS, LANES), lambda bi, h, i: (0, 0, 0)),
            pl.BlockSpec((1, 1, hps * V_EXT, META_ROWS), lambda bi, h, i: (0, 0, h, 0)),
            pl.BlockSpec((1, hps * V_DIM), lambda bi, h, i: (0, h)),
        ],
        out_specs=pl.BlockSpec((1, tq, hps * V_DIM), lambda bi, h, i: (bi, i, h)),
        out_shape=jax.ShapeDtypeStruct((b, l, ATTN_W), BF16),
        scratch_shapes=[pltpu.VMEM((hps, 1, tq), F32), pltpu.VMEM((hps, V_EXT, tq), F32)],
        compiler_params=pltpu.CompilerParams(
            dimension_semantics=("arbitrary", "arbitrary", "arbitrary"), vmem_limit_bytes=VMEM_LIMIT),
        name="attn",
    )(qt, kn, kpe, vt, mkn, mkpe, mvt, g_attn)


def _mix_out_kernel(convn_ref, attn_ref, x_ref, wo_ref, gffn_ref, wr_ref, br_ref,
                    h1_ref, xn_ref, ri_ref, rf_ref, cnt_ref, carry_ref, *, tm):
    @pl.when(pl.program_id(0) == 0)
    def _():
        carry_ref[...] = jnp.zeros_like(carry_ref)

    h1 = x_ref[...] + _dot(convn_ref[...], wo_ref[:D_CONV]) + _dot(attn_ref[...], wo_ref[D_CONV:])
    h1_ref[...] = h1
    xn = _rms(h1, gffn_ref[...])
    xn_ref[...] = xn

    xh = xn.astype(BF16)
    xl = (xn - xh.astype(F32)).astype(BF16)
    ph = _dot(xh, wr_ref[...])
    logits = ph[:, :LANES] + ph[:, LANES:] + _dot(xl, wr_ref[:, :LANES]) + br_ref[...]

    lane = lax.broadcasted_iota(jnp.int32, (tm, LANES), 1).astype(F32)
    work = logits
    top_v, top_i, onehots = [], [], []
    for _ in range(TOP_K):
        mk = jnp.max(work, axis=-1, keepdims=True)
        ik = jnp.min(jnp.where(work == mk, lane, float(LANES)), axis=-1, keepdims=True)
        oh = lane == ik
        work = jnp.where(oh, -jnp.inf, work)
        top_v.append(mk)
        top_i.append(ik)
        onehots.append(oh)

    ex = [jnp.exp(v - top_v[0]) for v in top_v]
    denom = ex[0] + ex[1] + ex[2] + ex[3]
    gates = [e / denom for e in ex]

    sel = jnp.zeros((tm, LANES), F32)
    for oh in onehots:
        sel = sel + oh.astype(F32)
    r = lax.broadcasted_iota(jnp.int32, (tm, tm), 0)
    c = lax.broadcasted_iota(jnp.int32, (tm, tm), 1)
    tri = jnp.where(c < r, 1.0, 0.0).astype(BF16)
    cum = _dot(tri, sel.astype(BF16)) + carry_ref[...]
    carry_ref[...] = carry_ref[...] + jnp.sum(sel, axis=0, keepdims=True)
    cnt_ref[...] = carry_ref[...]

    ri = jnp.zeros((tm, LANES), F32)
    rf = jnp.zeros((tm, LANES), F32)
    for k in range(TOP_K):
        rank_k = jnp.sum(jnp.where(onehots[k], cum, 0.0), axis=-1, keepdims=True)
        ri = jnp.where(lane == float(k), top_i[k], ri)
        ri = jnp.where(lane == float(TOP_K + k), rank_k, ri)
        rf = jnp.where(lane == float(k), gates[k], rf)
    ri_ref[...] = ri.T[:2 * TOP_K].astype(jnp.int32)
    rf_ref[...] = rf


def _mix_out(convn, attn, x2, w, *, tm):
    t, d = x2.shape
    kern = functools.partial(_mix_out_kernel, tm=tm)
    row = lambda width: pl.BlockSpec((tm, width), lambda i: (i, 0))
    return pl.pallas_call(
        kern,
        grid=(t // tm,),
        in_specs=[
            row(D_CONV), row(ATTN_W), row(d),
            _resident((D_CONV + ATTN_W, d)),
            _resident((1, d)),
            _resident((d, 2 * LANES)),
            _resident((1, LANES)),
        ],
        out_specs=[row(d), row(d), pl.BlockSpec((2 * TOP_K, tm), lambda i: (0, i)), row(LANES),
                   pl.BlockSpec((1, LANES), lambda i: (0, 0))],
        out_shape=[
            jax.ShapeDtypeStruct((t, d), F32),
            jax.ShapeDtypeStruct((t, d), F32),
            jax.ShapeDtypeStruct((2 * TOP_K, t), jnp.int32),
            jax.ShapeDtypeStruct((t, LANES), F32),
            jax.ShapeDtypeStruct((1, LANES), F32),
        ],
        scratch_shapes=[pltpu.VMEM((1, LANES), F32)],
        compiler_params=pltpu.CompilerParams(
            dimension_semantics=("arbitrary",), vmem_limit_bytes=VMEM_LIMIT),
        name="mix_out",
    )(convn, attn, x2, w["wo"], w["g_ffn"], w["wr"], w["br"])


def _dispatch_kernel(dest_ref, pend_ref, xn_ref, xg_hbm, zbuf_ref, sem, zsem, *, tmd, sb):
    i = pl.program_id(0)

    @pl.when(i == 0)
    def _():
        zbuf_ref[...] = jnp.zeros_like(zbuf_ref)

        def zero_copy(e):
            end = pend_ref[e]
            start = pl.multiple_of(jnp.maximum(end - sb, 0), sb)
            return pltpu.make_async_copy(zbuf_ref, xg_hbm.at[pl.ds(start, sb)], zsem)

        def nonempty(e):
            prev = pend_ref[jnp.maximum(e - 1, 0)]
            return pend_ref[e] > jnp.where(e > 0, prev, 0)

        def start(e, carry):
            @pl.when(nonempty(e))
            def _():
                zero_copy(e).start()
            return carry

        def wait(e, carry):
            @pl.when(nonempty(e))
            def _():
                zero_copy(e).wait()
            return carry

        lax.fori_loop(0, N_EXPERTS, start, 0)
        lax.fori_loop(0, N_EXPERTS, wait, 0)
        _zero_tail(zbuf_ref, xg_hbm, pend_ref[N_EXPERTS - 1], zsem)

    base = i * tmd
    n_tok = pl.num_programs(0) * tmd

    def issue(r, carry):
        for k in range(TOP_K):
            dst = xg_hbm.at[pl.ds(dest_ref[k * n_tok + base + r], 1)]
            pltpu.make_async_copy(xn_ref.at[pl.ds(r, 1)], dst, sem).start()
        return carry

    lax.fori_loop(0, tmd, issue, 0)
    for k in range(TOP_K):
        pltpu.make_async_copy(xn_ref, xg_hbm.at[pl.ds(0, tmd)], sem).wait()


def _dispatch(dest, pad_end, xn, *, rows, tmd, sb):
    t, d = xn.shape
    kern = functools.partial(_dispatch_kernel, tmd=tmd, sb=sb)
    return pl.pallas_call(
        kern,
        grid_spec=pltpu.PrefetchScalarGridSpec(
            num_scalar_prefetch=2,
            grid=(t // tmd,),
            in_specs=[pl.BlockSpec((tmd, d), lambda i, dest, pend: (i, 0))],
            out_specs=pl.BlockSpec(memory_space=pl.ANY),
            scratch_shapes=[pltpu.VMEM((sb, d), F32), pltpu.SemaphoreType.DMA, pltpu.SemaphoreType.DMA],
        ),
        out_shape=jax.ShapeDtypeStruct((rows, d), F32),
        compiler_params=pltpu.CompilerParams(
            dimension_semantics=("arbitrary",), vmem_limit_bytes=VMEM_LIMIT),
        name="dispatch",
    )(dest, pad_end, xn)


def _expert_kernel(ie_ref, irow_ref, inr_ref, tail_ref, xg_hbm, wg_ref, wl_ref, wd_ref, bg_ref, bl_ref, bd_ref, yg_hbm,
                   xst_ref, xb_ref, act_ref, acc_ref, wgb_ref, wlb_ref, wdb_ref, sem_in, sem_out,
                   *, tm, sb, row_sizes):
    it = pl.program_id(0)
    s = pl.program_id(1)
    n_it = pl.num_programs(0)
    nrows = inr_ref[it]
    n_up = act_ref.shape[0]
    n_down, _, fc = acc_ref.shape
    last = n_up + n_down - 1
    prev_it = jnp.maximum(it - 1, 0)
    next_it = jnp.minimum(it + 1, n_it - 1)

    def piece(r):
        return pl.ds(r * sb, sb)

    def hbm_rows(item, r):
        return pl.ds(pl.multiple_of(irow_ref[item] + r * sb, sb), sb)

    def in_copy(item, r):
        return pltpu.make_async_copy(xg_hbm.at[hbm_rows(item, r)], xst_ref.at[piece(r)], sem_in.at[r])

    def out_copies(item, r):
        return [pltpu.make_async_copy(acc_ref.at[n, piece(r)], yg_hbm.at[hbm_rows(item, r), n * fc:(n + 1) * fc],
                                      sem_out.at[r]) for n in range(n_down)]

    def for_pieces_of(item, fn):
        for r in range(tm // sb):
            pl.when(r * sb < inr_ref[item])(functools.partial(fn, item, r))

    def for_valid_chunks(fn):
        for lo, size in zip((0,) + row_sizes[:-1], row_sizes):
            pl.when((lo < nrows) & (nrows <= size))(functools.partial(fn, pl.ds(0, size), True))

    def start_in(item, r):
        in_copy(item, r).start()

    def land_in(item, r):
        in_copy(item, r).wait()
        xb_ref[piece(r)] = xst_ref[piece(r)].astype(BF16)

    def start_out(item, r):
        for cp in out_copies(item, r):
            cp.start()

    def wait_out(item, r):
        for cp in out_copies(item, r):
            cp.wait()

    @pl.when((it == 0) & (s == 0))
    def _():
        xb_ref[...] = jnp.zeros_like(xb_ref)
        for_pieces_of(it, start_in)

    @pl.when(s == 0)
    def _():
        for_pieces_of(it, land_in)

    @pl.when((s == 1) & (it + 1 < n_it))
    def _():
        for_pieces_of(next_it, start_in)

    @pl.when((s == n_up) & (it > 0))
    def _():
        for_pieces_of(prev_it, wait_out)

    def up(rows, cast_weights):
        if cast_weights:
            wgb_ref[...] = wg_ref[0].astype(BF16)
            wlb_ref[...] = wl_ref[0].astype(BF16)
        xs = xb_ref[rows]
        g = jnp.minimum(_dot(xs, wgb_ref[...]) + bg_ref[0], SWIGLU_LIMIT)
        lin = jnp.clip(_dot(xs, wlb_ref[...]) + bl_ref[0], -SWIGLU_LIMIT, SWIGLU_LIMIT)
        act_ref[s, rows] = ((lin + 1.0) * (g * jax.nn.sigmoid(SWIGLU_ALPHA * g))).astype(BF16)

    def down(rows, cast_weights):
        if cast_weights:
            wdb_ref[...] = wd_ref[0].astype(BF16)
        act = jnp.concatenate([act_ref[jj, rows] for jj in range(n_up)], axis=-1)
        acc_ref[s - n_up, rows] = _dot(act, wdb_ref[...]) + bd_ref[0]

    @pl.when(s < n_up)
    def _():
        for_valid_chunks(up)

    @pl.when(s >= n_up)
    def _():
        for_valid_chunks(down)

    @pl.when(s == last)
    def _():
        for_pieces_of(it, start_out)

    @pl.when((it == n_it - 1) & (s == last))
    def _():
        for_pieces_of(it, wait_out)
        acc_ref[0, piece(0)] = jnp.zeros((sb, fc), F32)
        _zero_tail(acc_ref.at[0, piece(0)], yg_hbm, tail_ref[0], sem_out.at[0])


def _zero_tail(zeros_vmem, dst_hbm, first_row, sem):
    sb, width = zeros_vmem.shape
    n_blocks = (dst_hbm.shape[0] - first_row) // sb

    def fill(b, carry):
        rows = pl.ds(pl.multiple_of(first_row + b * sb, sb), sb)
        copies = [pltpu.make_async_copy(zeros_vmem, dst_hbm.at[rows, c * width:(c + 1) * width], sem)
                  for c in range(dst_hbm.shape[1] // width)]
        for cp in copies:
            cp.start()
        for cp in copies:
            cp.wait()
        return carry

    lax.fori_loop(0, n_blocks, fill, 0)


def _experts(n_used, item_e, item_row, item_nrows, tail, xg, w_gate_up, b_gate_up, w_down, b_down, *, tm, sb, fc):
    rows, d = xg.shape
    dff = w_down.shape[1]
    n_up = dff // fc
    fd = 2 * fc
    n_down = d // fd
    nsub = tm // sb
    kern = functools.partial(_expert_kernel, tm=tm, sb=sb, row_sizes=_row_sizes(tm, sb))

    def up_block(offset):
        def index_map(it, s, ie, ir, inr, tl):
            ahead = s >= n_up
            nxt = jnp.minimum(it + 1, tl[1] - 1)
            return jnp.where(ahead, ie[nxt], ie[it]), 0, offset + jnp.where(ahead, 0, s)
        return index_map

    def down_block(it, s, ie, ir, inr, tl):
        return ie[it], 0, jnp.maximum(s - n_up, 0)

    return pl.pallas_call(
        kern,
        grid_spec=pltpu.PrefetchScalarGridSpec(
            num_scalar_prefetch=4,
            grid=(n_used, n_up + n_down),
            in_specs=[
                pl.BlockSpec(memory_space=pl.ANY),
                pl.BlockSpec((1, d, fc), up_block(0)),
                pl.BlockSpec((1, d, fc), up_block(n_up)),
                pl.BlockSpec((1, dff, fd), down_block),
                pl.BlockSpec((1, 1, fc), up_block(0)),
                pl.BlockSpec((1, 1, fc), up_block(n_up)),
                pl.BlockSpec((1, 1, fd), down_block),
            ],
            out_specs=pl.BlockSpec(memory_space=pl.ANY),
            scratch_shapes=[
                pltpu.VMEM((tm, d), F32),
                pltpu.VMEM((tm, d), BF16),
                pltpu.VMEM((n_up, tm, fc), BF16),
                pltpu.VMEM((n_down, tm, fd), F32),
                pltpu.VMEM((d, fc), BF16),
                pltpu.VMEM((d, fc), BF16),
                pltpu.VMEM((dff, fd), BF16),
                pltpu.SemaphoreType.DMA((nsub,)),
                pltpu.SemaphoreType.DMA((nsub,)),
            ],
        ),
        out_shape=jax.ShapeDtypeStruct((rows, d), F32),
        compiler_params=pltpu.CompilerParams(
            dimension_semantics=("arbitrary", "arbitrary"), vmem_limit_bytes=VMEM_LIMIT),
        name="experts",
    )(item_e, item_row, item_nrows, tail, xg, w_gate_up, w_gate_up, w_down,
      b_gate_up, b_gate_up, b_down)


def _combine_kernel(dest_ref, yg_hbm, h1_ref, rf_ref, gfin_ref, o_ref, gbuf_ref, sem, *, tmc):
    i = pl.program_id(0)
    n_tiles = pl.num_programs(0)
    n_tok = n_tiles * tmc

    def issue_tile(tile):
        buf = tile % 2
        base = tile * tmc

        def issue(r, carry):
            for k in range(TOP_K):
                src = yg_hbm.at[pl.ds(dest_ref[k * n_tok + base + r], 1)]
                pltpu.make_async_copy(src, gbuf_ref.at[buf, k, pl.ds(r, 1)], sem.at[buf]).start()
            return carry

        lax.fori_loop(0, tmc, issue, 0)

    @pl.when(i == 0)
    def _():
        issue_tile(i)

    @pl.when(i + 1 < n_tiles)
    def _():
        issue_tile(i + 1)

    buf = i % 2
    for k in range(TOP_K):
        pltpu.make_async_copy(yg_hbm.at[pl.ds(0, tmc)], gbuf_ref.at[buf, k], sem.at[buf]).wait()

    gates = rf_ref[...]
    y = h1_ref[...]
    for k in range(TOP_K):
        y = y + gates[:, k:k + 1] * gbuf_ref[buf, k]
    o_ref[...] = _rms(y, gfin_ref[...])


def _combine(dest, yg, h1, rf, g_final, *, tmc):
    t, d = h1.shape
    kern = functools.partial(_combine_kernel, tmc=tmc)
    return pl.pallas_call(
        kern,
        grid_spec=pltpu.PrefetchScalarGridSpec(
            num_scalar_prefetch=1,
            grid=(t // tmc,),
            in_specs=[
                pl.BlockSpec(memory_space=pl.ANY),
                pl.BlockSpec((tmc, d), lambda i, dest: (i, 0)),
                pl.BlockSpec((tmc, LANES), lambda i, dest: (i, 0)),
                pl.BlockSpec((1, d), lambda i, dest: (0, 0)),
            ],
            out_specs=pl.BlockSpec((tmc, d), lambda i, dest: (i, 0)),
            scratch_shapes=[pltpu.VMEM((2, TOP_K, tmc, d), F32), pltpu.SemaphoreType.DMA((2,))],
        ),
        out_shape=jax.ShapeDtypeStruct((t, d), F32),
        compiler_params=pltpu.CompilerParams(
            dimension_semantics=("arbitrary",), vmem_limit_bytes=VMEM_LIMIT),
        name="combine",
    )(dest, yg, h1, rf, g_final)


def _cast_kernel(x_ref, o_ref):
    o_ref[...] = x_ref[...].astype(o_ref.dtype)


def _to_bf16(w, *, block_rows=256):
    rows, cols = w.shape
    spec = pl.BlockSpec((block_rows, cols), lambda i: (i, 0))
    return pl.pallas_call(
        _cast_kernel, grid=(rows // block_rows,), in_specs=[spec], out_specs=spec,
        out_shape=jax.ShapeDtypeStruct(w.shape, BF16), name="to_bf16",
    )(w)


def _rotate_half_cols(w):
    half = w.shape[-1] // 2
    return jnp.concatenate([-w[..., half:], w[..., :half]], axis=-1)


def _prep_weights(g_mix, w_in, conv_w, g_q, w_q_up, g_kv, w_kv_up, g_conv_out, w_out, g_ffn, w_router, b_router):
    d = w_in.shape[0]
    k_rope = w_in[:, O_KPE:O_KPE + QK_ROPE]
    pad = jnp.zeros((d, LANES - QK_ROPE), F32)
    wk2 = jnp.concatenate([k_rope, pad, _rotate_half_cols(k_rope), pad], axis=1).astype(BF16)

    wq = w_q_up.reshape(Q_LORA, N_HEADS, QK_NOPE + QK_ROPE)
    nope, pe = wq[:, :, :QK_NOPE], wq[:, :, QK_NOPE:]
    hpad = jnp.zeros((Q_LORA, N_HEADS, LANES - QK_ROPE), F32)
    wqa = jnp.concatenate([nope, pe, hpad], axis=2).reshape(Q_LORA, N_HEADS * HEAD_W)
    wqb = jnp.concatenate([_rotate_half_cols(pe), hpad], axis=2).reshape(Q_LORA, N_HEADS * LANES)
    wq_all = jnp.concatenate([wqa, wqb], axis=1).astype(BF16)

    wkv = w_kv_up.reshape(KV_LORA, N_HEADS, QK_NOPE + V_DIM)
    wkv2 = jnp.concatenate([wkv[:, :, :QK_NOPE].reshape(KV_LORA, ATTN_W),
                            wkv[:, :, QK_NOPE:].reshape(KV_LORA, ATTN_W)], axis=1).astype(BF16)

    wr_pad = jnp.zeros((d, LANES), F32).at[:, :N_EXPERTS].set(w_router)
    wr_hi = wr_pad.astype(BF16)
    wr_lo = (wr_pad - wr_hi.astype(F32)).astype(BF16)
    br = jnp.full((1, LANES), -1e30, F32).at[0, :N_EXPERTS].set(b_router)

    grp = jnp.arange(D_CONV) // (D_CONV // CONV_GROUPS)
    cw = jnp.zeros((SUBLANES, D_CONV), F32).at[:CONV_WIDTH].set(conv_w)
    return {
        "g_mix": g_mix[None], "w1": _to_bf16(w_in), "wk2": wk2, "conv_w": cw, "g_q": g_q[None], "wq": wq_all, "g_kv": g_kv[None],
        "wkv": wkv2, "g_conv_out": g_conv_out[None], "gmat": (grp[:, None] == grp[None, :]).astype(BF16),
        "wo": w_out.astype(BF16), "g_ffn": g_ffn[None], "wr": jnp.concatenate([wr_hi, wr_lo], axis=1), "br": br,
    }


def _rope_table(pos):
    half = QK_ROPE // 2
    inv_freq = np.float32(ROPE_THETA) ** (-np.arange(half, dtype=np.float32) / np.float32(half))
    ang = (np.asarray(pos, np.float32)[:, None] * inv_freq[None, :]).astype(np.float32)
    c, s = np.cos(ang), np.sin(ang)
    z = np.zeros((ang.shape[0], LANES - QK_ROPE), np.float32)
    return jnp.asarray(np.concatenate([c, c, z, s, s, z], axis=1), dtype=F32)


def _schedule_kernel(cnt_ref, ri_ref, dest_ref, pend_ref, ie_ref, irow_ref, inr_ref, tail_ref, pstart_ref,
                     *, tm, sb, n_items):
    def per_expert(e, carry):
        row, item = carry
        count = cnt_ref[e]
        pstart_ref[e] = row

        def per_item(li, item):
            ie_ref[item] = e
            irow_ref[item] = row + li * tm
            inr_ref[item] = jnp.minimum(count - li * tm, tm)
            return item + 1

        item = lax.fori_loop(0, lax.div(count + (tm - 1), tm), per_item, item)
        row = row + lax.div(count + (sb - 1), sb) * sb
        pend_ref[e] = row
        return row, item

    row, n_used = lax.fori_loop(0, N_EXPERTS, per_expert, (jnp.int32(0), jnp.int32(0)))
    tail_ref[0] = row
    tail_ref[1] = n_used

    def unused(item, carry):
        ie_ref[item] = 0
        irow_ref[item] = 0
        inr_ref[item] = 0
        return carry

    lax.fori_loop(n_used, n_items, unused, 0)

    eidx = ri_ref[:TOP_K]
    start_of = jnp.zeros_like(eidx)
    for e in range(N_EXPERTS):
        start_of = jnp.where(eidx == e, pstart_ref[e], start_of)
    dest_ref[...] = start_of + ri_ref[TOP_K:]


def _schedule(counts, ri, *, tm, sb, n_items):
    t = ri.shape[1]
    smem = pl.BlockSpec(memory_space=pltpu.SMEM)
    i32 = lambda n: jax.ShapeDtypeStruct((n,), jnp.int32)
    dest, pad_end, item_e, item_row, item_nrows, tail = pl.pallas_call(
        functools.partial(_schedule_kernel, tm=tm, sb=sb, n_items=n_items),
        in_specs=[smem, pl.BlockSpec(memory_space=pltpu.VMEM)],
        out_specs=[pl.BlockSpec(memory_space=pltpu.VMEM), smem, smem, smem, smem, smem],
        out_shape=[jax.ShapeDtypeStruct((TOP_K, t), jnp.int32), i32(N_EXPERTS), i32(n_items), i32(n_items),
                   i32(n_items), i32(2)],
        scratch_shapes=[pltpu.SMEM((N_EXPERTS,), jnp.int32)],
        name="schedule",
    )(counts, ri)
    return dest.reshape(-1), pad_end, item_e, item_row, item_nrows, tail


def _row_sizes(tm, sb):
    fine = [tm - k * sb for k in (2, 1, 0) if tm - k * sb > 0]
    coarse = [s for s in (tm // 4 // sb * sb, tm // 2 // sb * sb) if 0 < s < fine[0]]
    return tuple(sorted(set(coarse + fine)))


def _moe_tiles(t):
    sb = 128
    tm = 10 * sb
    a = t * TOP_K
    rows = (a + N_EXPERTS * (sb - 1) + sb - 1) // sb * sb
    n_items = N_EXPERTS + a // tm
    return sb, tm, rows, n_items


def _layer(x, meta_tokens, w, w_gate_up, b_gate_up, w_down, b_down, g_attn_out, g_final, *,
           tm_in, tq, tm_out, tmd, tmc, fc):
    b, seq, d = x.shape
    t = b * seq

    meta_blk = jnp.zeros((1, META_ROWS, d), F32).at[0, META_ROWS - N_META:].set(meta_tokens)
    meta_pos = np.maximum(np.arange(META_ROWS) - (META_ROWS - N_META), 0)
    zero_tail = jnp.zeros((SUBLANES, D_CONV), F32)
    _, _, mkn, mkpe, mv, u_tail = _mix_in(meta_blk, zero_tail, _rope_table(meta_pos), w, tm=META_ROWS)

    real_pos = np.arange(seq) + N_META
    convn, q, kn, kpe, v, _ = _mix_in(x, u_tail, _rope_table(real_pos), w, tm=tm_in)
    attn = _attention(q, kn, kpe, v, mkn, mkpe, mv, g_attn_out[None], tq=tq)

    h1, xn, ri, rf, cnt = _mix_out(convn.reshape(t, D_CONV), attn.reshape(t, ATTN_W), x.reshape(t, d), w, tm=tm_out)

    sb, tm_e, rows, n_items = _moe_tiles(t)
    counts = cnt[0, :N_EXPERTS].astype(jnp.int32)
    dest, pad_end, item_e, item_row, item_nrows, tail = _schedule(counts, ri, tm=tm_e, sb=sb, n_items=n_items)

    xg = _dispatch(dest, pad_end, xn, rows=rows, tmd=tmd, sb=sb)
    dff = w_down.shape[1]
    yg = _experts(tail[1], item_e, item_row, item_nrows, tail, xg, w_gate_up, b_gate_up.reshape(N_EXPERTS, 1, 2 * dff),
                  w_down, b_down.reshape(N_EXPERTS, 1, d), tm=tm_e, sb=sb, fc=fc)
    out = _combine(dest, yg, h1, rf, g_final[None], tmc=tmc)
    return out.reshape(b, seq, d)


def kernel(x, meta_tokens, g_mix, w_in, conv_w, g_q, w_q_up, g_kv, w_kv_up, g_conv_out, g_attn_out, w_out, g_ffn,
           w_router, b_router, w_gate_up, b_gate_up, w_down, b_down, g_final):
    w = _prep_weights(g_mix[0], w_in[0], conv_w[0], g_q[0], w_q_up[0], g_kv[0], w_kv_up[0], g_conv_out[0],
                      w_out[0], g_ffn[0], w_router[0], b_router[0])
    seq = x.shape[1]
    return _layer(x, meta_tokens, w, w_gate_up[0], b_gate_up[0], w_down[0], b_down[0], g_attn_out[0], g_final,
                  tm_in=min(512, seq), tq=min(512, seq), tm_out=min(512, seq), tmd=min(512, seq),
                  tmc=min(128, seq), fc=256)
```

```python
import functools

import jax
import jax.numpy as jnp
import numpy as np
from jax import lax
from jax.experimental import pallas as pl
from jax.experimental.pallas import tpu as pltpu

N_META = 16
EPS = 1e-6
D_CONV = 1024
CONV_GROUPS = 16
CONV_WIDTH = 3
N_HEADS = 8
QK_NOPE = 128
QK_ROPE = 64
V_DIM = 128
Q_LORA = 512
KV_LORA = 256
ROPE_THETA = 10000.0
N_EXPERTS = 32
TOP_K = 4
SWIGLU_LIMIT = 7.0
SWIGLU_ALPHA = 1.702

LANES = 128
SUBLANES = 8
META_ROWS = 128
HEAD_W = 2 * LANES
ATTN_W = N_HEADS * V_DIM
V_EXT = V_DIM + 16
VMEM_LIMIT = 56 * 1024 * 1024

O_B, O_C, O_U = 0, D_CONV, 2 * D_CONV
O_Q = 3 * D_CONV
O_KV = O_Q + Q_LORA
O_KPE = O_KV + KV_LORA
D_IN = O_KPE + QK_ROPE

F32 = jnp.float32
BF16 = jnp.bfloat16


def _rms(x, g):
    return x * lax.rsqrt(jnp.mean(x * x, axis=-1, keepdims=True) + EPS) * g


def _dot(a, b):
    return jnp.dot(a, b, preferred_element_type=F32)


def _resident(shape):
    zeros = (0,) * len(shape)
    return pl.BlockSpec(shape, lambda *_: zeros, pipeline_mode=pl.Buffered(1))


def _mix_in_kernel(x_ref, gmix_ref, w1_ref, wk2_ref, cw_ref, gq_ref, wq_ref, gkv_ref, wkv_ref, gco_ref, gmat_ref,
                   tab_ref, uinit_ref,
                   convn_ref, qt_ref, kn_ref, kpe_ref, vt_ref, utail_ref,
                   ubuf_ref, *, tm, scale):
    i = pl.program_id(1)
    hn = _rms(x_ref[0], gmix_ref[...]).astype(BF16)

    def proj(lo, hi):
        return _dot(hn, w1_ref[:, lo:hi])

    @pl.when(i == 0)
    def _():
        ubuf_ref[0:SUBLANES] = uinit_ref[...]

    @pl.when(i > 0)
    def _():
        ubuf_ref[0:SUBLANES] = ubuf_ref[tm:tm + SUBLANES]

    u = proj(O_C, O_U) * proj(O_U, O_Q)
    ubuf_ref[SUBLANES:SUBLANES + tm] = u
    cw = cw_ref[...]
    y = (cw[2:3] * u + cw[1:2] * ubuf_ref[SUBLANES - 1:SUBLANES - 1 + tm]
         + cw[0:1] * ubuf_ref[SUBLANES - 2:SUBLANES - 2 + tm])
    co = proj(O_B, O_C) * y
    ss = _dot((co * co).astype(BF16), gmat_ref[...])
    group = D_CONV // CONV_GROUPS
    convn_ref[0] = (co * lax.rsqrt(ss * (1.0 / group) + EPS) * gco_ref[...]).astype(BF16)
    utail_ref[...] = ubuf_ref[tm:tm + SUBLANES]

    cos = tab_ref[:, :LANES]
    sin = tab_ref[:, LANES:]

    qn = _rms(proj(O_Q, O_KV), gq_ref[...]).astype(BF16)
    qa = _dot(qn, wq_ref[:, :N_HEADS * HEAD_W])
    qb = _dot(qn, wq_ref[:, N_HEADS * HEAD_W:])
    for h in range(N_HEADS):
        c0 = h * HEAD_W
        qt_ref[0, c0:c0 + LANES, :] = (qa[:, c0:c0 + LANES] * scale).T.astype(BF16)
        pe = qa[:, c0 + LANES:c0 + HEAD_W] * cos + qb[:, h * LANES:(h + 1) * LANES] * sin
        qt_ref[0, c0 + LANES:c0 + HEAD_W, :] = (pe * scale).T.astype(BF16)

    kvn = _rms(proj(O_KV, O_KPE), gkv_ref[...]).astype(BF16)
    kv = _dot(kvn, wkv_ref[...])
    kn_ref[0] = kv[:, :ATTN_W].astype(BF16)
    vt = kv[:, ATTN_W:].T.astype(BF16)
    ones = jnp.ones((V_EXT - V_DIM, tm), BF16)
    for h in range(N_HEADS):
        vt_ref[0, 0, h * V_EXT:h * V_EXT + V_DIM, :] = vt[h * V_DIM:(h + 1) * V_DIM]
        vt_ref[0, 0, h * V_EXT + V_DIM:(h + 1) * V_EXT, :] = ones
    kk = _dot(hn, wk2_ref[...])
    kpe_ref[0] = (kk[:, :LANES] * cos + kk[:, LANES:] * sin).astype(BF16)


def _mix_in(x3, uinit, tab, w, *, tm):
    b, l, d = x3.shape
    nt = l // tm
    kern = functools.partial(_mix_in_kernel, tm=tm, scale=float((QK_NOPE + QK_ROPE) ** -0.5))
    row = lambda width: pl.BlockSpec((1, tm, width), lambda bi, i: (bi, i, 0))
    return pl.pallas_call(
        kern,
        grid=(b, nt),
        in_specs=[
            row(d),
            _resident((1, d)),
            _resident((d, D_IN)),
            _resident((d, 2 * LANES)),
            _resident((SUBLANES, D_CONV)),
            _resident((1, Q_LORA)),
            _resident((Q_LORA, N_HEADS * (HEAD_W + LANES))),
            _resident((1, KV_LORA)),
            _resident((KV_LORA, 2 * ATTN_W)),
            _resident((1, D_CONV)),
            _resident((D_CONV, D_CONV)),
            pl.BlockSpec((tm, 2 * LANES), lambda bi, i: (i, 0)),
            _resident((SUBLANES, D_CONV)),
        ],
        out_specs=[
            row(D_CONV),
            pl.BlockSpec((1, N_HEADS * HEAD_W, tm), lambda bi, i: (bi, 0, i)),
            row(ATTN_W), row(LANES),
            pl.BlockSpec((1, 1, N_HEADS * V_EXT, tm), lambda bi, i: (bi, i, 0, 0)),
            pl.BlockSpec((SUBLANES, D_CONV), lambda bi, i: (bi * nt + i, 0)),
        ],
        out_shape=[
            jax.ShapeDtypeStruct((b, l, D_CONV), BF16),
            jax.ShapeDtypeStruct((b, N_HEADS * HEAD_W, l), BF16),
            jax.ShapeDtypeStruct((b, l, ATTN_W), BF16),
            jax.ShapeDtypeStruct((b, l, LANES), BF16),
            jax.ShapeDtypeStruct((b, nt, N_HEADS * V_EXT, tm), BF16),
            jax.ShapeDtypeStruct((b * nt * SUBLANES, D_CONV), F32),
        ],
        scratch_shapes=[pltpu.VMEM((tm + SUBLANES, D_CONV), F32)],
        compiler_params=pltpu.CompilerParams(
            dimension_semantics=("arbitrary", "arbitrary"), vmem_limit_bytes=VMEM_LIMIT),
        name="mix_in",
    )(x3, w["g_mix"], w["w1"], w["wk2"], w["conv_w"], w["g_q"], w["wq"], w["g_kv"], w["wkv"], w["g_conv_out"],
      w["gmat"], tab, uinit)


def _attn_kernel(qt_ref, kn_ref, kpe_ref, vt_ref, mkn_ref, mkpe_ref, mvt_ref, g_ref, o_ref,
                 m_ref, acc_ref, *, tq, hps):
    qi = pl.program_id(2)
    tv = vt_ref.shape[3]

    def lanes(hh):
        return slice(hh * LANES, (hh + 1) * LANES)

    def vrows(hh):
        return slice(hh * V_EXT, (hh + 1) * V_EXT)

    def qt(hh):
        return qt_ref[0, hh * HEAD_W:(hh + 1) * HEAD_W, :]

    for hh in range(hps):
        km = jnp.concatenate([mkn_ref[0, :, lanes(hh)], mkpe_ref[0]], axis=-1)
        s = _dot(km, qt(hh))
        row = lax.broadcasted_iota(jnp.int32, s.shape, 0)
        s = jnp.where(row >= META_ROWS - N_META, s, -jnp.inf)
        m0 = jnp.max(s, axis=0, keepdims=True)
        m_ref[hh] = m0
        acc_ref[hh] = _dot(mvt_ref[0, 0, vrows(hh), :], jnp.exp((s - m0).astype(BF16)))

    def step(kb, diagonal):
        off = pl.multiple_of(kb * tq, tq)
        kpe = kpe_ref[0, pl.ds(off, tq), :]
        scores = []
        for hh in range(hps):
            k = jnp.concatenate([kn_ref[0, pl.ds(off, tq), lanes(hh)], kpe], axis=-1)
            s = _dot(k, qt(hh))
            if diagonal:
                r = lax.broadcasted_iota(jnp.int32, s.shape, 0)
                c = lax.broadcasted_iota(jnp.int32, s.shape, 1)
                s = jnp.where(r <= c, s, -jnp.inf)
            scores.append(s)
        probs, alphas = [], []
        for hh in range(hps):
            m_prev = m_ref[hh]
            m_new = jnp.maximum(m_prev, jnp.max(scores[hh], axis=0, keepdims=True))
            alphas.append(jnp.exp(m_prev - m_new))
            probs.append(jnp.exp((scores[hh] - m_new).astype(BF16)))
            m_ref[hh] = m_new
        for hh in range(hps):
            pb = probs[hh]
            pv = _dot(vt_ref[0, kb * (tq // tv), vrows(hh), :], pb[:tv])
            for c in range(1, tq // tv):
                pv = pv + _dot(vt_ref[0, kb * (tq // tv) + c, vrows(hh), :], pb[c * tv:(c + 1) * tv])
            acc_ref[hh] = alphas[hh] * acc_ref[hh] + pv

    def body(kb, carry):
        step(kb, False)
        return carry

    lax.fori_loop(0, qi, body, 0)
    step(qi, True)

    for hh in range(hps):
        acc = acc_ref[hh]
        o = (acc[:V_DIM] / acc[V_DIM:V_DIM + 1]).T
        o_ref[0, :, lanes(hh)] = _rms(o, g_ref[:, lanes(hh)]).astype(BF16)


def _attention(qt, kn, kpe, vt, mkn, mkpe, mvt, g_attn, *, tq, hps=4):
    b, l, _ = kn.shape
    _, nt, _, tv = vt.shape
    nq = l // tq
    kern = functools.partial(_attn_kernel, tq=tq, hps=hps)
    return pl.pallas_call(
        kern,
        grid=(b, N_HEADS // hps, nq),
        in_specs=[
            pl.BlockSpec((1, hps * HEAD_W, tq), lambda bi, h, i: (bi, h, i)),
            pl.BlockSpec((1, l, hps * LANES), lambda bi, h, i: (bi, 0, h)),
            pl.BlockSpec((1, l, LANES), lambda bi, h, i: (bi, 0, 0)),
            pl.BlockSpec((1, nt, hps * V_EXT, tv), lambda bi, h, i: (bi, 0, h, 0)),
            pl.BlockSpec((1, META_ROWS, hps * LANES), lambda bi, h, i: (0, 0, h)),
            pl.BlockSpec((1, META_ROWS, LANES), lambda bi, h, i: (0, 0, 0)),
            pl.BlockSpec((1, 1, hps * V_EXT, META_ROWS), lambda bi, h, i: (0, 0, h, 0)),
            pl.BlockSpec((1, hps * V_DIM), lambda bi, h, i: (0, h)),
        ],
        out_specs=pl.BlockSpec((1, tq, hps * V_DIM), lambda bi, h, i: (bi, i, h)),
        out_shape=jax.ShapeDtypeStruct((b, l, ATTN_W), BF16),
        scratch_shapes=[pltpu.VMEM((hps, 1, tq), F32), pltpu.VMEM((hps, V_EXT, tq), F32)],
        compiler_params=pltpu.CompilerParams(
            dimension_semantics=("arbitrary", "arbitrary", "arbitrary"), vmem_limit_bytes=VMEM_LIMIT),
        name="attn",
    )(qt, kn, kpe, vt, mkn, mkpe, mvt, g_attn)


def _mix_out_kernel(convn_ref, attn_ref, x_ref, wo_ref, gffn_ref, wr_ref, br_ref,
                    h1_ref, xn_ref, ri_ref, rf_ref, cnt_ref, carry_ref, *, tm):
    @pl.when(pl.program_id(0) == 0)
    def _():
        carry_ref[...] = jnp.zeros_like(carry_ref)

    h1 = x_ref[...] + _dot(convn_ref[...], wo_ref[:D_CONV]) + _dot(attn_ref[...], wo_ref[D_CONV:])
    h1_ref[...] = h1
    xn = _rms(h1, gffn_ref[...])
    xn_ref[...] = xn

    xh = xn.astype(BF16)
    xl = (xn - xh.astype(F32)).astype(BF16)
    ph = _dot(xh, wr_ref[...])
    logits = ph[:, :LANES] + ph[:, LANES:] + _dot(xl, wr_ref[:, :LANES]) + br_ref[...]

    lane = lax.broadcasted_iota(jnp.int32, (tm, LANES), 1).astype(F32)
    work = logits
    top_v, top_i, onehots = [], [], []
    for _ in range(TOP_K):
        mk = jnp.max(work, axis=-1, keepdims=True)
        ik = jnp.min(jnp.where(work == mk, lane, float(LANES)), axis=-1, keepdims=True)
        oh = lane == ik
        work = jnp.where(oh, -jnp.inf, work)
        top_v.append(mk)
        top_i.append(ik)
        onehots.append(oh)

    ex = [jnp.exp(v - top_v[0]) for v in top_v]
    denom = ex[0] + ex[1] + ex[2] + ex[3]
    gates = [e / denom for e in ex]

    sel = jnp.zeros((tm, LANES), F32)
    for oh in onehots:
        sel = sel + oh.astype(F32)
    r = lax.broadcasted_iota(jnp.int32, (tm, tm), 0)
    c = lax.broadcasted_iota(jnp.int32, (tm, tm), 1)
    tri = jnp.where(c < r, 1.0, 0.0).astype(BF16)
    cum = _dot(tri, sel.astype(BF16)) + carry_ref[...]
    carry_ref[...] = carry_ref[...] + jnp.sum(sel, axis=0, keepdims=True)
    cnt_ref[...] = carry_ref[...]

    ri = jnp.zeros((tm, LANES), F32)
    rf = jnp.zeros((tm, LANES), F32)
    for k in range(TOP_K):
        rank_k = jnp.sum(jnp.where(onehots[k], cum, 0.0), axis=-1, keepdims=True)
        ri = jnp.where(lane == float(k), top_i[k], ri)
        ri = jnp.where(lane == float(TOP_K + k), rank_k, ri)
        rf = jnp.where(lane == float(k), gates[k], rf)
    ri_ref[...] = ri.T[:2 * TOP_K].astype(jnp.int32)
    rf_ref[...] = rf


def _mix_out(convn, attn, x2, w, *, tm):
    t, d = x2.shape
    kern = functools.partial(_mix_out_kernel, tm=tm)
    row = lambda width: pl.BlockSpec((tm, width), lambda i: (i, 0))
    return pl.pallas_call(
        kern,
        grid=(t // tm,),
        in_specs=[
            row(D_CONV), row(ATTN_W), row(d),
            _resident((D_CONV + ATTN_W, d)),
            _resident((1, d)),
            _resident((d, 2 * LANES)),
            _resident((1, LANES)),
        ],
        out_specs=[row(d), row(d), pl.BlockSpec((2 * TOP_K, tm), lambda i: (0, i)), row(LANES),
                   pl.BlockSpec((1, LANES), lambda i: (0, 0))],
        out_shape=[
            jax.ShapeDtypeStruct((t, d), F32),
            jax.ShapeDtypeStruct((t, d), F32),
            jax.ShapeDtypeStruct((2 * TOP_K, t), jnp.int32),
            jax.ShapeDtypeStruct((t, LANES), F32),
            jax.ShapeDtypeStruct((1, LANES), F32),
        ],
        scratch_shapes=[pltpu.VMEM((1, LANES), F32)],
        compiler_params=pltpu.CompilerParams(
            dimension_semantics=("arbitrary",), vmem_limit_bytes=VMEM_LIMIT),
        name="mix_out",
    )(convn, attn, x2, w["wo"], w["g_ffn"], w["wr"], w["br"])


def _dispatch_kernel(dest_ref, pend_ref, xn_ref, xg_hbm, zbuf_ref, sem, zsem, *, tmd, sb):
    i = pl.program_id(0)

    @pl.when(i == 0)
    def _():
        zbuf_ref[...] = jnp.zeros_like(zbuf_ref)

        def zero_copy(e):
            end = pend_ref[e]
            start = pl.multiple_of(jnp.maximum(end - sb, 0), sb)
            return pltpu.make_async_copy(zbuf_ref, xg_hbm.at[pl.ds(start, sb)], zsem)

        def nonempty(e):
            prev = pend_ref[jnp.maximum(e - 1, 0)]
            return pend_ref[e] > jnp.where(e > 0, prev, 0)

        def start(e, carry):
            @pl.when(nonempty(e))
            def _():
                zero_copy(e).start()
            return carry

        def wait(e, carry):
            @pl.when(nonempty(e))
            def _():
                zero_copy(e).wait()
            return carry

        lax.fori_loop(0, N_EXPERTS, start, 0)
        lax.fori_loop(0, N_EXPERTS, wait, 0)
        _zero_tail(zbuf_ref, xg_hbm, pend_ref[N_EXPERTS - 1], zsem)

    base = i * tmd
    n_tok = pl.num_programs(0) * tmd

    def issue(r, carry):
        for k in range(TOP_K):
            dst = xg_hbm.at[pl.ds(dest_ref[k * n_tok + base + r], 1)]
            pltpu.make_async_copy(xn_ref.at[pl.ds(r, 1)], dst, sem).start(priority=k % 2)
        return carry

    lax.fori_loop(0, tmd, issue, 0)
    for k in range(TOP_K):
        pltpu.make_async_copy(xn_ref, xg_hbm.at[pl.ds(0, tmd)], sem).wait()


def _dispatch(dest, pad_end, xn, *, rows, tmd, sb):
    t, d = xn.shape
    kern = functools.partial(_dispatch_kernel, tmd=tmd, sb=sb)
    return pl.pallas_call(
        kern,
        grid_spec=pltpu.PrefetchScalarGridSpec(
            num_scalar_prefetch=2,
            grid=(t // tmd,),
            in_specs=[pl.BlockSpec((tmd, d), lambda i, dest, pend: (i, 0))],
            out_specs=pl.BlockSpec(memory_space=pl.ANY),
            scratch_shapes=[pltpu.VMEM((sb, d), F32), pltpu.SemaphoreType.DMA, pltpu.SemaphoreType.DMA],
        ),
        out_shape=jax.ShapeDtypeStruct((rows, d), F32),
        compiler_params=pltpu.CompilerParams(
            dimension_semantics=("arbitrary",), vmem_limit_bytes=VMEM_LIMIT),
        name="dispatch",
    )(dest, pad_end, xn)


def _expert_kernel(ie_ref, irow_ref, inr_ref, tail_ref, xg_hbm, wg_ref, wl_ref, wd_ref, bg_ref, bl_ref, bd_ref, yg_hbm,
                   xst_ref, xb_ref, act_ref, acc_ref, wgb_ref, wlb_ref, wdb_ref, sem_in, sem_out,
                   *, tm, sb, row_sizes):
    it = pl.program_id(0)
    s = pl.program_id(1)
    n_it = pl.num_programs(0)
    nrows = inr_ref[it]
    n_up = act_ref.shape[0]
    n_down, _, fc = acc_ref.shape
    last = n_up + n_down - 1
    prev_it = jnp.maximum(it - 1, 0)
    next_it = jnp.minimum(it + 1, n_it - 1)

    def piece(r):
        return pl.ds(r * sb, sb)

    def hbm_rows(item, r):
        return pl.ds(pl.multiple_of(irow_ref[item] + r * sb, sb), sb)

    def in_copy(item, r):
        return pltpu.make_async_copy(xg_hbm.at[hbm_rows(item, r)], xst_ref.at[piece(r)], sem_in.at[r])

    def out_copies(item, r):
        return [pltpu.make_async_copy(acc_ref.at[n, piece(r)], yg_hbm.at[hbm_rows(item, r), n * fc:(n + 1) * fc],
                                      sem_out.at[r]) for n in range(n_down)]

    def for_pieces_of(item, fn):
        for r in range(tm // sb):
            pl.when(r * sb < inr_ref[item])(functools.partial(fn, item, r))

    def for_valid_chunks(fn):
        for lo, size in zip((0,) + row_sizes[:-1], row_sizes):
            pl.when((lo < nrows) & (nrows <= size))(functools.partial(fn, pl.ds(0, size), True))

    def start_in(item, r):
        in_copy(item, r).start()

    def land_in(item, r):
        in_copy(item, r).wait()
        xb_ref[piece(r)] = xst_ref[piece(r)].astype(BF16)

    def start_out(item, r):
        for cp in out_copies(item, r):
            cp.start()

    def wait_out(item, r):
        for cp in out_copies(item, r):
            cp.wait()

    @pl.when((it == 0) & (s == 0))
    def _():
        xb_ref[...] = jnp.zeros_like(xb_ref)
        for_pieces_of(it, start_in)

    @pl.when(s == 0)
    def _():
        for_pieces_of(it, land_in)

    @pl.when((s == 1) & (it + 1 < n_it))
    def _():
        for_pieces_of(next_it, start_in)

    @pl.when((s == n_up) & (it > 0))
    def _():
        for_pieces_of(prev_it, wait_out)

    def up(rows, cast_weights):
        if cast_weights:
            wgb_ref[...] = wg_ref[0].astype(BF16)
            wlb_ref[...] = wl_ref[0].astype(BF16)
        xs = xb_ref[rows]
        g = jnp.minimum(_dot(xs, wgb_ref[...]) + bg_ref[0], SWIGLU_LIMIT)
        lin = jnp.clip(_dot(xs, wlb_ref[...]) + bl_ref[0], -SWIGLU_LIMIT, SWIGLU_LIMIT)
        act_ref[s, rows] = ((lin + 1.0) * (g * jax.nn.sigmoid(SWIGLU_ALPHA * g))).astype(BF16)

    def down(rows, cast_weights):
        if cast_weights:
            wdb_ref[...] = wd_ref[0].astype(BF16)
        act = jnp.concatenate([act_ref[jj, rows] for jj in range(n_up)], axis=-1)
        acc_ref[s - n_up, rows] = _dot(act, wdb_ref[...]) + bd_ref[0]

    @pl.when(s < n_up)
    def _():
        for_valid_chunks(up)

    @pl.when(s >= n_up)
    def _():
        for_valid_chunks(down)

    @pl.when(s == last)
    def _():
        for_pieces_of(it, start_out)

    @pl.when((it == n_it - 1) & (s == last))
    def _():
        for_pieces_of(it, wait_out)
        acc_ref[0, piece(0)] = jnp.zeros((sb, fc), F32)
        _zero_tail(acc_ref.at[0, piece(0)], yg_hbm, tail_ref[0], sem_out.at[0])


def _zero_tail(zeros_vmem, dst_hbm, first_row, sem):
    sb, width = zeros_vmem.shape
    n_blocks = (dst_hbm.shape[0] - first_row) // sb

    def fill(b, carry):
        rows = pl.ds(pl.multiple_of(first_row + b * sb, sb), sb)
        copies = [pltpu.make_async_copy(zeros_vmem, dst_hbm.at[rows, c * width:(c + 1) * width], sem)
                  for c in range(dst_hbm.shape[1] // width)]
        for cp in copies:
            cp.start()
        for cp in copies:
            cp.wait()
        return carry

    lax.fori_loop(0, n_blocks, fill, 0)


def _experts(n_used, item_e, item_row, item_nrows, tail, xg, w_gate_up, b_gate_up, w_down, b_down, *, tm, sb, fc):
    rows, d = xg.shape
    dff = w_down.shape[1]
    n_up = dff // fc
    fd = 2 * fc
    n_down = d // fd
    nsub = tm // sb
    kern = functools.partial(_expert_kernel, tm=tm, sb=sb, row_sizes=_row_sizes(tm, sb))

    def up_block(offset):
        def index_map(it, s, ie, ir, inr, tl):
            ahead = s >= n_up
            nxt = jnp.minimum(it + 1, tl[1] - 1)
            return jnp.where(ahead, ie[nxt], ie[it]), 0, offset + jnp.where(ahead, 0, s)
        return index_map

    def down_block(it, s, ie, ir, inr, tl):
        return ie[it], 0, jnp.maximum(s - n_up, 0)

    return pl.pallas_call(
        kern,
        grid_spec=pltpu.PrefetchScalarGridSpec(
            num_scalar_prefetch=4,
            grid=(n_used, n_up + n_down),
            in_specs=[
                pl.BlockSpec(memory_space=pl.ANY),
                pl.BlockSpec((1, d, fc), up_block(0)),
                pl.BlockSpec((1, d, fc), up_block(n_up)),
                pl.BlockSpec((1, dff, fd), down_block),
                pl.BlockSpec((1, 1, fc), up_block(0)),
                pl.BlockSpec((1, 1, fc), up_block(n_up)),
                pl.BlockSpec((1, 1, fd), down_block),
            ],
            out_specs=pl.BlockSpec(memory_space=pl.ANY),
            scratch_shapes=[
                pltpu.VMEM((tm, d), F32),
                pltpu.VMEM((tm, d), BF16),
                pltpu.VMEM((n_up, tm, fc), BF16),
                pltpu.VMEM((n_down, tm, fd), F32),
                pltpu.VMEM((d, fc), BF16),
                pltpu.VMEM((d, fc), BF16),
                pltpu.VMEM((dff, fd), BF16),
                pltpu.SemaphoreType.DMA((nsub,)),
                pltpu.SemaphoreType.DMA((nsub,)),
            ],
        ),
        out_shape=jax.ShapeDtypeStruct((rows, d), F32),
        compiler_params=pltpu.CompilerParams(
            dimension_semantics=("arbitrary", "arbitrary"), vmem_limit_bytes=VMEM_LIMIT),
        name="experts",
    )(item_e, item_row, item_nrows, tail, xg, w_gate_up, w_gate_up, w_down,
      b_gate_up, b_gate_up, b_down)


def _combine_kernel(dest_ref, yg_hbm, h1_ref, rf_ref, gfin_ref, o_ref, gbuf_ref, sem, *, tmc):
    i = pl.program_id(0)
    n_tiles = pl.num_programs(0)
    n_tok = n_tiles * tmc

    def issue_tile(tile):
        buf = tile % 2
        base = tile * tmc

        def issue(r, carry):
            for k in range(TOP_K):
                src = yg_hbm.at[pl.ds(dest_ref[k * n_tok + base + r], 1)]
                pltpu.make_async_copy(src, gbuf_ref.at[buf, k, pl.ds(r, 1)], sem.at[buf]).start(priority=k % 2)
            return carry

        lax.fori_loop(0, tmc, issue, 0)

    @pl.when(i == 0)
    def _():
        issue_tile(i)

    @pl.when(i + 1 < n_tiles)
    def _():
        issue_tile(i + 1)

    buf = i % 2
    for k in range(TOP_K):
        pltpu.make_async_copy(yg_hbm.at[pl.ds(0, tmc)], gbuf_ref.at[buf, k], sem.at[buf]).wait()

    gates = rf_ref[...]
    y = h1_ref[...]
    for k in range(TOP_K):
        y = y + gates[:, k:k + 1] * gbuf_ref[buf, k]
    o_ref[...] = _rms(y, gfin_ref[...])


def _combine(dest, yg, h1, rf, g_final, *, tmc):
    t, d = h1.shape
    kern = functools.partial(_combine_kernel, tmc=tmc)
    return pl.pallas_call(
        kern,
        grid_spec=pltpu.PrefetchScalarGridSpec(
            num_scalar_prefetch=1,
            grid=(t // tmc,),
            in_specs=[
                pl.BlockSpec(memory_space=pl.ANY),
                pl.BlockSpec((tmc, d), lambda i, dest: (i, 0)),
                pl.BlockSpec((tmc, LANES), lambda i, dest: (i, 0)),
                pl.BlockSpec((1, d), lambda i, dest: (0, 0)),
            ],
            out_specs=pl.BlockSpec((tmc, d), lambda i, dest: (i, 0)),
            scratch_shapes=[pltpu.VMEM((2, TOP_K, tmc, d), F32), pltpu.SemaphoreType.DMA((2,))],
        ),
        out_shape=jax.ShapeDtypeStruct((t, d), F32),
        compiler_params=pltpu.CompilerParams(
            dimension_semantics=("arbitrary",), vmem_limit_bytes=VMEM_LIMIT),
        name="combine",
    )(dest, yg, h1, rf, g_final)


def _cast_kernel(x_ref, o_ref):
    o_ref[...] = x_ref[...].astype(o_ref.dtype)


def _to_bf16(w, *, block_rows=256):
    rows, cols = w.shape
    spec = pl.BlockSpec((block_rows, cols), lambda i: (i, 0))
    return pl.pallas_call(
        _cast_kernel, grid=(rows // block_rows,), in_specs=[spec], out_specs=spec,
        out_shape=jax.ShapeDtypeStruct(w.shape, BF16), name="to_bf16",
    )(w)


def _rotate_half_cols(w):
    half = w.shape[-1] // 2
    return jnp.concatenate([-w[..., half:], w[..., :half]], axis=-1)


def _prep_weights(g_mix, w_in, conv_w, g_q, w_q_up, g_kv, w_kv_up, g_conv_out, w_out, g_ffn, w_router, b_router):
    d = w_in.shape[0]
    w1 = _to_bf16(w_in)
    k_rope = w1[:, O_KPE:O_KPE + QK_ROPE]
    pad = jnp.zeros((d, LANES - QK_ROPE), BF16)
    wk2 = jnp.concatenate([k_rope, pad, _rotate_half_cols(k_rope), pad], axis=1)

    wq = w_q_up.reshape(Q_LORA, N_HEADS, QK_NOPE + QK_ROPE)
    nope, pe = wq[:, :, :QK_NOPE], wq[:, :, QK_NOPE:]
    hpad = jnp.zeros((Q_LORA, N_HEADS, LANES - QK_ROPE), F32)
    wqa = jnp.concatenate([nope, pe, hpad], axis=2).reshape(Q_LORA, N_HEADS * HEAD_W)
    wqb = jnp.concatenate([_rotate_half_cols(pe), hpad], axis=2).reshape(Q_LORA, N_HEADS * LANES)
    wq_all = jnp.concatenate([wqa, wqb], axis=1).astype(BF16)

    wkv = w_kv_up.reshape(KV_LORA, N_HEADS, QK_NOPE + V_DIM)
    wkv2 = jnp.concatenate([wkv[:, :, :QK_NOPE].reshape(KV_LORA, ATTN_W),
                            wkv[:, :, QK_NOPE:].reshape(KV_LORA, ATTN_W)], axis=1).astype(BF16)

    wr_pad = jnp.zeros((d, LANES), F32).at[:, :N_EXPERTS].set(w_router)
    wr_hi = wr_pad.astype(BF16)
    wr_lo = (wr_pad - wr_hi.astype(F32)).astype(BF16)
    br = jnp.full((1, LANES), -1e30, F32).at[0, :N_EXPERTS].set(b_router)

    grp = jnp.arange(D_CONV) // (D_CONV // CONV_GROUPS)
    cw = jnp.zeros((SUBLANES, D_CONV), F32).at[:CONV_WIDTH].set(conv_w)
    return {
        "g_mix": g_mix[None], "w1": w1, "wk2": wk2, "conv_w": cw, "g_q": g_q[None], "wq": wq_all, "g_kv": g_kv[None],
        "wkv": wkv2, "g_conv_out": g_conv_out[None], "gmat": (grp[:, None] == grp[None, :]).astype(BF16),
        "wo": w_out.astype(BF16), "g_ffn": g_ffn[None], "wr": jnp.concatenate([wr_hi, wr_lo], axis=1), "br": br,
    }


def _rope_table(pos):
    half = QK_ROPE // 2
    inv_freq = np.float32(ROPE_THETA) ** (-np.arange(half, dtype=np.float32) / np.float32(half))
    ang = (np.asarray(pos, np.float32)[:, None] * inv_freq[None, :]).astype(np.float32)
    c, s = np.cos(ang), np.sin(ang)
    z = np.zeros((ang.shape[0], LANES - QK_ROPE), np.float32)
    return jnp.asarray(np.concatenate([c, c, z, s, s, z], axis=1), dtype=F32)


def _schedule_kernel(cnt_ref, ri_ref, dest_ref, pend_ref, ie_ref, irow_ref, inr_ref, tail_ref, pstart_ref,
                     *, tm, sb, n_items):
    def per_expert(e, carry):
        row, item = carry
        count = cnt_ref[e]
        pstart_ref[e] = row

        def per_item(li, item):
            ie_ref[item] = e
            irow_ref[item] = row + li * tm
            inr_ref[item] = jnp.minimum(count - li * tm, tm)
            return item + 1

        item = lax.fori_loop(0, lax.div(count + (tm - 1), tm), per_item, item)
        row = row + lax.div(count + (sb - 1), sb) * sb
        pend_ref[e] = row
        return row, item

    row, n_used = lax.fori_loop(0, N_EXPERTS, per_expert, (jnp.int32(0), jnp.int32(0)))
    tail_ref[0] = row
    tail_ref[1] = n_used

    def unused(item, carry):
        ie_ref[item] = 0
        irow_ref[item] = 0
        inr_ref[item] = 0
        return carry

    lax.fori_loop(n_used, n_items, unused, 0)

    eidx = ri_ref[:TOP_K]
    start_of = jnp.zeros_like(eidx)
    for e in range(N_EXPERTS):
        start_of = jnp.where(eidx == e, pstart_ref[e], start_of)
    dest_ref[...] = start_of + ri_ref[TOP_K:]


def _schedule(counts, ri, *, tm, sb, n_items):
    t = ri.shape[1]
    smem = pl.BlockSpec(memory_space=pltpu.SMEM)
    i32 = lambda n: jax.ShapeDtypeStruct((n,), jnp.int32)
    dest, pad_end, item_e, item_row, item_nrows, tail = pl.pallas_call(
        functools.partial(_schedule_kernel, tm=tm, sb=sb, n_items=n_items),
        in_specs=[smem, pl.BlockSpec(memory_space=pltpu.VMEM)],
        out_specs=[pl.BlockSpec(memory_space=pltpu.VMEM), smem, smem, smem, smem, smem],
        out_shape=[jax.ShapeDtypeStruct((TOP_K, t), jnp.int32), i32(N_EXPERTS), i32(n_items), i32(n_items),
                   i32(n_items), i32(2)],
        scratch_shapes=[pltpu.SMEM((N_EXPERTS,), jnp.int32)],
        name="schedule",
    )(counts, ri)
    return dest.reshape(-1), pad_end, item_e, item_row, item_nrows, tail


def _row_sizes(tm, sb):
    fine = [tm - k * sb for k in (2, 1, 0) if tm - k * sb > 0]
    coarse = [s for s in (tm // 4 // sb * sb, tm // 2 // sb * sb) if 0 < s < fine[0]]
    return tuple(sorted(set(coarse + fine)))


def _moe_tiles(t):
    sb = 128
    tm = 10 * sb
    a = t * TOP_K
    rows = (a + N_EXPERTS * (sb - 1) + sb - 1) // sb * sb
    n_items = N_EXPERTS + a // tm
    return sb, tm, rows, n_items


def _layer(x, meta_tokens, w, w_gate_up, b_gate_up, w_down, b_down, g_attn_out, g_final, *,
           tm_in, tq, tm_out, tmd, tmc, fc):
    b, seq, d = x.shape
    t = b * seq

    meta_blk = jnp.zeros((1, META_ROWS, d), F32).at[0, META_ROWS - N_META:].set(meta_tokens)
    meta_pos = np.maximum(np.arange(META_ROWS) - (META_ROWS - N_META), 0)
    zero_tail = jnp.zeros((SUBLANES, D_CONV), F32)
    _, _, mkn, mkpe, mv, u_tail = _mix_in(meta_blk, zero_tail, _rope_table(meta_pos), w, tm=META_ROWS)

    real_pos = np.arange(seq) + N_META
    convn, q, kn, kpe, v, _ = _mix_in(x, u_tail, _rope_table(real_pos), w, tm=tm_in)
    attn = _attention(q, kn, kpe, v, mkn, mkpe, mv, g_attn_out[None], tq=tq)

    h1, xn, ri, rf, cnt = _mix_out(convn.reshape(t, D_CONV), attn.reshape(t, ATTN_W), x.reshape(t, d), w, tm=tm_out)

    sb, tm_e, rows, n_items = _moe_tiles(t)
    counts = cnt[0, :N_EXPERTS].astype(jnp.int32)
    dest, pad_end, item_e, item_row, item_nrows, tail = _schedule(counts, ri, tm=tm_e, sb=sb, n_items=n_items)

    xg = _dispatch(dest, pad_end, xn, rows=rows, tmd=tmd, sb=sb)
    dff = w_down.shape[1]
    yg = _experts(tail[1], item_e, item_row, item_nrows, tail, xg, w_gate_up, b_gate_up.reshape(N_EXPERTS, 1, 2 * dff),
                  w_down, b_down.reshape(N_EXPERTS, 1, d), tm=tm_e, sb=sb, fc=fc)
    out = _combine(dest, yg, h1, rf, g_final[None], tmc=tmc)
    return out.reshape(b, seq, d)


def kernel(x, meta_tokens, g_mix, w_in, conv_w, g_q, w_q_up, g_kv, w_kv_up, g_conv_out, g_attn_out, w_out, g_ffn,
           w_router, b_router, w_gate_up, b_gate_up, w_down, b_down, g_final):
    w = _prep_weights(g_mix[0], w_in[0], conv_w[0], g_q[0], w_q_up[0], g_kv[0], w_kv_up[0], g_conv_out[0],
                      w_out[0], g_ffn[0], w_router[0], b_router[0])
    seq = x.shape[1]
    return _layer(x, meta_tokens, w, w_gate_up[0], b_gate_up[0], w_down[0], b_down[0], g_attn_out[0], g_final,
                  tm_in=min(512, seq), tq=min(512, seq), tm_out=min(512, seq), tmd=min(512, seq),
                  tmc=min(128, seq), fc=256)
```

```python
import functools

import jax
import jax.numpy as jnp
import numpy as np
from jax import lax
from jax.experimental import pallas as pl
from jax.experimental.pallas import tpu as pltpu

N_META = 16
EPS = 1e-6
D_CONV = 1024
CONV_GROUPS = 16
CONV_WIDTH = 3
N_HEADS = 8
QK_NOPE = 128
QK_ROPE = 64
V_DIM = 128
Q_LORA = 512
KV_LORA = 256
ROPE_THETA = 10000.0
N_EXPERTS = 32
TOP_K = 4
SWIGLU_LIMIT = 7.0
SWIGLU_ALPHA = 1.702

LANES = 128
SUBLANES = 8
META_ROWS = 128
HEAD_W = 2 * LANES
ATTN_W = N_HEADS * V_DIM
V_EXT = V_DIM + 16
VMEM_LIMIT = 56 * 1024 * 1024

O_B, O_C, O_U = 0, D_CONV, 2 * D_CONV
O_Q = 3 * D_CONV
O_KV = O_Q + Q_LORA
O_KPE = O_KV + KV_LORA
D_IN = O_KPE + QK_ROPE

F32 = jnp.float32
BF16 = jnp.bfloat16
NT_DIMS = (((1,), (1,)), ((), ()))


def _rms(x, g):
    return x * lax.rsqrt(jnp.mean(x * x, axis=-1, keepdims=True) + EPS) * g


def _dot(a, b):
    return jnp.dot(a, b, preferred_element_type=F32)


def _resident(shape):
    zeros = (0,) * len(shape)
    return pl.BlockSpec(shape, lambda *_: zeros, pipeline_mode=pl.Buffered(1))


def _mix_in_kernel(x_ref, gmix_ref, w1_ref, wk2_ref, cw_ref, gq_ref, wq_ref, gkv_ref, wkv_ref, gco_ref, gmat_ref,
                   tab_ref, uinit_ref,
                   convn_ref, qt_ref, kn_ref, kpe_ref, vt_ref, utail_ref,
                   ubuf_ref, *, tm, scale):
    i = pl.program_id(1)
    hn = _rms(x_ref[0], gmix_ref[...]).astype(BF16)

    def proj(lo, hi):
        return lax.dot_general(hn, w1_ref[lo:hi, :], NT_DIMS, preferred_element_type=F32)

    @pl.when(i == 0)
    def _():
        ubuf_ref[0:SUBLANES] = uinit_ref[...]

    @pl.when(i > 0)
    def _():
        ubuf_ref[0:SUBLANES] = ubuf_ref[tm:tm + SUBLANES]

    u = proj(O_C, O_U) * proj(O_U, O_Q)
    ubuf_ref[SUBLANES:SUBLANES + tm] = u
    cw = cw_ref[...]
    y = (cw[2:3] * u + cw[1:2] * ubuf_ref[SUBLANES - 1:SUBLANES - 1 + tm]
         + cw[0:1] * ubuf_ref[SUBLANES - 2:SUBLANES - 2 + tm])
    co = proj(O_B, O_C) * y
    ss = _dot((co * co).astype(BF16), gmat_ref[...])
    group = D_CONV // CONV_GROUPS
    convn_ref[0] = (co * lax.rsqrt(ss * (1.0 / group) + EPS) * gco_ref[...]).astype(BF16)
    utail_ref[...] = ubuf_ref[tm:tm + SUBLANES]

    cos = tab_ref[:, :LANES]
    sin = tab_ref[:, LANES:]

    qn = _rms(proj(O_Q, O_KV), gq_ref[...]).astype(BF16)
    qa = _dot(qn, wq_ref[:, :N_HEADS * HEAD_W])
    qb = _dot(qn, wq_ref[:, N_HEADS * HEAD_W:])
    for h in range(N_HEADS):
        c0 = h * HEAD_W
        qt_ref[0, c0:c0 + LANES, :] = (qa[:, c0:c0 + LANES] * scale).T.astype(BF16)
        pe = qa[:, c0 + LANES:c0 + HEAD_W] * cos + qb[:, h * LANES:(h + 1) * LANES] * sin
        qt_ref[0, c0 + LANES:c0 + HEAD_W, :] = (pe * scale).T.astype(BF16)

    kvn = _rms(proj(O_KV, O_KPE), gkv_ref[...]).astype(BF16)
    kv = _dot(kvn, wkv_ref[...])
    kn_ref[0] = kv[:, :ATTN_W].astype(BF16)
    vt = kv[:, ATTN_W:].T.astype(BF16)
    ones = jnp.ones((V_EXT - V_DIM, tm), BF16)
    for h in range(N_HEADS):
        vt_ref[0, 0, h * V_EXT:h * V_EXT + V_DIM, :] = vt[h * V_DIM:(h + 1) * V_DIM]
        vt_ref[0, 0, h * V_EXT + V_DIM:(h + 1) * V_EXT, :] = ones
    kk = lax.dot_general(hn, wk2_ref[...], NT_DIMS, preferred_element_type=F32)
    kpe_ref[0] = (kk[:, :LANES] * cos + kk[:, LANES:] * sin).astype(BF16)


def _mix_in(x3, uinit, tab, w, *, tm):
    b, l, d = x3.shape
    nt = l // tm
    kern = functools.partial(_mix_in_kernel, tm=tm, scale=float((QK_NOPE + QK_ROPE) ** -0.5))
    row = lambda width: pl.BlockSpec((1, tm, width), lambda bi, i: (bi, i, 0))
    return pl.pallas_call(
        kern,
        grid=(b, nt),
        in_specs=[
            row(d),
            _resident((1, d)),
            _resident((D_IN, d)),
            _resident((2 * LANES, d)),
            _resident((SUBLANES, D_CONV)),
            _resident((1, Q_LORA)),
            _resident((Q_LORA, N_HEADS * (HEAD_W + LANES))),
            _resident((1, KV_LORA)),
            _resident((KV_LORA, 2 * ATTN_W)),
            _resident((1, D_CONV)),
            _resident((D_CONV, D_CONV)),
            pl.BlockSpec((tm, 2 * LANES), lambda bi, i: (i, 0)),
            _resident((SUBLANES, D_CONV)),
        ],
        out_specs=[
            row(D_CONV),
            pl.BlockSpec((1, N_HEADS * HEAD_W, tm), lambda bi, i: (bi, 0, i)),
            row(ATTN_W), row(LANES),
            pl.BlockSpec((1, 1, N_HEADS * V_EXT, tm), lambda bi, i: (bi, i, 0, 0)),
            pl.BlockSpec((SUBLANES, D_CONV), lambda bi, i: (bi * nt + i, 0)),
        ],
        out_shape=[
            jax.ShapeDtypeStruct((b, l, D_CONV), BF16),
            jax.ShapeDtypeStruct((b, N_HEADS * HEAD_W, l), BF16),
            jax.ShapeDtypeStruct((b, l, ATTN_W), BF16),
            jax.ShapeDtypeStruct((b, l, LANES), BF16),
            jax.ShapeDtypeStruct((b, nt, N_HEADS * V_EXT, tm), BF16),
            jax.ShapeDtypeStruct((b * nt * SUBLANES, D_CONV), F32),
        ],
        scratch_shapes=[pltpu.VMEM((tm + SUBLANES, D_CONV), F32)],
        compiler_params=pltpu.CompilerParams(
            dimension_semantics=("arbitrary", "arbitrary"), vmem_limit_bytes=VMEM_LIMIT),
        name="mix_in",
    )(x3, w["g_mix"], w["w1"], w["wk2"], w["conv_w"], w["g_q"], w["wq"], w["g_kv"], w["wkv"], w["g_conv_out"],
      w["gmat"], tab, uinit)


def _attn_kernel(qt_ref, kn_ref, kpe_ref, vt_ref, mkn_ref, mkpe_ref, mvt_ref, g_ref, o_ref,
                 m_ref, acc_ref, *, tq, hps):
    qi = pl.program_id(2)
    tv = vt_ref.shape[3]

    def lanes(hh):
        return slice(hh * LANES, (hh + 1) * LANES)

    def vrows(hh):
        return slice(hh * V_EXT, (hh + 1) * V_EXT)

    def qt(hh):
        return qt_ref[0, hh * HEAD_W:(hh + 1) * HEAD_W, :]

    for hh in range(hps):
        km = jnp.concatenate([mkn_ref[0, :, lanes(hh)], mkpe_ref[0]], axis=-1)
        s = _dot(km, qt(hh))
        row = lax.broadcasted_iota(jnp.int32, s.shape, 0)
        s = jnp.where(row >= META_ROWS - N_META, s, -jnp.inf)
        m0 = jnp.max(s, axis=0, keepdims=True)
        m_ref[hh] = m0
        acc_ref[hh] = _dot(mvt_ref[0, 0, vrows(hh), :], jnp.exp((s - m0).astype(BF16)))

    def step(kb, diagonal):
        off = pl.multiple_of(kb * tq, tq)
        kpe = kpe_ref[0, pl.ds(off, tq), :]
        scores = []
        for hh in range(hps):
            k = jnp.concatenate([kn_ref[0, pl.ds(off, tq), lanes(hh)], kpe], axis=-1)
            s = _dot(k, qt(hh))
            if diagonal:
                r = lax.broadcasted_iota(jnp.int32, s.shape, 0)
                c = lax.broadcasted_iota(jnp.int32, s.shape, 1)
                s = jnp.where(r <= c, s, -jnp.inf)
            scores.append(s)
        probs, alphas = [], []
        for hh in range(hps):
            m_prev = m_ref[hh]
            m_new = jnp.maximum(m_prev, jnp.max(scores[hh], axis=0, keepdims=True))
            alphas.append(jnp.exp(m_prev - m_new))
            probs.append(jnp.exp((scores[hh] - m_new).astype(BF16)))
            m_ref[hh] = m_new
        for hh in range(hps):
            pb = probs[hh]
            pv = _dot(vt_ref[0, kb * (tq // tv), vrows(hh), :], pb[:tv])
            for c in range(1, tq // tv):
                pv = pv + _dot(vt_ref[0, kb * (tq // tv) + c, vrows(hh), :], pb[c * tv:(c + 1) * tv])
            acc_ref[hh] = alphas[hh] * acc_ref[hh] + pv

    def body(kb, carry):
        step(kb, False)
        return carry

    lax.fori_loop(0, qi, body, 0)
    step(qi, True)

    for hh in range(hps):
        acc = acc_ref[hh]
        o = (acc[:V_DIM] / acc[V_DIM:V_DIM + 1]).T
        o_ref[0, :, lanes(hh)] = _rms(o, g_ref[:, lanes(hh)]).astype(BF16)


def _attention(qt, kn, kpe, vt, mkn, mkpe, mvt, g_attn, *, tq, hps=4):
    b, l, _ = kn.shape
    _, nt, _, tv = vt.shape
    nq = l // tq
    kern = functools.partial(_attn_kernel, tq=tq, hps=hps)
    return pl.pallas_call(
        kern,
        grid=(b, N_HEADS // hps, nq),
        in_specs=[
            pl.BlockSpec((1, hps * HEAD_W, tq), lambda bi, h, i: (bi, h, i)),
            pl.BlockSpec((1, l, hps * LANES), lambda bi, h, i: (bi, 0, h)),
            pl.BlockSpec((1, l, LANES), lambda bi, h, i: (bi, 0, 0)),
            pl.BlockSpec((1, nt, hps * V_EXT, tv), lambda bi, h, i: (bi, 0, h, 0)),
            pl.BlockSpec((1, META_ROWS, hps * LANES), lambda bi, h, i: (0, 0, h)),
            pl.BlockSpec((1, META_ROWS, LANES), lambda bi, h, i: (0, 0, 0)),
            pl.BlockSpec((1, 1, hps * V_EXT, META_ROWS), lambda bi, h, i: (0, 0, h, 0)),
            pl.BlockSpec((1, hps * V_DIM), lambda bi, h, i: (0, h)),
        ],
        out_specs=pl.BlockSpec((1, tq, hps * V_DIM), lambda bi, h, i: (bi, i, h)),
        out_shape=jax.ShapeDtypeStruct((b, l, ATTN_W), BF16),
        scratch_shapes=[pltpu.VMEM((hps, 1, tq), F32), pltpu.VMEM((hps, V_EXT, tq), F32)],
        compiler_params=pltpu.CompilerParams(
            dimension_semantics=("arbitrary", "arbitrary", "arbitrary"), vmem_limit_bytes=VMEM_LIMIT),
        name="attn",
    )(qt, kn, kpe, vt, mkn, mkpe, mvt, g_attn)


def _mix_out_kernel(convn_ref, attn_ref, x_ref, wo_ref, gffn_ref, wr_ref, br_ref,
                    h1_ref, xn_ref, ri_ref, rf_ref, cnt_ref, carry_ref, *, tm):
    @pl.when(pl.program_id(0) == 0)
    def _():
        carry_ref[...] = jnp.zeros_like(carry_ref)

    h1 = x_ref[...] + _dot(convn_ref[...], wo_ref[:D_CONV]) + _dot(attn_ref[...], wo_ref[D_CONV:])
    h1_ref[...] = h1
    xn = _rms(h1, gffn_ref[...])
    xn_ref[...] = xn

    xh = xn.astype(BF16)
    xl = (xn - xh.astype(F32)).astype(BF16)
    ph = _dot(xh, wr_ref[...])
    logits = ph[:, :LANES] + ph[:, LANES:] + _dot(xl, wr_ref[:, :LANES]) + br_ref[...]

    lane = lax.broadcasted_iota(jnp.int32, (tm, LANES), 1).astype(F32)
    work = logits
    top_v, top_i, onehots = [], [], []
    for _ in range(TOP_K):
        mk = jnp.max(work, axis=-1, keepdims=True)
        ik = jnp.min(jnp.where(work == mk, lane, float(LANES)), axis=-1, keepdims=True)
        oh = lane == ik
        work = jnp.where(oh, -jnp.inf, work)
        top_v.append(mk)
        top_i.append(ik)
        onehots.append(oh)

    ex = [jnp.exp(v - top_v[0]) for v in top_v]
    denom = ex[0] + ex[1] + ex[2] + ex[3]
    gates = [e / denom for e in ex]

    sel = jnp.zeros((tm, LANES), F32)
    for oh in onehots:
        sel = sel + oh.astype(F32)
    r = lax.broadcasted_iota(jnp.int32, (tm, tm), 0)
    c = lax.broadcasted_iota(jnp.int32, (tm, tm), 1)
    tri = jnp.where(c < r, 1.0, 0.0).astype(BF16)
    cum = _dot(tri, sel.astype(BF16)) + carry_ref[...]
    carry_ref[...] = carry_ref[...] + jnp.sum(sel, axis=0, keepdims=True)
    cnt_ref[...] = carry_ref[...]

    ri = jnp.zeros((tm, LANES), F32)
    rf = jnp.zeros((tm, LANES), F32)
    for k in range(TOP_K):
        rank_k = jnp.sum(jnp.where(onehots[k], cum, 0.0), axis=-1, keepdims=True)
        ri = jnp.where(lane == float(k), top_i[k], ri)
        ri = jnp.where(lane == float(TOP_K + k), rank_k, ri)
        rf = jnp.where(lane == float(k), gates[k], rf)
    ri_ref[...] = ri.T[:2 * TOP_K].astype(jnp.int32)
    rf_ref[...] = rf


def _mix_out(convn, attn, x2, w, *, tm):
    t, d = x2.shape
    kern = functools.partial(_mix_out_kernel, tm=tm)
    row = lambda width: pl.BlockSpec((tm, width), lambda i: (i, 0))
    return pl.pallas_call(
        kern,
        grid=(t // tm,),
        in_specs=[
            row(D_CONV), row(ATTN_W), row(d),
            _resident((D_CONV + ATTN_W, d)),
            _resident((1, d)),
            _resident((d, 2 * LANES)),
            _resident((1, LANES)),
        ],
        out_specs=[row(d), row(d), pl.BlockSpec((2 * TOP_K, tm), lambda i: (0, i)), row(LANES),
                   pl.BlockSpec((1, LANES), lambda i: (0, 0))],
        out_shape=[
            jax.ShapeDtypeStruct((t, d), F32),
            jax.ShapeDtypeStruct((t, d), F32),
            jax.ShapeDtypeStruct((2 * TOP_K, t), jnp.int32),
            jax.ShapeDtypeStruct((t, LANES), F32),
            jax.ShapeDtypeStruct((1, LANES), F32),
        ],
        scratch_shapes=[pltpu.VMEM((1, LANES), F32)],
        compiler_params=pltpu.CompilerParams(
            dimension_semantics=("arbitrary",), vmem_limit_bytes=VMEM_LIMIT),
        name="mix_out",
    )(convn, attn, x2, w["wo"], w["g_ffn"], w["wr"], w["br"])


def _dispatch_kernel(dest_ref, pend_ref, xn_ref, xg_hbm, zbuf_ref, sem, zsem, *, tmd, sb):
    i = pl.program_id(0)

    @pl.when(i == 0)
    def _():
        zbuf_ref[...] = jnp.zeros_like(zbuf_ref)

        def zero_copy(e):
            end = pend_ref[e]
            start = pl.multiple_of(jnp.maximum(end - sb, 0), sb)
            return pltpu.make_async_copy(zbuf_ref, xg_hbm.at[pl.ds(start, sb)], zsem)

        def nonempty(e):
            prev = pend_ref[jnp.maximum(e - 1, 0)]
            return pend_ref[e] > jnp.where(e > 0, prev, 0)

        def start(e, carry):
            @pl.when(nonempty(e))
            def _():
                zero_copy(e).start()
            return carry

        def wait(e, carry):
            @pl.when(nonempty(e))
            def _():
                zero_copy(e).wait()
            return carry

        lax.fori_loop(0, N_EXPERTS, start, 0)
        lax.fori_loop(0, N_EXPERTS, wait, 0)
        _zero_tail(zbuf_ref, xg_hbm, pend_ref[N_EXPERTS - 1], zsem)

    base = i * tmd
    n_tok = pl.num_programs(0) * tmd

    def issue(r, carry):
        for k in range(TOP_K):
            dst = xg_hbm.at[pl.ds(dest_ref[k * n_tok + base + r], 1)]
            pltpu.make_async_copy(xn_ref.at[pl.ds(r, 1)], dst, sem).start(priority=k % 2)
        return carry

    lax.fori_loop(0, tmd, issue, 0)
    for k in range(TOP_K):
        pltpu.make_async_copy(xn_ref, xg_hbm.at[pl.ds(0, tmd)], sem).wait()


def _dispatch(dest, pad_end, xn, *, rows, tmd, sb):
    t, d = xn.shape
    kern = functools.partial(_dispatch_kernel, tmd=tmd, sb=sb)
    return pl.pallas_call(
        kern,
        grid_spec=pltpu.PrefetchScalarGridSpec(
            num_scalar_prefetch=2,
            grid=(t // tmd,),
            in_specs=[pl.BlockSpec((tmd, d), lambda i, dest, pend: (i, 0))],
            out_specs=pl.BlockSpec(memory_space=pl.ANY),
            scratch_shapes=[pltpu.VMEM((sb, d), F32), pltpu.SemaphoreType.DMA, pltpu.SemaphoreType.DMA],
        ),
        out_shape=jax.ShapeDtypeStruct((rows, d), F32),
        compiler_params=pltpu.CompilerParams(
            dimension_semantics=("arbitrary",), vmem_limit_bytes=VMEM_LIMIT),
        name="dispatch",
    )(dest, pad_end, xn)


def _expert_kernel(ie_ref, irow_ref, inr_ref, tail_ref, xg_hbm, wg_ref, wl_ref, wd_ref, bg_ref, bl_ref, bd_ref, yg_hbm,
                   xst_ref, xb_ref, act_ref, acc_ref, wgb_ref, wlb_ref, wdb_ref, sem_in, sem_out,
                   *, tm, sb, row_sizes):
    it = pl.program_id(0)
    s = pl.program_id(1)
    n_it = pl.num_programs(0)
    nrows = inr_ref[it]
    n_up = act_ref.shape[0]
    n_down, _, fc = acc_ref.shape
    last = n_up + n_down - 1
    prev_it = jnp.maximum(it - 1, 0)
    next_it = jnp.minimum(it + 1, n_it - 1)

    def piece(r):
        return pl.ds(r * sb, sb)

    def hbm_rows(item, r):
        return pl.ds(pl.multiple_of(irow_ref[item] + r * sb, sb), sb)

    def in_copy(item, r):
        return pltpu.make_async_copy(xg_hbm.at[hbm_rows(item, r)], xst_ref.at[piece(r)], sem_in.at[r])

    def out_copies(item, r):
        return [pltpu.make_async_copy(acc_ref.at[n, piece(r)], yg_hbm.at[hbm_rows(item, r), n * fc:(n + 1) * fc],
                                      sem_out.at[r]) for n in range(n_down)]

    def for_pieces_of(item, fn):
        for r in range(tm // sb):
            pl.when(r * sb < inr_ref[item])(functools.partial(fn, item, r))

    def for_valid_chunks(fn):
        for lo, size in zip((0,) + row_sizes[:-1], row_sizes):
            pl.when((lo < nrows) & (nrows <= size))(functools.partial(fn, pl.ds(0, size), True))

    def start_in(item, r):
        in_copy(item, r).start()

    def land_in(item, r):
        in_copy(item, r).wait()
        xb_ref[piece(r)] = xst_ref[piece(r)].astype(BF16)

    def start_out(item, r):
        for cp in out_copies(item, r):
            cp.start()

    def wait_out(item, r):
        for cp in out_copies(item, r):
            cp.wait()

    @pl.when((it == 0) & (s == 0))
    def _():
        xb_ref[...] = jnp.zeros_like(xb_ref)
        for_pieces_of(it, start_in)

    @pl.when(s == 0)
    def _():
        for_pieces_of(it, land_in)

    @pl.when((s == 1) & (it + 1 < n_it))
    def _():
        for_pieces_of(next_it, start_in)

    @pl.when((s == n_up) & (it > 0))
    def _():
        for_pieces_of(prev_it, wait_out)

    def up(rows, cast_weights):
        if cast_weights:
            wgb_ref[...] = wg_ref[0].astype(BF16)
            wlb_ref[...] = wl_ref[0].astype(BF16)
        xs = xb_ref[rows]
        g = jnp.minimum(_dot(xs, wgb_ref[...]) + bg_ref[0], SWIGLU_LIMIT)
        lin = jnp.clip(_dot(xs, wlb_ref[...]) + bl_ref[0], -SWIGLU_LIMIT, SWIGLU_LIMIT)
        act_ref[s, rows] = ((lin + 1.0) * (g * jax.nn.sigmoid(SWIGLU_ALPHA * g))).astype(BF16)

    def down(rows, cast_weights):
        if cast_weights:
            wdb_ref[...] = wd_ref[0].astype(BF16)
        act = jnp.concatenate([act_ref[jj, rows] for jj in range(n_up)], axis=-1)
        acc_ref[s - n_up, rows] = _dot(act, wdb_ref[...]) + bd_ref[0]

    @pl.when(s < n_up)
    def _():
        for_valid_chunks(up)

    @pl.when(s >= n_up)
    def _():
        for_valid_chunks(down)

    @pl.when(s == last)
    def _():
        for_pieces_of(it, start_out)

    @pl.when((it == n_it - 1) & (s == last))
    def _():
        for_pieces_of(it, wait_out)
        acc_ref[0, piece(0)] = jnp.zeros((sb, fc), F32)
        _zero_tail(acc_ref.at[0, piece(0)], yg_hbm, tail_ref[0], sem_out.at[0])


def _zero_tail(zeros_vmem, dst_hbm, first_row, sem):
    sb, width = zeros_vmem.shape
    n_blocks = (dst_hbm.shape[0] - first_row) // sb

    def fill(b, carry):
        rows = pl.ds(pl.multiple_of(first_row + b * sb, sb), sb)
        copies = [pltpu.make_async_copy(zeros_vmem, dst_hbm.at[rows, c * width:(c + 1) * width], sem)
                  for c in range(dst_hbm.shape[1] // width)]
        for cp in copies:
            cp.start()
        for cp in copies:
            cp.wait()
        return carry

    lax.fori_loop(0, n_blocks, fill, 0)


def _experts(n_used, item_e, item_row, item_nrows, tail, xg, w_gate_up, b_gate_up, w_down, b_down, *, tm, sb, fc):
    rows, d = xg.shape
    dff = w_down.shape[1]
    n_up = dff // fc
    fd = 2 * fc
    n_down = d // fd
    nsub = tm // sb
    kern = functools.partial(_expert_kernel, tm=tm, sb=sb, row_sizes=_row_sizes(tm, sb))

    def up_block(offset):
        def index_map(it, s, ie, ir, inr, tl):
            ahead = s >= n_up
            nxt = jnp.minimum(it + 1, tl[1] - 1)
            return jnp.where(ahead, ie[nxt], ie[it]), 0, offset + jnp.where(ahead, 0, s)
        return index_map

    def down_block(it, s, ie, ir, inr, tl):
        return ie[it], 0, jnp.maximum(s - n_up, 0)

    return pl.pallas_call(
        kern,
        grid_spec=pltpu.PrefetchScalarGridSpec(
            num_scalar_prefetch=4,
            grid=(n_used, n_up + n_down),
            in_specs=[
                pl.BlockSpec(memory_space=pl.ANY),
                pl.BlockSpec((1, d, fc), up_block(0)),
                pl.BlockSpec((1, d, fc), up_block(n_up)),
                pl.BlockSpec((1, dff, fd), down_block),
                pl.BlockSpec((1, 1, fc), up_block(0)),
                pl.BlockSpec((1, 1, fc), up_block(n_up)),
                pl.BlockSpec((1, 1, fd), down_block),
            ],
            out_specs=pl.BlockSpec(memory_space=pl.ANY),
            scratch_shapes=[
                pltpu.VMEM((tm, d), F32),
                pltpu.VMEM((tm, d), BF16),
                pltpu.VMEM((n_up, tm, fc), BF16),
                pltpu.VMEM((n_down, tm, fd), F32),
                pltpu.VMEM((d, fc), BF16),
                pltpu.VMEM((d, fc), BF16),
                pltpu.VMEM((dff, fd), BF16),
                pltpu.SemaphoreType.DMA((nsub,)),
                pltpu.SemaphoreType.DMA((nsub,)),
            ],
        ),
        out_shape=jax.ShapeDtypeStruct((rows, d), F32),
        compiler_params=pltpu.CompilerParams(
            dimension_semantics=("arbitrary", "arbitrary"), vmem_limit_bytes=VMEM_LIMIT),
        name="experts",
    )(item_e, item_row, item_nrows, tail, xg, w_gate_up, w_gate_up, w_down,
      b_gate_up, b_gate_up, b_down)


def _combine_kernel(dest_ref, yg_hbm, h1_ref, rf_ref, gfin_ref, o_ref, gbuf_ref, sem, *, tmc):
    i = pl.program_id(0)
    n_tiles = pl.num_programs(0)
    n_tok = n_tiles * tmc

    def issue_tile(tile):
        buf = tile % 2
        base = tile * tmc

        def issue(r, carry):
            for k in range(TOP_K):
                src = yg_hbm.at[pl.ds(dest_ref[k * n_tok + base + r], 1)]
                pltpu.make_async_copy(src, gbuf_ref.at[buf, k, pl.ds(r, 1)], sem.at[buf]).start(priority=k % 2)
            return carry

        lax.fori_loop(0, tmc, issue, 0)

    @pl.when(i == 0)
    def _():
        issue_tile(i)

    @pl.when(i + 1 < n_tiles)
    def _():
        issue_tile(i + 1)

    buf = i % 2
    for k in range(TOP_K):
        pltpu.make_async_copy(yg_hbm.at[pl.ds(0, tmc)], gbuf_ref.at[buf, k], sem.at[buf]).wait()

    gates = rf_ref[...]
    y = h1_ref[...]
    for k in range(TOP_K):
        y = y + gates[:, k:k + 1] * gbuf_ref[buf, k]
    o_ref[...] = _rms(y, gfin_ref[...])


def _combine(dest, yg, h1, rf, g_final, *, tmc):
    t, d = h1.shape
    kern = functools.partial(_combine_kernel, tmc=tmc)
    return pl.pallas_call(
        kern,
        grid_spec=pltpu.PrefetchScalarGridSpec(
            num_scalar_prefetch=1,
            grid=(t // tmc,),
            in_specs=[
                pl.BlockSpec(memory_space=pl.ANY),
                pl.BlockSpec((tmc, d), lambda i, dest: (i, 0)),
                pl.BlockSpec((tmc, LANES), lambda i, dest: (i, 0)),
                pl.BlockSpec((1, d), lambda i, dest: (0, 0)),
            ],
            out_specs=pl.BlockSpec((tmc, d), lambda i, dest: (i, 0)),
            scratch_shapes=[pltpu.VMEM((2, TOP_K, tmc, d), F32), pltpu.SemaphoreType.DMA((2,))],
        ),
        out_shape=jax.ShapeDtypeStruct((t, d), F32),
        compiler_params=pltpu.CompilerParams(
            dimension_semantics=("arbitrary",), vmem_limit_bytes=VMEM_LIMIT),
        name="combine",
    )(dest, yg, h1, rf, g_final)


def _cast_kernel(x_ref, o_ref):
    o_ref[...] = x_ref[...].astype(o_ref.dtype)


def _to_bf16(w, *, n_blocks):
    rows, cols = w.shape
    spec = pl.BlockSpec((rows // n_blocks, cols), lambda i: (i, 0))
    return pl.pallas_call(
        _cast_kernel, grid=(n_blocks,), in_specs=[spec], out_specs=spec,
        out_shape=jax.ShapeDtypeStruct(w.shape, BF16),
        compiler_params=pltpu.CompilerParams(dimension_semantics=("arbitrary",), vmem_limit_bytes=VMEM_LIMIT),
        name="to_bf16",
    )(w)


def _rotate_half_cols(w):
    half = w.shape[-1] // 2
    return jnp.concatenate([-w[..., half:], w[..., :half]], axis=-1)


def _prep_weights(g_mix, w_in, conv_w, g_q, w_q_up, g_kv, w_kv_up, g_conv_out, w_out, g_ffn, w_router, b_router):
    d = w_in.shape[0]
    w1 = _to_bf16(w_in.T, n_blocks=4)
    k_rope = w1[O_KPE:O_KPE + QK_ROPE]
    pad = jnp.zeros((LANES - QK_ROPE, d), BF16)
    half = QK_ROPE // 2
    k_rot = jnp.concatenate([-k_rope[half:], k_rope[:half]], axis=0)
    wk2 = jnp.concatenate([k_rope, pad, k_rot, pad], axis=0)

    wq = w_q_up.reshape(Q_LORA, N_HEADS, QK_NOPE + QK_ROPE)
    nope, pe = wq[:, :, :QK_NOPE], wq[:, :, QK_NOPE:]
    hpad = jnp.zeros((Q_LORA, N_HEADS, LANES - QK_ROPE), F32)
    wqa = jnp.concatenate([nope, pe, hpad], axis=2).reshape(Q_LORA, N_HEADS * HEAD_W)
    wqb = jnp.concatenate([_rotate_half_cols(pe), hpad], axis=2).reshape(Q_LORA, N_HEADS * LANES)
    wq_all = jnp.concatenate([wqa, wqb], axis=1).astype(BF16)

    wkv = w_kv_up.reshape(KV_LORA, N_HEADS, QK_NOPE + V_DIM)
    wkv2 = jnp.concatenate([wkv[:, :, :QK_NOPE].reshape(KV_LORA, ATTN_W),
                            wkv[:, :, QK_NOPE:].reshape(KV_LORA, ATTN_W)], axis=1).astype(BF16)

    wr_pad = jnp.zeros((d, LANES), F32).at[:, :N_EXPERTS].set(w_router)
    wr_hi = wr_pad.astype(BF16)
    wr_lo = (wr_pad - wr_hi.astype(F32)).astype(BF16)
    br = jnp.full((1, LANES), -1e30, F32).at[0, :N_EXPERTS].set(b_router)

    grp = jnp.arange(D_CONV) // (D_CONV // CONV_GROUPS)
    cw = jnp.zeros((SUBLANES, D_CONV), F32).at[:CONV_WIDTH].set(conv_w)
    return {
        "g_mix": g_mix[None], "w1": w1, "wk2": wk2, "conv_w": cw, "g_q": g_q[None], "wq": wq_all, "g_kv": g_kv[None],
        "wkv": wkv2, "g_conv_out": g_conv_out[None], "gmat": (grp[:, None] == grp[None, :]).astype(BF16),
        "wo": w_out.astype(BF16), "g_ffn": g_ffn[None], "wr": jnp.concatenate([wr_hi, wr_lo], axis=1), "br": br,
    }


def _rope_table(pos):
    half = QK_ROPE // 2
    inv_freq = np.float32(ROPE_THETA) ** (-np.arange(half, dtype=np.float32) / np.float32(half))
    ang = (np.asarray(pos, np.float32)[:, None] * inv_freq[None, :]).astype(np.float32)
    c, s = np.cos(ang), np.sin(ang)
    z = np.zeros((ang.shape[0], LANES - QK_ROPE), np.float32)
    return jnp.asarray(np.concatenate([c, c, z, s, s, z], axis=1), dtype=F32)


def _schedule_kernel(cnt_ref, ri_ref, dest_ref, pend_ref, ie_ref, irow_ref, inr_ref, tail_ref, pstart_ref,
                     *, tm, sb, n_items):
    def per_expert(e, carry):
        row, item = carry
        count = cnt_ref[e]
        pstart_ref[e] = row

        def per_item(li, item):
            ie_ref[item] = e
            irow_ref[item] = row + li * tm
            inr_ref[item] = jnp.minimum(count - li * tm, tm)
            return item + 1

        item = lax.fori_loop(0, lax.div(count + (tm - 1), tm), per_item, item)
        row = row + lax.div(count + (sb - 1), sb) * sb
        pend_ref[e] = row
        return row, item

    row, n_used = lax.fori_loop(0, N_EXPERTS, per_expert, (jnp.int32(0), jnp.int32(0)))
    tail_ref[0] = row
    tail_ref[1] = n_used

    def unused(item, carry):
        ie_ref[item] = 0
        irow_ref[item] = 0
        inr_ref[item] = 0
        return carry

    lax.fori_loop(n_used, n_items, unused, 0)

    eidx = ri_ref[:TOP_K]
    start_of = jnp.zeros_like(eidx)
    for e in range(N_EXPERTS):
        start_of = jnp.where(eidx == e, pstart_ref[e], start_of)
    dest_ref[...] = start_of + ri_ref[TOP_K:]


def _schedule(counts, ri, *, tm, sb, n_items):
    t = ri.shape[1]
    smem = pl.BlockSpec(memory_space=pltpu.SMEM)
    i32 = lambda n: jax.ShapeDtypeStruct((n,), jnp.int32)
    dest, pad_end, item_e, item_row, item_nrows, tail = pl.pallas_call(
        functools.partial(_schedule_kernel, tm=tm, sb=sb, n_items=n_items),
        in_specs=[smem, pl.BlockSpec(memory_space=pltpu.VMEM)],
        out_specs=[pl.BlockSpec(memory_space=pltpu.VMEM), smem, smem, smem, smem, smem],
        out_shape=[jax.ShapeDtypeStruct((TOP_K, t), jnp.int32), i32(N_EXPERTS), i32(n_items), i32(n_items),
                   i32(n_items), i32(2)],
        scratch_shapes=[pltpu.SMEM((N_EXPERTS,), jnp.int32)],
        name="schedule",
    )(counts, ri)
    return dest.reshape(-1), pad_end, item_e, item_row, item_nrows, tail


def _row_sizes(tm, sb):
    fine = [tm - k * sb for k in range(7) if tm - k * sb > 0]
    coarse = [s for s in (tm // 4 // sb * sb, tm // 2 // sb * sb) if 0 < s < min(fine)]
    return tuple(sorted(set(coarse + fine)))


def _moe_tiles(t):
    sb = 64
    tm = 20 * sb
    a = t * TOP_K
    rows = (a + N_EXPERTS * (sb - 1) + sb - 1) // sb * sb
    n_items = N_EXPERTS + a // tm
    return sb, tm, rows, n_items


def _layer(x, meta_tokens, w, w_gate_up, b_gate_up, w_down, b_down, g_attn_out, g_final, *,
           tm_in, tq, tm_out, tmd, tmc, fc):
    b, seq, d = x.shape
    t = b * seq

    meta_blk = jnp.zeros((1, META_ROWS, d), F32).at[0, META_ROWS - N_META:].set(meta_tokens)
    meta_pos = np.maximum(np.arange(META_ROWS) - (META_ROWS - N_META), 0)
    zero_tail = jnp.zeros((SUBLANES, D_CONV), F32)
    _, _, mkn, mkpe, mv, u_tail = _mix_in(meta_blk, zero_tail, _rope_table(meta_pos), w, tm=META_ROWS)

    real_pos = np.arange(seq) + N_META
    convn, q, kn, kpe, v, _ = _mix_in(x, u_tail, _rope_table(real_pos), w, tm=tm_in)
    attn = _attention(q, kn, kpe, v, mkn, mkpe, mv, g_attn_out[None], tq=tq)

    h1, xn, ri, rf, cnt = _mix_out(convn.reshape(t, D_CONV), attn.reshape(t, ATTN_W), x.reshape(t, d), w, tm=tm_out)

    sb, tm_e, rows, n_items = _moe_tiles(t)
    counts = cnt[0, :N_EXPERTS].astype(jnp.int32)
    dest, pad_end, item_e, item_row, item_nrows, tail = _schedule(counts, ri, tm=tm_e, sb=sb, n_items=n_items)

    xg = _dispatch(dest, pad_end, xn, rows=rows, tmd=tmd, sb=sb)
    dff = w_down.shape[1]
    yg = _experts(tail[1], item_e, item_row, item_nrows, tail, xg, w_gate_up, b_gate_up.reshape(N_EXPERTS, 1, 2 * dff),
                  w_down, b_down.reshape(N_EXPERTS, 1, d), tm=tm_e, sb=sb, fc=fc)
    out = _combine(dest, yg, h1, rf, g_final[None], tmc=tmc)
    return out.reshape(b, seq, d)


def kernel(x, meta_tokens, g_mix, w_in, conv_w, g_q, w_q_up, g_kv, w_kv_up, g_conv_out, g_attn_out, w_out, g_ffn,
           w_router, b_router, w_gate_up, b_gate_up, w_down, b_down, g_final):
    w = _prep_weights(g_mix[0], w_in[0], conv_w[0], g_q[0], w_q_up[0], g_kv[0], w_kv_up[0], g_conv_out[0],
                      w_out[0], g_ffn[0], w_router[0], b_router[0])
    seq = x.shape[1]
    return _layer(x, meta_tokens, w, w_gate_up[0], b_gate_up[0], w_down[0], b_down[0], g_attn_out[0], g_final,
                  tm_in=min(512, seq), tq=min(512, seq), tm_out=min(512, seq), tmd=min(512, seq),
                  tmc=min(128, seq), fc=256)
```

```python
import functools

import jax
import jax.numpy as jnp
import numpy as np
from jax import lax
from jax.experimental import pallas as pl
from jax.experimental.pallas import tpu as pltpu

N_META = 16
EPS = 1e-6
D_CONV = 1024
CONV_GROUPS = 16
CONV_WIDTH = 3
N_HEADS = 8
QK_NOPE = 128
QK_ROPE = 64
V_DIM = 128
Q_LORA = 512
KV_LORA = 256
ROPE_THETA = 10000.0
N_EXPERTS = 32
TOP_K = 4
SWIGLU_LIMIT = 7.0
SWIGLU_ALPHA = 1.702

LANES = 128
SUBLANES = 8
META_ROWS = 128
HEAD_W = 2 * LANES
ATTN_W = N_HEADS * V_DIM
V_EXT = V_DIM + 16
VMEM_LIMIT = 56 * 1024 * 1024

O_B, O_C, O_U = 0, D_CONV, 2 * D_CONV
O_Q = 3 * D_CONV
O_KV = O_Q + Q_LORA
O_KPE = O_KV + KV_LORA
D_IN = O_KPE + QK_ROPE

F32 = jnp.float32
BF16 = jnp.bfloat16
NT_DIMS = (((1,), (1,)), ((), ()))


def _rms(x, g):
    return x * lax.rsqrt(jnp.mean(x * x, axis=-1, keepdims=True) + EPS) * g


def _dot(a, b):
    return jnp.dot(a, b, preferred_element_type=F32)


def _resident(shape):
    zeros = (0,) * len(shape)
    return pl.BlockSpec(shape, lambda *_: zeros, pipeline_mode=pl.Buffered(1))


def _mix_in_kernel(x_ref, gmix_ref, w1_ref, wk2_ref, cw_ref, gq_ref, wq_ref, gkv_ref, wkv_ref, gco_ref, gmat_ref,
                   tab_ref, uinit_ref,
                   convn_ref, qt_ref, kn_ref, kpe_ref, vt_ref, utail_ref,
                   ubuf_ref, *, tm, scale):
    i = pl.program_id(1)
    hn = _rms(x_ref[0], gmix_ref[...]).astype(BF16)

    def proj(lo, hi):
        return lax.dot_general(hn, w1_ref[lo:hi, :], NT_DIMS, preferred_element_type=F32)

    @pl.when(i == 0)
    def _():
        ubuf_ref[0:SUBLANES] = uinit_ref[...]

    @pl.when(i > 0)
    def _():
        ubuf_ref[0:SUBLANES] = ubuf_ref[tm:tm + SUBLANES]

    u = proj(O_C, O_U) * proj(O_U, O_Q)
    ubuf_ref[SUBLANES:SUBLANES + tm] = u
    cw = cw_ref[...]
    y = (cw[2:3] * u + cw[1:2] * ubuf_ref[SUBLANES - 1:SUBLANES - 1 + tm]
         + cw[0:1] * ubuf_ref[SUBLANES - 2:SUBLANES - 2 + tm])
    co = proj(O_B, O_C) * y
    ss = _dot((co * co).astype(BF16), gmat_ref[...])
    group = D_CONV // CONV_GROUPS
    convn_ref[0] = (co * lax.rsqrt(ss * (1.0 / group) + EPS) * gco_ref[...]).astype(BF16)
    utail_ref[...] = ubuf_ref[tm:tm + SUBLANES]

    cos = tab_ref[:, :LANES]
    sin = tab_ref[:, LANES:]

    qn = _rms(proj(O_Q, O_KV), gq_ref[...]).astype(BF16)
    qa = _dot(qn, wq_ref[:, :N_HEADS * HEAD_W])
    qb = _dot(qn, wq_ref[:, N_HEADS * HEAD_W:])
    for h in range(N_HEADS):
        c0 = h * HEAD_W
        qt_ref[0, c0:c0 + LANES, :] = (qa[:, c0:c0 + LANES] * scale).T.astype(BF16)
        pe = qa[:, c0 + LANES:c0 + HEAD_W] * cos + qb[:, h * LANES:(h + 1) * LANES] * sin
        qt_ref[0, c0 + LANES:c0 + HEAD_W, :] = (pe * scale).T.astype(BF16)

    kvn = _rms(proj(O_KV, O_KPE), gkv_ref[...]).astype(BF16)
    kv = _dot(kvn, wkv_ref[...])
    kn_ref[0] = kv[:, :ATTN_W].astype(BF16)
    vt = kv[:, ATTN_W:].T.astype(BF16)
    ones = jnp.ones((V_EXT - V_DIM, tm), BF16)
    for h in range(N_HEADS):
        vt_ref[0, 0, h * V_EXT:h * V_EXT + V_DIM, :] = vt[h * V_DIM:(h + 1) * V_DIM]
        vt_ref[0, 0, h * V_EXT + V_DIM:(h + 1) * V_EXT, :] = ones
    kk = lax.dot_general(hn, wk2_ref[...], NT_DIMS, preferred_element_type=F32)
    kpe_ref[0] = (kk[:, :LANES] * cos + kk[:, LANES:] * sin).astype(BF16)


def _mix_in(x3, uinit, tab, w, *, tm):
    b, l, d = x3.shape
    nt = l // tm
    kern = functools.partial(_mix_in_kernel, tm=tm, scale=float((QK_NOPE + QK_ROPE) ** -0.5))
    row = lambda width: pl.BlockSpec((1, tm, width), lambda bi, i: (bi, i, 0))
    return pl.pallas_call(
        kern,
        grid=(b, nt),
        in_specs=[
            row(d),
            _resident((1, d)),
            _resident((D_IN, d)),
            _resident((2 * LANES, d)),
            _resident((SUBLANES, D_CONV)),
            _resident((1, Q_LORA)),
            _resident((Q_LORA, N_HEADS * (HEAD_W + LANES))),
            _resident((1, KV_LORA)),
            _resident((KV_LORA, 2 * ATTN_W)),
            _resident((1, D_CONV)),
            _resident((D_CONV, D_CONV)),
            pl.BlockSpec((tm, 2 * LANES), lambda bi, i: (i, 0)),
            _resident((SUBLANES, D_CONV)),
        ],
        out_specs=[
            row(D_CONV),
            pl.BlockSpec((1, N_HEADS * HEAD_W, tm), lambda bi, i: (bi, 0, i)),
            row(ATTN_W), row(LANES),
            pl.BlockSpec((1, 1, N_HEADS * V_EXT, tm), lambda bi, i: (bi, i, 0, 0)),
            pl.BlockSpec((SUBLANES, D_CONV), lambda bi, i: (bi * nt + i, 0)),
        ],
        out_shape=[
            jax.ShapeDtypeStruct((b, l, D_CONV), BF16),
            jax.ShapeDtypeStruct((b, N_HEADS * HEAD_W, l), BF16),
            jax.ShapeDtypeStruct((b, l, ATTN_W), BF16),
            jax.ShapeDtypeStruct((b, l, LANES), BF16),
            jax.ShapeDtypeStruct((b, nt, N_HEADS * V_EXT, tm), BF16),
            jax.ShapeDtypeStruct((b * nt * SUBLANES, D_CONV), F32),
        ],
        scratch_shapes=[pltpu.VMEM((tm + SUBLANES, D_CONV), F32)],
        compiler_params=pltpu.CompilerParams(
            dimension_semantics=("arbitrary", "arbitrary"), vmem_limit_bytes=VMEM_LIMIT),
        name="mix_in",
    )(x3, w["g_mix"], w["w1"], w["wk2"], w["conv_w"], w["g_q"], w["wq"], w["g_kv"], w["wkv"], w["g_conv_out"],
      w["gmat"], tab, uinit)


def _attn_kernel(qt_ref, kn_ref, kpe_ref, vt_ref, mkn_ref, mkpe_ref, mvt_ref, g_ref, o_ref,
                 m_ref, acc_ref, *, tq, hps):
    qi = pl.program_id(2)
    tv = vt_ref.shape[3]

    def lanes(hh):
        return slice(hh * LANES, (hh + 1) * LANES)

    def vrows(hh):
        return slice(hh * V_EXT, (hh + 1) * V_EXT)

    def qt(hh):
        return qt_ref[0, hh * HEAD_W:(hh + 1) * HEAD_W, :]

    for hh in range(hps):
        km = jnp.concatenate([mkn_ref[0, :, lanes(hh)], mkpe_ref[0]], axis=-1)
        s = _dot(km, qt(hh))
        row = lax.broadcasted_iota(jnp.int32, s.shape, 0)
        s = jnp.where(row >= META_ROWS - N_META, s, -jnp.inf)
        m0 = jnp.max(s, axis=0, keepdims=True)
        m_ref[hh] = m0
        acc_ref[hh] = _dot(mvt_ref[0, 0, vrows(hh), :], jnp.exp((s - m0).astype(BF16)))

    def step(kb, diagonal):
        off = pl.multiple_of(kb * tq, tq)
        kpe = kpe_ref[0, pl.ds(off, tq), :]
        scores = []
        for hh in range(hps):
            k = jnp.concatenate([kn_ref[0, pl.ds(off, tq), lanes(hh)], kpe], axis=-1)
            s = _dot(k, qt(hh))
            if diagonal:
                r = lax.broadcasted_iota(jnp.int32, s.shape, 0)
                c = lax.broadcasted_iota(jnp.int32, s.shape, 1)
                s = jnp.where(r <= c, s, -jnp.inf)
            scores.append(s)
        probs, alphas = [], []
        for hh in range(hps):
            m_prev = m_ref[hh]
            m_new = jnp.maximum(m_prev, jnp.max(scores[hh], axis=0, keepdims=True))
            alphas.append(jnp.exp(m_prev - m_new))
            probs.append(jnp.exp((scores[hh] - m_new).astype(BF16)))
            m_ref[hh] = m_new
        for hh in range(hps):
            pb = probs[hh]
            pv = _dot(vt_ref[0, kb * (tq // tv), vrows(hh), :], pb[:tv])
            for c in range(1, tq // tv):
                pv = pv + _dot(vt_ref[0, kb * (tq // tv) + c, vrows(hh), :], pb[c * tv:(c + 1) * tv])
            acc_ref[hh] = alphas[hh] * acc_ref[hh] + pv

    def body(kb, carry):
        step(kb, False)
        return carry

    lax.fori_loop(0, qi, body, 0)
    step(qi, True)

    for hh in range(hps):
        acc = acc_ref[hh]
        o = (acc[:V_DIM] / acc[V_DIM:V_DIM + 1]).T
        o_ref[0, :, lanes(hh)] = _rms(o, g_ref[:, lanes(hh)]).astype(BF16)


def _attention(qt, kn, kpe, vt, mkn, mkpe, mvt, g_attn, *, tq, hps=4):
    b, l, _ = kn.shape
    _, nt, _, tv = vt.shape
    nq = l // tq
    kern = functools.partial(_attn_kernel, tq=tq, hps=hps)
    return pl.pallas_call(
        kern,
        grid=(b, N_HEADS // hps, nq),
        in_specs=[
            pl.BlockSpec((1, hps * HEAD_W, tq), lambda bi, h, i: (bi, h, i)),
            pl.BlockSpec((1, l, hps * LANES), lambda bi, h, i: (bi, 0, h)),
            pl.BlockSpec((1, l, LANES), lambda bi, h, i: (bi, 0, 0)),
            pl.BlockSpec((1, nt, hps * V_EXT, tv), lambda bi, h, i: (bi, 0, h, 0)),
            pl.BlockSpec((1, META_ROWS, hps * LANES), lambda bi, h, i: (0, 0, h)),
            pl.BlockSpec((1, META_ROWS, LANES), lambda bi, h, i: (0, 0, 0)),
            pl.BlockSpec((1, 1, hps * V_EXT, META_ROWS), lambda bi, h, i: (0, 0, h, 0)),
            pl.BlockSpec((1, hps * V_DIM), lambda bi, h, i: (0, h)),
        ],
        out_specs=pl.BlockSpec((1, tq, hps * V_DIM), lambda bi, h, i: (bi, i, h)),
        out_shape=jax.ShapeDtypeStruct((b, l, ATTN_W), BF16),
        scratch_shapes=[pltpu.VMEM((hps, 1, tq), F32), pltpu.VMEM((hps, V_EXT, tq), F32)],
        compiler_params=pltpu.CompilerParams(
            dimension_semantics=("arbitrary", "arbitrary", "arbitrary"), vmem_limit_bytes=VMEM_LIMIT),
        name="attn",
    )(qt, kn, kpe, vt, mkn, mkpe, mvt, g_attn)


def _mix_out_kernel(convn_ref, attn_ref, x_ref, wo_ref, gffn_ref, wr_ref, br_ref,
                    h1_ref, xn_ref, ri_ref, rf_ref, cnt_ref, carry_ref, *, tm):
    @pl.when(pl.program_id(0) == 0)
    def _():
        carry_ref[...] = jnp.zeros_like(carry_ref)

    h1 = x_ref[...] + _dot(convn_ref[...], wo_ref[:D_CONV]) + _dot(attn_ref[...], wo_ref[D_CONV:])
    h1_ref[...] = h1
    xn = _rms(h1, gffn_ref[...])
    xn_ref[...] = xn

    xh = xn.astype(BF16)
    xl = (xn - xh.astype(F32)).astype(BF16)
    ph = _dot(xh, wr_ref[...])
    logits = ph[:, :LANES] + ph[:, LANES:] + _dot(xl, wr_ref[:, :LANES]) + br_ref[...]

    lane = lax.broadcasted_iota(jnp.int32, (tm, LANES), 1).astype(F32)
    work = logits
    top_v, top_i, onehots = [], [], []
    for _ in range(TOP_K):
        mk = jnp.max(work, axis=-1, keepdims=True)
        ik = jnp.min(jnp.where(work == mk, lane, float(LANES)), axis=-1, keepdims=True)
        oh = lane == ik
        work = jnp.where(oh, -jnp.inf, work)
        top_v.append(mk)
        top_i.append(ik)
        onehots.append(oh)

    ex = [jnp.exp(v - top_v[0]) for v in top_v]
    denom = ex[0] + ex[1] + ex[2] + ex[3]
    gates = [e / denom for e in ex]

    sel = jnp.zeros((tm, LANES), F32)
    for oh in onehots:
        sel = sel + oh.astype(F32)
    r = lax.broadcasted_iota(jnp.int32, (tm, tm), 0)
    c = lax.broadcasted_iota(jnp.int32, (tm, tm), 1)
    tri = jnp.where(c < r, 1.0, 0.0).astype(BF16)
    cum = _dot(tri, sel.astype(BF16)) + carry_ref[...]
    carry_ref[...] = carry_ref[...] + jnp.sum(sel, axis=0, keepdims=True)
    cnt_ref[...] = carry_ref[...]

    ri = jnp.zeros((tm, LANES), F32)
    rf = jnp.zeros((tm, LANES), F32)
    for k in range(TOP_K):
        rank_k = jnp.sum(jnp.where(onehots[k], cum, 0.0), axis=-1, keepdims=True)
        ri = jnp.where(lane == float(k), top_i[k], ri)
        ri = jnp.where(lane == float(TOP_K + k), rank_k, ri)
        rf = jnp.where(lane == float(k), gates[k], rf)
    ri_ref[...] = ri.T[:2 * TOP_K].astype(jnp.int32)
    rf_ref[...] = rf


def _mix_out(convn, attn, x2, w, *, tm):
    t, d = x2.shape
    kern = functools.partial(_mix_out_kernel, tm=tm)
    row = lambda width: pl.BlockSpec((tm, width), lambda i: (i, 0))
    return pl.pallas_call(
        kern,
        grid=(t // tm,),
        in_specs=[
            row(D_CONV), row(ATTN_W), row(d),
            _resident((D_CONV + ATTN_W, d)),
            _resident((1, d)),
            _resident((d, 2 * LANES)),
            _resident((1, LANES)),
        ],
        out_specs=[row(d), row(d), pl.BlockSpec((2 * TOP_K, tm), lambda i: (0, i)), row(LANES),
                   pl.BlockSpec((1, LANES), lambda i: (0, 0))],
        out_shape=[
            jax.ShapeDtypeStruct((t, d), F32),
            jax.ShapeDtypeStruct((t, d), F32),
            jax.ShapeDtypeStruct((2 * TOP_K, t), jnp.int32),
            jax.ShapeDtypeStruct((t, LANES), F32),
            jax.ShapeDtypeStruct((1, LANES), F32),
        ],
        scratch_shapes=[pltpu.VMEM((1, LANES), F32)],
        compiler_params=pltpu.CompilerParams(
            dimension_semantics=("arbitrary",), vmem_limit_bytes=VMEM_LIMIT),
        name="mix_out",
    )(convn, attn, x2, w["wo"], w["g_ffn"], w["wr"], w["br"])


def _dispatch_kernel(dest_ref, pend_ref, xn_ref, xg_hbm, zbuf_ref, sem, zsem, *, tmd, sb):
    i = pl.program_id(0)

    @pl.when(i == 0)
    def _():
        zbuf_ref[...] = jnp.zeros_like(zbuf_ref)

        def zero_copy(e):
            end = pend_ref[e]
            start = pl.multiple_of(jnp.maximum(end - sb, 0), sb)
            return pltpu.make_async_copy(zbuf_ref, xg_hbm.at[pl.ds(start, sb)], zsem)

        def nonempty(e):
            prev = pend_ref[jnp.maximum(e - 1, 0)]
            return pend_ref[e] > jnp.where(e > 0, prev, 0)

        def start(e, carry):
            @pl.when(nonempty(e))
            def _():
                zero_copy(e).start()
            return carry

        def wait(e, carry):
            @pl.when(nonempty(e))
            def _():
                zero_copy(e).wait()
            return carry

        lax.fori_loop(0, N_EXPERTS, start, 0)
        lax.fori_loop(0, N_EXPERTS, wait, 0)
        _zero_tail(zbuf_ref, xg_hbm, pend_ref[N_EXPERTS - 1], zsem)

    base = i * tmd
    n_tok = pl.num_programs(0) * tmd

    def issue(r, carry):
        for k in range(TOP_K):
            dst = xg_hbm.at[pl.ds(dest_ref[k * n_tok + base + r], 1)]
            pltpu.make_async_copy(xn_ref.at[pl.ds(r, 1)], dst, sem).start(priority=k % 2)
        return carry

    lax.fori_loop(0, tmd, issue, 0)
    for k in range(TOP_K):
        pltpu.make_async_copy(xn_ref, xg_hbm.at[pl.ds(0, tmd)], sem).wait()


def _dispatch(dest, pad_end, xn, *, rows, tmd, sb):
    t, d = xn.shape
    kern = functools.partial(_dispatch_kernel, tmd=tmd, sb=sb)
    return pl.pallas_call(
        kern,
        grid_spec=pltpu.PrefetchScalarGridSpec(
            num_scalar_prefetch=2,
            grid=(t // tmd,),
            in_specs=[pl.BlockSpec((tmd, d), lambda i, dest, pend: (i, 0))],
            out_specs=pl.BlockSpec(memory_space=pl.ANY),
            scratch_shapes=[pltpu.VMEM((sb, d), F32), pltpu.SemaphoreType.DMA, pltpu.SemaphoreType.DMA],
        ),
        out_shape=jax.ShapeDtypeStruct((rows, d), F32),
        compiler_params=pltpu.CompilerParams(
            dimension_semantics=("arbitrary",), vmem_limit_bytes=VMEM_LIMIT),
        name="dispatch",
    )(dest, pad_end, xn)


def _expert_kernel(ie_ref, irow_ref, inr_ref, tail_ref, xg_hbm, wg_ref, wl_ref, wd_ref, bg_ref, bl_ref, bd_ref, yg_hbm,
                   xst_ref, xb_ref, act_ref, acc_ref, sem_in, sem_out,
                   *, tm, sb, row_sizes):
    it = pl.program_id(0)
    s = pl.program_id(1)
    n_it = pl.num_programs(0)
    nrows = inr_ref[it]
    n_up = act_ref.shape[0]
    n_down, _, fc = acc_ref.shape
    last = n_up + n_down - 1
    prev_it = jnp.maximum(it - 1, 0)
    next_it = jnp.minimum(it + 1, n_it - 1)

    def piece(r):
        return pl.ds(r * sb, sb)

    def hbm_rows(item, r):
        return pl.ds(pl.multiple_of(irow_ref[item] + r * sb, sb), sb)

    def in_copy(item, r):
        return pltpu.make_async_copy(xg_hbm.at[hbm_rows(item, r)], xst_ref.at[piece(r)], sem_in.at[r])

    def out_copies(item, r):
        return [pltpu.make_async_copy(acc_ref.at[n, piece(r)], yg_hbm.at[hbm_rows(item, r), n * fc:(n + 1) * fc],
                                      sem_out.at[r]) for n in range(n_down)]

    def for_pieces_of(item, fn):
        for r in range(tm // sb):
            pl.when(r * sb < inr_ref[item])(functools.partial(fn, item, r))

    def for_valid_chunks(fn):
        for lo, size in zip((0,) + row_sizes[:-1], row_sizes):
            pl.when((lo < nrows) & (nrows <= size))(functools.partial(fn, pl.ds(0, size)))

    def start_in(item, r):
        in_copy(item, r).start()

    def land_in(item, r):
        in_copy(item, r).wait()
        xb_ref[piece(r)] = xst_ref[piece(r)].astype(BF16)

    def start_out(item, r):
        for cp in out_copies(item, r):
            cp.start()

    def wait_out(item, r):
        for cp in out_copies(item, r):
            cp.wait()

    @pl.when((it == 0) & (s == 0))
    def _():
        xb_ref[...] = jnp.zeros_like(xb_ref)
        for_pieces_of(it, start_in)

    @pl.when(s == 0)
    def _():
        for_pieces_of(it, land_in)

    @pl.when((s == 1) & (it + 1 < n_it))
    def _():
        for_pieces_of(next_it, start_in)

    @pl.when((s == n_up) & (it > 0))
    def _():
        for_pieces_of(prev_it, wait_out)

    def up(rows):
        xs = xb_ref[rows]
        g = jnp.minimum(_dot(xs, wg_ref[0].astype(BF16)) + bg_ref[0], SWIGLU_LIMIT)
        lin = jnp.clip(_dot(xs, wl_ref[0].astype(BF16)) + bl_ref[0], -SWIGLU_LIMIT, SWIGLU_LIMIT)
        act_ref[s, rows] = ((lin + 1.0) * (g * jax.nn.sigmoid(SWIGLU_ALPHA * g))).astype(BF16)

    def down(rows):
        act = jnp.concatenate([act_ref[jj, rows] for jj in range(n_up)], axis=-1)
        acc_ref[s - n_up, rows] = _dot(act, wd_ref[0].astype(BF16)) + bd_ref[0]

    @pl.when(s < n_up)
    def _():
        for_valid_chunks(up)

    @pl.when(s >= n_up)
    def _():
        for_valid_chunks(down)

    @pl.when(s == last)
    def _():
        for_pieces_of(it, start_out)

    @pl.when((it == n_it - 1) & (s == last))
    def _():
        for_pieces_of(it, wait_out)
        acc_ref[0, piece(0)] = jnp.zeros((sb, fc), F32)
        _zero_tail(acc_ref.at[0, piece(0)], yg_hbm, tail_ref[0], sem_out.at[0])


def _zero_tail(zeros_vmem, dst_hbm, first_row, sem):
    sb, width = zeros_vmem.shape
    n_blocks = (dst_hbm.shape[0] - first_row) // sb

    def fill(b, carry):
        rows = pl.ds(pl.multiple_of(first_row + b * sb, sb), sb)
        copies = [pltpu.make_async_copy(zeros_vmem, dst_hbm.at[rows, c * width:(c + 1) * width], sem)
                  for c in range(dst_hbm.shape[1] // width)]
        for cp in copies:
            cp.start()
        for cp in copies:
            cp.wait()
        return carry

    lax.fori_loop(0, n_blocks, fill, 0)


def _experts(n_used, item_e, item_row, item_nrows, tail, xg, w_gate_up, b_gate_up, w_down, b_down, *, tm, sb, fc):
    rows, d = xg.shape
    dff = w_down.shape[1]
    n_up = dff // fc
    fd = 2 * fc
    n_down = d // fd
    nsub = tm // sb
    kern = functools.partial(_expert_kernel, tm=tm, sb=sb, row_sizes=_row_sizes(tm, sb))

    def up_block(offset):
        def index_map(it, s, ie, ir, inr, tl):
            ahead = s >= n_up
            nxt = jnp.minimum(it + 1, tl[1] - 1)
            return jnp.where(ahead, ie[nxt], ie[it]), 0, offset + jnp.where(ahead, 0, s)
        return index_map

    def down_block(it, s, ie, ir, inr, tl):
        return ie[it], 0, jnp.maximum(s - n_up, 0)

    return pl.pallas_call(
        kern,
        grid_spec=pltpu.PrefetchScalarGridSpec(
            num_scalar_prefetch=4,
            grid=(n_used, n_up + n_down),
            in_specs=[
                pl.BlockSpec(memory_space=pl.ANY),
                pl.BlockSpec((1, d, fc), up_block(0)),
                pl.BlockSpec((1, d, fc), up_block(n_up)),
                pl.BlockSpec((1, dff, fd), down_block),
                pl.BlockSpec((1, 1, fc), up_block(0)),
                pl.BlockSpec((1, 1, fc), up_block(n_up)),
                pl.BlockSpec((1, 1, fd), down_block),
            ],
            out_specs=pl.BlockSpec(memory_space=pl.ANY),
            scratch_shapes=[
                pltpu.VMEM((tm, d), F32),
                pltpu.VMEM((tm, d), BF16),
                pltpu.VMEM((n_up, tm, fc), BF16),
                pltpu.VMEM((n_down, tm, fd), F32),
                pltpu.SemaphoreType.DMA((nsub,)),
                pltpu.SemaphoreType.DMA((nsub,)),
            ],
        ),
        out_shape=jax.ShapeDtypeStruct((rows, d), F32),
        compiler_params=pltpu.CompilerParams(
            dimension_semantics=("arbitrary", "arbitrary"), vmem_limit_bytes=VMEM_LIMIT),
        name="experts",
    )(item_e, item_row, item_nrows, tail, xg, w_gate_up, w_gate_up, w_down,
      b_gate_up, b_gate_up, b_down)


def _combine_kernel(dest_ref, yg_hbm, h1_ref, rf_ref, gfin_ref, o_ref, gbuf_ref, sem, *, tmc):
    i = pl.program_id(0)
    n_tiles = pl.num_programs(0)
    n_tok = n_tiles * tmc

    def issue_tile(tile):
        buf = tile % 2
        base = tile * tmc

        def issue(r, carry):
            for k in range(TOP_K):
                src = yg_hbm.at[pl.ds(dest_ref[k * n_tok + base + r], 1)]
                pltpu.make_async_copy(src, gbuf_ref.at[buf, k, pl.ds(r, 1)], sem.at[buf]).start(priority=k % 2)
            return carry

        lax.fori_loop(0, tmc, issue, 0)

    @pl.when(i == 0)
    def _():
        issue_tile(i)

    @pl.when(i + 1 < n_tiles)
    def _():
        issue_tile(i + 1)

    buf = i % 2
    for k in range(TOP_K):
        pltpu.make_async_copy(yg_hbm.at[pl.ds(0, tmc)], gbuf_ref.at[buf, k], sem.at[buf]).wait()

    gates = rf_ref[...]
    y = h1_ref[...]
    for k in range(TOP_K):
        y = y + gates[:, k:k + 1] * gbuf_ref[buf, k]
    o_ref[...] = _rms(y, gfin_ref[...])


def _combine(dest, yg, h1, rf, g_final, *, tmc):
    t, d = h1.shape
    kern = functools.partial(_combine_kernel, tmc=tmc)
    return pl.pallas_call(
        kern,
        grid_spec=pltpu.PrefetchScalarGridSpec(
            num_scalar_prefetch=1,
            grid=(t // tmc,),
            in_specs=[
                pl.BlockSpec(memory_space=pl.ANY),
                pl.BlockSpec((tmc, d), lambda i, dest: (i, 0)),
                pl.BlockSpec((tmc, LANES), lambda i, dest: (i, 0)),
                pl.BlockSpec((1, d), lambda i, dest: (0, 0)),
            ],
            out_specs=pl.BlockSpec((tmc, d), lambda i, dest: (i, 0)),
            scratch_shapes=[pltpu.VMEM((2, TOP_K, tmc, d), F32), pltpu.SemaphoreType.DMA((2,))],
        ),
        out_shape=jax.ShapeDtypeStruct((t, d), F32),
        compiler_params=pltpu.CompilerParams(
            dimension_semantics=("arbitrary",), vmem_limit_bytes=VMEM_LIMIT),
        name="combine",
    )(dest, yg, h1, rf, g_final)


def _cast_kernel(x_ref, o_ref):
    o_ref[...] = x_ref[...].astype(o_ref.dtype)


def _to_bf16(w, *, n_blocks):
    rows, cols = w.shape
    spec = pl.BlockSpec((rows // n_blocks, cols), lambda i: (i, 0))
    return pl.pallas_call(
        _cast_kernel, grid=(n_blocks,), in_specs=[spec], out_specs=spec,
        out_shape=jax.ShapeDtypeStruct(w.shape, BF16),
        compiler_params=pltpu.CompilerParams(dimension_semantics=("arbitrary",), vmem_limit_bytes=VMEM_LIMIT),
        name="to_bf16",
    )(w)


def _rotate_half_cols(w):
    half = w.shape[-1] // 2
    return jnp.concatenate([-w[..., half:], w[..., :half]], axis=-1)


def _prep_weights(g_mix, w_in, conv_w, g_q, w_q_up, g_kv, w_kv_up, g_conv_out, w_out, g_ffn, w_router, b_router):
    d = w_in.shape[0]
    w1 = _to_bf16(w_in.T, n_blocks=4)
    k_rope = w1[O_KPE:O_KPE + QK_ROPE]
    pad = jnp.zeros((LANES - QK_ROPE, d), BF16)
    half = QK_ROPE // 2
    k_rot = jnp.concatenate([-k_rope[half:], k_rope[:half]], axis=0)
    wk2 = jnp.concatenate([k_rope, pad, k_rot, pad], axis=0)

    wq = w_q_up.reshape(Q_LORA, N_HEADS, QK_NOPE + QK_ROPE)
    nope, pe = wq[:, :, :QK_NOPE], wq[:, :, QK_NOPE:]
    hpad = jnp.zeros((Q_LORA, N_HEADS, LANES - QK_ROPE), F32)
    wqa = jnp.concatenate([nope, pe, hpad], axis=2).reshape(Q_LORA, N_HEADS * HEAD_W)
    wqb = jnp.concatenate([_rotate_half_cols(pe), hpad], axis=2).reshape(Q_LORA, N_HEADS * LANES)
    wq_all = jnp.concatenate([wqa, wqb], axis=1).astype(BF16)

    wkv = w_kv_up.reshape(KV_LORA, N_HEADS, QK_NOPE + V_DIM)
    wkv2 = jnp.concatenate([wkv[:, :, :QK_NOPE].reshape(KV_LORA, ATTN_W),
                            wkv[:, :, QK_NOPE:].reshape(KV_LORA, ATTN_W)], axis=1).astype(BF16)

    wr_pad = jnp.zeros((d, LANES), F32).at[:, :N_EXPERTS].set(w_router)
    wr_hi = wr_pad.astype(BF16)
    wr_lo = (wr_pad - wr_hi.astype(F32)).astype(BF16)
    br = jnp.full((1, LANES), -1e30, F32).at[0, :N_EXPERTS].set(b_router)

    grp = jnp.arange(D_CONV) // (D_CONV // CONV_GROUPS)
    cw = jnp.zeros((SUBLANES, D_CONV), F32).at[:CONV_WIDTH].set(conv_w)
    return {
        "g_mix": g_mix[None], "w1": w1, "wk2": wk2, "conv_w": cw, "g_q": g_q[None], "wq": wq_all, "g_kv": g_kv[None],
        "wkv": wkv2, "g_conv_out": g_conv_out[None], "gmat": (grp[:, None] == grp[None, :]).astype(BF16),
        "wo": w_out.astype(BF16), "g_ffn": g_ffn[None], "wr": jnp.concatenate([wr_hi, wr_lo], axis=1), "br": br,
    }


def _rope_table(pos):
    half = QK_ROPE // 2
    inv_freq = np.float32(ROPE_THETA) ** (-np.arange(half, dtype=np.float32) / np.float32(half))
    ang = (np.asarray(pos, np.float32)[:, None] * inv_freq[None, :]).astype(np.float32)
    c, s = np.cos(ang), np.sin(ang)
    z = np.zeros((ang.shape[0], LANES - QK_ROPE), np.float32)
    return jnp.asarray(np.concatenate([c, c, z, s, s, z], axis=1), dtype=F32)


def _schedule_kernel(cnt_ref, ri_ref, dest_ref, pend_ref, ie_ref, irow_ref, inr_ref, tail_ref, pstart_ref,
                     *, tm, sb, n_items):
    def per_expert(e, carry):
        row, item = carry
        count = cnt_ref[e]
        pstart_ref[e] = row

        def per_item(li, item):
            ie_ref[item] = e
            irow_ref[item] = row + li * tm
            inr_ref[item] = jnp.minimum(count - li * tm, tm)
            return item + 1

        item = lax.fori_loop(0, lax.div(count + (tm - 1), tm), per_item, item)
        row = row + lax.div(count + (sb - 1), sb) * sb
        pend_ref[e] = row
        return row, item

    row, n_used = lax.fori_loop(0, N_EXPERTS, per_expert, (jnp.int32(0), jnp.int32(0)))
    tail_ref[0] = row
    tail_ref[1] = n_used

    def unused(item, carry):
        ie_ref[item] = 0
        irow_ref[item] = 0
        inr_ref[item] = 0
        return carry

    lax.fori_loop(n_used, n_items, unused, 0)

    eidx = ri_ref[:TOP_K]
    start_of = jnp.zeros_like(eidx)
    for e in range(N_EXPERTS):
        start_of = jnp.where(eidx == e, pstart_ref[e], start_of)
    dest_ref[...] = start_of + ri_ref[TOP_K:]


def _schedule(counts, ri, *, tm, sb, n_items):
    t = ri.shape[1]
    smem = pl.BlockSpec(memory_space=pltpu.SMEM)
    i32 = lambda n: jax.ShapeDtypeStruct((n,), jnp.int32)
    dest, pad_end, item_e, item_row, item_nrows, tail = pl.pallas_call(
        functools.partial(_schedule_kernel, tm=tm, sb=sb, n_items=n_items),
        in_specs=[smem, pl.BlockSpec(memory_space=pltpu.VMEM)],
        out_specs=[pl.BlockSpec(memory_space=pltpu.VMEM), smem, smem, smem, smem, smem],
        out_shape=[jax.ShapeDtypeStruct((TOP_K, t), jnp.int32), i32(N_EXPERTS), i32(n_items), i32(n_items),
                   i32(n_items), i32(2)],
        scratch_shapes=[pltpu.SMEM((N_EXPERTS,), jnp.int32)],
        name="schedule",
    )(counts, ri)
    return dest.reshape(-1), pad_end, item_e, item_row, item_nrows, tail


def _row_sizes(tm, sb):
    fine = [tm - k * sb for k in (2, 1, 0) if tm - k * sb > 0]
    coarse = [s for s in (tm // 4 // sb * sb, tm // 2 // sb * sb) if 0 < s < min(fine)]
    return tuple(sorted(set(coarse + fine)))


def _moe_tiles(t):
    sb = 128
    tm = 10 * sb
    a = t * TOP_K
    rows = (a + N_EXPERTS * (sb - 1) + sb - 1) // sb * sb
    n_items = N_EXPERTS + a // tm
    return sb, tm, rows, n_items


def _layer(x, meta_tokens, w, w_gate_up, b_gate_up, w_down, b_down, g_attn_out, g_final, *,
           tm_in, tq, tm_out, tmd, tmc, fc):
    b, seq, d = x.shape
    t = b * seq

    meta_blk = jnp.zeros((1, META_ROWS, d), F32).at[0, META_ROWS - N_META:].set(meta_tokens)
    meta_pos = np.maximum(np.arange(META_ROWS) - (META_ROWS - N_META), 0)
    zero_tail = jnp.zeros((SUBLANES, D_CONV), F32)
    _, _, mkn, mkpe, mv, u_tail = _mix_in(meta_blk, zero_tail, _rope_table(meta_pos), w, tm=META_ROWS)

    real_pos = np.arange(seq) + N_META
    convn, q, kn, kpe, v, _ = _mix_in(x, u_tail, _rope_table(real_pos), w, tm=tm_in)
    attn = _attention(q, kn, kpe, v, mkn, mkpe, mv, g_attn_out[None], tq=tq)

    h1, xn, ri, rf, cnt = _mix_out(convn.reshape(t, D_CONV), attn.reshape(t, ATTN_W), x.reshape(t, d), w, tm=tm_out)

    sb, tm_e, rows, n_items = _moe_tiles(t)
    counts = cnt[0, :N_EXPERTS].astype(jnp.int32)
    dest, pad_end, item_e, item_row, item_nrows, tail = _schedule(counts, ri, tm=tm_e, sb=sb, n_items=n_items)

    xg = _dispatch(dest, pad_end, xn, rows=rows, tmd=tmd, sb=sb)
    dff = w_down.shape[1]
    yg = _experts(tail[1], item_e, item_row, item_nrows, tail, xg, w_gate_up, b_gate_up.reshape(N_EXPERTS, 1, 2 * dff),
                  w_down, b_down.reshape(N_EXPERTS, 1, d), tm=tm_e, sb=sb, fc=fc)
    out = _combine(dest, yg, h1, rf, g_final[None], tmc=tmc)
    return out.reshape(b, seq, d)


def kernel(x, meta_tokens, g_mix, w_in, conv_w, g_q, w_q_up, g_kv, w_kv_up, g_conv_out, g_attn_out, w_out, g_ffn,
           w_router, b_router, w_gate_up, b_gate_up, w_down, b_down, g_final):
    w = _prep_weights(g_mix[0], w_in[0], conv_w[0], g_q[0], w_q_up[0], g_kv[0], w_kv_up[0], g_conv_out[0],
                      w_out[0], g_ffn[0], w_router[0], b_router[0])
    seq = x.shape[1]
    return _layer(x, meta_tokens, w, w_gate_up[0], b_gate_up[0], w_down[0], b_down[0], g_attn_out[0], g_final,
                  tm_in=min(512, seq), tq=min(512, seq), tm_out=min(512, seq), tmd=min(512, seq),
                  tmc=min(128, seq), fc=256)
```

```python
import functools

import jax
import jax.numpy as jnp
import numpy as np
from jax import lax
from jax.experimental import pallas as pl
from jax.experimental.pallas import tpu as pltpu

N_META = 16
EPS = 1e-6
D_CONV = 1024
CONV_GROUPS = 16
CONV_WIDTH = 3
N_HEADS = 8
QK_NOPE = 128
QK_ROPE = 64
V_DIM = 128
Q_LORA = 512
KV_LORA = 256
ROPE_THETA = 10000.0
N_EXPERTS = 32
TOP_K = 4
SWIGLU_LIMIT = 7.0
SWIGLU_ALPHA = 1.702

LANES = 128
SUBLANES = 8
META_ROWS = 128
HEAD_W = 2 * LANES
ATTN_W = N_HEADS * V_DIM
V_EXT = V_DIM + 16
VMEM_LIMIT = 56 * 1024 * 1024

O_B, O_C, O_U = 0, D_CONV, 2 * D_CONV
O_Q = 3 * D_CONV
O_KV = O_Q + Q_LORA
O_KPE = O_KV + KV_LORA
D_IN = O_KPE + QK_ROPE

F32 = jnp.float32
BF16 = jnp.bfloat16
NT_DIMS = (((1,), (1,)), ((), ()))


def _rms(x, g):
    return x * lax.rsqrt(jnp.mean(x * x, axis=-1, keepdims=True) + EPS) * g


def _dot(a, b):
    return jnp.dot(a, b, preferred_element_type=F32)


def _resident(shape):
    zeros = (0,) * len(shape)
    return pl.BlockSpec(shape, lambda *_: zeros, pipeline_mode=pl.Buffered(1))


def _mix_in_kernel(x_ref, gmix_ref, w1_ref, wk2_ref, cw_ref, gq_ref, wq_ref, gkv_ref, wkv_ref, gco_ref, gmat_ref,
                   tab_ref, uinit_ref,
                   convn_ref, qt_ref, kn_ref, kpe_ref, vt_ref, utail_ref,
                   ubuf_ref, *, tm, scale):
    i = pl.program_id(1)
    hn = _rms(x_ref[0], gmix_ref[...]).astype(BF16)

    def proj(lo, hi):
        return lax.dot_general(hn, w1_ref[lo:hi, :], NT_DIMS, preferred_element_type=F32)

    @pl.when(i == 0)
    def _():
        ubuf_ref[0:SUBLANES] = uinit_ref[...]

    @pl.when(i > 0)
    def _():
        ubuf_ref[0:SUBLANES] = ubuf_ref[tm:tm + SUBLANES]

    u = proj(O_C, O_U) * proj(O_U, O_Q)
    ubuf_ref[SUBLANES:SUBLANES + tm] = u
    cw = cw_ref[...]
    y = (cw[2:3] * u + cw[1:2] * ubuf_ref[SUBLANES - 1:SUBLANES - 1 + tm]
         + cw[0:1] * ubuf_ref[SUBLANES - 2:SUBLANES - 2 + tm])
    co = proj(O_B, O_C) * y
    ss = _dot((co * co).astype(BF16), gmat_ref[...])
    group = D_CONV // CONV_GROUPS
    convn_ref[0] = (co * lax.rsqrt(ss * (1.0 / group) + EPS) * gco_ref[...]).astype(BF16)
    utail_ref[...] = ubuf_ref[tm:tm + SUBLANES]

    cos = tab_ref[:, :LANES]
    sin = tab_ref[:, LANES:]

    qn = _rms(proj(O_Q, O_KV), gq_ref[...]).astype(BF16)
    qa = _dot(qn, wq_ref[:, :N_HEADS * HEAD_W])
    qb = _dot(qn, wq_ref[:, N_HEADS * HEAD_W:])
    for h in range(N_HEADS):
        c0 = h * HEAD_W
        qt_ref[0, c0:c0 + LANES, :] = (qa[:, c0:c0 + LANES] * scale).T.astype(BF16)
        pe = qa[:, c0 + LANES:c0 + HEAD_W] * cos + qb[:, h * LANES:(h + 1) * LANES] * sin
        qt_ref[0, c0 + LANES:c0 + HEAD_W, :] = (pe * scale).T.astype(BF16)

    kvn = _rms(proj(O_KV, O_KPE), gkv_ref[...]).astype(BF16)
    kv = _dot(kvn, wkv_ref[...])
    kn_ref[0] = kv[:, :ATTN_W].astype(BF16)
    vt = kv[:, ATTN_W:].T.astype(BF16)
    ones = jnp.ones((V_EXT - V_DIM, tm), BF16)
    for h in range(N_HEADS):
        vt_ref[0, 0, h * V_EXT:h * V_EXT + V_DIM, :] = vt[h * V_DIM:(h + 1) * V_DIM]
        vt_ref[0, 0, h * V_EXT + V_DIM:(h + 1) * V_EXT, :] = ones
    kk = lax.dot_general(hn, wk2_ref[...], NT_DIMS, preferred_element_type=F32)
    kpe_ref[0] = (kk[:, :LANES] * cos + kk[:, LANES:] * sin).astype(BF16)


def _mix_in(x3, uinit, tab, w, *, tm):
    b, l, d = x3.shape
    nt = l // tm
    kern = functools.partial(_mix_in_kernel, tm=tm, scale=float((QK_NOPE + QK_ROPE) ** -0.5))
    row = lambda width: pl.BlockSpec((1, tm, width), lambda bi, i: (bi, i, 0))
    return pl.pallas_call(
        kern,
        grid=(b, nt),
        in_specs=[
            row(d),
            _resident((1, d)),
            _resident((D_IN, d)),
            _resident((2 * LANES, d)),
            _resident((SUBLANES, D_CONV)),
            _resident((1, Q_LORA)),
            _resident((Q_LORA, N_HEADS * (HEAD_W + LANES))),
            _resident((1, KV_LORA)),
            _resident((KV_LORA, 2 * ATTN_W)),
            _resident((1, D_CONV)),
            _resident((D_CONV, D_CONV)),
            pl.BlockSpec((tm, 2 * LANES), lambda bi, i: (i, 0)),
            _resident((SUBLANES, D_CONV)),
        ],
        out_specs=[
            row(D_CONV),
            pl.BlockSpec((1, N_HEADS * HEAD_W, tm), lambda bi, i: (bi, 0, i)),
            row(ATTN_W), row(LANES),
            pl.BlockSpec((1, 1, N_HEADS * V_EXT, tm), lambda bi, i: (bi, i, 0, 0)),
            pl.BlockSpec((SUBLANES, D_CONV), lambda bi, i: (bi * nt + i, 0)),
        ],
        out_shape=[
            jax.ShapeDtypeStruct((b, l, D_CONV), BF16),
            jax.ShapeDtypeStruct((b, N_HEADS * HEAD_W, l), BF16),
            jax.ShapeDtypeStruct((b, l, ATTN_W), BF16),
            jax.ShapeDtypeStruct((b, l, LANES), BF16),
            jax.ShapeDtypeStruct((b, nt, N_HEADS * V_EXT, tm), BF16),
            jax.ShapeDtypeStruct((b * nt * SUBLANES, D_CONV), F32),
        ],
        scratch_shapes=[pltpu.VMEM((tm + SUBLANES, D_CONV), F32)],
        compiler_params=pltpu.CompilerParams(
            dimension_semantics=("arbitrary", "arbitrary"), vmem_limit_bytes=VMEM_LIMIT),
        name="mix_in",
    )(x3, w["g_mix"], w["w1"], w["wk2"], w["conv_w"], w["g_q"], w["wq"], w["g_kv"], w["wkv"], w["g_conv_out"],
      w["gmat"], tab, uinit)


def _attn_kernel(qt_ref, kn_ref, kpe_ref, vt_ref, mkn_ref, mkpe_ref, mvt_ref, g_ref, o_ref,
                 m_ref, acc_ref, *, tq, hps):
    qi = pl.program_id(2)
    tv = vt_ref.shape[3]

    def lanes(hh):
        return slice(hh * LANES, (hh + 1) * LANES)

    def vrows(hh):
        return slice(hh * V_EXT, (hh + 1) * V_EXT)

    def qt(hh):
        return qt_ref[0, hh * HEAD_W:(hh + 1) * HEAD_W, :]

    for hh in range(hps):
        km = jnp.concatenate([mkn_ref[0, :, lanes(hh)], mkpe_ref[0]], axis=-1)
        s = _dot(km, qt(hh))
        row = lax.broadcasted_iota(jnp.int32, s.shape, 0)
        s = jnp.where(row >= META_ROWS - N_META, s, -jnp.inf)
        m0 = jnp.max(s, axis=0, keepdims=True)
        m_ref[hh] = m0
        acc_ref[hh] = _dot(mvt_ref[0, 0, vrows(hh), :], jnp.exp((s - m0).astype(BF16)))

    def step(kb, diagonal):
        off = pl.multiple_of(kb * tq, tq)
        kpe = kpe_ref[0, pl.ds(off, tq), :]
        scores = []
        for hh in range(hps):
            k = jnp.concatenate([kn_ref[0, pl.ds(off, tq), lanes(hh)], kpe], axis=-1)
            s = _dot(k, qt(hh))
            if diagonal:
                r = lax.broadcasted_iota(jnp.int32, s.shape, 0)
                c = lax.broadcasted_iota(jnp.int32, s.shape, 1)
                s = jnp.where(r <= c, s, -jnp.inf)
            scores.append(s)
        probs, alphas = [], []
        for hh in range(hps):
            m_prev = m_ref[hh]
            m_new = jnp.maximum(m_prev, jnp.max(scores[hh], axis=0, keepdims=True))
            alphas.append(jnp.exp(m_prev - m_new))
            probs.append(jnp.exp((scores[hh] - m_new).astype(BF16)))
            m_ref[hh] = m_new
        for hh in range(hps):
            pb = probs[hh]
            pv = _dot(vt_ref[0, kb * (tq // tv), vrows(hh), :], pb[:tv])
            for c in range(1, tq // tv):
                pv = pv + _dot(vt_ref[0, kb * (tq // tv) + c, vrows(hh), :], pb[c * tv:(c + 1) * tv])
            acc_ref[hh] = alphas[hh] * acc_ref[hh] + pv

    def body(kb, carry):
        step(kb, False)
        return carry

    lax.fori_loop(0, qi, body, 0)
    step(qi, True)

    for hh in range(hps):
        acc = acc_ref[hh]
        o = (acc[:V_DIM] / acc[V_DIM:V_DIM + 1]).T
        o_ref[0, :, lanes(hh)] = _rms(o, g_ref[:, lanes(hh)]).astype(BF16)


def _attention(qt, kn, kpe, vt, mkn, mkpe, mvt, g_attn, *, tq, hps=4):
    b, l, _ = kn.shape
    _, nt, _, tv = vt.shape
    nq = l // tq
    kern = functools.partial(_attn_kernel, tq=tq, hps=hps)
    return pl.pallas_call(
        kern,
        grid=(b, N_HEADS // hps, nq),
        in_specs=[
            pl.BlockSpec((1, hps * HEAD_W, tq), lambda bi, h, i: (bi, h, i)),
            pl.BlockSpec((1, l, hps * LANES), lambda bi, h, i: (bi, 0, h)),
            pl.BlockSpec((1, l, LANES), lambda bi, h, i: (bi, 0, 0)),
            pl.BlockSpec((1, nt, hps * V_EXT, tv), lambda bi, h, i: (bi, 0, h, 0)),
            pl.BlockSpec((1, META_ROWS, hps * LANES), lambda bi, h, i: (0, 0, h)),
            pl.BlockSpec((1, META_ROWS, LANES), lambda bi, h, i: (0, 0, 0)),
            pl.BlockSpec((1, 1, hps * V_EXT, META_ROWS), lambda bi, h, i: (0, 0, h, 0)),
            pl.BlockSpec((1, hps * V_DIM), lambda bi, h, i: (0, h)),
        ],
        out_specs=pl.BlockSpec((1, tq, hps * V_DIM), lambda bi, h, i: (bi, i, h)),
        out_shape=jax.ShapeDtypeStruct((b, l, ATTN_W), BF16),
        scratch_shapes=[pltpu.VMEM((hps, 1, tq), F32), pltpu.VMEM((hps, V_EXT, tq), F32)],
        compiler_params=pltpu.CompilerParams(
            dimension_semantics=("arbitrary", "arbitrary", "arbitrary"), vmem_limit_bytes=VMEM_LIMIT),
        name="attn",
    )(qt, kn, kpe, vt, mkn, mkpe, mvt, g_attn)


def _mix_out_kernel(convn_ref, attn_ref, x_ref, wo_ref, gffn_ref, wr_ref, br_ref,
                    h1_ref, xn_ref, ri_ref, rf_ref, cnt_ref, carry_ref, *, tm):
    @pl.when(pl.program_id(0) == 0)
    def _():
        carry_ref[...] = jnp.zeros_like(carry_ref)

    h1 = x_ref[...] + _dot(convn_ref[...], wo_ref[:D_CONV]) + _dot(attn_ref[...], wo_ref[D_CONV:])
    h1_ref[...] = h1
    xn = _rms(h1, gffn_ref[...])
    xn_ref[...] = xn

    xh = xn.astype(BF16)
    xl = (xn - xh.astype(F32)).astype(BF16)
    ph = _dot(xh, wr_ref[...])
    logits = ph[:, :LANES] + ph[:, LANES:] + _dot(xl, wr_ref[:, :LANES]) + br_ref[...]

    lane = lax.broadcasted_iota(jnp.int32, (tm, LANES), 1).astype(F32)
    work = logits
    top_v, top_i, onehots = [], [], []
    for _ in range(TOP_K):
        mk = jnp.max(work, axis=-1, keepdims=True)
        ik = jnp.min(jnp.where(work == mk, lane, float(LANES)), axis=-1, keepdims=True)
        oh = lane == ik
        work = jnp.where(oh, -jnp.inf, work)
        top_v.append(mk)
        top_i.append(ik)
        onehots.append(oh)

    ex = [jnp.exp(v - top_v[0]) for v in top_v]
    denom = ex[0] + ex[1] + ex[2] + ex[3]
    gates = [e / denom for e in ex]

    sel = jnp.zeros((tm, LANES), F32)
    for oh in onehots:
        sel = sel + oh.astype(F32)
    r = lax.broadcasted_iota(jnp.int32, (tm, tm), 0)
    c = lax.broadcasted_iota(jnp.int32, (tm, tm), 1)
    tri = jnp.where(c < r, 1.0, 0.0).astype(BF16)
    cum = _dot(tri, sel.astype(BF16)) + carry_ref[...]
    carry_ref[...] = carry_ref[...] + jnp.sum(sel, axis=0, keepdims=True)
    cnt_ref[...] = carry_ref[...]

    ri = jnp.zeros((tm, LANES), F32)
    rf = jnp.zeros((tm, LANES), F32)
    for k in range(TOP_K):
        rank_k = jnp.sum(jnp.where(onehots[k], cum, 0.0), axis=-1, keepdims=True)
        ri = jnp.where(lane == float(k), top_i[k], ri)
        ri = jnp.where(lane == float(TOP_K + k), rank_k, ri)
        rf = jnp.where(lane == float(k), gates[k], rf)
    ri_ref[...] = ri.T[:2 * TOP_K].astype(jnp.int32)
    rf_ref[...] = rf


def _mix_out(convn, attn, x2, w, *, tm):
    t, d = x2.shape
    kern = functools.partial(_mix_out_kernel, tm=tm)
    row = lambda width: pl.BlockSpec((tm, width), lambda i: (i, 0))
    return pl.pallas_call(
        kern,
        grid=(t // tm,),
        in_specs=[
            row(D_CONV), row(ATTN_W), row(d),
            _resident((D_CONV + ATTN_W, d)),
            _resident((1, d)),
            _resident((d, 2 * LANES)),
            _resident((1, LANES)),
        ],
        out_specs=[row(d), row(d), pl.BlockSpec((2 * TOP_K, tm), lambda i: (0, i)), row(LANES),
                   pl.BlockSpec((1, LANES), lambda i: (0, 0))],
        out_shape=[
            jax.ShapeDtypeStruct((t, d), F32),
            jax.ShapeDtypeStruct((t, d), F32),
            jax.ShapeDtypeStruct((2 * TOP_K, t), jnp.int32),
            jax.ShapeDtypeStruct((t, LANES), F32),
            jax.ShapeDtypeStruct((1, LANES), F32),
        ],
        scratch_shapes=[pltpu.VMEM((1, LANES), F32)],
        compiler_params=pltpu.CompilerParams(
            dimension_semantics=("arbitrary",), vmem_limit_bytes=VMEM_LIMIT),
        name="mix_out",
    )(convn, attn, x2, w["wo"], w["g_ffn"], w["wr"], w["br"])


def _dispatch_kernel(dest_ref, pend_ref, xn_ref, xg_hbm, zbuf_ref, sem, zsem, *, tmd, sb):
    i = pl.program_id(0)

    @pl.when(i == 0)
    def _():
        zbuf_ref[...] = jnp.zeros_like(zbuf_ref)

        def zero_copy(e):
            end = pend_ref[e]
            start = pl.multiple_of(jnp.maximum(end - sb, 0), sb)
            return pltpu.make_async_copy(zbuf_ref, xg_hbm.at[pl.ds(start, sb)], zsem)

        def nonempty(e):
            prev = pend_ref[jnp.maximum(e - 1, 0)]
            return pend_ref[e] > jnp.where(e > 0, prev, 0)

        def start(e, carry):
            @pl.when(nonempty(e))
            def _():
                zero_copy(e).start()
            return carry

        def wait(e, carry):
            @pl.when(nonempty(e))
            def _():
                zero_copy(e).wait()
            return carry

        lax.fori_loop(0, N_EXPERTS, start, 0)
        lax.fori_loop(0, N_EXPERTS, wait, 0)
        _zero_tail(zbuf_ref, xg_hbm, pend_ref[N_EXPERTS - 1], zsem)

    base = i * tmd
    n_tok = pl.num_programs(0) * tmd

    def issue(r, carry):
        for k in range(TOP_K):
            dst = xg_hbm.at[pl.ds(dest_ref[k * n_tok + base + r], 1)]
            pltpu.make_async_copy(xn_ref.at[pl.ds(r, 1)], dst, sem).start(priority=k % 2)
        return carry

    lax.fori_loop(0, tmd, issue, 0)
    for k in range(TOP_K):
        pltpu.make_async_copy(xn_ref, xg_hbm.at[pl.ds(0, tmd)], sem).wait()


def _dispatch(dest, pad_end, xn, *, rows, tmd, sb):
    t, d = xn.shape
    kern = functools.partial(_dispatch_kernel, tmd=tmd, sb=sb)
    return pl.pallas_call(
        kern,
        grid_spec=pltpu.PrefetchScalarGridSpec(
            num_scalar_prefetch=2,
            grid=(t // tmd,),
            in_specs=[pl.BlockSpec((tmd, d), lambda i, dest, pend: (i, 0))],
            out_specs=pl.BlockSpec(memory_space=pl.ANY),
            scratch_shapes=[pltpu.VMEM((sb, d), F32), pltpu.SemaphoreType.DMA, pltpu.SemaphoreType.DMA],
        ),
        out_shape=jax.ShapeDtypeStruct((rows, d), F32),
        compiler_params=pltpu.CompilerParams(
            dimension_semantics=("arbitrary",), vmem_limit_bytes=VMEM_LIMIT),
        name="dispatch",
    )(dest, pad_end, xn)


def _expert_kernel(ie_ref, irow_ref, inr_ref, tail_ref, xg_hbm, wg_ref, wl_ref, wd_ref, bias_ref, yg_hbm,
                   xst_ref, xb_ref, act_ref, acc_ref, sem_in, sem_out,
                   *, tm, sb, row_sizes):
    it = pl.program_id(0)
    s = pl.program_id(1)
    n_it = pl.num_programs(0)
    nrows = inr_ref[it]
    n_up = act_ref.shape[0]
    n_down, _, fc = acc_ref.shape
    last = n_up + n_down - 1
    prev_it = jnp.maximum(it - 1, 0)
    next_it = jnp.minimum(it + 1, n_it - 1)

    def piece(r):
        return pl.ds(r * sb, sb)

    def hbm_rows(item, r):
        return pl.ds(pl.multiple_of(irow_ref[item] + r * sb, sb), sb)

    def in_copy(item, r):
        return pltpu.make_async_copy(xg_hbm.at[hbm_rows(item, r)], xst_ref.at[piece(r)], sem_in.at[r])

    def out_copies(item, r):
        return [pltpu.make_async_copy(acc_ref.at[n, piece(r)], yg_hbm.at[hbm_rows(item, r), n * fc:(n + 1) * fc],
                                      sem_out.at[r]) for n in range(n_down)]

    def for_pieces_of(item, fn):
        for r in range(tm // sb):
            pl.when(r * sb < inr_ref[item])(functools.partial(fn, item, r))

    def for_valid_chunks(fn):
        for lo, size in zip((0,) + row_sizes[:-1], row_sizes):
            pl.when((lo < nrows) & (nrows <= size))(functools.partial(fn, pl.ds(0, size)))

    def start_in(item, r):
        in_copy(item, r).start()

    def land_in(item, r):
        in_copy(item, r).wait()
        xb_ref[piece(r)] = xst_ref[piece(r)].astype(BF16)

    def start_out(item, r):
        for cp in out_copies(item, r):
            cp.start()

    def wait_out(item, r):
        for cp in out_copies(item, r):
            cp.wait()

    @pl.when((it == 0) & (s == 0))
    def _():
        xb_ref[...] = jnp.zeros_like(xb_ref)
        for_pieces_of(it, start_in)

    @pl.when(s == 0)
    def _():
        for_pieces_of(it, land_in)

    @pl.when((s == 1) & (it + 1 < n_it))
    def _():
        for_pieces_of(next_it, start_in)

    @pl.when((s == n_up) & (it > 0))
    def _():
        for_pieces_of(prev_it, wait_out)

    def up(rows):
        xs = xb_ref[rows]
        bg = bias_ref[0, pl.ds(s, 1), :]
        bl = bias_ref[0, pl.ds(n_up + s, 1), :]
        g = jnp.minimum(_dot(xs, wg_ref[0].astype(BF16)) + bg, SWIGLU_LIMIT)
        lin = jnp.clip(_dot(xs, wl_ref[0].astype(BF16)) + bl, -SWIGLU_LIMIT, SWIGLU_LIMIT)
        act_ref[s, rows] = ((lin + 1.0) * (g * jax.nn.sigmoid(SWIGLU_ALPHA * g))).astype(BF16)

    def down(rows):
        act = jnp.concatenate([act_ref[jj, rows] for jj in range(n_up)], axis=-1)
        per_step = fc // act_ref.shape[2]
        first = 2 * n_up + (s - n_up) * per_step
        bd = jnp.concatenate([bias_ref[0, pl.ds(first + c, 1), :] for c in range(per_step)], axis=-1)
        acc_ref[s - n_up, rows] = _dot(act, wd_ref[0].astype(BF16)) + bd

    @pl.when(s < n_up)
    def _():
        for_valid_chunks(up)

    @pl.when(s >= n_up)
    def _():
        for_valid_chunks(down)

    @pl.when(s == last)
    def _():
        for_pieces_of(it, start_out)

    @pl.when((it == n_it - 1) & (s == last))
    def _():
        for_pieces_of(it, wait_out)
        acc_ref[0, piece(0)] = jnp.zeros((sb, fc), F32)
        _zero_tail(acc_ref.at[0, piece(0)], yg_hbm, tail_ref[0], sem_out.at[0])


def _zero_tail(zeros_vmem, dst_hbm, first_row, sem):
    sb, width = zeros_vmem.shape
    n_blocks = (dst_hbm.shape[0] - first_row) // sb

    def fill(b, carry):
        rows = pl.ds(pl.multiple_of(first_row + b * sb, sb), sb)
        copies = [pltpu.make_async_copy(zeros_vmem, dst_hbm.at[rows, c * width:(c + 1) * width], sem)
                  for c in range(dst_hbm.shape[1] // width)]
        for cp in copies:
            cp.start()
        for cp in copies:
            cp.wait()
        return carry

    lax.fori_loop(0, n_blocks, fill, 0)


def _experts(n_used, item_e, item_row, item_nrows, tail, xg, w_gate_up, b_gate_up, w_down, b_down, *, tm, sb, fc):
    rows, d = xg.shape
    dff = w_down.shape[1]
    n_up = dff // fc
    fd = 2 * fc
    n_down = d // fd
    nsub = tm // sb
    kern = functools.partial(_expert_kernel, tm=tm, sb=sb, row_sizes=_row_sizes(tm, sb))
    n_e = w_down.shape[0]
    bias_rows = [b_gate_up.reshape(n_e, 2 * n_up, fc), b_down.reshape(n_e, d // fc, fc)]
    n_rows = 2 * n_up + d // fc
    bias_rows.append(jnp.zeros((n_e, -n_rows % SUBLANES, fc), F32))
    biases = jnp.concatenate(bias_rows, axis=1)

    def up_block(offset):
        def index_map(it, s, ie, ir, inr, tl):
            ahead = s >= n_up
            nxt = jnp.minimum(it + 1, tl[1] - 1)
            return jnp.where(ahead, ie[nxt], ie[it]), 0, offset + jnp.where(ahead, 0, s)
        return index_map

    def down_block(it, s, ie, ir, inr, tl):
        return ie[it], 0, jnp.maximum(s - n_up, 0)

    return pl.pallas_call(
        kern,
        grid_spec=pltpu.PrefetchScalarGridSpec(
            num_scalar_prefetch=4,
            grid=(n_used, n_up + n_down),
            in_specs=[
                pl.BlockSpec(memory_space=pl.ANY),
                pl.BlockSpec((1, d, fc), up_block(0)),
                pl.BlockSpec((1, d, fc), up_block(n_up)),
                pl.BlockSpec((1, dff, fd), down_block),
                pl.BlockSpec((1, biases.shape[1], fc), lambda it, s, ie, ir, inr, tl: (ie[it], 0, 0)),
            ],
            out_specs=pl.BlockSpec(memory_space=pl.ANY),
            scratch_shapes=[
                pltpu.VMEM((tm, d), F32),
                pltpu.VMEM((tm, d), BF16),
                pltpu.VMEM((n_up, tm, fc), BF16),
                pltpu.VMEM((n_down, tm, fd), F32),
                pltpu.SemaphoreType.DMA((nsub,)),
                pltpu.SemaphoreType.DMA((nsub,)),
            ],
        ),
        out_shape=jax.ShapeDtypeStruct((rows, d), F32),
        compiler_params=pltpu.CompilerParams(
            dimension_semantics=("arbitrary", "arbitrary"), vmem_limit_bytes=VMEM_LIMIT),
        name="experts",
    )(item_e, item_row, item_nrows, tail, xg, w_gate_up, w_gate_up, w_down, biases)


def _combine_kernel(dest_ref, yg_hbm, h1_ref, rf_ref, gfin_ref, o_ref, gbuf_ref, sem, *, tmc):
    i = pl.program_id(0)
    n_tiles = pl.num_programs(0)
    n_tok = n_tiles * tmc

    def issue_tile(tile):
        buf = tile % 2
        base = tile * tmc

        def issue(r, carry):
            for k in range(TOP_K):
                src = yg_hbm.at[pl.ds(dest_ref[k * n_tok + base + r], 1)]
                pltpu.make_async_copy(src, gbuf_ref.at[buf, k, pl.ds(r, 1)], sem.at[buf]).start(priority=k % 2)
            return carry

        lax.fori_loop(0, tmc, issue, 0)

    @pl.when(i == 0)
    def _():
        issue_tile(i)

    @pl.when(i + 1 < n_tiles)
    def _():
        issue_tile(i + 1)

    buf = i % 2
    for k in range(TOP_K):
        pltpu.make_async_copy(yg_hbm.at[pl.ds(0, tmc)], gbuf_ref.at[buf, k], sem.at[buf]).wait()

    gates = rf_ref[...]
    y = h1_ref[...]
    for k in range(TOP_K):
        y = y + gates[:, k:k + 1] * gbuf_ref[buf, k]
    o_ref[...] = _rms(y, gfin_ref[...])


def _combine(dest, yg, h1, rf, g_final, *, tmc):
    t, d = h1.shape
    kern = functools.partial(_combine_kernel, tmc=tmc)
    return pl.pallas_call(
        kern,
        grid_spec=pltpu.PrefetchScalarGridSpec(
            num_scalar_prefetch=1,
            grid=(t // tmc,),
            in_specs=[
                pl.BlockSpec(memory_space=pl.ANY),
                pl.BlockSpec((tmc, d), lambda i, dest: (i, 0)),
                pl.BlockSpec((tmc, LANES), lambda i, dest: (i, 0)),
                pl.BlockSpec((1, d), lambda i, dest: (0, 0)),
            ],
            out_specs=pl.BlockSpec((tmc, d), lambda i, dest: (i, 0)),
            scratch_shapes=[pltpu.VMEM((2, TOP_K, tmc, d), F32), pltpu.SemaphoreType.DMA((2,))],
        ),
        out_shape=jax.ShapeDtypeStruct((t, d), F32),
        compiler_params=pltpu.CompilerParams(
            dimension_semantics=("arbitrary",), vmem_limit_bytes=VMEM_LIMIT),
        name="combine",
    )(dest, yg, h1, rf, g_final)


def _cast_kernel(x_ref, o_ref):
    o_ref[...] = x_ref[...].astype(o_ref.dtype)


def _to_bf16(w, *, n_blocks):
    rows, cols = w.shape
    spec = pl.BlockSpec((rows // n_blocks, cols), lambda i: (i, 0))
    return pl.pallas_call(
        _cast_kernel, grid=(n_blocks,), in_specs=[spec], out_specs=spec,
        out_shape=jax.ShapeDtypeStruct(w.shape, BF16),
        compiler_params=pltpu.CompilerParams(dimension_semantics=("arbitrary",), vmem_limit_bytes=VMEM_LIMIT),
        name="to_bf16",
    )(w)


def _rotate_half_cols(w):
    half = w.shape[-1] // 2
    return jnp.concatenate([-w[..., half:], w[..., :half]], axis=-1)


def _prep_weights(g_mix, w_in, conv_w, g_q, w_q_up, g_kv, w_kv_up, g_conv_out, w_out, g_ffn, w_router, b_router):
    d = w_in.shape[0]
    w1 = _to_bf16(w_in.T, n_blocks=4)
    k_rope = w1[O_KPE:O_KPE + QK_ROPE]
    pad = jnp.zeros((LANES - QK_ROPE, d), BF16)
    half = QK_ROPE // 2
    k_rot = jnp.concatenate([-k_rope[half:], k_rope[:half]], axis=0)
    wk2 = jnp.concatenate([k_rope, pad, k_rot, pad], axis=0)

    wq = w_q_up.reshape(Q_LORA, N_HEADS, QK_NOPE + QK_ROPE)
    nope, pe = wq[:, :, :QK_NOPE], wq[:, :, QK_NOPE:]
    hpad = jnp.zeros((Q_LORA, N_HEADS, LANES - QK_ROPE), F32)
    wqa = jnp.concatenate([nope, pe, hpad], axis=2).reshape(Q_LORA, N_HEADS * HEAD_W)
    wqb = jnp.concatenate([_rotate_half_cols(pe), hpad], axis=2).reshape(Q_LORA, N_HEADS * LANES)
    wq_all = jnp.concatenate([wqa, wqb], axis=1).astype(BF16)

    wkv = w_kv_up.reshape(KV_LORA, N_HEADS, QK_NOPE + V_DIM)
    wkv2 = jnp.concatenate([wkv[:, :, :QK_NOPE].reshape(KV_LORA, ATTN_W),
                            wkv[:, :, QK_NOPE:].reshape(KV_LORA, ATTN_W)], axis=1).astype(BF16)

    wr_pad = jnp.zeros((d, LANES), F32).at[:, :N_EXPERTS].set(w_router)
    wr_hi = wr_pad.astype(BF16)
    wr_lo = (wr_pad - wr_hi.astype(F32)).astype(BF16)
    br = jnp.full((1, LANES), -1e30, F32).at[0, :N_EXPERTS].set(b_router)

    grp = jnp.arange(D_CONV) // (D_CONV // CONV_GROUPS)
    cw = jnp.zeros((SUBLANES, D_CONV), F32).at[:CONV_WIDTH].set(conv_w)
    return {
        "g_mix": g_mix[None], "w1": w1, "wk2": wk2, "conv_w": cw, "g_q": g_q[None], "wq": wq_all, "g_kv": g_kv[None],
        "wkv": wkv2, "g_conv_out": g_conv_out[None], "gmat": (grp[:, None] == grp[None, :]).astype(BF16),
        "wo": w_out.astype(BF16), "g_ffn": g_ffn[None], "wr": jnp.concatenate([wr_hi, wr_lo], axis=1), "br": br,
    }


def _rope_table(pos):
    half = QK_ROPE // 2
    inv_freq = np.float32(ROPE_THETA) ** (-np.arange(half, dtype=np.float32) / np.float32(half))
    ang = (np.asarray(pos, np.float32)[:, None] * inv_freq[None, :]).astype(np.float32)
    c, s = np.cos(ang), np.sin(ang)
    z = np.zeros((ang.shape[0], LANES - QK_ROPE), np.float32)
    return jnp.asarray(np.concatenate([c, c, z, s, s, z], axis=1), dtype=F32)


def _schedule_kernel(cnt_ref, ri_ref, dest_ref, pend_ref, ie_ref, irow_ref, inr_ref, tail_ref, pstart_ref,
                     *, tm, sb, n_items):
    def per_expert(e, carry):
        row, item = carry
        count = cnt_ref[e]
        pstart_ref[e] = row

        def per_item(li, item):
            ie_ref[item] = e
            irow_ref[item] = row + li * tm
            inr_ref[item] = jnp.minimum(count - li * tm, tm)
            return item + 1

        item = lax.fori_loop(0, lax.div(count + (tm - 1), tm), per_item, item)
        row = row + lax.div(count + (sb - 1), sb) * sb
        pend_ref[e] = row
        return row, item

    row, n_used = lax.fori_loop(0, N_EXPERTS, per_expert, (jnp.int32(0), jnp.int32(0)))
    tail_ref[0] = row
    tail_ref[1] = n_used

    def unused(item, carry):
        ie_ref[item] = 0
        irow_ref[item] = 0
        inr_ref[item] = 0
        return carry

    lax.fori_loop(n_used, n_items, unused, 0)

    eidx = ri_ref[:TOP_K]
    start_of = jnp.zeros_like(eidx)
    for e in range(N_EXPERTS):
        start_of = jnp.where(eidx == e, pstart_ref[e], start_of)
    dest_ref[...] = start_of + ri_ref[TOP_K:]


def _schedule(counts, ri, *, tm, sb, n_items):
    t = ri.shape[1]
    smem = pl.BlockSpec(memory_space=pltpu.SMEM)
    i32 = lambda n: jax.ShapeDtypeStruct((n,), jnp.int32)
    dest, pad_end, item_e, item_row, item_nrows, tail = pl.pallas_call(
        functools.partial(_schedule_kernel, tm=tm, sb=sb, n_items=n_items),
        in_specs=[smem, pl.BlockSpec(memory_space=pltpu.VMEM)],
        out_specs=[pl.BlockSpec(memory_space=pltpu.VMEM), smem, smem, smem, smem, smem],
        out_shape=[jax.ShapeDtypeStruct((TOP_K, t), jnp.int32), i32(N_EXPERTS), i32(n_items), i32(n_items),
                   i32(n_items), i32(2)],
        scratch_shapes=[pltpu.SMEM((N_EXPERTS,), jnp.int32)],
        name="schedule",
    )(counts, ri)
    return dest.reshape(-1), pad_end, item_e, item_row, item_nrows, tail


def _row_sizes(tm, sb):
    fine = [tm - k * sb for k in (2, 1, 0) if tm - k * sb > 0]
    coarse = [s for s in (tm // 4 // sb * sb, tm // 2 // sb * sb) if 0 < s < min(fine)]
    return tuple(sorted(set(coarse + fine)))


def _moe_tiles(t):
    sb = 128
    tm = 10 * sb
    a = t * TOP_K
    rows = (a + N_EXPERTS * (sb - 1) + sb - 1) // sb * sb
    n_items = N_EXPERTS + a // tm
    return sb, tm, rows, n_items


def _layer(x, meta_tokens, w, w_gate_up, b_gate_up, w_down, b_down, g_attn_out, g_final, *,
           tm_in, tq, tm_out, tmd, tmc, fc):
    b, seq, d = x.shape
    t = b * seq

    meta_blk = jnp.zeros((1, META_ROWS, d), F32).at[0, META_ROWS - N_META:].set(meta_tokens)
    meta_pos = np.maximum(np.arange(META_ROWS) - (META_ROWS - N_META), 0)
    zero_tail = jnp.zeros((SUBLANES, D_CONV), F32)
    _, _, mkn, mkpe, mv, u_tail = _mix_in(meta_blk, zero_tail, _rope_table(meta_pos), w, tm=META_ROWS)

    real_pos = np.arange(seq) + N_META
    convn, q, kn, kpe, v, _ = _mix_in(x, u_tail, _rope_table(real_pos), w, tm=tm_in)
    attn = _attention(q, kn, kpe, v, mkn, mkpe, mv, g_attn_out[None], tq=tq)

    h1, xn, ri, rf, cnt = _mix_out(convn.reshape(t, D_CONV), attn.reshape(t, ATTN_W), x.reshape(t, d), w, tm=tm_out)

    sb, tm_e, rows, n_items = _moe_tiles(t)
    counts = cnt[0, :N_EXPERTS].astype(jnp.int32)
    dest, pad_end, item_e, item_row, item_nrows, tail = _schedule(counts, ri, tm=tm_e, sb=sb, n_items=n_items)

    xg = _dispatch(dest, pad_end, xn, rows=rows, tmd=tmd, sb=sb)
    dff = w_down.shape[1]
    yg = _experts(tail[1], item_e, item_row, item_nrows, tail, xg, w_gate_up, b_gate_up.reshape(N_EXPERTS, 1, 2 * dff),
                  w_down, b_down.reshape(N_EXPERTS, 1, d), tm=tm_e, sb=sb, fc=fc)
    out = _combine(dest, yg, h1, rf, g_final[None], tmc=tmc)
    return out.reshape(b, seq, d)


def kernel(x, meta_tokens, g_mix, w_in, conv_w, g_q, w_q_up, g_kv, w_kv_up, g_conv_out, g_attn_out, w_out, g_ffn,
           w_router, b_router, w_gate_up, b_gate_up, w_down, b_down, g_final):
    w = _prep_weights(g_mix[0], w_in[0], conv_w[0], g_q[0], w_q_up[0], g_kv[0], w_kv_up[0], g_conv_out[0],
                      w_out[0], g_ffn[0], w_router[0], b_router[0])
    seq = x.shape[1]
    return _layer(x, meta_tokens, w, w_gate_up[0], b_gate_up[0], w_down[0], b_down[0], g_attn_out[0], g_final,
                  tm_in=min(512, seq), tq=min(512, seq), tm_out=min(512, seq), tmd=min(512, seq),
                  tmc=min(256, seq), fc=256)
```

```python
import functools

import jax
import jax.numpy as jnp
import numpy as np
from jax import lax
from jax.experimental import pallas as pl
from jax.experimental.pallas import tpu as pltpu

N_META = 16
EPS = 1e-6
D_CONV = 1024
CONV_GROUPS = 16
CONV_WIDTH = 3
N_HEADS = 8
QK_NOPE = 128
QK_ROPE = 64
V_DIM = 128
Q_LORA = 512
KV_LORA = 256
ROPE_THETA = 10000.0
N_EXPERTS = 32
TOP_K = 4
SWIGLU_LIMIT = 7.0
SWIGLU_ALPHA = 1.702

LANES = 128
SUBLANES = 8
META_ROWS = 128
HEAD_W = 2 * LANES
ATTN_W = N_HEADS * V_DIM
V_EXT = V_DIM + 16
VMEM_LIMIT = 56 * 1024 * 1024

O_B, O_C, O_U = 0, D_CONV, 2 * D_CONV
O_Q = 3 * D_CONV
O_KV = O_Q + Q_LORA
O_KPE = O_KV + KV_LORA
D_IN = O_KPE + QK_ROPE

F32 = jnp.float32
BF16 = jnp.bfloat16
NT_DIMS = (((1,), (1,)), ((), ()))


def _rms(x, g):
    return x * lax.rsqrt(jnp.mean(x * x, axis=-1, keepdims=True) + EPS) * g


def _dot(a, b):
    return jnp.dot(a, b, preferred_element_type=F32)


def _resident(shape):
    zeros = (0,) * len(shape)
    return pl.BlockSpec(shape, lambda *_: zeros, pipeline_mode=pl.Buffered(1))


def _mix_in_kernel(x_ref, gmix_ref, w1_ref, wk2_ref, cw_ref, gq_ref, wq_ref, gkv_ref, wkv_ref, gco_ref, gmat_ref,
                   tab_ref, uinit_ref,
                   convn_ref, qt_ref, kn_ref, kpe_ref, vt_ref, utail_ref,
                   ubuf_ref, *, tm, scale):
    i = pl.program_id(1)
    hn = _rms(x_ref[0], gmix_ref[...]).astype(BF16)

    def proj(lo, hi):
        return lax.dot_general(hn, w1_ref[lo:hi, :], NT_DIMS, preferred_element_type=F32)

    @pl.when(i == 0)
    def _():
        ubuf_ref[0:SUBLANES] = uinit_ref[...]

    @pl.when(i > 0)
    def _():
        ubuf_ref[0:SUBLANES] = ubuf_ref[tm:tm + SUBLANES]

    u = proj(O_C, O_U) * proj(O_U, O_Q)
    ubuf_ref[SUBLANES:SUBLANES + tm] = u
    cw = cw_ref[...]
    y = (cw[2:3] * u + cw[1:2] * ubuf_ref[SUBLANES - 1:SUBLANES - 1 + tm]
         + cw[0:1] * ubuf_ref[SUBLANES - 2:SUBLANES - 2 + tm])
    co = proj(O_B, O_C) * y
    ss = _dot((co * co).astype(BF16), gmat_ref[...])
    group = D_CONV // CONV_GROUPS
    convn_ref[0] = (co * lax.rsqrt(ss * (1.0 / group) + EPS) * gco_ref[...]).astype(BF16)
    utail_ref[...] = ubuf_ref[tm:tm + SUBLANES]

    cos = tab_ref[:, :LANES]
    sin = tab_ref[:, LANES:]

    qn = _rms(proj(O_Q, O_KV), gq_ref[...]).astype(BF16)
    qa = _dot(qn, wq_ref[:, :N_HEADS * HEAD_W])
    qb = _dot(qn, wq_ref[:, N_HEADS * HEAD_W:])
    for h in range(N_HEADS):
        c0 = h * HEAD_W
        qt_ref[0, c0:c0 + LANES, :] = (qa[:, c0:c0 + LANES] * scale).T.astype(BF16)
        pe = qa[:, c0 + LANES:c0 + HEAD_W] * cos + qb[:, h * LANES:(h + 1) * LANES] * sin
        qt_ref[0, c0 + LANES:c0 + HEAD_W, :] = (pe * scale).T.astype(BF16)

    kvn = _rms(proj(O_KV, O_KPE), gkv_ref[...]).astype(BF16)
    kv = _dot(kvn, wkv_ref[...])
    kn_ref[0] = kv[:, :ATTN_W].astype(BF16)
    vt = kv[:, ATTN_W:].T.astype(BF16)
    ones = jnp.ones((V_EXT - V_DIM, tm), BF16)
    for h in range(N_HEADS):
        vt_ref[0, 0, h * V_EXT:h * V_EXT + V_DIM, :] = vt[h * V_DIM:(h + 1) * V_DIM]
        vt_ref[0, 0, h * V_EXT + V_DIM:(h + 1) * V_EXT, :] = ones
    kk = lax.dot_general(hn, wk2_ref[...], NT_DIMS, preferred_element_type=F32)
    kpe_ref[0] = (kk[:, :LANES] * cos + kk[:, LANES:] * sin).astype(BF16)


def _mix_in(x3, uinit, tab, w, *, tm):
    b, l, d = x3.shape
    nt = l // tm
    kern = functools.partial(_mix_in_kernel, tm=tm, scale=float((QK_NOPE + QK_ROPE) ** -0.5))
    row = lambda width: pl.BlockSpec((1, tm, width), lambda bi, i: (bi, i, 0))
    return pl.pallas_call(
        kern,
        grid=(b, nt),
        in_specs=[
            row(d),
            _resident((1, d)),
            _resident((D_IN, d)),
            _resident((2 * LANES, d)),
            _resident((SUBLANES, D_CONV)),
            _resident((1, Q_LORA)),
            _resident((Q_LORA, N_HEADS * (HEAD_W + LANES))),
            _resident((1, KV_LORA)),
            _resident((KV_LORA, 2 * ATTN_W)),
            _resident((1, D_CONV)),
            _resident((D_CONV, D_CONV)),
            pl.BlockSpec((tm, 2 * LANES), lambda bi, i: (i, 0)),
            _resident((SUBLANES, D_CONV)),
        ],
        out_specs=[
            row(D_CONV),
            pl.BlockSpec((1, N_HEADS * HEAD_W, tm), lambda bi, i: (bi, 0, i)),
            row(ATTN_W), row(LANES),
            pl.BlockSpec((1, 1, N_HEADS * V_EXT, tm), lambda bi, i: (bi, i, 0, 0)),
            pl.BlockSpec((SUBLANES, D_CONV), lambda bi, i: (bi * nt + i, 0)),
        ],
        out_shape=[
            jax.ShapeDtypeStruct((b, l, D_CONV), BF16),
            jax.ShapeDtypeStruct((b, N_HEADS * HEAD_W, l), BF16),
            jax.ShapeDtypeStruct((b, l, ATTN_W), BF16),
            jax.ShapeDtypeStruct((b, l, LANES), BF16),
            jax.ShapeDtypeStruct((b, nt, N_HEADS * V_EXT, tm), BF16),
            jax.ShapeDtypeStruct((b * nt * SUBLANES, D_CONV), F32),
        ],
        scratch_shapes=[pltpu.VMEM((tm + SUBLANES, D_CONV), F32)],
        compiler_params=pltpu.CompilerParams(
            dimension_semantics=("arbitrary", "arbitrary"), vmem_limit_bytes=VMEM_LIMIT),
        name="mix_in",
    )(x3, w["g_mix"], w["w1"], w["wk2"], w["conv_w"], w["g_q"], w["wq"], w["g_kv"], w["wkv"], w["g_conv_out"],
      w["gmat"], tab, uinit)


def _attn_kernel(qt_ref, kn_ref, kpe_ref, vt_ref, mkn_ref, mkpe_ref, mvt_ref, g_ref, o_ref,
                 m_ref, acc_ref, *, tq, hps):
    qi = pl.program_id(2)
    tv = vt_ref.shape[3]

    def lanes(hh):
        return slice(hh * LANES, (hh + 1) * LANES)

    def vrows(hh):
        return slice(hh * V_EXT, (hh + 1) * V_EXT)

    def qt(hh):
        return qt_ref[0, hh * HEAD_W:(hh + 1) * HEAD_W, :]

    for hh in range(hps):
        km = jnp.concatenate([mkn_ref[0, :, lanes(hh)], mkpe_ref[0]], axis=-1)
        s = _dot(km, qt(hh))
        row = lax.broadcasted_iota(jnp.int32, s.shape, 0)
        s = jnp.where(row >= META_ROWS - N_META, s, -jnp.inf)
        m0 = jnp.max(s, axis=0, keepdims=True)
        m_ref[hh] = m0
        acc_ref[hh] = _dot(mvt_ref[0, 0, vrows(hh), :], jnp.exp((s - m0).astype(BF16)))

    def step(kb, diagonal):
        off = pl.multiple_of(kb * tq, tq)
        kpe = kpe_ref[0, pl.ds(off, tq), :]
        scores = []
        for hh in range(hps):
            k = jnp.concatenate([kn_ref[0, pl.ds(off, tq), lanes(hh)], kpe], axis=-1)
            s = _dot(k, qt(hh))
            if diagonal:
                r = lax.broadcasted_iota(jnp.int32, s.shape, 0)
                c = lax.broadcasted_iota(jnp.int32, s.shape, 1)
                s = jnp.where(r <= c, s, -jnp.inf)
            scores.append(s)
        probs, alphas = [], []
        for hh in range(hps):
            m_prev = m_ref[hh]
            m_new = jnp.maximum(m_prev, jnp.max(scores[hh], axis=0, keepdims=True))
            alphas.append(jnp.exp(m_prev - m_new))
            probs.append(jnp.exp((scores[hh] - m_new).astype(BF16)))
            m_ref[hh] = m_new
        for hh in range(hps):
            pb = probs[hh]
            pv = _dot(vt_ref[0, kb * (tq // tv), vrows(hh), :], pb[:tv])
            for c in range(1, tq // tv):
                pv = pv + _dot(vt_ref[0, kb * (tq // tv) + c, vrows(hh), :], pb[c * tv:(c + 1) * tv])
            acc_ref[hh] = alphas[hh] * acc_ref[hh] + pv

    def body(kb, carry):
        step(kb, False)
        return carry

    lax.fori_loop(0, qi, body, 0)
    step(qi, True)

    for hh in range(hps):
        acc = acc_ref[hh]
        o = (acc[:V_DIM] / acc[V_DIM:V_DIM + 1]).T
        o_ref[0, :, lanes(hh)] = _rms(o, g_ref[:, lanes(hh)]).astype(BF16)


def _attention(qt, kn, kpe, vt, mkn, mkpe, mvt, g_attn, *, tq, hps=4):
    b, l, _ = kn.shape
    _, nt, _, tv = vt.shape
    nq = l // tq
    kern = functools.partial(_attn_kernel, tq=tq, hps=hps)
    return pl.pallas_call(
        kern,
        grid=(b, N_HEADS // hps, nq),
        in_specs=[
            pl.BlockSpec((1, hps * HEAD_W, tq), lambda bi, h, i: (bi, h, i)),
            pl.BlockSpec((1, l, hps * LANES), lambda bi, h, i: (bi, 0, h)),
            pl.BlockSpec((1, l, LANES), lambda bi, h, i: (bi, 0, 0)),
            pl.BlockSpec((1, nt, hps * V_EXT, tv), lambda bi, h, i: (bi, 0, h, 0)),
            pl.BlockSpec((1, META_ROWS, hps * LANES), lambda bi, h, i: (0, 0, h)),
            pl.BlockSpec((1, META_ROWS, LANES), lambda bi, h, i: (0, 0, 0)),
            pl.BlockSpec((1, 1, hps * V_EXT, META_ROWS), lambda bi, h, i: (0, 0, h, 0)),
            pl.BlockSpec((1, hps * V_DIM), lambda bi, h, i: (0, h)),
        ],
        out_specs=pl.BlockSpec((1, tq, hps * V_DIM), lambda bi, h, i: (bi, i, h)),
        out_shape=jax.ShapeDtypeStruct((b, l, ATTN_W), BF16),
        scratch_shapes=[pltpu.VMEM((hps, 1, tq), F32), pltpu.VMEM((hps, V_EXT, tq), F32)],
        compiler_params=pltpu.CompilerParams(
            dimension_semantics=("arbitrary", "arbitrary", "arbitrary"), vmem_limit_bytes=VMEM_LIMIT),
        name="attn",
    )(qt, kn, kpe, vt, mkn, mkpe, mvt, g_attn)


def _mix_out_kernel(convn_ref, attn_ref, x_ref, wo_ref, gffn_ref, wr_ref, br_ref,
                    h1_ref, xn_ref, ri_ref, rf_ref, cnt_ref, carry_ref, *, tm):
    @pl.when(pl.program_id(0) == 0)
    def _():
        carry_ref[...] = jnp.zeros_like(carry_ref)

    h1 = x_ref[...] + _dot(convn_ref[...], wo_ref[:D_CONV]) + _dot(attn_ref[...], wo_ref[D_CONV:])
    h1_ref[...] = h1
    xn = _rms(h1, gffn_ref[...])
    xn_ref[...] = xn

    xh = xn.astype(BF16)
    xl = (xn - xh.astype(F32)).astype(BF16)
    ph = _dot(xh, wr_ref[...])
    logits = ph[:, :LANES] + ph[:, LANES:] + _dot(xl, wr_ref[:, :LANES]) + br_ref[...]

    lane = lax.broadcasted_iota(jnp.int32, (tm, LANES), 1).astype(F32)
    work = logits
    top_v, top_i, onehots = [], [], []
    for _ in range(TOP_K):
        mk = jnp.max(work, axis=-1, keepdims=True)
        ik = jnp.min(jnp.where(work == mk, lane, float(LANES)), axis=-1, keepdims=True)
        oh = lane == ik
        work = jnp.where(oh, -jnp.inf, work)
        top_v.append(mk)
        top_i.append(ik)
        onehots.append(oh)

    ex = [jnp.exp(v - top_v[0]) for v in top_v]
    denom = ex[0] + ex[1] + ex[2] + ex[3]
    gates = [e / denom for e in ex]

    sel = jnp.zeros((tm, LANES), F32)
    for oh in onehots:
        sel = sel + oh.astype(F32)
    r = lax.broadcasted_iota(jnp.int32, (tm, tm), 0)
    c = lax.broadcasted_iota(jnp.int32, (tm, tm), 1)
    tri = jnp.where(c < r, 1.0, 0.0).astype(BF16)
    cum = _dot(tri, sel.astype(BF16)) + carry_ref[...]
    carry_ref[...] = carry_ref[...] + jnp.sum(sel, axis=0, keepdims=True)
    cnt_ref[...] = carry_ref[...]

    ri = jnp.zeros((tm, LANES), F32)
    rf = jnp.zeros((tm, LANES), F32)
    for k in range(TOP_K):
        rank_k = jnp.sum(jnp.where(onehots[k], cum, 0.0), axis=-1, keepdims=True)
        ri = jnp.where(lane == float(k), top_i[k], ri)
        ri = jnp.where(lane == float(TOP_K + k), rank_k, ri)
        rf = jnp.where(lane == float(k), gates[k], rf)
    ri_ref[...] = ri.T[:2 * TOP_K].astype(jnp.int32)
    rf_ref[...] = rf


def _mix_out(convn, attn, x2, w, *, tm):
    t, d = x2.shape
    kern = functools.partial(_mix_out_kernel, tm=tm)
    row = lambda width: pl.BlockSpec((tm, width), lambda i: (i, 0))
    return pl.pallas_call(
        kern,
        grid=(t // tm,),
        in_specs=[
            row(D_CONV), row(ATTN_W), row(d),
            _resident((D_CONV + ATTN_W, d)),
            _resident((1, d)),
            _resident((d, 2 * LANES)),
            _resident((1, LANES)),
        ],
        out_specs=[row(d), row(d), pl.BlockSpec((2 * TOP_K, tm), lambda i: (0, i)), row(LANES),
                   pl.BlockSpec((1, LANES), lambda i: (0, 0))],
        out_shape=[
            jax.ShapeDtypeStruct((t, d), F32),
            jax.ShapeDtypeStruct((t, d), F32),
            jax.ShapeDtypeStruct((2 * TOP_K, t), jnp.int32),
            jax.ShapeDtypeStruct((t, LANES), F32),
            jax.ShapeDtypeStruct((1, LANES), F32),
        ],
        scratch_shapes=[pltpu.VMEM((1, LANES), F32)],
        compiler_params=pltpu.CompilerParams(
            dimension_semantics=("arbitrary",), vmem_limit_bytes=VMEM_LIMIT),
        name="mix_out",
    )(convn, attn, x2, w["wo"], w["g_ffn"], w["wr"], w["br"])


def _dispatch_kernel(dest_ref, pend_ref, xn_ref, xg_hbm, zbuf_ref, sem, zsem, *, tmd, sb):
    i = pl.program_id(0)

    @pl.when(i == 0)
    def _():
        zbuf_ref[...] = jnp.zeros_like(zbuf_ref)

        def zero_copy(e):
            end = pend_ref[e]
            start = pl.multiple_of(jnp.maximum(end - sb, 0), sb)
            return pltpu.make_async_copy(zbuf_ref, xg_hbm.at[pl.ds(start, sb)], zsem)

        def nonempty(e):
            prev = pend_ref[jnp.maximum(e - 1, 0)]
            return pend_ref[e] > jnp.where(e > 0, prev, 0)

        def start(e, carry):
            @pl.when(nonempty(e))
            def _():
                zero_copy(e).start()
            return carry

        def wait(e, carry):
            @pl.when(nonempty(e))
            def _():
                zero_copy(e).wait()
            return carry

        lax.fori_loop(0, N_EXPERTS, start, 0)
        lax.fori_loop(0, N_EXPERTS, wait, 0)
        _zero_tail(zbuf_ref, xg_hbm, pend_ref[N_EXPERTS - 1], zsem)

    base = i * tmd
    n_tok = pl.num_programs(0) * tmd

    def issue(r, carry):
        for k in range(TOP_K):
            dst = xg_hbm.at[pl.ds(dest_ref[k * n_tok + base + r], 1)]
            pltpu.make_async_copy(xn_ref.at[pl.ds(r, 1)], dst, sem).start(priority=k % 2)
        return carry

    lax.fori_loop(0, tmd, issue, 0)
    for k in range(TOP_K):
        pltpu.make_async_copy(xn_ref, xg_hbm.at[pl.ds(0, tmd)], sem).wait()


def _dispatch(dest, pad_end, xn, *, rows, tmd, sb):
    t, d = xn.shape
    kern = functools.partial(_dispatch_kernel, tmd=tmd, sb=sb)
    return pl.pallas_call(
        kern,
        grid_spec=pltpu.PrefetchScalarGridSpec(
            num_scalar_prefetch=2,
            grid=(t // tmd,),
            in_specs=[pl.BlockSpec((tmd, d), lambda i, dest, pend: (i, 0))],
            out_specs=pl.BlockSpec(memory_space=pl.ANY),
            scratch_shapes=[pltpu.VMEM((sb, d), F32), pltpu.SemaphoreType.DMA, pltpu.SemaphoreType.DMA],
        ),
        out_shape=jax.ShapeDtypeStruct((rows, d), F32),
        compiler_params=pltpu.CompilerParams(
            dimension_semantics=("arbitrary",), vmem_limit_bytes=VMEM_LIMIT),
        name="dispatch",
    )(dest, pad_end, xn)


def _expert_kernel(ie_ref, irow_ref, inr_ref, tail_ref, xg_hbm, wg_ref, wl_ref, wd_ref, bias_ref, yg_hbm,
                   xst_ref, xb_ref, act_ref, acc_ref, sem_in, sem_out,
                   *, tm, sb, row_sizes):
    it = pl.program_id(0)
    s = pl.program_id(1)
    n_it = pl.num_programs(0)
    nrows = inr_ref[it]
    n_up = act_ref.shape[0]
    n_down, _, fc = acc_ref.shape
    last = n_up + n_down - 1
    prev_it = jnp.maximum(it - 1, 0)
    next_it = jnp.minimum(it + 1, n_it - 1)

    def piece(r):
        return pl.ds(r * sb, sb)

    def hbm_rows(item, r):
        return pl.ds(pl.multiple_of(irow_ref[item] + r * sb, sb), sb)

    pin = tm // n_down

    def in_piece(q):
        return pl.ds(pl.multiple_of(q * pin, 16), pin)

    def in_copy(item, q):
        src = xg_hbm.at[pl.ds(pl.multiple_of(irow_ref[item] + q * pin, 8), pin)]
        return pltpu.make_async_copy(src, xst_ref.at[in_piece(q)], sem_in.at[q])

    def land_in(item, q):
        in_copy(item, q).wait()
        xb_ref[in_piece(q)] = xst_ref[in_piece(q)].astype(BF16)

    def out_copies(item, r):
        return [pltpu.make_async_copy(acc_ref.at[n, piece(r)], yg_hbm.at[hbm_rows(item, r), n * fc:(n + 1) * fc],
                                      sem_out.at[r]) for n in range(n_down)]

    def for_pieces_of(item, fn):
        for r in range(tm // sb):
            pl.when(r * sb < inr_ref[item])(functools.partial(fn, item, r))

    def for_valid_chunks(fn):
        for lo, size in zip((0,) + row_sizes[:-1], row_sizes):
            pl.when((lo < nrows) & (nrows <= size))(functools.partial(fn, pl.ds(0, size)))

    def start_out(item, r):
        for cp in out_copies(item, r):
            cp.start()

    def wait_out(item, r):
        for cp in out_copies(item, r):
            cp.wait()

    @pl.when((it == 0) & (s == 0))
    def _():
        for q in range(n_down):
            in_copy(it, q).start()
        for q in range(n_down):
            land_in(it, q)

    @pl.when(s == 1)
    def _():
        for q in range(n_down):
            in_copy(next_it, q).start()

    @pl.when((s == n_up) & (it > 0))
    def _():
        for_pieces_of(prev_it, wait_out)

    def up(rows):
        xs = xb_ref[rows]
        bg = bias_ref[0, pl.ds(s, 1), :]
        bl = bias_ref[0, pl.ds(n_up + s, 1), :]
        g = jnp.minimum(_dot(xs, wg_ref[0].astype(BF16)) + bg, SWIGLU_LIMIT)
        lin = jnp.clip(_dot(xs, wl_ref[0].astype(BF16)) + bl, -SWIGLU_LIMIT, SWIGLU_LIMIT)
        act_ref[s, rows] = ((lin + 1.0) * (g * jax.nn.sigmoid(SWIGLU_ALPHA * g))).astype(BF16)

    def down(rows):
        land_in(next_it, s - n_up)
        act = jnp.concatenate([act_ref[jj, rows] for jj in range(n_up)], axis=-1)
        per_step = fc // act_ref.shape[2]
        first = 2 * n_up + (s - n_up) * per_step
        bd = jnp.concatenate([bias_ref[0, pl.ds(first + c, 1), :] for c in range(per_step)], axis=-1)
        acc_ref[s - n_up, rows] = _dot(act, wd_ref[0].astype(BF16)) + bd

    @pl.when(s < n_up)
    def _():
        for_valid_chunks(up)

    @pl.when(s >= n_up)
    def _():
        for_valid_chunks(down)

    @pl.when(s == last)
    def _():
        for_pieces_of(it, start_out)

    @pl.when((it == n_it - 1) & (s == last))
    def _():
        for_pieces_of(it, wait_out)
        acc_ref[0, piece(0)] = jnp.zeros((sb, fc), F32)
        _zero_tail(acc_ref.at[0, piece(0)], yg_hbm, tail_ref[0], sem_out.at[0])


def _zero_tail(zeros_vmem, dst_hbm, first_row, sem):
    sb, width = zeros_vmem.shape
    n_blocks = (dst_hbm.shape[0] - first_row) // sb

    def fill(b, carry):
        rows = pl.ds(pl.multiple_of(first_row + b * sb, sb), sb)
        copies = [pltpu.make_async_copy(zeros_vmem, dst_hbm.at[rows, c * width:(c + 1) * width], sem)
                  for c in range(dst_hbm.shape[1] // width)]
        for cp in copies:
            cp.start()
        for cp in copies:
            cp.wait()
        return carry

    lax.fori_loop(0, n_blocks, fill, 0)


def _experts(n_used, item_e, item_row, item_nrows, tail, xg, w_gate_up, b_gate_up, w_down, b_down, *, tm, sb, fc):
    d = xg.shape[1]
    rows = xg.shape[0] - tm
    dff = w_down.shape[1]
    n_up = dff // fc
    fd = 2 * fc
    n_down = d // fd
    nsub = tm // sb
    kern = functools.partial(_expert_kernel, tm=tm, sb=sb, row_sizes=_row_sizes(tm, sb))
    n_e = w_down.shape[0]
    bias_rows = [b_gate_up.reshape(n_e, 2 * n_up, fc), b_down.reshape(n_e, d // fc, fc)]
    n_rows = 2 * n_up + d // fc
    bias_rows.append(jnp.zeros((n_e, -n_rows % SUBLANES, fc), F32))
    biases = jnp.concatenate(bias_rows, axis=1)

    def up_block(offset):
        def index_map(it, s, ie, ir, inr, tl):
            ahead = s >= n_up
            nxt = jnp.minimum(it + 1, tl[1] - 1)
            return jnp.where(ahead, ie[nxt], ie[it]), 0, offset + jnp.where(ahead, 0, s)
        return index_map

    def down_block(it, s, ie, ir, inr, tl):
        return ie[it], 0, jnp.maximum(s - n_up, 0)

    return pl.pallas_call(
        kern,
        grid_spec=pltpu.PrefetchScalarGridSpec(
            num_scalar_prefetch=4,
            grid=(n_used, n_up + n_down),
            in_specs=[
                pl.BlockSpec(memory_space=pl.ANY),
                pl.BlockSpec((1, d, fc), up_block(0)),
                pl.BlockSpec((1, d, fc), up_block(n_up)),
                pl.BlockSpec((1, dff, fd), down_block),
                pl.BlockSpec((1, biases.shape[1], fc), lambda it, s, ie, ir, inr, tl: (ie[it], 0, 0)),
            ],
            out_specs=pl.BlockSpec(memory_space=pl.ANY),
            scratch_shapes=[
                pltpu.VMEM((tm, d), F32),
                pltpu.VMEM((tm, d), BF16),
                pltpu.VMEM((n_up, tm, fc), BF16),
                pltpu.VMEM((n_down, tm, fd), F32),
                pltpu.SemaphoreType.DMA((n_down,)),
                pltpu.SemaphoreType.DMA((nsub,)),
            ],
        ),
        out_shape=jax.ShapeDtypeStruct((rows, d), F32),
        compiler_params=pltpu.CompilerParams(
            dimension_semantics=("arbitrary", "arbitrary"), vmem_limit_bytes=VMEM_LIMIT),
        name="experts",
    )(item_e, item_row, item_nrows, tail, xg, w_gate_up, w_gate_up, w_down, biases)


def _combine_kernel(dest_ref, yg_hbm, h1_ref, rf_ref, gfin_ref, o_ref, gbuf_ref, sem, *, tmc):
    i = pl.program_id(0)
    n_tiles = pl.num_programs(0)
    n_tok = n_tiles * tmc

    def issue_tile(tile):
        buf = tile % 2
        base = tile * tmc

        def issue(r, carry):
            for k in range(TOP_K):
                src = yg_hbm.at[pl.ds(dest_ref[k * n_tok + base + r], 1)]
                pltpu.make_async_copy(src, gbuf_ref.at[buf, k, pl.ds(r, 1)], sem.at[buf]).start(priority=k % 2)
            return carry

        lax.fori_loop(0, tmc, issue, 0)

    @pl.when(i == 0)
    def _():
        issue_tile(i)

    @pl.when(i + 1 < n_tiles)
    def _():
        issue_tile(i + 1)

    buf = i % 2
    for k in range(TOP_K):
        pltpu.make_async_copy(yg_hbm.at[pl.ds(0, tmc)], gbuf_ref.at[buf, k], sem.at[buf]).wait()

    gates = rf_ref[...]
    y = h1_ref[...]
    for k in range(TOP_K):
        y = y + gates[:, k:k + 1] * gbuf_ref[buf, k]
    o_ref[...] = _rms(y, gfin_ref[...])


def _combine(dest, yg, h1, rf, g_final, *, tmc):
    t, d = h1.shape
    kern = functools.partial(_combine_kernel, tmc=tmc)
    return pl.pallas_call(
        kern,
        grid_spec=pltpu.PrefetchScalarGridSpec(
            num_scalar_prefetch=1,
            grid=(t // tmc,),
            in_specs=[
                pl.BlockSpec(memory_space=pl.ANY),
                pl.BlockSpec((tmc, d), lambda i, dest: (i, 0)),
                pl.BlockSpec((tmc, LANES), lambda i, dest: (i, 0)),
                pl.BlockSpec((1, d), lambda i, dest: (0, 0)),
            ],
            out_specs=pl.BlockSpec((tmc, d), lambda i, dest: (i, 0)),
            scratch_shapes=[pltpu.VMEM((2, TOP_K, tmc, d), F32), pltpu.SemaphoreType.DMA((2,))],
        ),
        out_shape=jax.ShapeDtypeStruct((t, d), F32),
        compiler_params=pltpu.CompilerParams(
            dimension_semantics=("arbitrary",), vmem_limit_bytes=VMEM_LIMIT),
        name="combine",
    )(dest, yg, h1, rf, g_final)


def _cast_kernel(x_ref, o_ref):
    o_ref[...] = x_ref[...].astype(o_ref.dtype)


def _to_bf16(w, *, n_blocks):
    rows, cols = w.shape
    spec = pl.BlockSpec((rows // n_blocks, cols), lambda i: (i, 0))
    return pl.pallas_call(
        _cast_kernel, grid=(n_blocks,), in_specs=[spec], out_specs=spec,
        out_shape=jax.ShapeDtypeStruct(w.shape, BF16),
        compiler_params=pltpu.CompilerParams(dimension_semantics=("arbitrary",), vmem_limit_bytes=VMEM_LIMIT),
        name="to_bf16",
    )(w)


def _rotate_half_cols(w):
    half = w.shape[-1] // 2
    return jnp.concatenate([-w[..., half:], w[..., :half]], axis=-1)


def _prep_weights(g_mix, w_in, conv_w, g_q, w_q_up, g_kv, w_kv_up, g_conv_out, w_out, g_ffn, w_router, b_router):
    d = w_in.shape[0]
    w1 = _to_bf16(w_in.T, n_blocks=4)
    k_rope = w1[O_KPE:O_KPE + QK_ROPE]
    pad = jnp.zeros((LANES - QK_ROPE, d), BF16)
    half = QK_ROPE // 2
    k_rot = jnp.concatenate([-k_rope[half:], k_rope[:half]], axis=0)
    wk2 = jnp.concatenate([k_rope, pad, k_rot, pad], axis=0)

    wq = w_q_up.reshape(Q_LORA, N_HEADS, QK_NOPE + QK_ROPE)
    nope, pe = wq[:, :, :QK_NOPE], wq[:, :, QK_NOPE:]
    hpad = jnp.zeros((Q_LORA, N_HEADS, LANES - QK_ROPE), F32)
    wqa = jnp.concatenate([nope, pe, hpad], axis=2).reshape(Q_LORA, N_HEADS * HEAD_W)
    wqb = jnp.concatenate([_rotate_half_cols(pe), hpad], axis=2).reshape(Q_LORA, N_HEADS * LANES)
    wq_all = jnp.concatenate([wqa, wqb], axis=1).astype(BF16)

    wkv = w_kv_up.reshape(KV_LORA, N_HEADS, QK_NOPE + V_DIM)
    wkv2 = jnp.concatenate([wkv[:, :, :QK_NOPE].reshape(KV_LORA, ATTN_W),
                            wkv[:, :, QK_NOPE:].reshape(KV_LORA, ATTN_W)], axis=1).astype(BF16)

    wr_pad = jnp.zeros((d, LANES), F32).at[:, :N_EXPERTS].set(w_router)
    wr_hi = wr_pad.astype(BF16)
    wr_lo = (wr_pad - wr_hi.astype(F32)).astype(BF16)
    br = jnp.full((1, LANES), -1e30, F32).at[0, :N_EXPERTS].set(b_router)

    grp = jnp.arange(D_CONV) // (D_CONV // CONV_GROUPS)
    cw = jnp.zeros((SUBLANES, D_CONV), F32).at[:CONV_WIDTH].set(conv_w)
    return {
        "g_mix": g_mix[None], "w1": w1, "wk2": wk2, "conv_w": cw, "g_q": g_q[None], "wq": wq_all, "g_kv": g_kv[None],
        "wkv": wkv2, "g_conv_out": g_conv_out[None], "gmat": (grp[:, None] == grp[None, :]).astype(BF16),
        "wo": w_out.astype(BF16), "g_ffn": g_ffn[None], "wr": jnp.concatenate([wr_hi, wr_lo], axis=1), "br": br,
    }


def _rope_table(pos):
    half = QK_ROPE // 2
    inv_freq = np.float32(ROPE_THETA) ** (-np.arange(half, dtype=np.float32) / np.float32(half))
    ang = (np.asarray(pos, np.float32)[:, None] * inv_freq[None, :]).astype(np.float32)
    c, s = np.cos(ang), np.sin(ang)
    z = np.zeros((ang.shape[0], LANES - QK_ROPE), np.float32)
    return jnp.asarray(np.concatenate([c, c, z, s, s, z], axis=1), dtype=F32)


def _schedule_kernel(cnt_ref, ri_ref, dest_ref, pend_ref, ie_ref, irow_ref, inr_ref, tail_ref, pstart_ref,
                     *, tm, sb, n_items):
    def per_expert(e, carry):
        row, item = carry
        count = cnt_ref[e]
        pstart_ref[e] = row

        def per_item(li, item):
            ie_ref[item] = e
            irow_ref[item] = row + li * tm
            inr_ref[item] = jnp.minimum(count - li * tm, tm)
            return item + 1

        item = lax.fori_loop(0, lax.div(count + (tm - 1), tm), per_item, item)
        row = row + lax.div(count + (sb - 1), sb) * sb
        pend_ref[e] = row
        return row, item

    row, n_used = lax.fori_loop(0, N_EXPERTS, per_expert, (jnp.int32(0), jnp.int32(0)))
    tail_ref[0] = row
    tail_ref[1] = n_used

    def unused(item, carry):
        ie_ref[item] = 0
        irow_ref[item] = 0
        inr_ref[item] = 0
        return carry

    lax.fori_loop(n_used, n_items, unused, 0)

    eidx = ri_ref[:TOP_K]
    start_of = jnp.zeros_like(eidx)
    for e in range(N_EXPERTS):
        start_of = jnp.where(eidx == e, pstart_ref[e], start_of)
    dest_ref[...] = start_of + ri_ref[TOP_K:]


def _schedule(counts, ri, *, tm, sb, n_items):
    t = ri.shape[1]
    smem = pl.BlockSpec(memory_space=pltpu.SMEM)
    i32 = lambda n: jax.ShapeDtypeStruct((n,), jnp.int32)
    dest, pad_end, item_e, item_row, item_nrows, tail = pl.pallas_call(
        functools.partial(_schedule_kernel, tm=tm, sb=sb, n_items=n_items),
        in_specs=[smem, pl.BlockSpec(memory_space=pltpu.VMEM)],
        out_specs=[pl.BlockSpec(memory_space=pltpu.VMEM), smem, smem, smem, smem, smem],
        out_shape=[jax.ShapeDtypeStruct((TOP_K, t), jnp.int32), i32(N_EXPERTS), i32(n_items), i32(n_items),
                   i32(n_items), i32(2)],
        scratch_shapes=[pltpu.SMEM((N_EXPERTS,), jnp.int32)],
        name="schedule",
    )(counts, ri)
    return dest.reshape(-1), pad_end, item_e, item_row, item_nrows, tail


def _row_sizes(tm, sb):
    fine = [tm - k * sb for k in (2, 1, 0) if tm - k * sb > 0]
    coarse = [s for s in (tm // 4 // sb * sb, tm // 2 // sb * sb) if 0 < s < min(fine)]
    return tuple(sorted(set(coarse + fine)))


def _moe_tiles(t):
    sb = 128
    tm = 10 * sb
    a = t * TOP_K
    rows = (a + N_EXPERTS * (sb - 1) + sb - 1) // sb * sb
    n_items = N_EXPERTS + a // tm
    return sb, tm, rows, n_items


def _layer(x, meta_tokens, w, w_gate_up, b_gate_up, w_down, b_down, g_attn_out, g_final, *,
           tm_in, tq, tm_out, tmd, tmc, fc):
    b, seq, d = x.shape
    t = b * seq

    meta_blk = jnp.zeros((1, META_ROWS, d), F32).at[0, META_ROWS - N_META:].set(meta_tokens)
    meta_pos = np.maximum(np.arange(META_ROWS) - (META_ROWS - N_META), 0)
    zero_tail = jnp.zeros((SUBLANES, D_CONV), F32)
    _, _, mkn, mkpe, mv, u_tail = _mix_in(meta_blk, zero_tail, _rope_table(meta_pos), w, tm=META_ROWS)

    real_pos = np.arange(seq) + N_META
    convn, q, kn, kpe, v, _ = _mix_in(x, u_tail, _rope_table(real_pos), w, tm=tm_in)
    attn = _attention(q, kn, kpe, v, mkn, mkpe, mv, g_attn_out[None], tq=tq)

    h1, xn, ri, rf, cnt = _mix_out(convn.reshape(t, D_CONV), attn.reshape(t, ATTN_W), x.reshape(t, d), w, tm=tm_out)

    sb, tm_e, rows, n_items = _moe_tiles(t)
    counts = cnt[0, :N_EXPERTS].astype(jnp.int32)
    dest, pad_end, item_e, item_row, item_nrows, tail = _schedule(counts, ri, tm=tm_e, sb=sb, n_items=n_items)

    xg = _dispatch(dest, pad_end, xn, rows=rows + tm_e, tmd=tmd, sb=sb)
    dff = w_down.shape[1]
    yg = _experts(tail[1], item_e, item_row, item_nrows, tail, xg, w_gate_up, b_gate_up.reshape(N_EXPERTS, 1, 2 * dff),
                  w_down, b_down.reshape(N_EXPERTS, 1, d), tm=tm_e, sb=sb, fc=fc)
    out = _combine(dest, yg, h1, rf, g_final[None], tmc=tmc)
    return out.reshape(b, seq, d)


def kernel(x, meta_tokens, g_mix, w_in, conv_w, g_q, w_q_up, g_kv, w_kv_up, g_conv_out, g_attn_out, w_out, g_ffn,
           w_router, b_router, w_gate_up, b_gate_up, w_down, b_down, g_final):
    w = _prep_weights(g_mix[0], w_in[0], conv_w[0], g_q[0], w_q_up[0], g_kv[0], w_kv_up[0], g_conv_out[0],
                      w_out[0], g_ffn[0], w_router[0], b_router[0])
    seq = x.shape[1]
    return _layer(x, meta_tokens, w, w_gate_up[0], b_gate_up[0], w_down[0], b_down[0], g_attn_out[0], g_final,
                  tm_in=min(512, seq), tq=min(512, seq), tm_out=min(512, seq), tmd=min(512, seq),
                  tmc=min(256, seq), fc=256)
```

```python
import functools

import jax
import jax.numpy as jnp
import numpy as np
from jax import lax
from jax.experimental import pallas as pl
from jax.experimental.pallas import tpu as pltpu

N_META = 16
EPS = 1e-6
D_CONV = 1024
CONV_GROUPS = 16
CONV_WIDTH = 3
N_HEADS = 8
QK_NOPE = 128
QK_ROPE = 64
V_DIM = 128
Q_LORA = 512
KV_LORA = 256
ROPE_THETA = 10000.0
N_EXPERTS = 32
TOP_K = 4
SWIGLU_LIMIT = 7.0
SWIGLU_ALPHA = 1.702

LANES = 128
SUBLANES = 8
META_ROWS = 128
HEAD_W = 2 * LANES
ATTN_W = N_HEADS * V_DIM
V_EXT = V_DIM + 16
VMEM_LIMIT = 56 * 1024 * 1024

O_B, O_C, O_U = 0, D_CONV, 2 * D_CONV
O_Q = 3 * D_CONV
O_KV = O_Q + Q_LORA
O_KPE = O_KV + KV_LORA
D_IN = O_KPE + QK_ROPE

F32 = jnp.float32
BF16 = jnp.bfloat16
NT_DIMS = (((1,), (1,)), ((), ()))


def _rms(x, g):
    return x * lax.rsqrt(jnp.mean(x * x, axis=-1, keepdims=True) + EPS) * g


def _dot(a, b):
    return jnp.dot(a, b, preferred_element_type=F32)


def _resident(shape):
    zeros = (0,) * len(shape)
    return pl.BlockSpec(shape, lambda *_: zeros, pipeline_mode=pl.Buffered(1))


def _mix_in_kernel(x_ref, gmix_ref, w1_ref, wk2_ref, cw_ref, gq_ref, wq_ref, gkv_ref, wkv_ref, gco_ref, gmat_ref,
                   tab_ref, uinit_ref,
                   convn_ref, qt_ref, kn_ref, kpe_ref, vt_ref, utail_ref,
                   ubuf_ref, *, tm, scale):
    i = pl.program_id(1)
    hn = _rms(x_ref[0], gmix_ref[...]).astype(BF16)

    def proj(lo, hi):
        return lax.dot_general(hn, w1_ref[lo:hi, :], NT_DIMS, preferred_element_type=F32)

    @pl.when(i == 0)
    def _():
        ubuf_ref[0:SUBLANES] = uinit_ref[...]

    @pl.when(i > 0)
    def _():
        ubuf_ref[0:SUBLANES] = ubuf_ref[tm:tm + SUBLANES]

    u = proj(O_C, O_U) * proj(O_U, O_Q)
    ubuf_ref[SUBLANES:SUBLANES + tm] = u
    cw = cw_ref[...]
    y = (cw[2:3] * u + cw[1:2] * ubuf_ref[SUBLANES - 1:SUBLANES - 1 + tm]
         + cw[0:1] * ubuf_ref[SUBLANES - 2:SUBLANES - 2 + tm])
    co = proj(O_B, O_C) * y
    ss = _dot((co * co).astype(BF16), gmat_ref[...])
    group = D_CONV // CONV_GROUPS
    convn_ref[0] = (co * lax.rsqrt(ss * (1.0 / group) + EPS) * gco_ref[...]).astype(BF16)
    utail_ref[...] = ubuf_ref[tm:tm + SUBLANES]

    cos = tab_ref[:, :LANES]
    sin = tab_ref[:, LANES:]

    qn = _rms(proj(O_Q, O_KV), gq_ref[...]).astype(BF16)
    qa = _dot(qn, wq_ref[:, :N_HEADS * HEAD_W])
    qb = _dot(qn, wq_ref[:, N_HEADS * HEAD_W:])
    for h in range(N_HEADS):
        c0 = h * HEAD_W
        qt_ref[0, c0:c0 + LANES, :] = (qa[:, c0:c0 + LANES] * scale).T.astype(BF16)
        pe = qa[:, c0 + LANES:c0 + HEAD_W] * cos + qb[:, h * LANES:(h + 1) * LANES] * sin
        qt_ref[0, c0 + LANES:c0 + HEAD_W, :] = (pe * scale).T.astype(BF16)

    kvn = _rms(proj(O_KV, O_KPE), gkv_ref[...]).astype(BF16)
    kv = _dot(kvn, wkv_ref[...])
    kn_ref[0] = kv[:, :ATTN_W].astype(BF16)
    vt = kv[:, ATTN_W:].T.astype(BF16)
    ones = jnp.ones((V_EXT - V_DIM, tm), BF16)
    for h in range(N_HEADS):
        vt_ref[0, 0, h * V_EXT:h * V_EXT + V_DIM, :] = vt[h * V_DIM:(h + 1) * V_DIM]
        vt_ref[0, 0, h * V_EXT + V_DIM:(h + 1) * V_EXT, :] = ones
    kk = lax.dot_general(hn, wk2_ref[...], NT_DIMS, preferred_element_type=F32)
    kpe_ref[0] = (kk[:, :LANES] * cos + kk[:, LANES:] * sin).astype(BF16)


def _mix_in(x3, uinit, tab, w, *, tm):
    b, l, d = x3.shape
    nt = l // tm
    kern = functools.partial(_mix_in_kernel, tm=tm, scale=float((QK_NOPE + QK_ROPE) ** -0.5))
    row = lambda width: pl.BlockSpec((1, tm, width), lambda bi, i: (bi, i, 0))
    return pl.pallas_call(
        kern,
        grid=(b, nt),
        in_specs=[
            row(d),
            _resident((1, d)),
            _resident((D_IN, d)),
            _resident((2 * LANES, d)),
            _resident((SUBLANES, D_CONV)),
            _resident((1, Q_LORA)),
            _resident((Q_LORA, N_HEADS * (HEAD_W + LANES))),
            _resident((1, KV_LORA)),
            _resident((KV_LORA, 2 * ATTN_W)),
            _resident((1, D_CONV)),
            _resident((D_CONV, D_CONV)),
            pl.BlockSpec((tm, 2 * LANES), lambda bi, i: (i, 0)),
            _resident((SUBLANES, D_CONV)),
        ],
        out_specs=[
            row(D_CONV),
            pl.BlockSpec((1, N_HEADS * HEAD_W, tm), lambda bi, i: (bi, 0, i)),
            row(ATTN_W), row(LANES),
            pl.BlockSpec((1, 1, N_HEADS * V_EXT, tm), lambda bi, i: (bi, i, 0, 0)),
            pl.BlockSpec((SUBLANES, D_CONV), lambda bi, i: (bi * nt + i, 0)),
        ],
        out_shape=[
            jax.ShapeDtypeStruct((b, l, D_CONV), BF16),
            jax.ShapeDtypeStruct((b, N_HEADS * HEAD_W, l), BF16),
            jax.ShapeDtypeStruct((b, l, ATTN_W), BF16),
            jax.ShapeDtypeStruct((b, l, LANES), BF16),
            jax.ShapeDtypeStruct((b, nt, N_HEADS * V_EXT, tm), BF16),
            jax.ShapeDtypeStruct((b * nt * SUBLANES, D_CONV), F32),
        ],
        scratch_shapes=[pltpu.VMEM((tm + SUBLANES, D_CONV), F32)],
        compiler_params=pltpu.CompilerParams(
            dimension_semantics=("arbitrary", "arbitrary"), vmem_limit_bytes=VMEM_LIMIT),
        name="mix_in",
    )(x3, w["g_mix"], w["w1"], w["wk2"], w["conv_w"], w["g_q"], w["wq"], w["g_kv"], w["wkv"], w["g_conv_out"],
      w["gmat"], tab, uinit)


def _attn_kernel(qt_ref, kn_ref, kpe_ref, vt_ref, mkn_ref, mkpe_ref, mvt_ref, g_ref, o_ref,
                 m_ref, acc_ref, *, tq, hps):
    qi = pl.program_id(2)
    tv = vt_ref.shape[3]

    def lanes(hh):
        return slice(hh * LANES, (hh + 1) * LANES)

    def vrows(hh):
        return slice(hh * V_EXT, (hh + 1) * V_EXT)

    def qt(hh):
        return qt_ref[0, hh * HEAD_W:(hh + 1) * HEAD_W, :]

    for hh in range(hps):
        km = jnp.concatenate([mkn_ref[0, :, lanes(hh)], mkpe_ref[0]], axis=-1)
        s = _dot(km, qt(hh))
        row = lax.broadcasted_iota(jnp.int32, s.shape, 0)
        s = jnp.where(row >= META_ROWS - N_META, s, -jnp.inf)
        m0 = jnp.max(s, axis=0, keepdims=True)
        m_ref[hh] = m0
        acc_ref[hh] = _dot(mvt_ref[0, 0, vrows(hh), :], jnp.exp((s - m0).astype(BF16)))

    def step(kb, diagonal):
        off = pl.multiple_of(kb * tq, tq)
        kpe = kpe_ref[0, pl.ds(off, tq), :]
        scores = []
        for hh in range(hps):
            k = jnp.concatenate([kn_ref[0, pl.ds(off, tq), lanes(hh)], kpe], axis=-1)
            s = _dot(k, qt(hh))
            if diagonal:
                r = lax.broadcasted_iota(jnp.int32, s.shape, 0)
                c = lax.broadcasted_iota(jnp.int32, s.shape, 1)
                s = jnp.where(r <= c, s, -jnp.inf)
            scores.append(s)
        probs, alphas = [], []
        for hh in range(hps):
            m_prev = m_ref[hh]
            m_new = jnp.maximum(m_prev, jnp.max(scores[hh], axis=0, keepdims=True))
            alphas.append(jnp.exp(m_prev - m_new))
            probs.append(jnp.exp((scores[hh] - m_new).astype(BF16)))
            m_ref[hh] = m_new
        for hh in range(hps):
            pb = probs[hh]
            pv = _dot(vt_ref[0, kb * (tq // tv), vrows(hh), :], pb[:tv])
            for c in range(1, tq // tv):
                pv = pv + _dot(vt_ref[0, kb * (tq // tv) + c, vrows(hh), :], pb[c * tv:(c + 1) * tv])
            acc_ref[hh] = alphas[hh] * acc_ref[hh] + pv

    def body(kb, carry):
        step(kb, False)
        return carry

    lax.fori_loop(0, qi, body, 0)
    step(qi, True)

    for hh in range(hps):
        acc = acc_ref[hh]
        o = (acc[:V_DIM] / acc[V_DIM:V_DIM + 1]).T
        o_ref[0, :, lanes(hh)] = _rms(o, g_ref[:, lanes(hh)]).astype(BF16)


def _attention(qt, kn, kpe, vt, mkn, mkpe, mvt, g_attn, *, tq, hps=4):
    b, l, _ = kn.shape
    _, nt, _, tv = vt.shape
    nq = l // tq
    kern = functools.partial(_attn_kernel, tq=tq, hps=hps)
    return pl.pallas_call(
        kern,
        grid=(b, N_HEADS // hps, nq),
        in_specs=[
            pl.BlockSpec((1, hps * HEAD_W, tq), lambda bi, h, i: (bi, h, i)),
            pl.BlockSpec((1, l, hps * LANES), lambda bi, h, i: (bi, 0, h)),
            pl.BlockSpec((1, l, LANES), lambda bi, h, i: (bi, 0, 0)),
            pl.BlockSpec((1, nt, hps * V_EXT, tv), lambda bi, h, i: (bi, 0, h, 0)),
            pl.BlockSpec((1, META_ROWS, hps * LANES), lambda bi, h, i: (0, 0, h)),
            pl.BlockSpec((1, META_ROWS, LANES), lambda bi, h, i: (0, 0, 0)),
            pl.BlockSpec((1, 1, hps * V_EXT, META_ROWS), lambda bi, h, i: (0, 0, h, 0)),
            pl.BlockSpec((1, hps * V_DIM), lambda bi, h, i: (0, h)),
        ],
        out_specs=pl.BlockSpec((1, tq, hps * V_DIM), lambda bi, h, i: (bi, i, h)),
        out_shape=jax.ShapeDtypeStruct((b, l, ATTN_W), BF16),
        scratch_shapes=[pltpu.VMEM((hps, 1, tq), F32), pltpu.VMEM((hps, V_EXT, tq), F32)],
        compiler_params=pltpu.CompilerParams(
            dimension_semantics=("arbitrary", "arbitrary", "arbitrary"), vmem_limit_bytes=VMEM_LIMIT),
        name="attn",
    )(qt, kn, kpe, vt, mkn, mkpe, mvt, g_attn)


def _mix_out_kernel(convn_ref, attn_ref, x_ref, wo_ref, gffn_ref, wr_ref, br_ref,
                    h1_ref, xn_ref, ri_ref, rf_ref, cnt_ref, carry_ref, *, tm):
    @pl.when(pl.program_id(0) == 0)
    def _():
        carry_ref[...] = jnp.zeros_like(carry_ref)

    h1 = x_ref[...] + _dot(convn_ref[...], wo_ref[:D_CONV]) + _dot(attn_ref[...], wo_ref[D_CONV:])
    h1_ref[...] = h1
    xn = _rms(h1, gffn_ref[...])
    xn_ref[...] = xn

    xh = xn.astype(BF16)
    xl = (xn - xh.astype(F32)).astype(BF16)
    ph = _dot(xh, wr_ref[...])
    logits = ph[:, :LANES] + ph[:, LANES:] + _dot(xl, wr_ref[:, :LANES]) + br_ref[...]

    lane = lax.broadcasted_iota(jnp.int32, (tm, LANES), 1).astype(F32)
    work = logits
    top_v, top_i, onehots = [], [], []
    for _ in range(TOP_K):
        mk = jnp.max(work, axis=-1, keepdims=True)
        ik = jnp.min(jnp.where(work == mk, lane, float(LANES)), axis=-1, keepdims=True)
        oh = lane == ik
        work = jnp.where(oh, -jnp.inf, work)
        top_v.append(mk)
        top_i.append(ik)
        onehots.append(oh)

    ex = [jnp.exp(v - top_v[0]) for v in top_v]
    denom = ex[0] + ex[1] + ex[2] + ex[3]
    gates = [e / denom for e in ex]

    sel = jnp.zeros((tm, LANES), F32)
    for oh in onehots:
        sel = sel + oh.astype(F32)
    r = lax.broadcasted_iota(jnp.int32, (tm, tm), 0)
    c = lax.broadcasted_iota(jnp.int32, (tm, tm), 1)
    tri = jnp.where(c < r, 1.0, 0.0).astype(BF16)
    cum = _dot(tri, sel.astype(BF16)) + carry_ref[...]
    carry_ref[...] = carry_ref[...] + jnp.sum(sel, axis=0, keepdims=True)
    cnt_ref[...] = carry_ref[...]

    ri = jnp.zeros((tm, LANES), F32)
    rf = jnp.zeros((tm, LANES), F32)
    for k in range(TOP_K):
        rank_k = jnp.sum(jnp.where(onehots[k], cum, 0.0), axis=-1, keepdims=True)
        ri = jnp.where(lane == float(k), top_i[k], ri)
        ri = jnp.where(lane == float(TOP_K + k), rank_k, ri)
        rf = jnp.where(lane == float(k), gates[k], rf)
    ri_ref[...] = ri.T[:2 * TOP_K].astype(jnp.int32)
    rf_ref[...] = rf


def _mix_out(convn, attn, x2, w, *, tm):
    t, d = x2.shape
    kern = functools.partial(_mix_out_kernel, tm=tm)
    row = lambda width: pl.BlockSpec((tm, width), lambda i: (i, 0))
    return pl.pallas_call(
        kern,
        grid=(t // tm,),
        in_specs=[
            row(D_CONV), row(ATTN_W), row(d),
            _resident((D_CONV + ATTN_W, d)),
            _resident((1, d)),
            _resident((d, 2 * LANES)),
            _resident((1, LANES)),
        ],
        out_specs=[row(d), row(d), pl.BlockSpec((2 * TOP_K, tm), lambda i: (0, i)), row(LANES),
                   pl.BlockSpec((1, LANES), lambda i: (0, 0))],
        out_shape=[
            jax.ShapeDtypeStruct((t, d), F32),
            jax.ShapeDtypeStruct((t, d), F32),
            jax.ShapeDtypeStruct((2 * TOP_K, t), jnp.int32),
            jax.ShapeDtypeStruct((t, LANES), F32),
            jax.ShapeDtypeStruct((1, LANES), F32),
        ],
        scratch_shapes=[pltpu.VMEM((1, LANES), F32)],
        compiler_params=pltpu.CompilerParams(
            dimension_semantics=("arbitrary",), vmem_limit_bytes=VMEM_LIMIT),
        name="mix_out",
    )(convn, attn, x2, w["wo"], w["g_ffn"], w["wr"], w["br"])


def _dispatch_kernel(dest_ref, pend_ref, xn_ref, xg_hbm, zbuf_ref, sem, zsem, *, tmd, sb):
    i = pl.program_id(0)

    @pl.when(i == 0)
    def _():
        zbuf_ref[...] = jnp.zeros_like(zbuf_ref)

        def zero_copy(e):
            end = pend_ref[e]
            start = pl.multiple_of(jnp.maximum(end - sb, 0), sb)
            return pltpu.make_async_copy(zbuf_ref, xg_hbm.at[pl.ds(start, sb)], zsem)

        def nonempty(e):
            prev = pend_ref[jnp.maximum(e - 1, 0)]
            return pend_ref[e] > jnp.where(e > 0, prev, 0)

        def start(e, carry):
            @pl.when(nonempty(e))
            def _():
                zero_copy(e).start()
            return carry

        def wait(e, carry):
            @pl.when(nonempty(e))
            def _():
                zero_copy(e).wait()
            return carry

        lax.fori_loop(0, N_EXPERTS, start, 0)
        lax.fori_loop(0, N_EXPERTS, wait, 0)
        _zero_tail(zbuf_ref, xg_hbm, pend_ref[N_EXPERTS - 1], zsem)

    base = i * tmd
    n_tok = pl.num_programs(0) * tmd

    def issue(r, carry):
        for k in range(TOP_K):
            dst = xg_hbm.at[pl.ds(dest_ref[k * n_tok + base + r], 1)]
            pltpu.make_async_copy(xn_ref.at[pl.ds(r, 1)], dst, sem).start(priority=k % 2)
        return carry

    lax.fori_loop(0, tmd, issue, 0)
    for k in range(TOP_K):
        pltpu.make_async_copy(xn_ref, xg_hbm.at[pl.ds(0, tmd)], sem).wait()


def _dispatch(dest, pad_end, xn, *, rows, tmd, sb):
    t, d = xn.shape
    kern = functools.partial(_dispatch_kernel, tmd=tmd, sb=sb)
    return pl.pallas_call(
        kern,
        grid_spec=pltpu.PrefetchScalarGridSpec(
            num_scalar_prefetch=2,
            grid=(t // tmd,),
            in_specs=[pl.BlockSpec((tmd, d), lambda i, dest, pend: (i, 0))],
            out_specs=pl.BlockSpec(memory_space=pl.ANY),
            scratch_shapes=[pltpu.VMEM((sb, d), F32), pltpu.SemaphoreType.DMA, pltpu.SemaphoreType.DMA],
        ),
        out_shape=jax.ShapeDtypeStruct((rows, d), F32),
        compiler_params=pltpu.CompilerParams(
            dimension_semantics=("arbitrary",), vmem_limit_bytes=VMEM_LIMIT),
        name="dispatch",
    )(dest, pad_end, xn)


def _expert_kernel(ie_ref, irow_ref, inr_ref, tail_ref, xg_hbm, wg_ref, wl_ref, wd_ref, bias_ref, yg_hbm,
                   xst_ref, xb_ref, act_ref, acc_ref, sem_in, sem_out,
                   *, tm, sb, row_sizes):
    it = pl.program_id(0)
    s = pl.program_id(1)
    n_it = pl.num_programs(0)
    nrows = inr_ref[it]
    n_up = act_ref.shape[0]
    n_down, _, fc = acc_ref.shape
    last = n_up + n_down - 1
    prev_it = jnp.maximum(it - 1, 0)
    next_it = jnp.minimum(it + 1, n_it - 1)

    def piece(r):
        return pl.ds(r * sb, sb)

    def hbm_rows(item, r):
        return pl.ds(pl.multiple_of(irow_ref[item] + r * sb, sb), sb)

    def in_copy(item, r):
        return pltpu.make_async_copy(xg_hbm.at[hbm_rows(item, r)], xst_ref.at[piece(r)], sem_in.at[r])

    def out_copies(item, r):
        return [pltpu.make_async_copy(acc_ref.at[n, piece(r)], yg_hbm.at[hbm_rows(item, r), n * fc:(n + 1) * fc],
                                      sem_out.at[r]) for n in range(n_down)]

    def for_pieces_of(item, fn):
        for r in range(tm // sb):
            pl.when(r * sb < inr_ref[item])(functools.partial(fn, item, r))

    def for_valid_chunks(fn):
        for lo, size in zip((0,) + row_sizes[:-1], row_sizes):
            pl.when((lo < nrows) & (nrows <= size))(functools.partial(fn, pl.ds(0, size)))

    def start_in(item, r):
        in_copy(item, r).start()

    def land_in(item, r):
        in_copy(item, r).wait()
        xb_ref[piece(r)] = xst_ref[piece(r)].astype(BF16)

    def start_out(item, r):
        for cp in out_copies(item, r):
            cp.start()

    def wait_out(item, r):
        for cp in out_copies(item, r):
            cp.wait()

    @pl.when((it == 0) & (s == 0))
    def _():
        xb_ref[...] = jnp.zeros_like(xb_ref)
        for_pieces_of(it, start_in)

    @pl.when(s == 0)
    def _():
        for_pieces_of(it, land_in)

    @pl.when((s == 1) & (it + 1 < n_it))
    def _():
        for_pieces_of(next_it, start_in)

    @pl.when((s == n_up) & (it > 0))
    def _():
        for_pieces_of(prev_it, wait_out)

    def up(rows):
        xs = xb_ref[rows]
        bg = bias_ref[0, pl.ds(s, 1), :]
        bl = bias_ref[0, pl.ds(n_up + s, 1), :]
        g = jnp.minimum(_dot(xs, wg_ref[0].astype(BF16)) + bg, SWIGLU_LIMIT)
        lin = jnp.clip(_dot(xs, wl_ref[0].astype(BF16)) + bl, -SWIGLU_LIMIT, SWIGLU_LIMIT)
        act_ref[s, rows] = ((lin + 1.0) * (g * jax.nn.sigmoid(SWIGLU_ALPHA * g))).astype(BF16)

    def down(rows):
        act = jnp.concatenate([act_ref[jj, rows] for jj in range(n_up)], axis=-1)
        per_step = fc // act_ref.shape[2]
        first = 2 * n_up + (s - n_up) * per_step
        bd = jnp.concatenate([bias_ref[0, pl.ds(first + c, 1), :] for c in range(per_step)], axis=-1)
        acc_ref[s - n_up, rows] = _dot(act, wd_ref[0].astype(BF16)) + bd

    @pl.when(s < n_up)
    def _():
        for_valid_chunks(up)

    @pl.when(s >= n_up)
    def _():
        for_valid_chunks(down)

    @pl.when(s == last)
    def _():
        for_pieces_of(it, start_out)

    @pl.when((it == n_it - 1) & (s == last))
    def _():
        for_pieces_of(it, wait_out)
        acc_ref[0, piece(0)] = jnp.zeros((sb, fc), F32)
        _zero_tail(acc_ref.at[0, piece(0)], yg_hbm, tail_ref[0], sem_out.at[0])


def _zero_tail(zeros_vmem, dst_hbm, first_row, sem):
    sb, width = zeros_vmem.shape
    n_blocks = (dst_hbm.shape[0] - first_row) // sb

    def fill(b, carry):
        rows = pl.ds(pl.multiple_of(first_row + b * sb, sb), sb)
        copies = [pltpu.make_async_copy(zeros_vmem, dst_hbm.at[rows, c * width:(c + 1) * width], sem)
                  for c in range(dst_hbm.shape[1] // width)]
        for cp in copies:
            cp.start()
        for cp in copies:
            cp.wait()
        return carry

    lax.fori_loop(0, n_blocks, fill, 0)


def _experts(n_used, item_e, item_row, item_nrows, tail, xg, w_gate_up, b_gate_up, w_down, b_down, *, tm, sb, fc):
    rows, d = xg.shape
    dff = w_down.shape[1]
    n_up = dff // fc
    fd = 2 * fc
    n_down = d // fd
    nsub = tm // sb
    kern = functools.partial(_expert_kernel, tm=tm, sb=sb, row_sizes=_row_sizes(tm, sb))
    n_e = w_down.shape[0]
    bias_rows = [b_gate_up.reshape(n_e, 2 * n_up, fc), b_down.reshape(n_e, d // fc, fc)]
    n_rows = 2 * n_up + d // fc
    bias_rows.append(jnp.zeros((n_e, -n_rows % SUBLANES, fc), F32))
    biases = jnp.concatenate(bias_rows, axis=1)

    def up_block(offset):
        def index_map(it, s, ie, ir, inr, tl):
            ahead = s >= n_up
            nxt = jnp.minimum(it + 1, tl[1] - 1)
            return jnp.where(ahead, ie[nxt], ie[it]), 0, offset + jnp.where(ahead, 0, s)
        return index_map

    def down_block(it, s, ie, ir, inr, tl):
        return ie[it], 0, jnp.maximum(s - n_up, 0)

    return pl.pallas_call(
        kern,
        grid_spec=pltpu.PrefetchScalarGridSpec(
            num_scalar_prefetch=4,
            grid=(n_used, n_up + n_down),
            in_specs=[
                pl.BlockSpec(memory_space=pl.ANY),
                pl.BlockSpec((1, d, fc), up_block(0)),
                pl.BlockSpec((1, d, fc), up_block(n_up)),
                pl.BlockSpec((1, dff, fd), down_block),
                pl.BlockSpec((1, biases.shape[1], fc), lambda it, s, ie, ir, inr, tl: (ie[it], 0, 0)),
            ],
            out_specs=pl.BlockSpec(memory_space=pl.ANY),
            scratch_shapes=[
                pltpu.VMEM((tm, d), F32),
                pltpu.VMEM((tm, d), BF16),
                pltpu.VMEM((n_up, tm, fc), BF16),
                pltpu.VMEM((n_down, tm, fd), F32),
                pltpu.SemaphoreType.DMA((nsub,)),
                pltpu.SemaphoreType.DMA((nsub,)),
            ],
        ),
        out_shape=jax.ShapeDtypeStruct((rows, d), F32),
        compiler_params=pltpu.CompilerParams(
            dimension_semantics=("arbitrary", "arbitrary"), vmem_limit_bytes=VMEM_LIMIT),
        name="experts",
    )(item_e, item_row, item_nrows, tail, xg, w_gate_up, w_gate_up, w_down, biases)


def _combine_kernel(dest_ref, yg_hbm, h1_ref, rf_ref, gfin_ref, o_ref, gbuf_ref, sem, *, tmc):
    i = pl.program_id(0)
    n_tiles = pl.num_programs(0)
    n_tok = n_tiles * tmc

    def issue_tile(tile):
        buf = tile % 2
        base = tile * tmc

        def issue(r, carry):
            for k in range(TOP_K):
                src = yg_hbm.at[pl.ds(dest_ref[k * n_tok + base + r], 1)]
                pltpu.make_async_copy(src, gbuf_ref.at[buf, k, pl.ds(r, 1)], sem.at[buf]).start(priority=k % 2)
            return carry

        lax.fori_loop(0, tmc, issue, 0)

    @pl.when(i == 0)
    def _():
        issue_tile(i)

    @pl.when(i + 1 < n_tiles)
    def _():
        issue_tile(i + 1)

    buf = i % 2
    for k in range(TOP_K):
        pltpu.make_async_copy(yg_hbm.at[pl.ds(0, tmc)], gbuf_ref.at[buf, k], sem.at[buf]).wait()

    gates = rf_ref[...]
    y = h1_ref[...]
    for k in range(TOP_K):
        y = y + gates[:, k:k + 1] * gbuf_ref[buf, k]
    o_ref[...] = _rms(y, gfin_ref[...])


def _combine(dest, yg, h1, rf, g_final, *, tmc):
    t, d = h1.shape
    kern = functools.partial(_combine_kernel, tmc=tmc)
    return pl.pallas_call(
        kern,
        grid_spec=pltpu.PrefetchScalarGridSpec(
            num_scalar_prefetch=1,
            grid=(t // tmc,),
            in_specs=[
                pl.BlockSpec(memory_space=pl.ANY),
                pl.BlockSpec((tmc, d), lambda i, dest: (i, 0)),
                pl.BlockSpec((tmc, LANES), lambda i, dest: (i, 0)),
                pl.BlockSpec((1, d), lambda i, dest: (0, 0)),
            ],
            out_specs=pl.BlockSpec((tmc, d), lambda i, dest: (i, 0)),
            scratch_shapes=[pltpu.VMEM((2, TOP_K, tmc, d), F32), pltpu.SemaphoreType.DMA((2,))],
        ),
        out_shape=jax.ShapeDtypeStruct((t, d), F32),
        compiler_params=pltpu.CompilerParams(
            dimension_semantics=("arbitrary",), vmem_limit_bytes=VMEM_LIMIT),
        name="combine",
    )(dest, yg, h1, rf, g_final)


def _cast_kernel(x_ref, o_ref):
    o_ref[...] = x_ref[...].astype(o_ref.dtype)


def _to_bf16(w, *, n_blocks):
    rows, cols = w.shape
    spec = pl.BlockSpec((rows // n_blocks, cols), lambda i: (i, 0))
    return pl.pallas_call(
        _cast_kernel, grid=(n_blocks,), in_specs=[spec], out_specs=spec,
        out_shape=jax.ShapeDtypeStruct(w.shape, BF16),
        compiler_params=pltpu.CompilerParams(dimension_semantics=("arbitrary",), vmem_limit_bytes=VMEM_LIMIT),
        name="to_bf16",
    )(w)


def _rotate_half_cols(w):
    half = w.shape[-1] // 2
    return jnp.concatenate([-w[..., half:], w[..., :half]], axis=-1)


def _prep_weights(g_mix, w_in, conv_w, g_q, w_q_up, g_kv, w_kv_up, g_conv_out, w_out, g_ffn, w_router, b_router):
    d = w_in.shape[0]
    w1 = _to_bf16(w_in.T, n_blocks=4)
    k_rope = w1[O_KPE:O_KPE + QK_ROPE]
    pad = jnp.zeros((LANES - QK_ROPE, d), BF16)
    half = QK_ROPE // 2
    k_rot = jnp.concatenate([-k_rope[half:], k_rope[:half]], axis=0)
    wk2 = jnp.concatenate([k_rope, pad, k_rot, pad], axis=0)

    wq = w_q_up.reshape(Q_LORA, N_HEADS, QK_NOPE + QK_ROPE)
    nope, pe = wq[:, :, :QK_NOPE], wq[:, :, QK_NOPE:]
    hpad = jnp.zeros((Q_LORA, N_HEADS, LANES - QK_ROPE), F32)
    wqa = jnp.concatenate([nope, pe, hpad], axis=2).reshape(Q_LORA, N_HEADS * HEAD_W)
    wqb = jnp.concatenate([_rotate_half_cols(pe), hpad], axis=2).reshape(Q_LORA, N_HEADS * LANES)
    wq_all = jnp.concatenate([wqa, wqb], axis=1).astype(BF16)

    wkv = w_kv_up.reshape(KV_LORA, N_HEADS, QK_NOPE + V_DIM)
    wkv2 = jnp.concatenate([wkv[:, :, :QK_NOPE].reshape(KV_LORA, ATTN_W),
                            wkv[:, :, QK_NOPE:].reshape(KV_LORA, ATTN_W)], axis=1).astype(BF16)

    wr_pad = jnp.zeros((d, LANES), F32).at[:, :N_EXPERTS].set(w_router)
    wr_hi = wr_pad.astype(BF16)
    wr_lo = (wr_pad - wr_hi.astype(F32)).astype(BF16)
    br = jnp.full((1, LANES), -1e30, F32).at[0, :N_EXPERTS].set(b_router)

    grp = jnp.arange(D_CONV) // (D_CONV // CONV_GROUPS)
    cw = jnp.zeros((SUBLANES, D_CONV), F32).at[:CONV_WIDTH].set(conv_w)
    return {
        "g_mix": g_mix[None], "w1": w1, "wk2": wk2, "conv_w": cw, "g_q": g_q[None], "wq": wq_all, "g_kv": g_kv[None],
        "wkv": wkv2, "g_conv_out": g_conv_out[None], "gmat": (grp[:, None] == grp[None, :]).astype(BF16),
        "wo": w_out.astype(BF16), "g_ffn": g_ffn[None], "wr": jnp.concatenate([wr_hi, wr_lo], axis=1), "br": br,
    }


def _rope_table(pos):
    half = QK_ROPE // 2
    inv_freq = np.float32(ROPE_THETA) ** (-np.arange(half, dtype=np.float32) / np.float32(half))
    ang = (np.asarray(pos, np.float32)[:, None] * inv_freq[None, :]).astype(np.float32)
    c, s = np.cos(ang), np.sin(ang)
    z = np.zeros((ang.shape[0], LANES - QK_ROPE), np.float32)
    return jnp.asarray(np.concatenate([c, c, z, s, s, z], axis=1), dtype=F32)


def _schedule_kernel(cnt_ref, ri_ref, dest_ref, pend_ref, ie_ref, irow_ref, inr_ref, tail_ref, pstart_ref,
                     *, tm, sb, n_items):
    def per_expert(e, carry):
        row, item = carry
        count = cnt_ref[e]
        pstart_ref[e] = row

        def per_item(li, item):
            ie_ref[item] = e
            irow_ref[item] = row + li * tm
            inr_ref[item] = jnp.minimum(count - li * tm, tm)
            return item + 1

        item = lax.fori_loop(0, lax.div(count + (tm - 1), tm), per_item, item)
        row = row + lax.div(count + (sb - 1), sb) * sb
        pend_ref[e] = row
        return row, item

    row, n_used = lax.fori_loop(0, N_EXPERTS, per_expert, (jnp.int32(0), jnp.int32(0)))
    tail_ref[0] = row
    tail_ref[1] = n_used

    def unused(item, carry):
        ie_ref[item] = 0
        irow_ref[item] = 0
        inr_ref[item] = 0
        return carry

    lax.fori_loop(n_used, n_items, unused, 0)

    eidx = ri_ref[:TOP_K]
    start_of = jnp.zeros_like(eidx)
    for e in range(N_EXPERTS):
        start_of = jnp.where(eidx == e, pstart_ref[e], start_of)
    dest_ref[...] = start_of + ri_ref[TOP_K:]


def _schedule(counts, ri, *, tm, sb, n_items):
    t = ri.shape[1]
    smem = pl.BlockSpec(memory_space=pltpu.SMEM)
    i32 = lambda n: jax.ShapeDtypeStruct((n,), jnp.int32)
    dest, pad_end, item_e, item_row, item_nrows, tail = pl.pallas_call(
        functools.partial(_schedule_kernel, tm=tm, sb=sb, n_items=n_items),
        in_specs=[smem, pl.BlockSpec(memory_space=pltpu.VMEM)],
        out_specs=[pl.BlockSpec(memory_space=pltpu.VMEM), smem, smem, smem, smem, smem],
        out_shape=[jax.ShapeDtypeStruct((TOP_K, t), jnp.int32), i32(N_EXPERTS), i32(n_items), i32(n_items),
                   i32(n_items), i32(2)],
        scratch_shapes=[pltpu.SMEM((N_EXPERTS,), jnp.int32)],
        name="schedule",
    )(counts, ri)
    return dest.reshape(-1), pad_end, item_e, item_row, item_nrows, tail


def _row_sizes(tm, sb):
    fine = [tm - k * sb for k in (2, 1, 0) if tm - k * sb > 0]
    coarse = [s for s in (tm // 4 // sb * sb, tm // 2 // sb * sb) if 0 < s < min(fine)]
    return tuple(sorted(set(coarse + fine)))


def _moe_tiles(t):
    sb = 128
    tm = 10 * sb
    a = t * TOP_K
    rows = (a + N_EXPERTS * (sb - 1) + sb - 1) // sb * sb
    n_items = N_EXPERTS + a // tm
    return sb, tm, rows, n_items


def _layer(x, meta_tokens, w, w_gate_up, b_gate_up, w_down, b_down, g_attn_out, g_final, *,
           tm_in, tq, tm_out, tmd, tmc, fc):
    b, seq, d = x.shape
    t = b * seq

    meta_blk = jnp.zeros((1, META_ROWS, d), F32).at[0, META_ROWS - N_META:].set(meta_tokens)
    meta_pos = np.maximum(np.arange(META_ROWS) - (META_ROWS - N_META), 0)
    zero_tail = jnp.zeros((SUBLANES, D_CONV), F32)
    _, _, mkn, mkpe, mv, u_tail = _mix_in(meta_blk, zero_tail, _rope_table(meta_pos), w, tm=META_ROWS)

    real_pos = np.arange(seq) + N_META
    convn, q, kn, kpe, v, _ = _mix_in(x, u_tail, _rope_table(real_pos), w, tm=tm_in)
    attn = _attention(q, kn, kpe, v, mkn, mkpe, mv, g_attn_out[None], tq=tq)

    h1, xn, ri, rf, cnt = _mix_out(convn.reshape(t, D_CONV), attn.reshape(t, ATTN_W), x.reshape(t, d), w, tm=tm_out)

    sb, tm_e, rows, n_items = _moe_tiles(t)
    counts = cnt[0, :N_EXPERTS].astype(jnp.int32)
    dest, pad_end, item_e, item_row, item_nrows, tail = _schedule(counts, ri, tm=tm_e, sb=sb, n_items=n_items)

    xg = _dispatch(dest, pad_end, xn, rows=rows, tmd=tmd, sb=sb)
    dff = w_down.shape[1]
    yg = _experts(tail[1], item_e, item_row, item_nrows, tail, xg, w_gate_up, b_gate_up.reshape(N_EXPERTS, 1, 2 * dff),
                  w_down, b_down.reshape(N_EXPERTS, 1, d), tm=tm_e, sb=sb, fc=fc)
    out = _combine(dest, yg, h1, rf, g_final[None], tmc=tmc)
    return out.reshape(b, seq, d)


def kernel(x, meta_tokens, g_mix, w_in, conv_w, g_q, w_q_up, g_kv, w_kv_up, g_conv_out, g_attn_out, w_out, g_ffn,
           w_router, b_router, w_gate_up, b_gate_up, w_down, b_down, g_final):
    w = _prep_weights(g_mix[0], w_in[0], conv_w[0], g_q[0], w_q_up[0], g_kv[0], w_kv_up[0], g_conv_out[0],
                      w_out[0], g_ffn[0], w_router[0], b_router[0])
    seq = x.shape[1]
    return _layer(x, meta_tokens, w, w_gate_up[0], b_gate_up[0], w_down[0], b_down[0], g_attn_out[0], g_final,
                  tm_in=min(512, seq), tq=min(512, seq), tm_out=min(512, seq), tmd=min(1024, seq),
                  tmc=min(512, seq), fc=256)
```

```python
import functools

import jax
import jax.numpy as jnp
import numpy as np
from jax import lax
from jax.experimental import pallas as pl
from jax.experimental.pallas import tpu as pltpu

N_META = 16
EPS = 1e-6
D_CONV = 1024
CONV_GROUPS = 16
CONV_WIDTH = 3
N_HEADS = 8
QK_NOPE = 128
QK_ROPE = 64
V_DIM = 128
Q_LORA = 512
KV_LORA = 256
ROPE_THETA = 10000.0
N_EXPERTS = 32
TOP_K = 4
SWIGLU_LIMIT = 7.0
SWIGLU_ALPHA = 1.702

LANES = 128
SUBLANES = 8
META_ROWS = 128
HEAD_W = 2 * LANES
ATTN_W = N_HEADS * V_DIM
V_EXT = V_DIM + 16
VMEM_LIMIT = 56 * 1024 * 1024

O_B, O_C, O_U = 0, D_CONV, 2 * D_CONV
O_Q = 3 * D_CONV
O_KV = O_Q + Q_LORA
O_KPE = O_KV + KV_LORA
D_IN = O_KPE + QK_ROPE

F32 = jnp.float32
BF16 = jnp.bfloat16
NT_DIMS = (((1,), (1,)), ((), ()))


def _rms(x, g):
    return x * lax.rsqrt(jnp.mean(x * x, axis=-1, keepdims=True) + EPS) * g


def _dot(a, b):
    return jnp.dot(a, b, preferred_element_type=F32)


def _resident(shape):
    zeros = (0,) * len(shape)
    return pl.BlockSpec(shape, lambda *_: zeros, pipeline_mode=pl.Buffered(1))


def _mix_in_kernel(x_ref, gmix_ref, w1_ref, wk2_ref, cw_ref, gq_ref, wq_ref, gkv_ref, wkv_ref, gco_ref, gmat_ref,
                   tab_ref, uinit_ref,
                   convn_ref, qt_ref, kn_ref, kpe_ref, vt_ref, utail_ref,
                   ubuf_ref, *, tm, scale):
    i = pl.program_id(1)
    hn = _rms(x_ref[0], gmix_ref[...]).astype(BF16)

    def proj(lo, hi):
        return lax.dot_general(hn, w1_ref[lo:hi, :], NT_DIMS, preferred_element_type=F32)

    @pl.when(i == 0)
    def _():
        ubuf_ref[0:SUBLANES] = uinit_ref[...]

    @pl.when(i > 0)
    def _():
        ubuf_ref[0:SUBLANES] = ubuf_ref[tm:tm + SUBLANES]

    u = proj(O_C, O_U) * proj(O_U, O_Q)
    ubuf_ref[SUBLANES:SUBLANES + tm] = u
    cw = cw_ref[...]
    y = (cw[2:3] * u + cw[1:2] * ubuf_ref[SUBLANES - 1:SUBLANES - 1 + tm]
         + cw[0:1] * ubuf_ref[SUBLANES - 2:SUBLANES - 2 + tm])
    co = proj(O_B, O_C) * y
    ss = _dot((co * co).astype(BF16), gmat_ref[...])
    group = D_CONV // CONV_GROUPS
    convn_ref[0] = (co * lax.rsqrt(ss * (1.0 / group) + EPS) * gco_ref[...]).astype(BF16)
    utail_ref[...] = ubuf_ref[tm:tm + SUBLANES]

    cos = tab_ref[:, :LANES]
    sin = tab_ref[:, LANES:]

    qn = _rms(proj(O_Q, O_KV), gq_ref[...]).astype(BF16)
    qa = _dot(qn, wq_ref[:, :N_HEADS * HEAD_W])
    qb = _dot(qn, wq_ref[:, N_HEADS * HEAD_W:])
    for h in range(N_HEADS):
        c0 = h * HEAD_W
        qt_ref[0, c0:c0 + LANES, :] = (qa[:, c0:c0 + LANES] * scale).T.astype(BF16)
        pe = qa[:, c0 + LANES:c0 + HEAD_W] * cos + qb[:, h * LANES:(h + 1) * LANES] * sin
        qt_ref[0, c0 + LANES:c0 + HEAD_W, :] = (pe * scale).T.astype(BF16)

    kvn = _rms(proj(O_KV, O_KPE), gkv_ref[...]).astype(BF16)
    kv = _dot(kvn, wkv_ref[...])
    kn_ref[0] = kv[:, :ATTN_W].astype(BF16)
    vt = kv[:, ATTN_W:].T.astype(BF16)
    ones = jnp.ones((V_EXT - V_DIM, tm), BF16)
    for h in range(N_HEADS):
        vt_ref[0, 0, h * V_EXT:h * V_EXT + V_DIM, :] = vt[h * V_DIM:(h + 1) * V_DIM]
        vt_ref[0, 0, h * V_EXT + V_DIM:(h + 1) * V_EXT, :] = ones
    kk = lax.dot_general(hn, wk2_ref[...], NT_DIMS, preferred_element_type=F32)
    kpe_ref[0] = (kk[:, :LANES] * cos + kk[:, LANES:] * sin).astype(BF16)


def _meta_kernel(x_ref, gmix_ref, wc_ref, wu_ref, wkvl_ref, wk2_ref, gkv_ref, wkv_ref, tab_ref,
                 kn_ref, kpe_ref, vt_ref, utail_ref):
    hn = _rms(x_ref[0], gmix_ref[...]).astype(BF16)

    def proj(w_ref):
        return lax.dot_general(hn, w_ref[...], NT_DIMS, preferred_element_type=F32)

    u = proj(wc_ref) * proj(wu_ref)
    utail_ref[...] = u[META_ROWS - SUBLANES:]
    kvn = _rms(proj(wkvl_ref), gkv_ref[...]).astype(BF16)
    kv = _dot(kvn, wkv_ref[...])
    kn_ref[0] = kv[:, :ATTN_W].astype(BF16)
    vt = kv[:, ATTN_W:].T.astype(BF16)
    ones = jnp.ones((V_EXT - V_DIM, META_ROWS), BF16)
    for h in range(N_HEADS):
        vt_ref[0, 0, h * V_EXT:h * V_EXT + V_DIM, :] = vt[h * V_DIM:(h + 1) * V_DIM]
        vt_ref[0, 0, h * V_EXT + V_DIM:(h + 1) * V_EXT, :] = ones
    kk = proj(wk2_ref)
    kpe_ref[0] = (kk[:, :LANES] * tab_ref[:, :LANES] + kk[:, LANES:] * tab_ref[:, LANES:]).astype(BF16)


def _mix_meta(meta_blk, tab, w):
    d = meta_blk.shape[2]
    whole = lambda shape: pl.BlockSpec(shape, lambda i: (0,) * len(shape))
    w1_rows = lambda first, n: pl.BlockSpec((n, d), lambda i: (first // n, 0))
    args = (meta_blk, w["g_mix"], w["w1"], w["w1"], w["w1"], w["wk2"], w["g_kv"], w["wkv"], tab)
    out_shape = [
        jax.ShapeDtypeStruct((1, META_ROWS, ATTN_W), BF16),
        jax.ShapeDtypeStruct((1, META_ROWS, LANES), BF16),
        jax.ShapeDtypeStruct((1, 1, N_HEADS * V_EXT, META_ROWS), BF16),
        jax.ShapeDtypeStruct((SUBLANES, D_CONV), F32),
    ]
    return pl.pallas_call(
        _meta_kernel,
        grid=(1,),
        in_specs=[whole(meta_blk.shape), whole(w["g_mix"].shape), w1_rows(O_C, D_CONV), w1_rows(O_U, D_CONV),
                  w1_rows(O_KV, KV_LORA), whole(w["wk2"].shape), whole(w["g_kv"].shape), whole(w["wkv"].shape),
                  whole(tab.shape)],
        out_specs=[whole(o.shape) for o in out_shape],
        out_shape=out_shape,
        compiler_params=pltpu.CompilerParams(dimension_semantics=("arbitrary",), vmem_limit_bytes=VMEM_LIMIT),
        name="mix_meta",
    )(*args)


def _mix_in(x3, uinit, tab, w, *, tm):
    b, l, d = x3.shape
    nt = l // tm
    kern = functools.partial(_mix_in_kernel, tm=tm, scale=float((QK_NOPE + QK_ROPE) ** -0.5))
    row = lambda width: pl.BlockSpec((1, tm, width), lambda bi, i: (bi, i, 0))
    return pl.pallas_call(
        kern,
        grid=(b, nt),
        in_specs=[
            row(d),
            _resident((1, d)),
            _resident((D_IN, d)),
            _resident((2 * LANES, d)),
            _resident((SUBLANES, D_CONV)),
            _resident((1, Q_LORA)),
            _resident((Q_LORA, N_HEADS * (HEAD_W + LANES))),
            _resident((1, KV_LORA)),
            _resident((KV_LORA, 2 * ATTN_W)),
            _resident((1, D_CONV)),
            _resident((D_CONV, D_CONV)),
            pl.BlockSpec((tm, 2 * LANES), lambda bi, i: (i, 0)),
            _resident((SUBLANES, D_CONV)),
        ],
        out_specs=[
            row(D_CONV),
            pl.BlockSpec((1, N_HEADS * HEAD_W, tm), lambda bi, i: (bi, 0, i)),
            row(ATTN_W), row(LANES),
            pl.BlockSpec((1, 1, N_HEADS * V_EXT, tm), lambda bi, i: (bi, i, 0, 0)),
            pl.BlockSpec((SUBLANES, D_CONV), lambda bi, i: (bi * nt + i, 0)),
        ],
        out_shape=[
            jax.ShapeDtypeStruct((b, l, D_CONV), BF16),
            jax.ShapeDtypeStruct((b, N_HEADS * HEAD_W, l), BF16),
            jax.ShapeDtypeStruct((b, l, ATTN_W), BF16),
            jax.ShapeDtypeStruct((b, l, LANES), BF16),
            jax.ShapeDtypeStruct((b, nt, N_HEADS * V_EXT, tm), BF16),
            jax.ShapeDtypeStruct((b * nt * SUBLANES, D_CONV), F32),
        ],
        scratch_shapes=[pltpu.VMEM((tm + SUBLANES, D_CONV), F32)],
        compiler_params=pltpu.CompilerParams(
            dimension_semantics=("arbitrary", "arbitrary"), vmem_limit_bytes=VMEM_LIMIT),
        name="mix_in",
    )(x3, w["g_mix"], w["w1"], w["wk2"], w["conv_w"], w["g_q"], w["wq"], w["g_kv"], w["wkv"], w["g_conv_out"],
      w["gmat"], tab, uinit)


def _attn_kernel(qt_ref, kn_ref, kpe_ref, vt_ref, mkn_ref, mkpe_ref, mvt_ref, g_ref, o_ref,
                 m_ref, acc_ref, *, tq, hps):
    qi = pl.program_id(2)
    tv = vt_ref.shape[3]

    def lanes(hh):
        return slice(hh * LANES, (hh + 1) * LANES)

    def vrows(hh):
        return slice(hh * V_EXT, (hh + 1) * V_EXT)

    def qt(hh):
        return qt_ref[0, hh * HEAD_W:(hh + 1) * HEAD_W, :]

    for hh in range(hps):
        km = jnp.concatenate([mkn_ref[0, :, lanes(hh)], mkpe_ref[0]], axis=-1)
        s = _dot(km, qt(hh))
        row = lax.broadcasted_iota(jnp.int32, s.shape, 0)
        s = jnp.where(row >= META_ROWS - N_META, s, -jnp.inf)
        m0 = jnp.max(s, axis=0, keepdims=True)
        m_ref[hh] = m0
        acc_ref[hh] = _dot(mvt_ref[0, 0, vrows(hh), :], jnp.exp((s - m0).astype(BF16)))

    def step(kb, diagonal):
        off = pl.multiple_of(kb * tq, tq)
        kpe = kpe_ref[0, pl.ds(off, tq), :]
        scores = []
        for hh in range(hps):
            k = jnp.concatenate([kn_ref[0, pl.ds(off, tq), lanes(hh)], kpe], axis=-1)
            s = _dot(k, qt(hh))
            if diagonal:
                r = lax.broadcasted_iota(jnp.int32, s.shape, 0)
                c = lax.broadcasted_iota(jnp.int32, s.shape, 1)
                s = jnp.where(r <= c, s, -jnp.inf)
            scores.append(s)
        probs, alphas = [], []
        for hh in range(hps):
            m_prev = m_ref[hh]
            m_new = jnp.maximum(m_prev, jnp.max(scores[hh], axis=0, keepdims=True))
            alphas.append(jnp.exp(m_prev - m_new))
            probs.append(jnp.exp((scores[hh] - m_new).astype(BF16)))
            m_ref[hh] = m_new
        for hh in range(hps):
            pb = probs[hh]
            pv = _dot(vt_ref[0, kb * (tq // tv), vrows(hh), :], pb[:tv])
            for c in range(1, tq // tv):
                pv = pv + _dot(vt_ref[0, kb * (tq // tv) + c, vrows(hh), :], pb[c * tv:(c + 1) * tv])
            acc_ref[hh] = alphas[hh] * acc_ref[hh] + pv

    def body(kb, carry):
        step(kb, False)
        return carry

    lax.fori_loop(0, qi, body, 0)
    step(qi, True)

    for hh in range(hps):
        acc = acc_ref[hh]
        o = (acc[:V_DIM] / acc[V_DIM:V_DIM + 1]).T
        o_ref[0, :, lanes(hh)] = _rms(o, g_ref[:, lanes(hh)]).astype(BF16)


def _attention(qt, kn, kpe, vt, mkn, mkpe, mvt, g_attn, *, tq, hps=4):
    b, l, _ = kn.shape
    _, nt, _, tv = vt.shape
    nq = l // tq
    kern = functools.partial(_attn_kernel, tq=tq, hps=hps)
    return pl.pallas_call(
        kern,
        grid=(b, N_HEADS // hps, nq),
        in_specs=[
            pl.BlockSpec((1, hps * HEAD_W, tq), lambda bi, h, i: (bi, h, i)),
            pl.BlockSpec((1, l, hps * LANES), lambda bi, h, i: (bi, 0, h)),
            pl.BlockSpec((1, l, LANES), lambda bi, h, i: (bi, 0, 0)),
            pl.BlockSpec((1, nt, hps * V_EXT, tv), lambda bi, h, i: (bi, 0, h, 0)),
            pl.BlockSpec((1, META_ROWS, hps * LANES), lambda bi, h, i: (0, 0, h)),
            pl.BlockSpec((1, META_ROWS, LANES), lambda bi, h, i: (0, 0, 0)),
            pl.BlockSpec((1, 1, hps * V_EXT, META_ROWS), lambda bi, h, i: (0, 0, h, 0)),
            pl.BlockSpec((1, hps * V_DIM), lambda bi, h, i: (0, h)),
        ],
        out_specs=pl.BlockSpec((1, tq, hps * V_DIM), lambda bi, h, i: (bi, i, h)),
        out_shape=jax.ShapeDtypeStruct((b, l, ATTN_W), BF16),
        scratch_shapes=[pltpu.VMEM((hps, 1, tq), F32), pltpu.VMEM((hps, V_EXT, tq), F32)],
        compiler_params=pltpu.CompilerParams(
            dimension_semantics=("arbitrary", "arbitrary", "arbitrary"), vmem_limit_bytes=VMEM_LIMIT),
        name="attn",
    )(qt, kn, kpe, vt, mkn, mkpe, mvt, g_attn)


def _mix_out_kernel(convn_ref, attn_ref, x_ref, wo_ref, gffn_ref, wr_ref, br_ref,
                    h1_ref, xn_ref, ri_ref, rf_ref, cnt_ref, carry_ref, *, tm):
    @pl.when(pl.program_id(0) == 0)
    def _():
        carry_ref[...] = jnp.zeros_like(carry_ref)

    h1 = x_ref[...] + _dot(convn_ref[...], wo_ref[:D_CONV]) + _dot(attn_ref[...], wo_ref[D_CONV:])
    h1_ref[...] = h1
    xn = _rms(h1, gffn_ref[...])
    xn_ref[...] = xn

    xh = xn.astype(BF16)
    xl = (xn - xh.astype(F32)).astype(BF16)
    ph = _dot(xh, wr_ref[...])
    logits = ph[:, :LANES] + ph[:, LANES:] + _dot(xl, wr_ref[:, :LANES]) + br_ref[...]

    lane = lax.broadcasted_iota(jnp.int32, (tm, LANES), 1).astype(F32)
    work = logits
    top_v, top_i, onehots = [], [], []
    for _ in range(TOP_K):
        mk = jnp.max(work, axis=-1, keepdims=True)
        ik = jnp.min(jnp.where(work == mk, lane, float(LANES)), axis=-1, keepdims=True)
        oh = lane == ik
        work = jnp.where(oh, -jnp.inf, work)
        top_v.append(mk)
        top_i.append(ik)
        onehots.append(oh)

    ex = [jnp.exp(v - top_v[0]) for v in top_v]
    denom = ex[0] + ex[1] + ex[2] + ex[3]
    gates = [e / denom for e in ex]

    sel = jnp.zeros((tm, LANES), F32)
    for oh in onehots:
        sel = sel + oh.astype(F32)
    r = lax.broadcasted_iota(jnp.int32, (tm, tm), 0)
    c = lax.broadcasted_iota(jnp.int32, (tm, tm), 1)
    tri = jnp.where(c < r, 1.0, 0.0).astype(BF16)
    cum = _dot(tri, sel.astype(BF16)) + carry_ref[...]
    carry_ref[...] = carry_ref[...] + jnp.sum(sel, axis=0, keepdims=True)
    cnt_ref[...] = carry_ref[...]

    ri = jnp.zeros((tm, LANES), F32)
    rf = jnp.zeros((tm, LANES), F32)
    for k in range(TOP_K):
        rank_k = jnp.sum(jnp.where(onehots[k], cum, 0.0), axis=-1, keepdims=True)
        ri = jnp.where(lane == float(k), top_i[k], ri)
        ri = jnp.where(lane == float(TOP_K + k), rank_k, ri)
        rf = jnp.where(lane == float(k), gates[k], rf)
    ri_ref[...] = ri.T[:2 * TOP_K].astype(jnp.int32)
    rf_ref[...] = rf


def _mix_out(convn, attn, x2, w, *, tm):
    t, d = x2.shape
    kern = functools.partial(_mix_out_kernel, tm=tm)
    row = lambda width: pl.BlockSpec((tm, width), lambda i: (i, 0))
    return pl.pallas_call(
        kern,
        grid=(t // tm,),
        in_specs=[
            row(D_CONV), row(ATTN_W), row(d),
            _resident((D_CONV + ATTN_W, d)),
            _resident((1, d)),
            _resident((d, 2 * LANES)),
            _resident((1, LANES)),
        ],
        out_specs=[row(d), row(d), pl.BlockSpec((2 * TOP_K, tm), lambda i: (0, i)), row(LANES),
                   pl.BlockSpec((1, LANES), lambda i: (0, 0))],
        out_shape=[
            jax.ShapeDtypeStruct((t, d), F32),
            jax.ShapeDtypeStruct((t, d), F32),
            jax.ShapeDtypeStruct((2 * TOP_K, t), jnp.int32),
            jax.ShapeDtypeStruct((t, LANES), F32),
            jax.ShapeDtypeStruct((1, LANES), F32),
        ],
        scratch_shapes=[pltpu.VMEM((1, LANES), F32)],
        compiler_params=pltpu.CompilerParams(
            dimension_semantics=("arbitrary",), vmem_limit_bytes=VMEM_LIMIT),
        name="mix_out",
    )(convn, attn, x2, w["wo"], w["g_ffn"], w["wr"], w["br"])


def _dispatch_kernel(dest_ref, pend_ref, xn_ref, xg_hbm, zbuf_ref, sem, zsem, *, tmd, sb):
    i = pl.program_id(0)

    @pl.when(i == 0)
    def _():
        zbuf_ref[...] = jnp.zeros_like(zbuf_ref)

        def zero_copy(e):
            end = pend_ref[e]
            start = pl.multiple_of(jnp.maximum(end - sb, 0), sb)
            return pltpu.make_async_copy(zbuf_ref, xg_hbm.at[pl.ds(start, sb)], zsem)

        def nonempty(e):
            prev = pend_ref[jnp.maximum(e - 1, 0)]
            return pend_ref[e] > jnp.where(e > 0, prev, 0)

        def start(e, carry):
            @pl.when(nonempty(e))
            def _():
                zero_copy(e).start()
            return carry

        def wait(e, carry):
            @pl.when(nonempty(e))
            def _():
                zero_copy(e).wait()
            return carry

        lax.fori_loop(0, N_EXPERTS, start, 0)
        lax.fori_loop(0, N_EXPERTS, wait, 0)
        _zero_tail(zbuf_ref, xg_hbm, pend_ref[N_EXPERTS - 1], zsem)

    base = i * tmd
    n_tok = pl.num_programs(0) * tmd

    def issue(r, carry):
        for k in range(TOP_K):
            dst = xg_hbm.at[pl.ds(dest_ref[k * n_tok + base + r], 1)]
            pltpu.make_async_copy(xn_ref.at[pl.ds(r, 1)], dst, sem).start(priority=k % 2)
        return carry

    lax.fori_loop(0, tmd, issue, 0)
    for k in range(TOP_K):
        pltpu.make_async_copy(xn_ref, xg_hbm.at[pl.ds(0, tmd)], sem).wait()


def _dispatch(dest, pad_end, xn, *, rows, tmd, sb):
    t, d = xn.shape
    kern = functools.partial(_dispatch_kernel, tmd=tmd, sb=sb)
    return pl.pallas_call(
        kern,
        grid_spec=pltpu.PrefetchScalarGridSpec(
            num_scalar_prefetch=2,
            grid=(t // tmd,),
            in_specs=[pl.BlockSpec((tmd, d), lambda i, dest, pend: (i, 0))],
            out_specs=pl.BlockSpec(memory_space=pl.ANY),
            scratch_shapes=[pltpu.VMEM((sb, d), F32), pltpu.SemaphoreType.DMA, pltpu.SemaphoreType.DMA],
        ),
        out_shape=jax.ShapeDtypeStruct((rows, d), F32),
        compiler_params=pltpu.CompilerParams(
            dimension_semantics=("arbitrary",), vmem_limit_bytes=VMEM_LIMIT),
        name="dispatch",
    )(dest, pad_end, xn)


def _expert_kernel(ie_ref, irow_ref, inr_ref, tail_ref, xg_hbm, wg_ref, wl_ref, wd_ref, bias_ref, yg_hbm,
                   xst_ref, xb_ref, act_ref, acc_ref, sem_in, sem_out,
                   *, tm, sb, row_sizes):
    it = pl.program_id(0)
    s = pl.program_id(1)
    n_it = pl.num_programs(0)
    nrows = inr_ref[it]
    n_up = act_ref.shape[0]
    n_down, _, fc = acc_ref.shape
    last = n_up + n_down - 1
    prev_it = jnp.maximum(it - 1, 0)
    next_it = jnp.minimum(it + 1, n_it - 1)

    def piece(r):
        return pl.ds(r * sb, sb)

    def hbm_rows(item, r):
        return pl.ds(pl.multiple_of(irow_ref[item] + r * sb, sb), sb)

    def in_copy(item, r):
        return pltpu.make_async_copy(xg_hbm.at[hbm_rows(item, r)], xst_ref.at[piece(r)], sem_in.at[r])

    def out_copies(item, r):
        return [pltpu.make_async_copy(acc_ref.at[n, piece(r)], yg_hbm.at[hbm_rows(item, r), n * fc:(n + 1) * fc],
                                      sem_out.at[r]) for n in range(n_down)]

    def for_pieces_of(item, fn):
        for r in range(tm // sb):
            pl.when(r * sb < inr_ref[item])(functools.partial(fn, item, r))

    def for_valid_chunks(fn):
        for lo, size in zip((0,) + row_sizes[:-1], row_sizes):
            pl.when((lo < nrows) & (nrows <= size))(functools.partial(fn, pl.ds(0, size)))

    def start_in(item, r):
        in_copy(item, r).start()

    def land_in(item, r):
        in_copy(item, r).wait()
        xb_ref[piece(r)] = xst_ref[piece(r)].astype(BF16)

    def start_out(item, r):
        for cp in out_copies(item, r):
            cp.start()

    def wait_out(item, r):
        for cp in out_copies(item, r):
            cp.wait()

    @pl.when((it == 0) & (s == 0))
    def _():
        xb_ref[...] = jnp.zeros_like(xb_ref)
        for_pieces_of(it, start_in)

    @pl.when(s == 0)
    def _():
        for_pieces_of(it, land_in)

    @pl.when((s == 1) & (it + 1 < n_it))
    def _():
        for_pieces_of(next_it, start_in)

    @pl.when((s == n_up) & (it > 0))
    def _():
        for_pieces_of(prev_it, wait_out)

    def up(rows):
        xs = xb_ref[rows]
        bg = bias_ref[0, pl.ds(s, 1), :]
        bl = bias_ref[0, pl.ds(n_up + s, 1), :]
        g = jnp.minimum(_dot(xs, wg_ref[0].astype(BF16)) + bg, SWIGLU_LIMIT)
        lin = jnp.clip(_dot(xs, wl_ref[0].astype(BF16)) + bl, -SWIGLU_LIMIT, SWIGLU_LIMIT)
        act_ref[s, rows] = ((lin + 1.0) * (g * jax.nn.sigmoid(SWIGLU_ALPHA * g))).astype(BF16)

    def down(rows):
        act = jnp.concatenate([act_ref[jj, rows] for jj in range(n_up)], axis=-1)
        per_step = fc // act_ref.shape[2]
        first = 2 * n_up + (s - n_up) * per_step
        bd = jnp.concatenate([bias_ref[0, pl.ds(first + c, 1), :] for c in range(per_step)], axis=-1)
        acc_ref[s - n_up, rows] = _dot(act, wd_ref[0].astype(BF16)) + bd

    @pl.when(s < n_up)
    def _():
        for_valid_chunks(up)

    @pl.when(s >= n_up)
    def _():
        for_valid_chunks(down)

    @pl.when(s == last)
    def _():
        for_pieces_of(it, start_out)

    @pl.when((it == n_it - 1) & (s == last))
    def _():
        for_pieces_of(it, wait_out)
        acc_ref[0, piece(0)] = jnp.zeros((sb, fc), F32)
        _zero_tail(acc_ref.at[0, piece(0)], yg_hbm, tail_ref[0], sem_out.at[0])


def _zero_tail(zeros_vmem, dst_hbm, first_row, sem):
    sb, width = zeros_vmem.shape
    n_blocks = (dst_hbm.shape[0] - first_row) // sb

    def fill(b, carry):
        rows = pl.ds(pl.multiple_of(first_row + b * sb, sb), sb)
        copies = [pltpu.make_async_copy(zeros_vmem, dst_hbm.at[rows, c * width:(c + 1) * width], sem)
                  for c in range(dst_hbm.shape[1] // width)]
        for cp in copies:
            cp.start()
        for cp in copies:
            cp.wait()
        return carry

    lax.fori_loop(0, n_blocks, fill, 0)


def _experts(n_used, item_e, item_row, item_nrows, tail, xg, w_gate_up, b_gate_up, w_down, b_down, *, tm, sb, fc):
    rows, d = xg.shape
    dff = w_down.shape[1]
    n_up = dff // fc
    fd = 2 * fc
    n_down = d // fd
    nsub = tm // sb
    kern = functools.partial(_expert_kernel, tm=tm, sb=sb, row_sizes=_row_sizes(tm, sb))
    n_e = w_down.shape[0]
    bias_rows = [b_gate_up.reshape(n_e, 2 * n_up, fc), b_down.reshape(n_e, d // fc, fc)]
    n_rows = 2 * n_up + d // fc
    bias_rows.append(jnp.zeros((n_e, -n_rows % SUBLANES, fc), F32))
    biases = jnp.concatenate(bias_rows, axis=1)

    def up_block(offset):
        def index_map(it, s, ie, ir, inr, tl):
            ahead = s >= n_up
            nxt = jnp.minimum(it + 1, tl[1] - 1)
            return jnp.where(ahead, ie[nxt], ie[it]), 0, offset + jnp.where(ahead, 0, s)
        return index_map

    def down_block(it, s, ie, ir, inr, tl):
        return ie[it], 0, jnp.maximum(s - n_up, 0)

    return pl.pallas_call(
        kern,
        grid_spec=pltpu.PrefetchScalarGridSpec(
            num_scalar_prefetch=4,
            grid=(n_used, n_up + n_down),
            in_specs=[
                pl.BlockSpec(memory_space=pl.ANY),
                pl.BlockSpec((1, d, fc), up_block(0)),
                pl.BlockSpec((1, d, fc), up_block(n_up)),
                pl.BlockSpec((1, dff, fd), down_block),
                pl.BlockSpec((1, biases.shape[1], fc), lambda it, s, ie, ir, inr, tl: (ie[it], 0, 0)),
            ],
            out_specs=pl.BlockSpec(memory_space=pl.ANY),
            scratch_shapes=[
                pltpu.VMEM((tm, d), F32),
                pltpu.VMEM((tm, d), BF16),
                pltpu.VMEM((n_up, tm, fc), BF16),
                pltpu.VMEM((n_down, tm, fd), F32),
                pltpu.SemaphoreType.DMA((nsub,)),
                pltpu.SemaphoreType.DMA((nsub,)),
            ],
        ),
        out_shape=jax.ShapeDtypeStruct((rows, d), F32),
        compiler_params=pltpu.CompilerParams(
            dimension_semantics=("arbitrary", "arbitrary"), vmem_limit_bytes=VMEM_LIMIT),
        name="experts",
    )(item_e, item_row, item_nrows, tail, xg, w_gate_up, w_gate_up, w_down, biases)


def _combine_kernel(dest_ref, yg_hbm, h1_ref, rf_ref, gfin_ref, o_ref, gbuf_ref, sem, *, tmc):
    i = pl.program_id(0)
    n_tiles = pl.num_programs(0)
    n_tok = n_tiles * tmc

    def issue_tile(tile):
        buf = tile % 2
        base = tile * tmc

        def issue(r, carry):
            for k in range(TOP_K):
                src = yg_hbm.at[pl.ds(dest_ref[k * n_tok + base + r], 1)]
                pltpu.make_async_copy(src, gbuf_ref.at[buf, k, pl.ds(r, 1)], sem.at[buf]).start(priority=k % 2)
            return carry

        lax.fori_loop(0, tmc, issue, 0)

    @pl.when(i == 0)
    def _():
        issue_tile(i)

    @pl.when(i + 1 < n_tiles)
    def _():
        issue_tile(i + 1)

    buf = i % 2
    for k in range(TOP_K):
        pltpu.make_async_copy(yg_hbm.at[pl.ds(0, tmc)], gbuf_ref.at[buf, k], sem.at[buf]).wait()

    gates = rf_ref[...]
    y = h1_ref[...]
    for k in range(TOP_K):
        y = y + gates[:, k:k + 1] * gbuf_ref[buf, k]
    o_ref[...] = _rms(y, gfin_ref[...])


def _combine(dest, yg, h1, rf, g_final, *, tmc):
    t, d = h1.shape
    kern = functools.partial(_combine_kernel, tmc=tmc)
    return pl.pallas_call(
        kern,
        grid_spec=pltpu.PrefetchScalarGridSpec(
            num_scalar_prefetch=1,
            grid=(t // tmc,),
            in_specs=[
                pl.BlockSpec(memory_space=pl.ANY),
                pl.BlockSpec((tmc, d), lambda i, dest: (i, 0)),
                pl.BlockSpec((tmc, LANES), lambda i, dest: (i, 0)),
                pl.BlockSpec((1, d), lambda i, dest: (0, 0)),
            ],
            out_specs=pl.BlockSpec((tmc, d), lambda i, dest: (i, 0)),
            scratch_shapes=[pltpu.VMEM((2, TOP_K, tmc, d), F32), pltpu.SemaphoreType.DMA((2,))],
        ),
        out_shape=jax.ShapeDtypeStruct((t, d), F32),
        compiler_params=pltpu.CompilerParams(
            dimension_semantics=("arbitrary",), vmem_limit_bytes=VMEM_LIMIT),
        name="combine",
    )(dest, yg, h1, rf, g_final)


def _cast_kernel(x_ref, o_ref):
    o_ref[...] = x_ref[...].astype(o_ref.dtype)


def _to_bf16(w, *, n_blocks):
    rows, cols = w.shape
    spec = pl.BlockSpec((rows // n_blocks, cols), lambda i: (i, 0))
    return pl.pallas_call(
        _cast_kernel, grid=(n_blocks,), in_specs=[spec], out_specs=spec,
        out_shape=jax.ShapeDtypeStruct(w.shape, BF16),
        compiler_params=pltpu.CompilerParams(dimension_semantics=("arbitrary",), vmem_limit_bytes=VMEM_LIMIT),
        name="to_bf16",
    )(w)


def _rotate_half_cols(w):
    half = w.shape[-1] // 2
    return jnp.concatenate([-w[..., half:], w[..., :half]], axis=-1)


def _prep_weights(g_mix, w_in, conv_w, g_q, w_q_up, g_kv, w_kv_up, g_conv_out, w_out, g_ffn, w_router, b_router):
    d = w_in.shape[0]
    w1 = _to_bf16(w_in.T, n_blocks=4)
    k_rope = w1[O_KPE:O_KPE + QK_ROPE]
    pad = jnp.zeros((LANES - QK_ROPE, d), BF16)
    half = QK_ROPE // 2
    k_rot = jnp.concatenate([-k_rope[half:], k_rope[:half]], axis=0)
    wk2 = jnp.concatenate([k_rope, pad, k_rot, pad], axis=0)

    wq = w_q_up.reshape(Q_LORA, N_HEADS, QK_NOPE + QK_ROPE)
    nope, pe = wq[:, :, :QK_NOPE], wq[:, :, QK_NOPE:]
    hpad = jnp.zeros((Q_LORA, N_HEADS, LANES - QK_ROPE), F32)
    wqa = jnp.concatenate([nope, pe, hpad], axis=2).reshape(Q_LORA, N_HEADS * HEAD_W)
    wqb = jnp.concatenate([_rotate_half_cols(pe), hpad], axis=2).reshape(Q_LORA, N_HEADS * LANES)
    wq_all = jnp.concatenate([wqa, wqb], axis=1).astype(BF16)

    wkv = w_kv_up.reshape(KV_LORA, N_HEADS, QK_NOPE + V_DIM)
    wkv2 = jnp.concatenate([wkv[:, :, :QK_NOPE].reshape(KV_LORA, ATTN_W),
                            wkv[:, :, QK_NOPE:].reshape(KV_LORA, ATTN_W)], axis=1).astype(BF16)

    wr_pad = jnp.zeros((d, LANES), F32).at[:, :N_EXPERTS].set(w_router)
    wr_hi = wr_pad.astype(BF16)
    wr_lo = (wr_pad - wr_hi.astype(F32)).astype(BF16)
    br = jnp.full((1, LANES), -1e30, F32).at[0, :N_EXPERTS].set(b_router)

    grp = jnp.arange(D_CONV) // (D_CONV // CONV_GROUPS)
    cw = jnp.zeros((SUBLANES, D_CONV), F32).at[:CONV_WIDTH].set(conv_w)
    return {
        "g_mix": g_mix[None], "w1": w1, "wk2": wk2, "conv_w": cw, "g_q": g_q[None], "wq": wq_all, "g_kv": g_kv[None],
        "wkv": wkv2, "g_conv_out": g_conv_out[None], "gmat": (grp[:, None] == grp[None, :]).astype(BF16),
        "wo": w_out.astype(BF16), "g_ffn": g_ffn[None], "wr": jnp.concatenate([wr_hi, wr_lo], axis=1), "br": br,
    }


def _rope_table(pos):
    half = QK_ROPE // 2
    inv_freq = np.float32(ROPE_THETA) ** (-np.arange(half, dtype=np.float32) / np.float32(half))
    ang = (np.asarray(pos, np.float32)[:, None] * inv_freq[None, :]).astype(np.float32)
    c, s = np.cos(ang), np.sin(ang)
    z = np.zeros((ang.shape[0], LANES - QK_ROPE), np.float32)
    return jnp.asarray(np.concatenate([c, c, z, s, s, z], axis=1), dtype=F32)


def _schedule_kernel(cnt_ref, ri_ref, dest_ref, pend_ref, ie_ref, irow_ref, inr_ref, tail_ref, pstart_ref,
                     *, tm, sb, n_items):
    def per_expert(e, carry):
        row, item = carry
        count = cnt_ref[e]
        pstart_ref[e] = row

        def per_item(li, item):
            ie_ref[item] = e
            irow_ref[item] = row + li * tm
            inr_ref[item] = jnp.minimum(count - li * tm, tm)
            return item + 1

        item = lax.fori_loop(0, lax.div(count + (tm - 1), tm), per_item, item)
        row = row + lax.div(count + (sb - 1), sb) * sb
        pend_ref[e] = row
        return row, item

    row, n_used = lax.fori_loop(0, N_EXPERTS, per_expert, (jnp.int32(0), jnp.int32(0)))
    tail_ref[0] = row
    tail_ref[1] = n_used

    def unused(item, carry):
        ie_ref[item] = 0
        irow_ref[item] = 0
        inr_ref[item] = 0
        return carry

    lax.fori_loop(n_used, n_items, unused, 0)

    eidx = ri_ref[:TOP_K]
    start_of = jnp.zeros_like(eidx)
    for e in range(N_EXPERTS):
        start_of = jnp.where(eidx == e, pstart_ref[e], start_of)
    dest_ref[...] = start_of + ri_ref[TOP_K:]


def _schedule(counts, ri, *, tm, sb, n_items):
    t = ri.shape[1]
    smem = pl.BlockSpec(memory_space=pltpu.SMEM)
    i32 = lambda n: jax.ShapeDtypeStruct((n,), jnp.int32)
    dest, pad_end, item_e, item_row, item_nrows, tail = pl.pallas_call(
        functools.partial(_schedule_kernel, tm=tm, sb=sb, n_items=n_items),
        in_specs=[smem, pl.BlockSpec(memory_space=pltpu.VMEM)],
        out_specs=[pl.BlockSpec(memory_space=pltpu.VMEM), smem, smem, smem, smem, smem],
        out_shape=[jax.ShapeDtypeStruct((TOP_K, t), jnp.int32), i32(N_EXPERTS), i32(n_items), i32(n_items),
                   i32(n_items), i32(2)],
        scratch_shapes=[pltpu.SMEM((N_EXPERTS,), jnp.int32)],
        name="schedule",
    )(counts, ri)
    return dest.reshape(-1), pad_end, item_e, item_row, item_nrows, tail


def _row_sizes(tm, sb):
    fine = [tm - k * sb for k in (2, 1, 0) if tm - k * sb > 0]
    coarse = [s for s in (tm // 4 // sb * sb, tm // 2 // sb * sb) if 0 < s < min(fine)]
    return tuple(sorted(set(coarse + fine)))


def _moe_tiles(t):
    sb = 128
    tm = 10 * sb
    a = t * TOP_K
    rows = (a + N_EXPERTS * (sb - 1) + sb - 1) // sb * sb
    n_items = N_EXPERTS + a // tm
    return sb, tm, rows, n_items


def _layer(x, meta_tokens, w, w_gate_up, b_gate_up, w_down, b_down, g_attn_out, g_final, *,
           tm_in, tq, tm_out, tmd, tmc, fc):
    b, seq, d = x.shape
    t = b * seq

    meta_blk = jnp.zeros((1, META_ROWS, d), F32).at[0, META_ROWS - N_META:].set(meta_tokens)
    meta_pos = np.maximum(np.arange(META_ROWS) - (META_ROWS - N_META), 0)
    mkn, mkpe, mv, u_tail = _mix_meta(meta_blk, _rope_table(meta_pos), w)

    real_pos = np.arange(seq) + N_META
    convn, q, kn, kpe, v, _ = _mix_in(x, u_tail, _rope_table(real_pos), w, tm=tm_in)
    attn = _attention(q, kn, kpe, v, mkn, mkpe, mv, g_attn_out[None], tq=tq)

    h1, xn, ri, rf, cnt = _mix_out(convn.reshape(t, D_CONV), attn.reshape(t, ATTN_W), x.reshape(t, d), w, tm=tm_out)

    sb, tm_e, rows, n_items = _moe_tiles(t)
    counts = cnt[0, :N_EXPERTS].astype(jnp.int32)
    dest, pad_end, item_e, item_row, item_nrows, tail = _schedule(counts, ri, tm=tm_e, sb=sb, n_items=n_items)

    xg = _dispatch(dest, pad_end, xn, rows=rows, tmd=tmd, sb=sb)
    dff = w_down.shape[1]
    yg = _experts(tail[1], item_e, item_row, item_nrows, tail, xg, w_gate_up, b_gate_up.reshape(N_EXPERTS, 1, 2 * dff),
                  w_down, b_down.reshape(N_EXPERTS, 1, d), tm=tm_e, sb=sb, fc=fc)
    out = _combine(dest, yg, h1, rf, g_final[None], tmc=tmc)
    return out.reshape(b, seq, d)


def kernel(x, meta_tokens, g_mix, w_in, conv_w, g_q, w_q_up, g_kv, w_kv_up, g_conv_out, g_attn_out, w_out, g_ffn,
           w_router, b_router, w_gate_up, b_gate_up, w_down, b_down, g_final):
    w = _prep_weights(g_mix[0], w_in[0], conv_w[0], g_q[0], w_q_up[0], g_kv[0], w_kv_up[0], g_conv_out[0],
                      w_out[0], g_ffn[0], w_router[0], b_router[0])
    seq = x.shape[1]
    return _layer(x, meta_tokens, w, w_gate_up[0], b_gate_up[0], w_down[0], b_down[0], g_attn_out[0], g_final,
                  tm_in=min(512, seq), tq=min(512, seq), tm_out=min(512, seq), tmd=min(1024, seq),
                  tmc=min(512, seq), fc=256)
```

```python
import functools

import jax
import jax.numpy as jnp
import numpy as np
from jax import lax
from jax.experimental import pallas as pl
from jax.experimental.pallas import tpu as pltpu

N_META = 16
EPS = 1e-6
D_CONV = 1024
CONV_GROUPS = 16
CONV_WIDTH = 3
N_HEADS = 8
QK_NOPE = 128
QK_ROPE = 64
V_DIM = 128
Q_LORA = 512
KV_LORA = 256
ROPE_THETA = 10000.0
N_EXPERTS = 32
TOP_K = 4
SWIGLU_LIMIT = 7.0
SWIGLU_ALPHA = 1.702

LANES = 128
SUBLANES = 8
META_ROWS = 128
HEAD_W = 2 * LANES
ATTN_W = N_HEADS * V_DIM
V_EXT = V_DIM + 16
VMEM_LIMIT = 56 * 1024 * 1024

O_B, O_C, O_U = 0, D_CONV, 2 * D_CONV
O_Q = 3 * D_CONV
O_KV = O_Q + Q_LORA
O_KPE = O_KV + KV_LORA
D_IN = O_KPE + QK_ROPE

F32 = jnp.float32
BF16 = jnp.bfloat16
NT_DIMS = (((1,), (1,)), ((), ()))


def _rms(x, g):
    return x * lax.rsqrt(jnp.mean(x * x, axis=-1, keepdims=True) + EPS) * g


def _dot(a, b):
    return jnp.dot(a, b, preferred_element_type=F32)


def _resident(shape):
    zeros = (0,) * len(shape)
    return pl.BlockSpec(shape, lambda *_: zeros, pipeline_mode=pl.Buffered(1))


def _mix_in_kernel(x_ref, gmix_ref, w1_ref, wk2_ref, cw_ref, gq_ref, wq_ref, gkv_ref, wkv_ref, gco_ref, gmat_ref,
                   tab_ref, uinit_ref,
                   convn_ref, qt_ref, kn_ref, kpe_ref, vt_ref, utail_ref,
                   ubuf_ref, *, tm, scale):
    i = pl.program_id(1)
    hn = _rms(x_ref[0], gmix_ref[...]).astype(BF16)

    def proj(lo, hi):
        return lax.dot_general(hn, w1_ref[lo:hi, :], NT_DIMS, preferred_element_type=F32)

    @pl.when(i == 0)
    def _():
        ubuf_ref[0:SUBLANES] = uinit_ref[...]

    @pl.when(i > 0)
    def _():
        ubuf_ref[0:SUBLANES] = ubuf_ref[tm:tm + SUBLANES]

    u = proj(O_C, O_U) * proj(O_U, O_Q)
    ubuf_ref[SUBLANES:SUBLANES + tm] = u
    cw = cw_ref[...]
    y = (cw[2:3] * u + cw[1:2] * ubuf_ref[SUBLANES - 1:SUBLANES - 1 + tm]
         + cw[0:1] * ubuf_ref[SUBLANES - 2:SUBLANES - 2 + tm])
    co = proj(O_B, O_C) * y
    ss = _dot((co * co).astype(BF16), gmat_ref[...])
    group = D_CONV // CONV_GROUPS
    convn_ref[0] = (co * lax.rsqrt(ss * (1.0 / group) + EPS) * gco_ref[...]).astype(BF16)
    utail_ref[...] = ubuf_ref[tm:tm + SUBLANES]

    cos = tab_ref[:, :LANES]
    sin = tab_ref[:, LANES:]

    qn = _rms(proj(O_Q, O_KV), gq_ref[...]).astype(BF16)
    qa = _dot(qn, wq_ref[:, :N_HEADS * HEAD_W])
    qb = _dot(qn, wq_ref[:, N_HEADS * HEAD_W:])
    for h in range(N_HEADS):
        c0 = h * HEAD_W
        qt_ref[0, c0:c0 + LANES, :] = (qa[:, c0:c0 + LANES] * scale).T.astype(BF16)
        pe = qa[:, c0 + LANES:c0 + HEAD_W] * cos + qb[:, h * LANES:(h + 1) * LANES] * sin
        qt_ref[0, c0 + LANES:c0 + HEAD_W, :] = (pe * scale).T.astype(BF16)

    kvn = _rms(proj(O_KV, O_KPE), gkv_ref[...]).astype(BF16)
    kv = _dot(kvn, wkv_ref[...])
    kn_ref[0] = kv[:, :ATTN_W].astype(BF16)
    vt = kv[:, ATTN_W:].T.astype(BF16)
    ones = jnp.ones((V_EXT - V_DIM, tm), BF16)
    for h in range(N_HEADS):
        vt_ref[0, 0, h * V_EXT:h * V_EXT + V_DIM, :] = vt[h * V_DIM:(h + 1) * V_DIM]
        vt_ref[0, 0, h * V_EXT + V_DIM:(h + 1) * V_EXT, :] = ones
    kk = lax.dot_general(hn, wk2_ref[...], NT_DIMS, preferred_element_type=F32)
    kpe_ref[0] = (kk[:, :LANES] * cos + kk[:, LANES:] * sin).astype(BF16)


def _meta_kernel(x_ref, gmix_ref, wc_ref, wu_ref, wkvl_ref, wk2_ref, gkv_ref, wkv_ref, tab_ref,
                 kn_ref, kpe_ref, vt_ref, utail_ref):
    hn = _rms(x_ref[0], gmix_ref[...]).astype(BF16)

    def proj(w_ref):
        return lax.dot_general(hn, w_ref[...], NT_DIMS, preferred_element_type=F32)

    u = proj(wc_ref) * proj(wu_ref)
    utail_ref[...] = u[META_ROWS - SUBLANES:]
    kvn = _rms(proj(wkvl_ref), gkv_ref[...]).astype(BF16)
    kv = _dot(kvn, wkv_ref[...])
    kn_ref[0] = kv[:, :ATTN_W].astype(BF16)
    vt = kv[:, ATTN_W:].T.astype(BF16)
    ones = jnp.ones((V_EXT - V_DIM, META_ROWS), BF16)
    for h in range(N_HEADS):
        vt_ref[0, 0, h * V_EXT:h * V_EXT + V_DIM, :] = vt[h * V_DIM:(h + 1) * V_DIM]
        vt_ref[0, 0, h * V_EXT + V_DIM:(h + 1) * V_EXT, :] = ones
    kk = proj(wk2_ref)
    kpe_ref[0] = (kk[:, :LANES] * tab_ref[:, :LANES] + kk[:, LANES:] * tab_ref[:, LANES:]).astype(BF16)


def _mix_meta(meta_blk, tab, w):
    d = meta_blk.shape[2]
    whole = lambda shape: pl.BlockSpec(shape, lambda i: (0,) * len(shape))
    w1_rows = lambda first, n: pl.BlockSpec((n, d), lambda i: (first // n, 0))
    args = (meta_blk, w["g_mix"], w["w1"], w["w1"], w["w1"], w["wk2"], w["g_kv"], w["wkv"], tab)
    out_shape = [
        jax.ShapeDtypeStruct((1, META_ROWS, ATTN_W), BF16),
        jax.ShapeDtypeStruct((1, META_ROWS, LANES), BF16),
        jax.ShapeDtypeStruct((1, 1, N_HEADS * V_EXT, META_ROWS), BF16),
        jax.ShapeDtypeStruct((SUBLANES, D_CONV), F32),
    ]
    return pl.pallas_call(
        _meta_kernel,
        grid=(1,),
        in_specs=[whole(meta_blk.shape), whole(w["g_mix"].shape), w1_rows(O_C, D_CONV), w1_rows(O_U, D_CONV),
                  w1_rows(O_KV, KV_LORA), whole(w["wk2"].shape), whole(w["g_kv"].shape), whole(w["wkv"].shape),
                  whole(tab.shape)],
        out_specs=[whole(o.shape) for o in out_shape],
        out_shape=out_shape,
        compiler_params=pltpu.CompilerParams(dimension_semantics=("arbitrary",), vmem_limit_bytes=VMEM_LIMIT),
        name="mix_meta",
    )(*args)


def _mix_in(x3, uinit, tab, w, *, tm):
    b, l, d = x3.shape
    nt = l // tm
    kern = functools.partial(_mix_in_kernel, tm=tm, scale=float((QK_NOPE + QK_ROPE) ** -0.5))
    row = lambda width: pl.BlockSpec((1, tm, width), lambda bi, i: (bi, i, 0))
    return pl.pallas_call(
        kern,
        grid=(b, nt),
        in_specs=[
            row(d),
            _resident((1, d)),
            _resident((D_IN, d)),
            _resident((2 * LANES, d)),
            _resident((SUBLANES, D_CONV)),
            _resident((1, Q_LORA)),
            _resident((Q_LORA, N_HEADS * (HEAD_W + LANES))),
            _resident((1, KV_LORA)),
            _resident((KV_LORA, 2 * ATTN_W)),
            _resident((1, D_CONV)),
            _resident((D_CONV, D_CONV)),
            pl.BlockSpec((tm, 2 * LANES), lambda bi, i: (i, 0)),
            _resident((SUBLANES, D_CONV)),
        ],
        out_specs=[
            row(D_CONV),
            pl.BlockSpec((1, N_HEADS * HEAD_W, tm), lambda bi, i: (bi, 0, i)),
            row(ATTN_W), row(LANES),
            pl.BlockSpec((1, 1, N_HEADS * V_EXT, tm), lambda bi, i: (bi, i, 0, 0)),
            pl.BlockSpec((SUBLANES, D_CONV), lambda bi, i: (bi * nt + i, 0)),
        ],
        out_shape=[
            jax.ShapeDtypeStruct((b, l, D_CONV), BF16),
            jax.ShapeDtypeStruct((b, N_HEADS * HEAD_W, l), BF16),
            jax.ShapeDtypeStruct((b, l, ATTN_W), BF16),
            jax.ShapeDtypeStruct((b, l, LANES), BF16),
            jax.ShapeDtypeStruct((b, nt, N_HEADS * V_EXT, tm), BF16),
            jax.ShapeDtypeStruct((b * nt * SUBLANES, D_CONV), F32),
        ],
        scratch_shapes=[pltpu.VMEM((tm + SUBLANES, D_CONV), F32)],
        compiler_params=pltpu.CompilerParams(
            dimension_semantics=("arbitrary", "arbitrary"), vmem_limit_bytes=VMEM_LIMIT),
        name="mix_in",
    )(x3, w["g_mix"], w["w1"], w["wk2"], w["conv_w"], w["g_q"], w["wq"], w["g_kv"], w["wkv"], w["g_conv_out"],
      w["gmat"], tab, uinit)


def _attn_kernel(qt_ref, kn_ref, kpe_ref, vt_ref, mkn_ref, mkpe_ref, mvt_ref, g_ref, o_ref,
                 m_ref, acc_ref, *, tq, hps):
    qi = pl.program_id(2)
    tv = vt_ref.shape[3]

    def lanes(hh):
        return slice(hh * LANES, (hh + 1) * LANES)

    def vrows(hh):
        return slice(hh * V_EXT, (hh + 1) * V_EXT)

    def qt(hh):
        return qt_ref[0, hh * HEAD_W:(hh + 1) * HEAD_W, :]

    for hh in range(hps):
        km = jnp.concatenate([mkn_ref[0, :, lanes(hh)], mkpe_ref[0]], axis=-1)
        s = _dot(km, qt(hh))
        row = lax.broadcasted_iota(jnp.int32, s.shape, 0)
        s = jnp.where(row >= META_ROWS - N_META, s, -jnp.inf)
        m0 = jnp.max(s, axis=0, keepdims=True)
        m_ref[hh] = m0
        acc_ref[hh] = _dot(mvt_ref[0, 0, vrows(hh), :], jnp.exp((s - m0).astype(BF16)))

    def step(kb, diagonal):
        off = pl.multiple_of(kb * tq, tq)
        kpe = kpe_ref[0, pl.ds(off, tq), :]
        scores = []
        for hh in range(hps):
            k = jnp.concatenate([kn_ref[0, pl.ds(off, tq), lanes(hh)], kpe], axis=-1)
            s = _dot(k, qt(hh))
            if diagonal:
                r = lax.broadcasted_iota(jnp.int32, s.shape, 0)
                c = lax.broadcasted_iota(jnp.int32, s.shape, 1)
                s = jnp.where(r <= c, s, -jnp.inf)
            scores.append(s)
        probs, alphas = [], []
        for hh in range(hps):
            m_prev = m_ref[hh]
            m_new = jnp.maximum(m_prev, jnp.max(scores[hh], axis=0, keepdims=True))
            alphas.append(jnp.exp(m_prev - m_new))
            probs.append(jnp.exp((scores[hh] - m_new).astype(BF16)))
            m_ref[hh] = m_new
        for hh in range(hps):
            pb = probs[hh]
            pv = _dot(vt_ref[0, kb * (tq // tv), vrows(hh), :], pb[:tv])
            for c in range(1, tq // tv):
                pv = pv + _dot(vt_ref[0, kb * (tq // tv) + c, vrows(hh), :], pb[c * tv:(c + 1) * tv])
            acc_ref[hh] = alphas[hh] * acc_ref[hh] + pv

    def body(kb, carry):
        step(kb, False)
        return carry

    lax.fori_loop(0, qi, body, 0)
    step(qi, True)

    for hh in range(hps):
        acc = acc_ref[hh]
        o = (acc[:V_DIM] / acc[V_DIM:V_DIM + 1]).T
        o_ref[0, :, lanes(hh)] = _rms(o, g_ref[:, lanes(hh)]).astype(BF16)


def _attention(qt, kn, kpe, vt, mkn, mkpe, mvt, g_attn, *, tq, hps=4):
    b, l, _ = kn.shape
    _, nt, _, tv = vt.shape
    nq = l // tq
    kern = functools.partial(_attn_kernel, tq=tq, hps=hps)
    return pl.pallas_call(
        kern,
        grid=(b, N_HEADS // hps, nq),
        in_specs=[
            pl.BlockSpec((1, hps * HEAD_W, tq), lambda bi, h, i: (bi, h, i)),
            pl.BlockSpec((1, l, hps * LANES), lambda bi, h, i: (bi, 0, h)),
            pl.BlockSpec((1, l, LANES), lambda bi, h, i: (bi, 0, 0)),
            pl.BlockSpec((1, nt, hps * V_EXT, tv), lambda bi, h, i: (bi, 0, h, 0)),
            pl.BlockSpec((1, META_ROWS, hps * LANES), lambda bi, h, i: (0, 0, h)),
            pl.BlockSpec((1, META_ROWS, LANES), lambda bi, h, i: (0, 0, 0)),
            pl.BlockSpec((1, 1, hps * V_EXT, META_ROWS), lambda bi, h, i: (0, 0, h, 0)),
            pl.BlockSpec((1, hps * V_DIM), lambda bi, h, i: (0, h)),
        ],
        out_specs=pl.BlockSpec((1, tq, hps * V_DIM), lambda bi, h, i: (bi, i, h)),
        out_shape=jax.ShapeDtypeStruct((b, l, ATTN_W), BF16),
        scratch_shapes=[pltpu.VMEM((hps, 1, tq), F32), pltpu.VMEM((hps, V_EXT, tq), F32)],
        compiler_params=pltpu.CompilerParams(
            dimension_semantics=("arbitrary", "arbitrary", "arbitrary"), vmem_limit_bytes=VMEM_LIMIT),
        name="attn",
    )(qt, kn, kpe, vt, mkn, mkpe, mvt, g_attn)


def _mix_out_kernel(convn_ref, attn_ref, x_ref, wo_ref, gffn_ref, wr_ref, br_ref,
                    h1_ref, xn_ref, ri_ref, rf_ref, cnt_ref, carry_ref, *, tm):
    @pl.when(pl.program_id(0) == 0)
    def _():
        carry_ref[...] = jnp.zeros_like(carry_ref)

    h1 = x_ref[...] + _dot(convn_ref[...], wo_ref[:D_CONV]) + _dot(attn_ref[...], wo_ref[D_CONV:])
    h1_ref[...] = h1
    xn = _rms(h1, gffn_ref[...])
    xn_ref[...] = xn

    xh = xn.astype(BF16)
    xl = (xn - xh.astype(F32)).astype(BF16)
    ph = _dot(xh, wr_ref[...])
    logits = ph[:, :LANES] + ph[:, LANES:] + _dot(xl, wr_ref[:, :LANES]) + br_ref[...]

    lane = lax.broadcasted_iota(jnp.int32, (tm, LANES), 1).astype(F32)
    work = logits
    top_v, top_i, onehots = [], [], []
    for _ in range(TOP_K):
        mk = jnp.max(work, axis=-1, keepdims=True)
        ik = jnp.min(jnp.where(work == mk, lane, float(LANES)), axis=-1, keepdims=True)
        oh = lane == ik
        work = jnp.where(oh, -jnp.inf, work)
        top_v.append(mk)
        top_i.append(ik)
        onehots.append(oh)

    ex = [jnp.exp(v - top_v[0]) for v in top_v]
    denom = ex[0] + ex[1] + ex[2] + ex[3]
    gates = [e / denom for e in ex]

    sel = jnp.zeros((tm, LANES), F32)
    for oh in onehots:
        sel = sel + oh.astype(F32)
    r = lax.broadcasted_iota(jnp.int32, (tm, tm), 0)
    c = lax.broadcasted_iota(jnp.int32, (tm, tm), 1)
    tri = jnp.where(c < r, 1.0, 0.0).astype(BF16)
    cum = _dot(tri, sel.astype(BF16)) + carry_ref[...]
    carry_ref[...] = carry_ref[...] + jnp.sum(sel, axis=0, keepdims=True)
    cnt_ref[...] = carry_ref[...]

    ri = jnp.zeros((tm, LANES), F32)
    rf = jnp.zeros((tm, LANES), F32)
    for k in range(TOP_K):
        rank_k = jnp.sum(jnp.where(onehots[k], cum, 0.0), axis=-1, keepdims=True)
        ri = jnp.where(lane == float(k), top_i[k], ri)
        ri = jnp.where(lane == float(TOP_K + k), rank_k, ri)
        rf = jnp.where(lane == float(k), gates[k], rf)
    ri_ref[...] = ri.T[:2 * TOP_K].astype(jnp.int32)
    rf_ref[...] = rf


def _mix_out(convn, attn, x2, w, *, tm):
    t, d = x2.shape
    kern = functools.partial(_mix_out_kernel, tm=tm)
    row = lambda width: pl.BlockSpec((tm, width), lambda i: (i, 0))
    return pl.pallas_call(
        kern,
        grid=(t // tm,),
        in_specs=[
            row(D_CONV), row(ATTN_W), row(d),
            _resident((D_CONV + ATTN_W, d)),
            _resident((1, d)),
            _resident((d, 2 * LANES)),
            _resident((1, LANES)),
        ],
        out_specs=[row(d), row(d), pl.BlockSpec((2 * TOP_K, tm), lambda i: (0, i)), row(LANES),
                   pl.BlockSpec((1, LANES), lambda i: (0, 0))],
        out_shape=[
            jax.ShapeDtypeStruct((t, d), F32),
            jax.ShapeDtypeStruct((t, d), F32),
            jax.ShapeDtypeStruct((2 * TOP_K, t), jnp.int32),
            jax.ShapeDtypeStruct((t, LANES), F32),
            jax.ShapeDtypeStruct((1, LANES), F32),
        ],
        scratch_shapes=[pltpu.VMEM((1, LANES), F32)],
        compiler_params=pltpu.CompilerParams(
            dimension_semantics=("arbitrary",), vmem_limit_bytes=VMEM_LIMIT),
        name="mix_out",
    )(convn, attn, x2, w["wo"], w["g_ffn"], w["wr"], w["br"])


def _dispatch_kernel(dest_ref, pend_ref, xn_ref, xg_hbm, zbuf_ref, sem, zsem, *, tmd, sb):
    i = pl.program_id(0)

    @pl.when(i == 0)
    def _():
        zbuf_ref[...] = jnp.zeros_like(zbuf_ref)

        def zero_copy(e):
            end = pend_ref[e]
            start = pl.multiple_of(jnp.maximum(end - sb, 0), sb)
            return pltpu.make_async_copy(zbuf_ref, xg_hbm.at[pl.ds(start, sb)], zsem)

        def nonempty(e):
            prev = pend_ref[jnp.maximum(e - 1, 0)]
            return pend_ref[e] > jnp.where(e > 0, prev, 0)

        def start(e, carry):
            @pl.when(nonempty(e))
            def _():
                zero_copy(e).start()
            return carry

        def wait(e, carry):
            @pl.when(nonempty(e))
            def _():
                zero_copy(e).wait()
            return carry

        lax.fori_loop(0, N_EXPERTS, start, 0)
        lax.fori_loop(0, N_EXPERTS, wait, 0)
        _zero_tail(zbuf_ref, xg_hbm, pend_ref[N_EXPERTS - 1], zsem)

    base = i * tmd
    n_tok = pl.num_programs(0) * tmd

    def issue(r, carry):
        for k in range(TOP_K):
            dst = xg_hbm.at[pl.ds(dest_ref[k * n_tok + base + r], 1)]
            pltpu.make_async_copy(xn_ref.at[pl.ds(r, 1)], dst, sem).start()
        return carry

    lax.fori_loop(0, tmd, issue, 0)
    for k in range(TOP_K):
        pltpu.make_async_copy(xn_ref, xg_hbm.at[pl.ds(0, tmd)], sem).wait()


def _dispatch(dest, pad_end, xn, *, rows, tmd, sb):
    t, d = xn.shape
    kern = functools.partial(_dispatch_kernel, tmd=tmd, sb=sb)
    return pl.pallas_call(
        kern,
        grid_spec=pltpu.PrefetchScalarGridSpec(
            num_scalar_prefetch=2,
            grid=(t // tmd,),
            in_specs=[pl.BlockSpec((tmd, d), lambda i, dest, pend: (i, 0))],
            out_specs=pl.BlockSpec(memory_space=pl.ANY),
            scratch_shapes=[pltpu.VMEM((sb, d), F32), pltpu.SemaphoreType.DMA, pltpu.SemaphoreType.DMA],
        ),
        out_shape=jax.ShapeDtypeStruct((rows, d), F32),
        compiler_params=pltpu.CompilerParams(
            dimension_semantics=("arbitrary",), vmem_limit_bytes=VMEM_LIMIT),
        name="dispatch",
    )(dest, pad_end, xn)


def _expert_kernel(ie_ref, irow_ref, inr_ref, tail_ref, xg_hbm, wg_ref, wl_ref, wd_ref, bias_ref, yg_hbm,
                   xst_ref, xb_ref, act_ref, acc_ref, sem_in, sem_out,
                   *, tm, sb, row_sizes):
    it = pl.program_id(0)
    s = pl.program_id(1)
    n_it = pl.num_programs(0)
    nrows = inr_ref[it]
    n_up = act_ref.shape[0]
    n_down, _, fc = acc_ref.shape
    last = n_up + n_down - 1
    prev_it = jnp.maximum(it - 1, 0)
    next_it = jnp.minimum(it + 1, n_it - 1)

    def piece(r):
        return pl.ds(r * sb, sb)

    def hbm_rows(item, r):
        return pl.ds(pl.multiple_of(irow_ref[item] + r * sb, sb), sb)

    def in_copy(item, r):
        return pltpu.make_async_copy(xg_hbm.at[hbm_rows(item, r)], xst_ref.at[piece(r)], sem_in.at[r])

    def out_copies(item, r):
        return [pltpu.make_async_copy(acc_ref.at[n, piece(r)], yg_hbm.at[hbm_rows(item, r), n * fc:(n + 1) * fc],
                                      sem_out.at[r]) for n in range(n_down)]

    def for_pieces_of(item, fn):
        for r in range(tm // sb):
            pl.when(r * sb < inr_ref[item])(functools.partial(fn, item, r))

    def for_valid_chunks(fn):
        for lo, size in zip((0,) + row_sizes[:-1], row_sizes):
            pl.when((lo < nrows) & (nrows <= size))(functools.partial(fn, pl.ds(0, size)))

    def start_in(item, r):
        in_copy(item, r).start()

    def land_in(item, r):
        in_copy(item, r).wait()
        xb_ref[piece(r)] = xst_ref[piece(r)].astype(BF16)

    def start_out(item, r):
        for cp in out_copies(item, r):
            cp.start()

    def wait_out(item, r):
        for cp in out_copies(item, r):
            cp.wait()

    @pl.when((it == 0) & (s == 0))
    def _():
        xb_ref[...] = jnp.zeros_like(xb_ref)
        for_pieces_of(it, start_in)

    @pl.when(s == 0)
    def _():
        for_pieces_of(it, land_in)

    @pl.when((s == 1) & (it + 1 < n_it))
    def _():
        for_pieces_of(next_it, start_in)

    @pl.when((s == n_up) & (it > 0))
    def _():
        for_pieces_of(prev_it, wait_out)

    def up(rows):
        xs = xb_ref[rows]
        bg = bias_ref[0, pl.ds(s, 1), :]
        bl = bias_ref[0, pl.ds(n_up + s, 1), :]
        g = jnp.minimum(_dot(xs, wg_ref[0].astype(BF16)) + bg, SWIGLU_LIMIT)
        lin = jnp.clip(_dot(xs, wl_ref[0].astype(BF16)) + bl, -SWIGLU_LIMIT, SWIGLU_LIMIT)
        act_ref[s, rows] = ((lin + 1.0) * (g * jax.nn.sigmoid(SWIGLU_ALPHA * g))).astype(BF16)

    def down(rows):
        act = jnp.concatenate([act_ref[jj, rows] for jj in range(n_up)], axis=-1)
        per_step = fc // act_ref.shape[2]
        first = 2 * n_up + (s - n_up) * per_step
        bd = jnp.concatenate([bias_ref[0, pl.ds(first + c, 1), :] for c in range(per_step)], axis=-1)
        acc_ref[s - n_up, rows] = _dot(act, wd_ref[0].astype(BF16)) + bd

    @pl.when(s < n_up)
    def _():
        for_valid_chunks(up)

    @pl.when(s >= n_up)
    def _():
        for_valid_chunks(down)

    @pl.when(s == last)
    def _():
        for_pieces_of(it, start_out)

    @pl.when((it == n_it - 1) & (s == last))
    def _():
        for_pieces_of(it, wait_out)
        acc_ref[0, piece(0)] = jnp.zeros((sb, fc), F32)
        _zero_tail(acc_ref.at[0, piece(0)], yg_hbm, tail_ref[0], sem_out.at[0])


def _zero_tail(zeros_vmem, dst_hbm, first_row, sem):
    sb, width = zeros_vmem.shape
    n_blocks = (dst_hbm.shape[0] - first_row) // sb

    def fill(b, carry):
        rows = pl.ds(pl.multiple_of(first_row + b * sb, sb), sb)
        copies = [pltpu.make_async_copy(zeros_vmem, dst_hbm.at[rows, c * width:(c + 1) * width], sem)
                  for c in range(dst_hbm.shape[1] // width)]
        for cp in copies:
            cp.start()
        for cp in copies:
            cp.wait()
        return carry

    lax.fori_loop(0, n_blocks, fill, 0)


def _experts(n_used, item_e, item_row, item_nrows, tail, xg, w_gate_up, b_gate_up, w_down, b_down, *, tm, sb, fc):
    rows, d = xg.shape
    dff = w_down.shape[1]
    n_up = dff // fc
    fd = 2 * fc
    n_down = d // fd
    nsub = tm // sb
    kern = functools.partial(_expert_kernel, tm=tm, sb=sb, row_sizes=_row_sizes(tm, sb))
    n_e = w_down.shape[0]
    bias_rows = [b_gate_up.reshape(n_e, 2 * n_up, fc), b_down.reshape(n_e, d // fc, fc)]
    n_rows = 2 * n_up + d // fc
    bias_rows.append(jnp.zeros((n_e, -n_rows % SUBLANES, fc), F32))
    biases = jnp.concatenate(bias_rows, axis=1)

    def up_block(offset):
        def index_map(it, s, ie, ir, inr, tl):
            ahead = s >= n_up
            nxt = jnp.minimum(it + 1, tl[1] - 1)
            return jnp.where(ahead, ie[nxt], ie[it]), 0, offset + jnp.where(ahead, 0, s)
        return index_map

    def down_block(it, s, ie, ir, inr, tl):
        return ie[it], 0, jnp.maximum(s - n_up, 0)

    return pl.pallas_call(
        kern,
        grid_spec=pltpu.PrefetchScalarGridSpec(
            num_scalar_prefetch=4,
            grid=(n_used, n_up + n_down),
            in_specs=[
                pl.BlockSpec(memory_space=pl.ANY),
                pl.BlockSpec((1, d, fc), up_block(0)),
                pl.BlockSpec((1, d, fc), up_block(n_up)),
                pl.BlockSpec((1, dff, fd), down_block),
                pl.BlockSpec((1, biases.shape[1], fc), lambda it, s, ie, ir, inr, tl: (ie[it], 0, 0)),
            ],
            out_specs=pl.BlockSpec(memory_space=pl.ANY),
            scratch_shapes=[
                pltpu.VMEM((tm, d), F32),
                pltpu.VMEM((tm, d), BF16),
                pltpu.VMEM((n_up, tm, fc), BF16),
                pltpu.VMEM((n_down, tm, fd), F32),
                pltpu.SemaphoreType.DMA((nsub,)),
                pltpu.SemaphoreType.DMA((nsub,)),
            ],
        ),
        out_shape=jax.ShapeDtypeStruct((rows, d), F32),
        compiler_params=pltpu.CompilerParams(
            dimension_semantics=("arbitrary", "arbitrary"), vmem_limit_bytes=VMEM_LIMIT),
        name="experts",
    )(item_e, item_row, item_nrows, tail, xg, w_gate_up, w_gate_up, w_down, biases)


def _combine_kernel(dest_ref, yg_hbm, h1_ref, rf_ref, gfin_ref, o_ref, gbuf_ref, sem, *, tmc):
    i = pl.program_id(0)
    n_tiles = pl.num_programs(0)
    n_tok = n_tiles * tmc

    def issue_tile(tile):
        buf = tile % 2
        base = tile * tmc

        def issue(r, carry):
            for k in range(TOP_K):
                src = yg_hbm.at[pl.ds(dest_ref[k * n_tok + base + r], 1)]
                pltpu.make_async_copy(src, gbuf_ref.at[buf, k, pl.ds(r, 1)], sem.at[buf]).start()
            return carry

        lax.fori_loop(0, tmc, issue, 0)

    @pl.when(i == 0)
    def _():
        issue_tile(i)

    @pl.when(i + 1 < n_tiles)
    def _():
        issue_tile(i + 1)

    buf = i % 2
    for k in range(TOP_K):
        pltpu.make_async_copy(yg_hbm.at[pl.ds(0, tmc)], gbuf_ref.at[buf, k], sem.at[buf]).wait()

    gates = rf_ref[...]
    y = h1_ref[...]
    for k in range(TOP_K):
        y = y + gates[:, k:k + 1] * gbuf_ref[buf, k]
    o_ref[...] = _rms(y, gfin_ref[...])


def _combine(dest, yg, h1, rf, g_final, *, tmc):
    t, d = h1.shape
    kern = functools.partial(_combine_kernel, tmc=tmc)
    return pl.pallas_call(
        kern,
        grid_spec=pltpu.PrefetchScalarGridSpec(
            num_scalar_prefetch=1,
            grid=(t // tmc,),
            in_specs=[
                pl.BlockSpec(memory_space=pl.ANY),
                pl.BlockSpec((tmc, d), lambda i, dest: (i, 0)),
                pl.BlockSpec((tmc, LANES), lambda i, dest: (i, 0)),
                pl.BlockSpec((1, d), lambda i, dest: (0, 0)),
            ],
            out_specs=pl.BlockSpec((tmc, d), lambda i, dest: (i, 0)),
            scratch_shapes=[pltpu.VMEM((2, TOP_K, tmc, d), F32), pltpu.SemaphoreType.DMA((2,))],
        ),
        out_shape=jax.ShapeDtypeStruct((t, d), F32),
        compiler_params=pltpu.CompilerParams(
            dimension_semantics=("arbitrary",), vmem_limit_bytes=VMEM_LIMIT),
        name="combine",
    )(dest, yg, h1, rf, g_final)


def _cast_kernel(x_ref, o_ref):
    o_ref[...] = x_ref[...].astype(o_ref.dtype)


def _to_bf16(w, *, n_blocks):
    rows, cols = w.shape
    spec = pl.BlockSpec((rows // n_blocks, cols), lambda i: (i, 0))
    return pl.pallas_call(
        _cast_kernel, grid=(n_blocks,), in_specs=[spec], out_specs=spec,
        out_shape=jax.ShapeDtypeStruct(w.shape, BF16),
        compiler_params=pltpu.CompilerParams(dimension_semantics=("arbitrary",), vmem_limit_bytes=VMEM_LIMIT),
        name="to_bf16",
    )(w)


def _rotate_half_cols(w):
    half = w.shape[-1] // 2
    return jnp.concatenate([-w[..., half:], w[..., :half]], axis=-1)


def _prep_weights(g_mix, w_in, conv_w, g_q, w_q_up, g_kv, w_kv_up, g_conv_out, w_out, g_ffn, w_router, b_router):
    d = w_in.shape[0]
    w1 = _to_bf16(w_in.T, n_blocks=4)
    k_rope = w1[O_KPE:O_KPE + QK_ROPE]
    pad = jnp.zeros((LANES - QK_ROPE, d), BF16)
    half = QK_ROPE // 2
    k_rot = jnp.concatenate([-k_rope[half:], k_rope[:half]], axis=0)
    wk2 = jnp.concatenate([k_rope, pad, k_rot, pad], axis=0)

    wq = w_q_up.reshape(Q_LORA, N_HEADS, QK_NOPE + QK_ROPE)
    nope, pe = wq[:, :, :QK_NOPE], wq[:, :, QK_NOPE:]
    hpad = jnp.zeros((Q_LORA, N_HEADS, LANES - QK_ROPE), F32)
    wqa = jnp.concatenate([nope, pe, hpad], axis=2).reshape(Q_LORA, N_HEADS * HEAD_W)
    wqb = jnp.concatenate([_rotate_half_cols(pe), hpad], axis=2).reshape(Q_LORA, N_HEADS * LANES)
    wq_all = jnp.concatenate([wqa, wqb], axis=1).astype(BF16)

    wkv = w_kv_up.reshape(KV_LORA, N_HEADS, QK_NOPE + V_DIM)
    wkv2 = jnp.concatenate([wkv[:, :, :QK_NOPE].reshape(KV_LORA, ATTN_W),
                            wkv[:, :, QK_NOPE:].reshape(KV_LORA, ATTN_W)], axis=1).astype(BF16)

    wr_pad = jnp.zeros((d, LANES), F32).at[:, :N_EXPERTS].set(w_router)
    wr_hi = wr_pad.astype(BF16)
    wr_lo = (wr_pad - wr_hi.astype(F32)).astype(BF16)
    br = jnp.full((1, LANES), -1e30, F32).at[0, :N_EXPERTS].set(b_router)

    grp = jnp.arange(D_CONV) // (D_CONV // CONV_GROUPS)
    cw = jnp.zeros((SUBLANES, D_CONV), F32).at[:CONV_WIDTH].set(conv_w)
    return {
        "g_mix": g_mix[None], "w1": w1, "wk2": wk2, "conv_w": cw, "g_q": g_q[None], "wq": wq_all, "g_kv": g_kv[None],
        "wkv": wkv2, "g_conv_out": g_conv_out[None], "gmat": (grp[:, None] == grp[None, :]).astype(BF16),
        "wo": w_out.astype(BF16), "g_ffn": g_ffn[None], "wr": jnp.concatenate([wr_hi, wr_lo], axis=1), "br": br,
    }


def _rope_table(pos):
    half = QK_ROPE // 2
    inv_freq = np.float32(ROPE_THETA) ** (-np.arange(half, dtype=np.float32) / np.float32(half))
    ang = (np.asarray(pos, np.float32)[:, None] * inv_freq[None, :]).astype(np.float32)
    c, s = np.cos(ang), np.sin(ang)
    z = np.zeros((ang.shape[0], LANES - QK_ROPE), np.float32)
    return jnp.asarray(np.concatenate([c, c, z, s, s, z], axis=1), dtype=F32)


def _schedule_kernel(cnt_ref, ri_ref, dest_ref, pend_ref, ie_ref, irow_ref, inr_ref, tail_ref, pstart_ref,
                     *, tm, sb, n_items):
    def per_expert(e, carry):
        row, item = carry
        count = cnt_ref[e]
        pstart_ref[e] = row

        def per_item(li, item):
            ie_ref[item] = e
            irow_ref[item] = row + li * tm
            inr_ref[item] = jnp.minimum(count - li * tm, tm)
            return item + 1

        item = lax.fori_loop(0, lax.div(count + (tm - 1), tm), per_item, item)
        row = row + lax.div(count + (sb - 1), sb) * sb
        pend_ref[e] = row
        return row, item

    row, n_used = lax.fori_loop(0, N_EXPERTS, per_expert, (jnp.int32(0), jnp.int32(0)))
    tail_ref[0] = row
    tail_ref[1] = n_used

    def unused(item, carry):
        ie_ref[item] = 0
        irow_ref[item] = 0
        inr_ref[item] = 0
        return carry

    lax.fori_loop(n_used, n_items, unused, 0)

    eidx = ri_ref[:TOP_K]
    start_of = jnp.zeros_like(eidx)
    for e in range(N_EXPERTS):
        start_of = jnp.where(eidx == e, pstart_ref[e], start_of)
    dest_ref[...] = start_of + ri_ref[TOP_K:]


def _schedule(counts, ri, *, tm, sb, n_items):
    t = ri.shape[1]
    smem = pl.BlockSpec(memory_space=pltpu.SMEM)
    i32 = lambda n: jax.ShapeDtypeStruct((n,), jnp.int32)
    dest, pad_end, item_e, item_row, item_nrows, tail = pl.pallas_call(
        functools.partial(_schedule_kernel, tm=tm, sb=sb, n_items=n_items),
        in_specs=[smem, pl.BlockSpec(memory_space=pltpu.VMEM)],
        out_specs=[pl.BlockSpec(memory_space=pltpu.VMEM), smem, smem, smem, smem, smem],
        out_shape=[jax.ShapeDtypeStruct((TOP_K, t), jnp.int32), i32(N_EXPERTS), i32(n_items), i32(n_items),
                   i32(n_items), i32(2)],
        scratch_shapes=[pltpu.SMEM((N_EXPERTS,), jnp.int32)],
        name="schedule",
    )(counts, ri)
    return dest.reshape(-1), pad_end, item_e, item_row, item_nrows, tail


def _row_sizes(tm, sb):
    fine = [tm - k * sb for k in (2, 1, 0) if tm - k * sb > 0]
    coarse = [s for s in (tm // 4 // sb * sb, tm // 2 // sb * sb) if 0 < s < min(fine)]
    return tuple(sorted(set(coarse + fine)))


def _moe_tiles(t):
    sb = 128
    tm = 10 * sb
    a = t * TOP_K
    rows = (a + N_EXPERTS * (sb - 1) + sb - 1) // sb * sb
    n_items = N_EXPERTS + a // tm
    return sb, tm, rows, n_items


def _layer(x, meta_tokens, w, w_gate_up, b_gate_up, w_down, b_down, g_attn_out, g_final, *,
           tm_in, tq, tm_out, tmd, tmc, fc):
    b, seq, d = x.shape
    t = b * seq

    meta_blk = jnp.zeros((1, META_ROWS, d), F32).at[0, META_ROWS - N_META:].set(meta_tokens)
    meta_pos = np.maximum(np.arange(META_ROWS) - (META_ROWS - N_META), 0)
    mkn, mkpe, mv, u_tail = _mix_meta(meta_blk, _rope_table(meta_pos), w)

    real_pos = np.arange(seq) + N_META
    convn, q, kn, kpe, v, _ = _mix_in(x, u_tail, _rope_table(real_pos), w, tm=tm_in)
    attn = _attention(q, kn, kpe, v, mkn, mkpe, mv, g_attn_out[None], tq=tq)

    h1, xn, ri, rf, cnt = _mix_out(convn.reshape(t, D_CONV), attn.reshape(t, ATTN_W), x.reshape(t, d), w, tm=tm_out)

    sb, tm_e, rows, n_items = _moe_tiles(t)
    counts = cnt[0, :N_EXPERTS].astype(jnp.int32)
    dest, pad_end, item_e, item_row, item_nrows, tail = _schedule(counts, ri, tm=tm_e, sb=sb, n_items=n_items)

    xg = _dispatch(dest, pad_end, xn, rows=rows, tmd=tmd, sb=sb)
    dff = w_down.shape[1]
    yg = _experts(tail[1], item_e, item_row, item_nrows, tail, xg, w_gate_up, b_gate_up.reshape(N_EXPERTS, 1, 2 * dff),
                  w_down, b_down.reshape(N_EXPERTS, 1, d), tm=tm_e, sb=sb, fc=fc)
    out = _combine(dest, yg, h1, rf, g_final[None], tmc=tmc)
    return out.reshape(b, seq, d)


def kernel(x, meta_tokens, g_mix, w_in, conv_w, g_q, w_q_up, g_kv, w_kv_up, g_conv_out, g_attn_out, w_out, g_ffn,
           w_router, b_router, w_gate_up, b_gate_up, w_down, b_down, g_final):
    w = _prep_weights(g_mix[0], w_in[0], conv_w[0], g_q[0], w_q_up[0], g_kv[0], w_kv_up[0], g_conv_out[0],
                      w_out[0], g_ffn[0], w_router[0], b_router[0])
    seq = x.shape[1]
    return _layer(x, meta_tokens, w, w_gate_up[0], b_gate_up[0], w_down[0], b_down[0], g_attn_out[0], g_final,
                  tm_in=min(512, seq), tq=min(512, seq), tm_out=min(512, seq), tmd=min(1024, seq),
                  tmc=min(512, seq), fc=256)
```

```python
import functools

import jax
import jax.numpy as jnp
import numpy as np
from jax import lax
from jax.experimental import pallas as pl
from jax.experimental.pallas import tpu as pltpu

N_META = 16
EPS = 1e-6
D_CONV = 1024
CONV_GROUPS = 16
CONV_WIDTH = 3
N_HEADS = 8
QK_NOPE = 128
QK_ROPE = 64
V_DIM = 128
Q_LORA = 512
KV_LORA = 256
ROPE_THETA = 10000.0
N_EXPERTS = 32
TOP_K = 4
SWIGLU_LIMIT = 7.0
SWIGLU_ALPHA = 1.702

LANES = 128
SUBLANES = 8
META_ROWS = 128
HEAD_W = 2 * LANES
ATTN_W = N_HEADS * V_DIM
V_EXT = V_DIM + 16
VMEM_LIMIT = 56 * 1024 * 1024

O_B, O_C, O_U = 0, D_CONV, 2 * D_CONV
O_Q = 3 * D_CONV
O_KV = O_Q + Q_LORA
O_KPE = O_KV + KV_LORA
D_IN = O_KPE + QK_ROPE

F32 = jnp.float32
BF16 = jnp.bfloat16
NT_DIMS = (((1,), (1,)), ((), ()))


def _rms(x, g):
    return x * lax.rsqrt(jnp.mean(x * x, axis=-1, keepdims=True) + EPS) * g


def _dot(a, b):
    return jnp.dot(a, b, preferred_element_type=F32)


def _resident(shape):
    zeros = (0,) * len(shape)
    return pl.BlockSpec(shape, lambda *_: zeros, pipeline_mode=pl.Buffered(1))


def _mix_in_kernel(x_ref, gmix_ref, w1_ref, wk2_ref, cw_ref, gq_ref, wq_ref, gkv_ref, wkv_ref, gco_ref, gmat_ref,
                   tab_ref, uinit_ref,
                   convn_ref, qt_ref, kn_ref, kpe_ref, vt_ref, utail_ref,
                   ubuf_ref, *, tm, scale):
    i = pl.program_id(1)
    hn = _rms(x_ref[0], gmix_ref[...]).astype(BF16)

    def proj(lo, hi):
        return lax.dot_general(hn, w1_ref[lo:hi, :], NT_DIMS, preferred_element_type=F32)

    @pl.when(i == 0)
    def _():
        ubuf_ref[0:SUBLANES] = uinit_ref[...]

    @pl.when(i > 0)
    def _():
        ubuf_ref[0:SUBLANES] = ubuf_ref[tm:tm + SUBLANES]

    u = proj(O_C, O_U) * proj(O_U, O_Q)
    ubuf_ref[SUBLANES:SUBLANES + tm] = u
    cw = cw_ref[...]
    y = (cw[2:3] * u + cw[1:2] * ubuf_ref[SUBLANES - 1:SUBLANES - 1 + tm]
         + cw[0:1] * ubuf_ref[SUBLANES - 2:SUBLANES - 2 + tm])
    co = proj(O_B, O_C) * y
    ss = _dot((co * co).astype(BF16), gmat_ref[...])
    group = D_CONV // CONV_GROUPS
    convn_ref[0] = (co * lax.rsqrt(ss * (1.0 / group) + EPS) * gco_ref[...]).astype(BF16)
    utail_ref[...] = ubuf_ref[tm:tm + SUBLANES]

    cos = tab_ref[:, :LANES]
    sin = tab_ref[:, LANES:]

    qn = _rms(proj(O_Q, O_KV), gq_ref[...]).astype(BF16)
    qa = _dot(qn, wq_ref[:, :N_HEADS * HEAD_W])
    qb = _dot(qn, wq_ref[:, N_HEADS * HEAD_W:])
    for h in range(N_HEADS):
        c0 = h * HEAD_W
        qt_ref[0, c0:c0 + LANES, :] = (qa[:, c0:c0 + LANES] * scale).T.astype(BF16)
        pe = qa[:, c0 + LANES:c0 + HEAD_W] * cos + qb[:, h * LANES:(h + 1) * LANES] * sin
        qt_ref[0, c0 + LANES:c0 + HEAD_W, :] = (pe * scale).T.astype(BF16)

    kvn = _rms(proj(O_KV, O_KPE), gkv_ref[...]).astype(BF16)
    kv = _dot(kvn, wkv_ref[...])
    kn_ref[0] = kv[:, :ATTN_W].astype(BF16)
    vt = kv[:, ATTN_W:].T.astype(BF16)
    ones = jnp.ones((V_EXT - V_DIM, tm), BF16)
    for h in range(N_HEADS):
        vt_ref[0, 0, h * V_EXT:h * V_EXT + V_DIM, :] = vt[h * V_DIM:(h + 1) * V_DIM]
        vt_ref[0, 0, h * V_EXT + V_DIM:(h + 1) * V_EXT, :] = ones
    kk = lax.dot_general(hn, wk2_ref[...], NT_DIMS, preferred_element_type=F32)
    kpe_ref[0] = (kk[:, :LANES] * cos + kk[:, LANES:] * sin).astype(BF16)


def _meta_kernel(x_ref, gmix_ref, wc_ref, wu_ref, wkvl_ref, wk2_ref, gkv_ref, wkv_ref, tab_ref,
                 kn_ref, kpe_ref, vt_ref, utail_ref):
    hn = _rms(x_ref[0], gmix_ref[...]).astype(BF16)

    def proj(w_ref):
        return lax.dot_general(hn, w_ref[...], NT_DIMS, preferred_element_type=F32)

    u = proj(wc_ref) * proj(wu_ref)
    utail_ref[...] = u[META_ROWS - SUBLANES:]
    kvn = _rms(proj(wkvl_ref), gkv_ref[...]).astype(BF16)
    kv = _dot(kvn, wkv_ref[...])
    kn_ref[0] = kv[:, :ATTN_W].astype(BF16)
    vt = kv[:, ATTN_W:].T.astype(BF16)
    ones = jnp.ones((V_EXT - V_DIM, META_ROWS), BF16)
    for h in range(N_HEADS):
        vt_ref[0, 0, h * V_EXT:h * V_EXT + V_DIM, :] = vt[h * V_DIM:(h + 1) * V_DIM]
        vt_ref[0, 0, h * V_EXT + V_DIM:(h + 1) * V_EXT, :] = ones
    kk = proj(wk2_ref)
    kpe_ref[0] = (kk[:, :LANES] * tab_ref[:, :LANES] + kk[:, LANES:] * tab_ref[:, LANES:]).astype(BF16)


def _mix_meta(meta_blk, tab, w):
    d = meta_blk.shape[2]
    whole = lambda shape: pl.BlockSpec(shape, lambda i: (0,) * len(shape))
    w1_rows = lambda first, n: pl.BlockSpec((n, d), lambda i: (first // n, 0))
    args = (meta_blk, w["g_mix"], w["w1"], w["w1"], w["w1"], w["wk2"], w["g_kv"], w["wkv"], tab)
    out_shape = [
        jax.ShapeDtypeStruct((1, META_ROWS, ATTN_W), BF16),
        jax.ShapeDtypeStruct((1, META_ROWS, LANES), BF16),
        jax.ShapeDtypeStruct((1, 1, N_HEADS * V_EXT, META_ROWS), BF16),
        jax.ShapeDtypeStruct((SUBLANES, D_CONV), F32),
    ]
    return pl.pallas_call(
        _meta_kernel,
        grid=(1,),
        in_specs=[whole(meta_blk.shape), whole(w["g_mix"].shape), w1_rows(O_C, D_CONV), w1_rows(O_U, D_CONV),
                  w1_rows(O_KV, KV_LORA), whole(w["wk2"].shape), whole(w["g_kv"].shape), whole(w["wkv"].shape),
                  whole(tab.shape)],
        out_specs=[whole(o.shape) for o in out_shape],
        out_shape=out_shape,
        compiler_params=pltpu.CompilerParams(dimension_semantics=("arbitrary",), vmem_limit_bytes=VMEM_LIMIT),
        name="mix_meta",
    )(*args)


def _mix_in(x3, uinit, tab, w, *, tm):
    b, l, d = x3.shape
    nt = l // tm
    kern = functools.partial(_mix_in_kernel, tm=tm, scale=float((QK_NOPE + QK_ROPE) ** -0.5))
    row = lambda width: pl.BlockSpec((1, tm, width), lambda bi, i: (bi, i, 0))
    return pl.pallas_call(
        kern,
        grid=(b, nt),
        in_specs=[
            row(d),
            _resident((1, d)),
            _resident((D_IN, d)),
            _resident((2 * LANES, d)),
            _resident((SUBLANES, D_CONV)),
            _resident((1, Q_LORA)),
            _resident((Q_LORA, N_HEADS * (HEAD_W + LANES))),
            _resident((1, KV_LORA)),
            _resident((KV_LORA, 2 * ATTN_W)),
            _resident((1, D_CONV)),
            _resident((D_CONV, D_CONV)),
            pl.BlockSpec((tm, 2 * LANES), lambda bi, i: (i, 0)),
            _resident((SUBLANES, D_CONV)),
        ],
        out_specs=[
            row(D_CONV),
            pl.BlockSpec((1, N_HEADS * HEAD_W, tm), lambda bi, i: (bi, 0, i)),
            row(ATTN_W), row(LANES),
            pl.BlockSpec((1, 1, N_HEADS * V_EXT, tm), lambda bi, i: (bi, i, 0, 0)),
            pl.BlockSpec((SUBLANES, D_CONV), lambda bi, i: (bi * nt + i, 0)),
        ],
        out_shape=[
            jax.ShapeDtypeStruct((b, l, D_CONV), BF16),
            jax.ShapeDtypeStruct((b, N_HEADS * HEAD_W, l), BF16),
            jax.ShapeDtypeStruct((b, l, ATTN_W), BF16),
            jax.ShapeDtypeStruct((b, l, LANES), BF16),
            jax.ShapeDtypeStruct((b, nt, N_HEADS * V_EXT, tm), BF16),
            jax.ShapeDtypeStruct((b * nt * SUBLANES, D_CONV), F32),
        ],
        scratch_shapes=[pltpu.VMEM((tm + SUBLANES, D_CONV), F32)],
        compiler_params=pltpu.CompilerParams(
            dimension_semantics=("arbitrary", "arbitrary"), vmem_limit_bytes=VMEM_LIMIT),
        name="mix_in",
    )(x3, w["g_mix"], w["w1"], w["wk2"], w["conv_w"], w["g_q"], w["wq"], w["g_kv"], w["wkv"], w["g_conv_out"],
      w["gmat"], tab, uinit)


def _attn_kernel(qt_ref, kn_ref, kpe_ref, vt_ref, mkn_ref, mkpe_ref, mvt_ref, g_ref, o_ref,
                 m_ref, acc_ref, *, tq, hps):
    qi = pl.program_id(2)
    tv = vt_ref.shape[3]

    def lanes(hh):
        return slice(hh * LANES, (hh + 1) * LANES)

    def vrows(hh):
        return slice(hh * V_EXT, (hh + 1) * V_EXT)

    def qt(hh):
        return qt_ref[0, hh * HEAD_W:(hh + 1) * HEAD_W, :]

    for hh in range(hps):
        km = jnp.concatenate([mkn_ref[0, :, lanes(hh)], mkpe_ref[0]], axis=-1)
        s = _dot(km, qt(hh))
        row = lax.broadcasted_iota(jnp.int32, s.shape, 0)
        s = jnp.where(row >= META_ROWS - N_META, s, -jnp.inf)
        m0 = jnp.max(s, axis=0, keepdims=True)
        m_ref[hh] = m0
        acc_ref[hh] = _dot(mvt_ref[0, 0, vrows(hh), :], jnp.exp((s - m0).astype(BF16)))

    def step(kb, diagonal):
        off = pl.multiple_of(kb * tq, tq)
        kpe = kpe_ref[0, pl.ds(off, tq), :]
        scores = []
        for hh in range(hps):
            k = jnp.concatenate([kn_ref[0, pl.ds(off, tq), lanes(hh)], kpe], axis=-1)
            s = _dot(k, qt(hh))
            if diagonal:
                r = lax.broadcasted_iota(jnp.int32, s.shape, 0)
                c = lax.broadcasted_iota(jnp.int32, s.shape, 1)
                s = jnp.where(r <= c, s, -jnp.inf)
            scores.append(s)
        probs, alphas = [], []
        for hh in range(hps):
            m_prev = m_ref[hh]
            m_new = jnp.maximum(m_prev, jnp.max(scores[hh], axis=0, keepdims=True))
            alphas.append(jnp.exp(m_prev - m_new))
            probs.append(jnp.exp((scores[hh] - m_new).astype(BF16)))
            m_ref[hh] = m_new
        for hh in range(hps):
            pb = probs[hh]
            pv = _dot(vt_ref[0, kb * (tq // tv), vrows(hh), :], pb[:tv])
            for c in range(1, tq // tv):
                pv = pv + _dot(vt_ref[0, kb * (tq // tv) + c, vrows(hh), :], pb[c * tv:(c + 1) * tv])
            acc_ref[hh] = alphas[hh] * acc_ref[hh] + pv

    def body(kb, carry):
        step(kb, False)
        return carry

    lax.fori_loop(0, qi, body, 0)
    step(qi, True)

    for hh in range(hps):
        acc = acc_ref[hh]
        o = (acc[:V_DIM] / acc[V_DIM:V_DIM + 1]).T
        o_ref[0, :, lanes(hh)] = _rms(o, g_ref[:, lanes(hh)]).astype(BF16)


def _attention(qt, kn, kpe, vt, mkn, mkpe, mvt, g_attn, *, tq, hps=4):
    b, l, _ = kn.shape
    _, nt, _, tv = vt.shape
    nq = l // tq
    kern = functools.partial(_attn_kernel, tq=tq, hps=hps)
    return pl.pallas_call(
        kern,
        grid=(b, N_HEADS // hps, nq),
        in_specs=[
            pl.BlockSpec((1, hps * HEAD_W, tq), lambda bi, h, i: (bi, h, i)),
            pl.BlockSpec((1, l, hps * LANES), lambda bi, h, i: (bi, 0, h)),
            pl.BlockSpec((1, l, LANES), lambda bi, h, i: (bi, 0, 0)),
            pl.BlockSpec((1, nt, hps * V_EXT, tv), lambda bi, h, i: (bi, 0, h, 0)),
            pl.BlockSpec((1, META_ROWS, hps * LANES), lambda bi, h, i: (0, 0, h)),
            pl.BlockSpec((1, META_ROWS, LANES), lambda bi, h, i: (0, 0, 0)),
            pl.BlockSpec((1, 1, hps * V_EXT, META_ROWS), lambda bi, h, i: (0, 0, h, 0)),
            pl.BlockSpec((1, hps * V_DIM), lambda bi, h, i: (0, h)),
        ],
        out_specs=pl.BlockSpec((1, tq, hps * V_DIM), lambda bi, h, i: (bi, i, h)),
        out_shape=jax.ShapeDtypeStruct((b, l, ATTN_W), BF16),
        scratch_shapes=[pltpu.VMEM((hps, 1, tq), F32), pltpu.VMEM((hps, V_EXT, tq), F32)],
        compiler_params=pltpu.CompilerParams(
            dimension_semantics=("arbitrary", "arbitrary", "arbitrary"), vmem_limit_bytes=VMEM_LIMIT),
        name="attn",
    )(qt, kn, kpe, vt, mkn, mkpe, mvt, g_attn)


def _mix_out_kernel(convn_ref, attn_ref, x_ref, wo_ref, gffn_ref, wr_ref, br_ref,
                    h1_ref, xn_ref, ri_ref, rf_ref, cnt_ref, carry_ref, *, tm):
    @pl.when(pl.program_id(0) == 0)
    def _():
        carry_ref[...] = jnp.zeros_like(carry_ref)

    h1 = x_ref[...] + _dot(convn_ref[...], wo_ref[:D_CONV]) + _dot(attn_ref[...], wo_ref[D_CONV:])
    h1_ref[...] = h1
    xn = _rms(h1, gffn_ref[...])
    xn_ref[...] = xn

    xh = xn.astype(BF16)
    xl = (xn - xh.astype(F32)).astype(BF16)
    ph = _dot(xh, wr_ref[...])
    logits = ph[:, :LANES] + ph[:, LANES:] + _dot(xl, wr_ref[:, :LANES]) + br_ref[...]

    lane = lax.broadcasted_iota(jnp.int32, (tm, LANES), 1).astype(F32)
    work = logits
    top_v, top_i, onehots = [], [], []
    for _ in range(TOP_K):
        mk = jnp.max(work, axis=-1, keepdims=True)
        ik = jnp.min(jnp.where(work == mk, lane, float(LANES)), axis=-1, keepdims=True)
        oh = lane == ik
        work = jnp.where(oh, -jnp.inf, work)
        top_v.append(mk)
        top_i.append(ik)
        onehots.append(oh)

    ex = [jnp.exp(v - top_v[0]) for v in top_v]
    denom = ex[0] + ex[1] + ex[2] + ex[3]
    gates = [e / denom for e in ex]

    sel = jnp.zeros((tm, LANES), F32)
    for oh in onehots:
        sel = sel + oh.astype(F32)
    r = lax.broadcasted_iota(jnp.int32, (tm, tm), 0)
    c = lax.broadcasted_iota(jnp.int32, (tm, tm), 1)
    tri = jnp.where(c < r, 1.0, 0.0).astype(BF16)
    cum = _dot(tri, sel.astype(BF16)) + carry_ref[...]
    carry_ref[...] = carry_ref[...] + jnp.sum(sel, axis=0, keepdims=True)
    cnt_ref[...] = carry_ref[...]

    ri = jnp.zeros((tm, LANES), F32)
    rf = jnp.zeros((tm, LANES), F32)
    for k in range(TOP_K):
        rank_k = jnp.sum(jnp.where(onehots[k], cum, 0.0), axis=-1, keepdims=True)
        ri = jnp.where(lane == float(k), top_i[k], ri)
        ri = jnp.where(lane == float(TOP_K + k), rank_k, ri)
        rf = jnp.where(lane == float(k), gates[k], rf)
    ri_ref[...] = ri.T[:2 * TOP_K].astype(jnp.int32)
    rf_ref[...] = rf


def _mix_out(convn, attn, x2, w, *, tm):
    t, d = x2.shape
    kern = functools.partial(_mix_out_kernel, tm=tm)
    row = lambda width: pl.BlockSpec((tm, width), lambda i: (i, 0))
    return pl.pallas_call(
        kern,
        grid=(t // tm,),
        in_specs=[
            row(D_CONV), row(ATTN_W), row(d),
            _resident((D_CONV + ATTN_W, d)),
            _resident((1, d)),
            _resident((d, 2 * LANES)),
            _resident((1, LANES)),
        ],
        out_specs=[row(d), row(d), pl.BlockSpec((2 * TOP_K, tm), lambda i: (0, i)), row(LANES),
                   pl.BlockSpec((1, LANES), lambda i: (0, 0))],
        out_shape=[
            jax.ShapeDtypeStruct((t, d), F32),
            jax.ShapeDtypeStruct((t, d), F32),
            jax.ShapeDtypeStruct((2 * TOP_K, t), jnp.int32),
            jax.ShapeDtypeStruct((t, LANES), F32),
            jax.ShapeDtypeStruct((1, LANES), F32),
        ],
        scratch_shapes=[pltpu.VMEM((1, LANES), F32)],
        compiler_params=pltpu.CompilerParams(
            dimension_semantics=("arbitrary",), vmem_limit_bytes=VMEM_LIMIT),
        name="mix_out",
    )(convn, attn, x2, w["wo"], w["g_ffn"], w["wr"], w["br"])


def _dispatch_kernel(dest_ref, pend_ref, xn_ref, xg_hbm, zbuf_ref, sem, zsem, *, tmd, sb):
    i = pl.program_id(0)

    @pl.when(i == 0)
    def _():
        zbuf_ref[...] = jnp.zeros_like(zbuf_ref)

        def zero_copy(e):
            end = pend_ref[e]
            start = pl.multiple_of(jnp.maximum(end - sb, 0), sb)
            return pltpu.make_async_copy(zbuf_ref, xg_hbm.at[pl.ds(start, sb)], zsem)

        def nonempty(e):
            prev = pend_ref[jnp.maximum(e - 1, 0)]
            return pend_ref[e] > jnp.where(e > 0, prev, 0)

        def start(e, carry):
            @pl.when(nonempty(e))
            def _():
                zero_copy(e).start()
            return carry

        def wait(e, carry):
            @pl.when(nonempty(e))
            def _():
                zero_copy(e).wait()
            return carry

        lax.fori_loop(0, N_EXPERTS, start, 0)
        lax.fori_loop(0, N_EXPERTS, wait, 0)
        _zero_tail(zbuf_ref, xg_hbm, pend_ref[N_EXPERTS - 1], zsem)

    base = i * tmd
    n_tok = pl.num_programs(0) * tmd

    def issue(g, carry):
        r0 = pl.multiple_of(g * SUBLANES, SUBLANES)
        for j in range(SUBLANES):
            for k in range(TOP_K):
                dst = xg_hbm.at[pl.ds(dest_ref[k * n_tok + base + r0 + j], 1)]
                pltpu.make_async_copy(xn_ref.at[pl.ds(r0 + j, 1)], dst, sem).start(priority=k % 2)
        return carry

    lax.fori_loop(0, tmd // SUBLANES, issue, 0)
    for k in range(TOP_K):
        pltpu.make_async_copy(xn_ref, xg_hbm.at[pl.ds(0, tmd)], sem).wait()


def _dispatch(dest, pad_end, xn, *, rows, tmd, sb):
    t, d = xn.shape
    kern = functools.partial(_dispatch_kernel, tmd=tmd, sb=sb)
    return pl.pallas_call(
        kern,
        grid_spec=pltpu.PrefetchScalarGridSpec(
            num_scalar_prefetch=2,
            grid=(t // tmd,),
            in_specs=[pl.BlockSpec((tmd, d), lambda i, dest, pend: (i, 0))],
            out_specs=pl.BlockSpec(memory_space=pl.ANY),
            scratch_shapes=[pltpu.VMEM((sb, d), F32), pltpu.SemaphoreType.DMA, pltpu.SemaphoreType.DMA],
        ),
        out_shape=jax.ShapeDtypeStruct((rows, d), F32),
        compiler_params=pltpu.CompilerParams(
            dimension_semantics=("arbitrary",), vmem_limit_bytes=VMEM_LIMIT),
        name="dispatch",
    )(dest, pad_end, xn)


def _expert_kernel(ie_ref, irow_ref, inr_ref, tail_ref, xg_hbm, wg_ref, wl_ref, wd_ref, bias_ref, yg_hbm,
                   xst_ref, xb_ref, act_ref, acc_ref, sem_in, sem_out,
                   *, tm, sb, row_sizes):
    it = pl.program_id(0)
    s = pl.program_id(1)
    n_it = pl.num_programs(0)
    nrows = inr_ref[it]
    n_up = act_ref.shape[0]
    n_down, _, fc = acc_ref.shape
    last = n_up + n_down - 1
    prev_it = jnp.maximum(it - 1, 0)
    next_it = jnp.minimum(it + 1, n_it - 1)

    def piece(r):
        return pl.ds(r * sb, sb)

    def hbm_rows(item, r):
        return pl.ds(pl.multiple_of(irow_ref[item] + r * sb, sb), sb)

    def in_copy(item, r):
        return pltpu.make_async_copy(xg_hbm.at[hbm_rows(item, r)], xst_ref.at[piece(r)], sem_in.at[r])

    def out_copies(item, r):
        return [pltpu.make_async_copy(acc_ref.at[n, piece(r)], yg_hbm.at[hbm_rows(item, r), n * fc:(n + 1) * fc],
                                      sem_out.at[r]) for n in range(n_down)]

    def for_pieces_of(item, fn):
        for r in range(tm // sb):
            pl.when(r * sb < inr_ref[item])(functools.partial(fn, item, r))

    def for_valid_chunks(fn):
        for lo, size in zip((0,) + row_sizes[:-1], row_sizes):
            pl.when((lo < nrows) & (nrows <= size))(functools.partial(fn, pl.ds(0, size)))

    def start_in(item, r):
        in_copy(item, r).start()

    def land_in(item, r):
        in_copy(item, r).wait()
        xb_ref[piece(r)] = xst_ref[piece(r)].astype(BF16)

    def start_out(item, r):
        for cp in out_copies(item, r):
            cp.start()

    def wait_out(item, r):
        for cp in out_copies(item, r):
            cp.wait()

    @pl.when((it == 0) & (s == 0))
    def _():
        xb_ref[...] = jnp.zeros_like(xb_ref)
        for_pieces_of(it, start_in)

    @pl.when(s == 0)
    def _():
        for_pieces_of(it, land_in)

    @pl.when((s == 1) & (it + 1 < n_it))
    def _():
        for_pieces_of(next_it, start_in)

    @pl.when((s == n_up) & (it > 0))
    def _():
        for_pieces_of(prev_it, wait_out)

    def up(rows):
        xs = xb_ref[rows]
        bg = bias_ref[0, pl.ds(s, 1), :]
        bl = bias_ref[0, pl.ds(n_up + s, 1), :]
        g = jnp.minimum(_dot(xs, wg_ref[0].astype(BF16)) + bg, SWIGLU_LIMIT)
        lin = jnp.clip(_dot(xs, wl_ref[0].astype(BF16)) + bl, -SWIGLU_LIMIT, SWIGLU_LIMIT)
        act_ref[s, rows] = ((lin + 1.0) * (g * jax.nn.sigmoid(SWIGLU_ALPHA * g))).astype(BF16)

    def down(rows):
        act = jnp.concatenate([act_ref[jj, rows] for jj in range(n_up)], axis=-1)
        per_step = fc // act_ref.shape[2]
        first = 2 * n_up + (s - n_up) * per_step
        bd = jnp.concatenate([bias_ref[0, pl.ds(first + c, 1), :] for c in range(per_step)], axis=-1)
        acc_ref[s - n_up, rows] = _dot(act, wd_ref[0].astype(BF16)) + bd

    @pl.when(s < n_up)
    def _():
        for_valid_chunks(up)

    @pl.when(s >= n_up)
    def _():
        for_valid_chunks(down)

    @pl.when(s == last)
    def _():
        for_pieces_of(it, start_out)

    @pl.when((it == n_it - 1) & (s == last))
    def _():
        for_pieces_of(it, wait_out)
        acc_ref[0, piece(0)] = jnp.zeros((sb, fc), F32)
        _zero_tail(acc_ref.at[0, piece(0)], yg_hbm, tail_ref[0], sem_out.at[0])


def _zero_tail(zeros_vmem, dst_hbm, first_row, sem):
    sb, width = zeros_vmem.shape
    n_blocks = (dst_hbm.shape[0] - first_row) // sb

    def fill(b, carry):
        rows = pl.ds(pl.multiple_of(first_row + b * sb, sb), sb)
        copies = [pltpu.make_async_copy(zeros_vmem, dst_hbm.at[rows, c * width:(c + 1) * width], sem)
                  for c in range(dst_hbm.shape[1] // width)]
        for cp in copies:
            cp.start()
        for cp in copies:
            cp.wait()
        return carry

    lax.fori_loop(0, n_blocks, fill, 0)


def _experts(n_used, item_e, item_row, item_nrows, tail, xg, w_gate_up, b_gate_up, w_down, b_down, *, tm, sb, fc):
    rows, d = xg.shape
    dff = w_down.shape[1]
    n_up = dff // fc
    fd = 2 * fc
    n_down = d // fd
    nsub = tm // sb
    kern = functools.partial(_expert_kernel, tm=tm, sb=sb, row_sizes=_row_sizes(tm, sb))
    n_e = w_down.shape[0]
    bias_rows = [b_gate_up.reshape(n_e, 2 * n_up, fc), b_down.reshape(n_e, d // fc, fc)]
    n_rows = 2 * n_up + d // fc
    bias_rows.append(jnp.zeros((n_e, -n_rows % SUBLANES, fc), F32))
    biases = jnp.concatenate(bias_rows, axis=1)

    def up_block(offset):
        def index_map(it, s, ie, ir, inr, tl):
            ahead = s >= n_up
            nxt = jnp.minimum(it + 1, tl[1] - 1)
            return jnp.where(ahead, ie[nxt], ie[it]), 0, offset + jnp.where(ahead, 0, s)
        return index_map

    def down_block(it, s, ie, ir, inr, tl):
        return ie[it], 0, jnp.maximum(s - n_up, 0)

    return pl.pallas_call(
        kern,
        grid_spec=pltpu.PrefetchScalarGridSpec(
            num_scalar_prefetch=4,
            grid=(n_used, n_up + n_down),
            in_specs=[
                pl.BlockSpec(memory_space=pl.ANY),
                pl.BlockSpec((1, d, fc), up_block(0)),
                pl.BlockSpec((1, d, fc), up_block(n_up)),
                pl.BlockSpec((1, dff, fd), down_block),
                pl.BlockSpec((1, biases.shape[1], fc), lambda it, s, ie, ir, inr, tl: (ie[it], 0, 0)),
            ],
            out_specs=pl.BlockSpec(memory_space=pl.ANY),
            scratch_shapes=[
                pltpu.VMEM((tm, d), F32),
                pltpu.VMEM((tm, d), BF16),
                pltpu.VMEM((n_up, tm, fc), BF16),
                pltpu.VMEM((n_down, tm, fd), F32),
                pltpu.SemaphoreType.DMA((nsub,)),
                pltpu.SemaphoreType.DMA((nsub,)),
            ],
        ),
        out_shape=jax.ShapeDtypeStruct((rows, d), F32),
        compiler_params=pltpu.CompilerParams(
            dimension_semantics=("arbitrary", "arbitrary"), vmem_limit_bytes=VMEM_LIMIT),
        name="experts",
    )(item_e, item_row, item_nrows, tail, xg, w_gate_up, w_gate_up, w_down, biases)


def _combine_kernel(dest_ref, yg_hbm, h1_ref, rf_ref, gfin_ref, o_ref, gbuf_ref, sem, *, tmc):
    i = pl.program_id(0)
    n_tiles = pl.num_programs(0)
    n_tok = n_tiles * tmc

    def issue_tile(tile):
        buf = tile % 2
        base = tile * tmc

        def issue(g, carry):
            r0 = pl.multiple_of(g * SUBLANES, SUBLANES)
            for j in range(SUBLANES):
                for k in range(TOP_K):
                    src = yg_hbm.at[pl.ds(dest_ref[k * n_tok + base + r0 + j], 1)]
                    dst = gbuf_ref.at[buf, k, pl.ds(r0 + j, 1)]
                    pltpu.make_async_copy(src, dst, sem.at[buf]).start(priority=k % 2)
            return carry

        lax.fori_loop(0, tmc // SUBLANES, issue, 0)

    @pl.when(i == 0)
    def _():
        issue_tile(i)

    @pl.when(i + 1 < n_tiles)
    def _():
        issue_tile(i + 1)

    buf = i % 2
    for k in range(TOP_K):
        pltpu.make_async_copy(yg_hbm.at[pl.ds(0, tmc)], gbuf_ref.at[buf, k], sem.at[buf]).wait()

    gates = rf_ref[...]
    y = h1_ref[...]
    for k in range(TOP_K):
        y = y + gates[:, k:k + 1] * gbuf_ref[buf, k]
    o_ref[...] = _rms(y, gfin_ref[...])


def _combine(dest, yg, h1, rf, g_final, *, tmc):
    t, d = h1.shape
    kern = functools.partial(_combine_kernel, tmc=tmc)
    return pl.pallas_call(
        kern,
        grid_spec=pltpu.PrefetchScalarGridSpec(
            num_scalar_prefetch=1,
            grid=(t // tmc,),
            in_specs=[
                pl.BlockSpec(memory_space=pl.ANY),
                pl.BlockSpec((tmc, d), lambda i, dest: (i, 0)),
                pl.BlockSpec((tmc, LANES), lambda i, dest: (i, 0)),
                pl.BlockSpec((1, d), lambda i, dest: (0, 0)),
            ],
            out_specs=pl.BlockSpec((tmc, d), lambda i, dest: (i, 0)),
            scratch_shapes=[pltpu.VMEM((2, TOP_K, tmc, d), F32), pltpu.SemaphoreType.DMA((2,))],
        ),
        out_shape=jax.ShapeDtypeStruct((t, d), F32),
        compiler_params=pltpu.CompilerParams(
            dimension_semantics=("arbitrary",), vmem_limit_bytes=VMEM_LIMIT),
        name="combine",
    )(dest, yg, h1, rf, g_final)


def _cast_kernel(x_ref, o_ref):
    o_ref[...] = x_ref[...].astype(o_ref.dtype)


def _to_bf16(w, *, n_blocks):
    rows, cols = w.shape
    spec = pl.BlockSpec((rows // n_blocks, cols), lambda i: (i, 0))
    return pl.pallas_call(
        _cast_kernel, grid=(n_blocks,), in_specs=[spec], out_specs=spec,
        out_shape=jax.ShapeDtypeStruct(w.shape, BF16),
        compiler_params=pltpu.CompilerParams(dimension_semantics=("arbitrary",), vmem_limit_bytes=VMEM_LIMIT),
        name="to_bf16",
    )(w)


def _rotate_half_cols(w):
    half = w.shape[-1] // 2
    return jnp.concatenate([-w[..., half:], w[..., :half]], axis=-1)


def _prep_weights(g_mix, w_in, conv_w, g_q, w_q_up, g_kv, w_kv_up, g_conv_out, w_out, g_ffn, w_router, b_router):
    d = w_in.shape[0]
    w1 = _to_bf16(w_in.T, n_blocks=4)
    k_rope = w1[O_KPE:O_KPE + QK_ROPE]
    pad = jnp.zeros((LANES - QK_ROPE, d), BF16)
    half = QK_ROPE // 2
    k_rot = jnp.concatenate([-k_rope[half:], k_rope[:half]], axis=0)
    wk2 = jnp.concatenate([k_rope, pad, k_rot, pad], axis=0)

    wq = w_q_up.reshape(Q_LORA, N_HEADS, QK_NOPE + QK_ROPE)
    nope, pe = wq[:, :, :QK_NOPE], wq[:, :, QK_NOPE:]
    hpad = jnp.zeros((Q_LORA, N_HEADS, LANES - QK_ROPE), F32)
    wqa = jnp.concatenate([nope, pe, hpad], axis=2).reshape(Q_LORA, N_HEADS * HEAD_W)
    wqb = jnp.concatenate([_rotate_half_cols(pe), hpad], axis=2).reshape(Q_LORA, N_HEADS * LANES)
    wq_all = jnp.concatenate([wqa, wqb], axis=1).astype(BF16)

    wkv = w_kv_up.reshape(KV_LORA, N_HEADS, QK_NOPE + V_DIM)
    wkv2 = jnp.concatenate([wkv[:, :, :QK_NOPE].reshape(KV_LORA, ATTN_W),
                            wkv[:, :, QK_NOPE:].reshape(KV_LORA, ATTN_W)], axis=1).astype(BF16)

    wr_pad = jnp.zeros((d, LANES), F32).at[:, :N_EXPERTS].set(w_router)
    wr_hi = wr_pad.astype(BF16)
    wr_lo = (wr_pad - wr_hi.astype(F32)).astype(BF16)
    br = jnp.full((1, LANES), -1e30, F32).at[0, :N_EXPERTS].set(b_router)

    grp = jnp.arange(D_CONV) // (D_CONV // CONV_GROUPS)
    cw = jnp.zeros((SUBLANES, D_CONV), F32).at[:CONV_WIDTH].set(conv_w)
    return {
        "g_mix": g_mix[None], "w1": w1, "wk2": wk2, "conv_w": cw, "g_q": g_q[None], "wq": wq_all, "g_kv": g_kv[None],
        "wkv": wkv2, "g_conv_out": g_conv_out[None], "gmat": (grp[:, None] == grp[None, :]).astype(BF16),
        "wo": w_out.astype(BF16), "g_ffn": g_ffn[None], "wr": jnp.concatenate([wr_hi, wr_lo], axis=1), "br": br,
    }


def _rope_table(pos):
    half = QK_ROPE // 2
    inv_freq = np.float32(ROPE_THETA) ** (-np.arange(half, dtype=np.float32) / np.float32(half))
    ang = (np.asarray(pos, np.float32)[:, None] * inv_freq[None, :]).astype(np.float32)
    c, s = np.cos(ang), np.sin(ang)
    z = np.zeros((ang.shape[0], LANES - QK_ROPE), np.float32)
    return jnp.asarray(np.concatenate([c, c, z, s, s, z], axis=1), dtype=F32)


def _schedule_kernel(cnt_ref, ri_ref, dest_ref, pend_ref, ie_ref, irow_ref, inr_ref, tail_ref, pstart_ref,
                     *, tm, sb, n_items):
    def per_expert(e, carry):
        row, item = carry
        count = cnt_ref[e]
        pstart_ref[e] = row

        def per_item(li, item):
            ie_ref[item] = e
            irow_ref[item] = row + li * tm
            inr_ref[item] = jnp.minimum(count - li * tm, tm)
            return item + 1

        item = lax.fori_loop(0, lax.div(count + (tm - 1), tm), per_item, item)
        row = row + lax.div(count + (sb - 1), sb) * sb
        pend_ref[e] = row
        return row, item

    row, n_used = lax.fori_loop(0, N_EXPERTS, per_expert, (jnp.int32(0), jnp.int32(0)))
    tail_ref[0] = row
    tail_ref[1] = n_used

    def unused(item, carry):
        ie_ref[item] = 0
        irow_ref[item] = 0
        inr_ref[item] = 0
        return carry

    lax.fori_loop(n_used, n_items, unused, 0)

    eidx = ri_ref[:TOP_K]
    start_of = jnp.zeros_like(eidx)
    for e in range(N_EXPERTS):
        start_of = jnp.where(eidx == e, pstart_ref[e], start_of)
    dest_ref[...] = start_of + ri_ref[TOP_K:]


def _schedule(counts, ri, *, tm, sb, n_items):
    t = ri.shape[1]
    smem = pl.BlockSpec(memory_space=pltpu.SMEM)
    i32 = lambda n: jax.ShapeDtypeStruct((n,), jnp.int32)
    dest, pad_end, item_e, item_row, item_nrows, tail = pl.pallas_call(
        functools.partial(_schedule_kernel, tm=tm, sb=sb, n_items=n_items),
        in_specs=[smem, pl.BlockSpec(memory_space=pltpu.VMEM)],
        out_specs=[pl.BlockSpec(memory_space=pltpu.VMEM), smem, smem, smem, smem, smem],
        out_shape=[jax.ShapeDtypeStruct((TOP_K, t), jnp.int32), i32(N_EXPERTS), i32(n_items), i32(n_items),
                   i32(n_items), i32(2)],
        scratch_shapes=[pltpu.SMEM((N_EXPERTS,), jnp.int32)],
        name="schedule",
    )(counts, ri)
    return dest.reshape(-1), pad_end, item_e, item_row, item_nrows, tail


def _row_sizes(tm, sb):
    fine = [tm - k * sb for k in (2, 1, 0) if tm - k * sb > 0]
    coarse = [s for s in (tm // 4 // sb * sb, tm // 2 // sb * sb) if 0 < s < min(fine)]
    return tuple(sorted(set(coarse + fine)))


def _moe_tiles(t):
    sb = 128
    tm = 10 * sb
    a = t * TOP_K
    rows = (a + N_EXPERTS * (sb - 1) + sb - 1) // sb * sb
    n_items = N_EXPERTS + a // tm
    return sb, tm, rows, n_items


def _layer(x, meta_tokens, w, w_gate_up, b_gate_up, w_down, b_down, g_attn_out, g_final, *,
           tm_in, tq, tm_out, tmd, tmc, fc):
    b, seq, d = x.shape
    t = b * seq

    meta_blk = jnp.zeros((1, META_ROWS, d), F32).at[0, META_ROWS - N_META:].set(meta_tokens)
    meta_pos = np.maximum(np.arange(META_ROWS) - (META_ROWS - N_META), 0)
    mkn, mkpe, mv, u_tail = _mix_meta(meta_blk, _rope_table(meta_pos), w)

    real_pos = np.arange(seq) + N_META
    convn, q, kn, kpe, v, _ = _mix_in(x, u_tail, _rope_table(real_pos), w, tm=tm_in)
    attn = _attention(q, kn, kpe, v, mkn, mkpe, mv, g_attn_out[None], tq=tq)

    h1, xn, ri, rf, cnt = _mix_out(convn.reshape(t, D_CONV), attn.reshape(t, ATTN_W), x.reshape(t, d), w, tm=tm_out)

    sb, tm_e, rows, n_items = _moe_tiles(t)
    counts = cnt[0, :N_EXPERTS].astype(jnp.int32)
    dest, pad_end, item_e, item_row, item_nrows, tail = _schedule(counts, ri, tm=tm_e, sb=sb, n_items=n_items)

    xg = _dispatch(dest, pad_end, xn, rows=rows, tmd=tmd, sb=sb)
    dff = w_down.shape[1]
    yg = _experts(tail[1], item_e, item_row, item_nrows, tail, xg, w_gate_up, b_gate_up.reshape(N_EXPERTS, 1, 2 * dff),
                  w_down, b_down.reshape(N_EXPERTS, 1, d), tm=tm_e, sb=sb, fc=fc)
    out = _combine(dest, yg, h1, rf, g_final[None], tmc=tmc)
    return out.reshape(b, seq, d)


def kernel(x, meta_tokens, g_mix, w_in, conv_w, g_q, w_q_up, g_kv, w_kv_up, g_conv_out, g_attn_out, w_out, g_ffn,
           w_router, b_router, w_gate_up, b_gate_up, w_down, b_down, g_final):
    w = _prep_weights(g_mix[0], w_in[0], conv_w[0], g_q[0], w_q_up[0], g_kv[0], w_kv_up[0], g_conv_out[0],
                      w_out[0], g_ffn[0], w_router[0], b_router[0])
    seq = x.shape[1]
    return _layer(x, meta_tokens, w, w_gate_up[0], b_gate_up[0], w_down[0], b_down[0], g_attn_out[0], g_final,
                  tm_in=min(512, seq), tq=min(512, seq), tm_out=min(512, seq), tmd=min(1024, seq),
                  tmc=min(512, seq), fc=256)
```
